```python
import math
import jax, jax.numpy as jnp
from jax import lax
import numpy as np

D_MODEL = 1024
BATCH = 8
SEQ = 2048
DEPTH = 2
DEC_BATCH = 128
DEC_SEQ = 4
PAST_LEN = 2048
PAGE_SIZE = 128

N_HEADS_A = 4
HEAD_DIM_A = D_MODEL // 8
RET_CHUNK = 128
N_HEADS_B = 4
HEAD_DIM_B = D_MODEL // 8
MOBA_BLOCK = 256
MOBA_TOPK = 3
MOBA_QCHUNK = 64
ROPE_THETA = 10000.0
GROUP_W = N_HEADS_A * HEAD_DIM_A
N_IN_COLS = 7
CONV_W = 31
D_CONF = D_MODEL
D_FF = 11 * D_MODEL // 4
FFN_CONV_W = 3
N_EVEN = (DEPTH + 1) // 2
N_ODD = DEPTH // 2
ALPHA = (2 * DEPTH) ** 0.25
BETA = (8 * DEPTH) ** -0.25
LN_EPS = 1e-5
GN_EPS = 1e-6

kernel_name = 'retnet_moba_conformer_convffn_decode_step'


def layer_norm(x, g, b):
    xf = x.astype(jnp.float32)
    mu = jnp.mean(xf, -1, keepdims=True)
    var = jnp.mean(jnp.square(xf - mu), -1, keepdims=True)
    y = (xf - mu) * lax.rsqrt(var + LN_EPS) * g.astype(jnp.float32) + b.astype(jnp.float32)
    return y.astype(x.dtype)


def adaln(c, w, b):
    return (jax.nn.silu(c) @ w + b).reshape(c.shape[0], 6, D_MODEL)


def modulate(x, shift, scale):
    return x * (1.0 + scale[:, None, :]) + shift[:, None, :]


def post_norm(x, y, gate, g, b):
    return layer_norm(ALPHA * x + (1.0 + gate[:, None, :]) * y, g, b)


def rope_rotate_half(x, pos):
    d = x.shape[-1]
    inv = ROPE_THETA ** (-jnp.arange(0, d, 2, dtype=jnp.float32) / d)
    ang = pos.astype(jnp.float32)[:, None] * inv[None, :]
    cos = jnp.cos(ang)[None, :, None, :]
    sin = jnp.sin(ang)[None, :, None, :]
    xf = x.astype(jnp.float32)
    x1, x2 = xf[..., : d // 2], xf[..., d // 2:]
    return jnp.concatenate([x1 * cos - x2 * sin, x2 * cos + x1 * sin], -1).astype(x.dtype)


def retnet_rotate(x, pos):
    d = x.shape[-1]
    inv = 1.0 / (ROPE_THETA ** jnp.linspace(0.0, 1.0, d // 2, dtype=jnp.float32))
    ang = pos.astype(jnp.float32)[:, None] * inv[None, :]
    cos = jnp.cos(ang)[None, :, None, :]
    sin = jnp.sin(ang)[None, :, None, :]
    xf = x.astype(jnp.float32).reshape(x.shape[:-1] + (d // 2, 2))
    x0, x1 = xf[..., 0], xf[..., 1]
    out = jnp.stack([x0 * cos - x1 * sin, x1 * cos + x0 * sin], -1).reshape(x.shape)
    return out.astype(x.dtype)


def chunk_retention(q, k, v, s0, chunk):
    B, H, T, dk = q.shape
    dv = v.shape[-1]
    n = T // chunk
    log_g = jnp.log1p(-jnp.exp2(-5.0 - jnp.arange(H, dtype=jnp.float32)))
    idx = jnp.arange(chunk, dtype=jnp.float32)
    diff = idx[:, None] - idx[None, :]
    decay_in = jnp.where(diff[None] >= 0, jnp.exp(jnp.maximum(diff, 0.0)[None] * log_g[:, None, None]), 0.0)
    q_dec = jnp.exp((idx + 1.0)[None, :] * log_g[:, None])
    k_dec = jnp.exp((chunk - 1.0 - idx)[None, :] * log_g[:, None])
    c_dec = jnp.exp(chunk * log_g)

    def step(s, inp):
        qc, kc, vc = inp
        att = jnp.einsum('bhtd,bhsd->bhts', qc, kc) * decay_in
        o = jnp.einsum('bhts,bhsv->bhtv', att, vc) + jnp.einsum('bhtd,bhdv->bhtv', qc * q_dec[:, :, None], s)
        s = c_dec[:, None, None] * s + jnp.einsum('bhsd,bhsv->bhdv', kc * k_dec[:, :, None], vc)
        return s, o

    split = lambda t: t.reshape(B, H, n, chunk, t.shape[-1]).transpose(2, 0, 1, 3, 4)
    s_final, o = lax.scan(step, s0, (split(q), split(k), split(v)))
    o = o.transpose(1, 2, 0, 3, 4).reshape(B, H, T, dv)
    return o, s_final


def moba_attend(q, k_all, v_all, q_pos, q_chunk):
    B, Tq, H, Dh = q.shape
    L = k_all.shape[1]
    nblk = -(-L // MOBA_BLOCK)
    pad = nblk * MOBA_BLOCK - L
    to_blocks = lambda t: jnp.pad(t, ((0, 0), (0, pad), (0, 0), (0, 0))).reshape(B, nblk, MOBA_BLOCK, H, Dh).transpose(0, 3, 1, 2, 4)
    kb = to_blocks(k_all)
    vb = to_blocks(v_all)
    kmean = jnp.mean(kb.astype(jnp.float32), axis=3)
    n_top = min(MOBA_TOPK, nblk)
    bi = jnp.arange(B)[:, None, None, None]
    hi = jnp.arange(H)[None, :, None, None]
    blk_ids = jnp.arange(nblk)
    scale = Dh ** -0.5

    def attend(args):
        qc, pc = args
        qcn = qc.shape[1]
        own = pc // MOBA_BLOCK
        gate = jnp.einsum('bthd,bhnd->bhtn', qc.astype(jnp.float32), kmean)
        gate = jnp.where(blk_ids[None, None, None, :] < own[None, None, :, None], gate, -jnp.inf)
        _, top = lax.top_k(gate, n_top)
        sel = jnp.concatenate([top.astype(jnp.int32), jnp.broadcast_to(own[None, None, :, None], (B, H, qcn, 1)).astype(jnp.int32)], -1)
        top_ok = jnp.broadcast_to((jnp.arange(n_top)[None, :] < own[:, None])[None, None], (B, H, qcn, n_top))
        sel_ok = jnp.concatenate([top_ok, jnp.ones((B, H, qcn, 1), bool)], -1)
        kg = kb[bi, hi, sel]
        vg = vb[bi, hi, sel]
        kpos = sel[..., None] * MOBA_BLOCK + jnp.arange(MOBA_BLOCK, dtype=jnp.int32)
        mask = sel_ok[..., None] & (kpos <= pc[None, None, :, None, None])
        logits = jnp.einsum('bthd,bhtskd->bhtsk', qc, kg, preferred_element_type=jnp.float32) * scale
        logits = jnp.where(mask, logits, -jnp.inf)
        p = jax.nn.softmax(logits.reshape(B, H, qcn, -1), axis=-1).reshape(logits.shape)
        o = jnp.einsum('bhtsk,bhtskd->bthd', p.astype(vg.dtype), vg, preferred_element_type=jnp.float32)
        return o.astype(q.dtype)

    nq = Tq // q_chunk
    qs = q.reshape(B, nq, q_chunk, H, Dh).transpose(1, 0, 2, 3, 4)
    ps = q_pos.reshape(nq, q_chunk)
    out = lax.map(attend, (qs, ps))
    return out.transpose(1, 0, 2, 3, 4).reshape(B, Tq, H, Dh)


def ab_mixer(h, pos, k_past, v_past, ret_state, w_in, w_out, q_chunk):
    B, T, _ = h.shape
    z = h @ w_in
    rq, rk, rv, rg, mq, mk, mv = jnp.split(z, N_IN_COLS, axis=-1)
    rq = retnet_rotate(rq.reshape(B, T, N_HEADS_A, HEAD_DIM_A), pos)
    rk = retnet_rotate(rk.reshape(B, T, N_HEADS_A, HEAD_DIM_A), pos) * (HEAD_DIM_A ** -0.5)
    rv = rv.reshape(B, T, N_HEADS_A, HEAD_DIM_A)
    to_bhtd = lambda t: t.astype(jnp.float32).transpose(0, 2, 1, 3)
    o_r, s_new = chunk_retention(to_bhtd(rq), to_bhtd(rk), to_bhtd(rv), ret_state.astype(jnp.float32), min(RET_CHUNK, T))
    o_r = o_r.transpose(0, 2, 1, 3)
    o_r = o_r * lax.rsqrt(jnp.mean(jnp.square(o_r), -1, keepdims=True) + GN_EPS)
    o_r = o_r.reshape(B, T, GROUP_W).astype(h.dtype) * jax.nn.silu(rg)
    mq = rope_rotate_half(mq.reshape(B, T, N_HEADS_B, HEAD_DIM_B), pos)
    mk = rope_rotate_half(mk.reshape(B, T, N_HEADS_B, HEAD_DIM_B), pos)
    mv = mv.reshape(B, T, N_HEADS_B, HEAD_DIM_B)
    if k_past is None:
        k_all, v_all = mk, mv
    else:
        k_all = jnp.concatenate([k_past.astype(mk.dtype), mk], axis=1)
        v_all = jnp.concatenate([v_past.astype(mv.dtype), mv], axis=1)
    o_m = moba_attend(mq, k_all, v_all, pos, q_chunk).reshape(B, T, GROUP_W)
    y = jnp.concatenate([o_r, o_m], axis=-1) @ w_out
    return y, mk, mv, s_new.astype(h.dtype)


def causal_dwconv(x, prev, w, b):
    width = w.shape[0]
    xp = jnp.concatenate([prev.astype(x.dtype), x], axis=1)
    y = lax.conv_general_dilated(xp, w[:, None, :].astype(x.dtype), window_strides=(1,), padding='VALID',
                                 dimension_numbers=('NWC', 'WIO', 'NWC'), feature_group_count=x.shape[-1])
    return y + b, xp[:, xp.shape[1] - (width - 1):]


def conformer_conv(h, prev, w1, b1, w_dw, b_dw, g_ln, b_ln, w2, b2):
    a, g = jnp.split(h @ w1 + b1, 2, axis=-1)
    glu = a * jax.nn.sigmoid(g)
    y, new_prev = causal_dwconv(glu, prev, w_dw, b_dw)
    y = jax.nn.silu(layer_norm(y, g_ln, b_ln))
    return y @ w2 + b2, new_prev


def conv_ffn(h, prev, w_up, w_dw, b_dw, w_down):
    u, v = jnp.split(h @ w_up, 2, axis=-1)
    uc, new_prev = causal_dwconv(u, prev, w_dw, b_dw)
    return (jax.nn.gelu(uc, approximate=False) * v) @ w_down, new_prev


def setup_inputs(seed: int = 0) -> dict:
    key = jax.random.key(seed)
    ks = jax.random.split(key, 32)
    n_pages = PAST_LEN // PAGE_SIZE
    n_used = DEC_BATCH * n_pages
    n_phys = n_used + n_used // 4
    nrm = lambda k, shape, s: jax.random.normal(k, shape, jnp.float32) * s
    page_table = jax.random.permutation(ks[7], n_phys)[:n_used].reshape(DEC_BATCH, n_pages).astype(jnp.int32)
    return {
        'x_prompt': nrm(ks[0], (BATCH, SEQ, D_MODEL), 1.0),
        'x_sample': nrm(ks[1], (DEC_BATCH, DEC_SEQ, D_MODEL), 1.0),
        'cache_k': nrm(ks[2], (N_EVEN, n_phys, PAGE_SIZE, N_HEADS_B, HEAD_DIM_B), 1.0),
        'cache_v': nrm(ks[3], (N_EVEN, n_phys, PAGE_SIZE, N_HEADS_B, HEAD_DIM_B), 1.0),
        'state_ret': nrm(ks[4], (N_EVEN, DEC_BATCH, N_HEADS_A, HEAD_DIM_A, HEAD_DIM_A), 0.5),
        'state_conv': nrm(ks[5], (N_ODD, DEC_BATCH, CONV_W - 1, D_CONF), 0.5),
        'state_ffn': nrm(ks[6], (DEPTH, DEC_BATCH, FFN_CONV_W - 1, D_FF), 1.0),
        'page_table': page_table,
        'c_prompt': nrm(ks[8], (BATCH, D_MODEL), 1.0),
        'c_sample': nrm(ks[9], (DEC_BATCH, D_MODEL), 1.0),
        'ab_w_in': nrm(ks[10], (N_EVEN, D_MODEL, N_IN_COLS * GROUP_W), D_MODEL ** -0.5),
        'ab_w_out': nrm(ks[11], (N_EVEN, 2 * GROUP_W, D_MODEL), BETA * (2 * GROUP_W) ** -0.5),
        'cf_w_pw1': nrm(ks[12], (N_ODD, D_MODEL, 2 * D_CONF), D_MODEL ** -0.5),
        'cf_b_pw1': nrm(ks[13], (N_ODD, 2 * D_CONF), 0.01),
        'cf_w_dw': nrm(ks[14], (N_ODD, CONV_W, D_CONF), CONV_W ** -0.5),
        'cf_b_dw': nrm(ks[15], (N_ODD, D_CONF), 0.01),
        'cf_ln_g': 1.0 + nrm(ks[16], (N_ODD, D_CONF), 0.01),
        'cf_ln_b': nrm(ks[17], (N_ODD, D_CONF), 0.01),
        'cf_w_pw2': nrm(ks[18], (N_ODD, D_CONF, D_MODEL), BETA * D_CONF ** -0.5),
        'cf_b_pw2': nrm(ks[19], (N_ODD, D_MODEL), 0.01),
        'ffn_w_up': nrm(ks[20], (DEPTH, D_MODEL, 2 * D_FF), D_MODEL ** -0.5),
        'ffn_w_dw': nrm(ks[21], (DEPTH, FFN_CONV_W, D_FF), FFN_CONV_W ** -0.5),
        'ffn_b_dw': nrm(ks[22], (DEPTH, D_FF), 0.01),
        'ffn_w_down': nrm(ks[23], (DEPTH, D_FF, D_MODEL), BETA * D_FF ** -0.5),
        'ada_w': nrm(ks[24], (DEPTH, D_MODEL, 6 * D_MODEL), 0.1 * D_MODEL ** -0.5),
        'ada_b': nrm(ks[25], (DEPTH, 6 * D_MODEL), 0.01),
        'ln_g': 1.0 + nrm(ks[26], (DEPTH, 2, D_MODEL), 0.01),
        'ln_b': nrm(ks[27], (DEPTH, 2, D_MODEL), 0.01),
    }


def reference(x_prompt, x_sample, cache_k, cache_v, state_ret, state_conv, state_ffn, page_table, c_prompt, c_sample,
              ab_w_in, ab_w_out, cf_w_pw1, cf_b_pw1, cf_w_dw, cf_b_dw, cf_ln_g, cf_ln_b, cf_w_pw2, cf_b_pw2,
              ffn_w_up, ffn_w_dw, ffn_b_dw, ffn_w_down, ada_w, ada_b, ln_g, ln_b):
    n_pages = PAST_LEN // PAGE_SIZE
    pos_p = jnp.arange(SEQ, dtype=jnp.int32)
    pos_s = PAST_LEN + jnp.arange(DEC_SEQ, dtype=jnp.int32)
    xp, xs = x_prompt, x_sample
    bp = x_prompt.shape[0]
    kp_l, vp_l, ks_l, vs_l, rp_l, rs_l, cp_l, cs_l, fp_l, fs_l = ([] for _ in range(10))
    for l in range(DEPTH):
        mp = adaln(c_prompt, ada_w[l], ada_b[l])
        ms = adaln(c_sample, ada_w[l], ada_b[l])
        hp = modulate(xp, mp[:, 0], mp[:, 1])
        hs = modulate(xs, ms[:, 0], ms[:, 1])
        i = l // 2
        if l % 2 == 0:
            k_past = cache_k[i][page_table].reshape(DEC_BATCH, n_pages * PAGE_SIZE, N_HEADS_B, HEAD_DIM_B)
            v_past = cache_v[i][page_table].reshape(DEC_BATCH, n_pages * PAGE_SIZE, N_HEADS_B, HEAD_DIM_B)
            s0 = jnp.zeros((bp, N_HEADS_A, HEAD_DIM_A, HEAD_DIM_A), jnp.float32)
            yp, kp, vp, rp = ab_mixer(hp, pos_p, None, None, s0, ab_w_in[i], ab_w_out[i], min(MOBA_QCHUNK, SEQ))
            ys, ksn, vsn, rs = ab_mixer(hs, pos_s, k_past, v_past, state_ret[i], ab_w_in[i], ab_w_out[i], 1)
            kp_l.append(kp); vp_l.append(vp); ks_l.append(ksn); vs_l.append(vsn)
            rp_l.append(rp); rs_l.append(rs)
        else:
            c0 = jnp.zeros((bp, CONV_W - 1, D_CONF), xp.dtype)
            yp, cp = conformer_conv(hp, c0, cf_w_pw1[i], cf_b_pw1[i], cf_w_dw[i], cf_b_dw[i], cf_ln_g[i], cf_ln_b[i], cf_w_pw2[i], cf_b_pw2[i])
            ys, cs = conformer_conv(hs, state_conv[i], cf_w_pw1[i], cf_b_pw1[i], cf_w_dw[i], cf_b_dw[i], cf_ln_g[i], cf_ln_b[i], cf_w_pw2[i], cf_b_pw2[i])
            cp_l.append(cp); cs_l.append(cs)
        xp = post_norm(xp, yp, mp[:, 2], ln_g[l, 0], ln_b[l, 0])
        xs = post_norm(xs, ys, ms[:, 2], ln_g[l, 0], ln_b[l, 0])
        hp = modulate(xp, mp[:, 3], mp[:, 4])
        hs = modulate(xs, ms[:, 3], ms[:, 4])
        f0 = jnp.zeros((bp, FFN_CONV_W - 1, D_FF), xp.dtype)
        fyp, fp = conv_ffn(hp, f0, ffn_w_up[l], ffn_w_dw[l], ffn_b_dw[l], ffn_w_down[l])
        fys, fs = conv_ffn(hs, state_ffn[l], ffn_w_up[l], ffn_w_dw[l], ffn_b_dw[l], ffn_w_down[l])
        fp_l.append(fp); fs_l.append(fs)
        xp = post_norm(xp, fyp, mp[:, 5], ln_g[l, 1], ln_b[l, 1])
        xs = post_norm(xs, fys, ms[:, 5], ln_g[l, 1], ln_b[l, 1])
    return (xp, xs, jnp.stack(kp_l), jnp.stack(vp_l), jnp.stack(ks_l), jnp.stack(vs_l),
            jnp.stack(rp_l), jnp.stack(rs_l), jnp.stack(cp_l), jnp.stack(cs_l), jnp.stack(fp_l), jnp.stack(fs_l))
```

```python
import functools

import jax
import jax.numpy as jnp
from jax import lax
from jax.experimental import pallas as pl
from jax.experimental.pallas import tpu as pltpu

HEAD_DIM = 128
N_HEADS = 4
GROUP_W = N_HEADS * HEAD_DIM
N_IN_COLS = 7
RET_CHUNK = 128
MOBA_BLOCK = 256
MOBA_TOPK = 3
ROPE_THETA = 10000.0
DEPTH = 2
ALPHA = (2 * DEPTH) ** 0.25
LN_EPS = 1e-5
GN_EPS = 1e-6
NEG = -1e30

F32 = jnp.float32
BF16 = jnp.bfloat16

_NT = (((1,), (1,)), ((), ()))
_TN = (((0,), (0,)), ((), ()))

VMEM_LIMIT = 56 * 1024 * 1024
SUBLANES = 8


def _params(*sem):
    return pltpu.CompilerParams(dimension_semantics=sem, vmem_limit_bytes=VMEM_LIMIT)


def _dot(a, b):
    return jnp.dot(a, b, preferred_element_type=F32)


def _layer_norm(v, g, b):
    mu = jnp.mean(v, axis=-1, keepdims=True)
    d = v - mu
    var = jnp.mean(d * d, axis=-1, keepdims=True)
    return d * lax.rsqrt(var + LN_EPS) * g + b


def _silu(x):
    return x * jax.nn.sigmoid(x)


def _mod_spec(mod, tm, rows_per_group):
    _, r, d = mod.shape
    if r == 1:
        return pl.BlockSpec((None, 1, d), lambda i: (i * tm // rows_per_group, 0, 0))
    return pl.BlockSpec((None, tm, d), lambda i: (0, i, 0))


def _resident(a):
    return pl.BlockSpec(a.shape, lambda *_: (0,) * a.ndim, pipeline_mode=pl.Buffered(1))


def _rows_to_tile(x, n_rows):
    t, w = x.shape
    r_i = lax.broadcasted_iota(jnp.int32, (n_rows, w), 0)
    out = jnp.zeros((n_rows, w), x.dtype)
    for ti in range(t):
        out = jnp.where(r_i == ti, x[ti:ti + 1, :], out)
    return out


def _adaln_kernel(c_ref, w_ref, b_ref, o_ref):
    a = _silu(c_ref[...]).astype(BF16)
    o_ref[...] = _dot(a, w_ref[...].astype(BF16)) + b_ref[...]


def adaln_all(c_all, ada_w, ada_b):
    n_layers, d, n_out = ada_w.shape
    m = c_all.shape[0]
    tn = 1536 if n_out % 1536 == 0 else n_out
    return pl.pallas_call(
        _adaln_kernel,
        grid=(n_layers, n_out // tn),
        in_specs=[
            pl.BlockSpec((m, d), lambda l, j: (0, 0)),
            pl.BlockSpec((None, d, tn), lambda l, j: (l, 0, j)),
            pl.BlockSpec((None, 1, tn), lambda l, j: (l, 0, j)),
        ],
        out_specs=pl.BlockSpec((None, m, tn), lambda l, j: (l, 0, j)),
        out_shape=jax.ShapeDtypeStruct((n_layers, m, n_out), F32),
        compiler_params=_params("parallel", "parallel"),
        name="adaln",
    )(c_all, ada_w, ada_b.reshape(n_layers, 1, n_out))


def _in_proj_kernel(x_ref, sh_ref, sc_ref, w_ref, cr_ref, sr_ref, cm_ref, sm_ref,
                    rq_ref, rk_ref, rv_ref, rg_ref, mq_ref, mk_ref, mv_ref):
    h = (x_ref[...] * (1.0 + sc_ref[...]) + sh_ref[...]).astype(BF16)
    cr, sr, cm, sm = cr_ref[...], sr_ref[...], cm_ref[...], sm_ref[...]
    even = (lax.broadcasted_iota(jnp.int32, cr.shape, 1) & 1) == 0
    outs = (rq_ref, rk_ref, rv_ref, rg_ref, mq_ref, mk_ref, mv_ref)
    for g, o_ref in enumerate(outs):
        z = _dot(h, w_ref[:, g * GROUP_W:(g + 1) * GROUP_W])
        for hh in range(N_HEADS):
            zs = z[:, hh * HEAD_DIM:(hh + 1) * HEAD_DIM]
            if g in (0, 1):
                nxt = pltpu.roll(zs, HEAD_DIM - 1, axis=1)
                prv = pltpu.roll(zs, 1, axis=1)
                zs = zs * cr + jnp.where(even, nxt, prv) * sr
                if g == 1:
                    zs = zs * (HEAD_DIM ** -0.5)
            elif g in (4, 5):
                zs = zs * cm + pltpu.roll(zs, HEAD_DIM // 2, axis=1) * sm
            o_ref[:, hh * HEAD_DIM:(hh + 1) * HEAD_DIM] = zs.astype(o_ref.dtype)


def in_proj(x2d, shift, scale, w_bf, tabs, rows_per_group, tm):
    n, d = x2d.shape
    p_tiles = tabs[0].shape[0] // tm
    mod_spec = _mod_spec(shift, tm, rows_per_group)
    tab_spec = pl.BlockSpec((tm, HEAD_DIM), lambda i: (i % p_tiles, 0))
    out_spec = pl.BlockSpec((tm, GROUP_W), lambda i: (i, 0))
    dts = (F32, F32, BF16, F32, BF16, F32, F32)
    return pl.pallas_call(
        _in_proj_kernel,
        grid=(n // tm,),
        in_specs=[pl.BlockSpec((tm, d), lambda i: (i, 0)), mod_spec, mod_spec,
                  _resident(w_bf), tab_spec, tab_spec, tab_spec, tab_spec],
        out_specs=[out_spec] * N_IN_COLS,
        out_shape=[jax.ShapeDtypeStruct((n, GROUP_W), dt) for dt in dts],
        compiler_params=_params("parallel"),
        name="in_proj",
    )(x2d, shift, scale, w_bf, *tabs)


def rotary_tables(pos):
    half = HEAD_DIM // 2
    posf = pos.astype(F32)[:, None]
    inv_r = 1.0 / (ROPE_THETA ** jnp.linspace(0.0, 1.0, half, dtype=F32))
    ang_r = posf * inv_r[None, :]
    cr = jnp.repeat(jnp.cos(ang_r), 2, axis=1)
    sr = jnp.stack([-jnp.sin(ang_r), jnp.sin(ang_r)], -1).reshape(-1, HEAD_DIM)
    inv_m = ROPE_THETA ** (-jnp.arange(0, HEAD_DIM, 2, dtype=F32) / HEAD_DIM)
    ang_m = posf * inv_m[None, :]
    cm = jnp.concatenate([jnp.cos(ang_m), jnp.cos(ang_m)], -1)
    sm = jnp.concatenate([-jnp.sin(ang_m), jnp.sin(ang_m)], -1)
    return cr, sr, cm, sm


def retention_tables(chunk, rows):
    log_g = jnp.log1p(-jnp.exp2(-5.0 - jnp.arange(N_HEADS, dtype=F32)))
    idx = jnp.arange(chunk, dtype=F32)
    diff = idx[:, None] - idx[None, :]
    decay_in = jnp.where(diff[None] >= 0,
                         jnp.exp(jnp.maximum(diff, 0.0)[None] * log_g[:, None, None]), 0.0)
    q_dec = jnp.exp((idx + 1.0)[None, :] * log_g[:, None])
    k_dec = jnp.exp((chunk - 1.0 - idx)[None, :] * log_g[:, None])
    c_dec = jnp.exp(chunk * log_g)
    pad = rows - chunk
    decay_in = jnp.pad(decay_in, ((0, 0), (0, pad), (0, rows - chunk)))
    bc = lambda t: jnp.broadcast_to(jnp.pad(t, ((0, 0), (0, pad)))[:, :, None], (N_HEADS, rows, HEAD_DIM))
    c_b = jnp.broadcast_to(c_dec[:, None, None], (N_HEADS, 1, HEAD_DIM))
    return decay_in, bc(q_dec), bc(k_dec), c_b


def _ret_chunk(qc, kc, vc, s, din, qd, kd, cd):
    att = lax.dot_general(qc.astype(BF16), kc.astype(BF16), _NT,
                          preferred_element_type=F32) * din
    o = _dot(att.astype(BF16), vc) + _dot((qc * qd).astype(BF16), s.astype(BF16))
    s_new = cd * s + lax.dot_general((kc * kd).astype(BF16), vc, _TN,
                                     preferred_element_type=F32)
    return o, s_new


def _group_norm_gate(o, g):
    o = o * lax.rsqrt(jnp.mean(o * o, axis=-1, keepdims=True) + GN_EPS)
    return o * _silu(g)


def _ret_prompt_kernel(q_ref, k_ref, v_ref, g_ref, din_ref, qd_ref, kd_ref, cd_ref,
                       o_ref, sout_ref, s_ref, *, n_sub):
    j = pl.program_id(1)

    @pl.when(j == 0)
    def _():
        s_ref[...] = jnp.zeros_like(s_ref)

    for hh in range(N_HEADS):
        cols = slice(hh * HEAD_DIM, (hh + 1) * HEAD_DIM)
        din, qd, kd, cd = din_ref[hh], qd_ref[hh], kd_ref[hh], cd_ref[hh]
        for c in range(n_sub):
            rows = slice(c * RET_CHUNK, (c + 1) * RET_CHUNK)
            o, s_new = _ret_chunk(q_ref[rows, cols], k_ref[rows, cols], v_ref[rows, cols],
                                  s_ref[hh], din, qd, kd, cd)
            s_ref[hh] = s_new
            o_ref[rows, cols] = _group_norm_gate(o, g_ref[rows, cols]).astype(o_ref.dtype)

    @pl.when(j == pl.num_programs(1) - 1)
    def _():
        sout_ref[...] = s_ref[...]


def retention_prompt(rq, rk, rv, rg, batch, seq):
    tc = min(4 * RET_CHUNK, seq)
    nj = seq // tc
    tabs = retention_tables(RET_CHUNK, RET_CHUNK)
    tok = pl.BlockSpec((tc, GROUP_W), lambda b, j: (b * nj + j, 0))
    tab = pl.BlockSpec((N_HEADS, RET_CHUNK, HEAD_DIM), lambda b, j: (0, 0, 0))
    return pl.pallas_call(
        functools.partial(_ret_prompt_kernel, n_sub=tc // RET_CHUNK),
        grid=(batch, nj),
        in_specs=[tok, tok, tok, tok, tab, tab, tab,
                  pl.BlockSpec((N_HEADS, 1, HEAD_DIM), lambda b, j: (0, 0, 0))],
        out_specs=[tok, pl.BlockSpec((None, N_HEADS, HEAD_DIM, HEAD_DIM), lambda b, j: (b, 0, 0, 0))],
        out_shape=[jax.ShapeDtypeStruct((batch * seq, GROUP_W), BF16),
                   jax.ShapeDtypeStruct((batch, N_HEADS, HEAD_DIM, HEAD_DIM), F32)],
        scratch_shapes=[pltpu.VMEM((N_HEADS, HEAD_DIM, HEAD_DIM), F32)],
        compiler_params=_params("parallel", "arbitrary"),
        name="retention_prompt",
    )(rq, rk, rv, rg, *tabs)


def _ret_sample_kernel(q_ref, k_ref, v_ref, g_ref, s0_ref, din_ref, qd_ref, kd_ref, cd_ref,
                       o_ref, sout_ref, *, bb, t):
    for hh in range(N_HEADS):
        cols = slice(hh * HEAD_DIM, (hh + 1) * HEAD_DIM)
        din, qd, kd, cd = din_ref[hh], qd_ref[hh], kd_ref[hh], cd_ref[hh]

        def body(b, carry):
            qc = _rows_to_tile(q_ref[b, :, cols], RET_CHUNK)
            kc = _rows_to_tile(k_ref[b, :, cols], RET_CHUNK)
            vc = _rows_to_tile(v_ref[b, :, cols].astype(F32), RET_CHUNK).astype(BF16)
            o, s_new = _ret_chunk(qc, kc, vc, s0_ref[b, hh], din, qd, kd, cd)
            sout_ref[b, hh] = s_new
            o_ref[b, :, cols] = _group_norm_gate(o[0:t, :], g_ref[b, :, cols]).astype(o_ref.dtype)
            return carry

        lax.fori_loop(0, bb, body, 0)


def retention_sample(rq, rk, rv, rg, state, t):
    batch = state.shape[0]
    bb = 8
    tabs = retention_tables(t, RET_CHUNK)
    tok = pl.BlockSpec((bb, t, GROUP_W), lambda i: (i, 0, 0))
    st = pl.BlockSpec((bb, N_HEADS, HEAD_DIM, HEAD_DIM), lambda i: (i, 0, 0, 0))
    tab = pl.BlockSpec((N_HEADS, RET_CHUNK, HEAD_DIM), lambda i: (0, 0, 0))
    return pl.pallas_call(
        functools.partial(_ret_sample_kernel, bb=bb, t=t),
        grid=(batch // bb,),
        in_specs=[tok, tok, tok, tok, st, tab, tab, tab,
                  pl.BlockSpec((N_HEADS, 1, HEAD_DIM), lambda i: (0, 0, 0))],
        out_specs=[tok, st],
        out_shape=[jax.ShapeDtypeStruct((batch, t, GROUP_W), BF16),
                   jax.ShapeDtypeStruct(state.shape, F32)],
        compiler_params=_params("parallel"),
        name="retention_sample",
    )(rq, rk, rv, rg, state, *tabs)


def _beaten_counts(rows, n_valid):
    counts = []
    for n in range(n_valid):
        cnt = jnp.zeros(rows[n].shape, jnp.int32)
        for m in range(n_valid):
            if m == n:
                continue
            beats = (rows[m] >= rows[n]) if m < n else (rows[m] > rows[n])
            cnt = cnt + beats.astype(jnp.int32)
        counts.append(cnt)
    return counts


def _moba_prompt_kernel(q_ref, k_ref, v_ref, o_ref, s_ref, *, nblk):
    blk = MOBA_BLOCK
    scale = HEAD_DIM ** -0.5
    kb = k_ref[...].astype(BF16)
    vt = v_ref[...].T.astype(BF16)
    row_id = lax.broadcasted_iota(jnp.int32, (SUBLANES, HEAD_DIM), 0)
    kmean = jnp.zeros((SUBLANES, HEAD_DIM), F32)
    for n in range(nblk):
        ks = jnp.sum(k_ref[n * blk:(n + 1) * blk, :], axis=0, keepdims=True) * (1.0 / blk)
        kmean = jnp.where(row_id == n, ks, kmean)
    kmean = kmean.astype(BF16)
    key_i = lax.broadcasted_iota(jnp.int32, (blk, blk), 0)
    qry_i = lax.broadcasted_iota(jnp.int32, (blk, blk), 1)
    causal = key_i <= qry_i

    for qi in range(nblk):
        qb = q_ref[qi * blk:(qi + 1) * blk, :]
        cnt = None
        if qi > MOBA_TOPK:
            gt = lax.dot_general(kmean, qb, _NT, preferred_element_type=F32)
            cnt = _beaten_counts([gt[m:m + 1, :] for m in range(qi)], qi)
        mx = jnp.full((1, blk), NEG, F32)
        for n in range(qi + 1):
            s = lax.dot_general(kb[n * blk:(n + 1) * blk], qb, _NT,
                                preferred_element_type=F32) * scale
            if n == qi:
                s = jnp.where(causal, s, NEG)
            elif cnt is not None:
                s = jnp.where(jnp.broadcast_to(cnt[n], s.shape) < MOBA_TOPK, s, NEG)
            s_ref[n * blk:(n + 1) * blk, :] = s
            mx = jnp.maximum(mx, jnp.max(s, axis=0, keepdims=True))
        l = jnp.zeros((1, blk), F32)
        ot = jnp.zeros((HEAD_DIM, blk), F32)
        for n in range(qi + 1):
            p = jnp.exp(s_ref[n * blk:(n + 1) * blk, :] - mx)
            l = l + jnp.sum(p, axis=0, keepdims=True)
            ot = ot + _dot(vt[:, n * blk:(n + 1) * blk], p.astype(BF16))
        ot = ot * (1.0 / l)
        o_ref[qi * blk:(qi + 1) * blk, :] = ot.T.astype(o_ref.dtype)


def moba_prompt(mq, mk, mv, batch, seq):
    nblk = seq // MOBA_BLOCK
    assert nblk <= SUBLANES
    spec = pl.BlockSpec((seq, HEAD_DIM), lambda b, h: (b, h))
    return pl.pallas_call(
        functools.partial(_moba_prompt_kernel, nblk=nblk),
        grid=(batch, N_HEADS),
        in_specs=[spec, spec, spec],
        out_specs=spec,
        out_shape=jax.ShapeDtypeStruct((batch * seq, GROUP_W), BF16),
        scratch_shapes=[pltpu.VMEM((seq, MOBA_BLOCK), F32)],
        compiler_params=_params("parallel", "parallel"),
        name="moba_prompt",
    )(mq, mk, mv)


def _moba_sample_kernel(pt_ref, q_ref, kn_ref, vn_ref, *rest, n_pages, page, t):
    k_refs = rest[:n_pages]
    v_refs = rest[n_pages:2 * n_pages]
    o_ref = rest[2 * n_pages]
    s_ref = rest[2 * n_pages + 1]
    scale = HEAD_DIM ** -0.5
    ppb = MOBA_BLOCK // page
    nblk = n_pages // ppb
    n_pairs = N_HEADS * t
    lanes = page

    q = q_ref[...].astype(F32)
    r_i = lax.broadcasted_iota(jnp.int32, (n_pairs, GROUP_W), 0)
    l_i = lax.broadcasted_iota(jnp.int32, (n_pairs, GROUP_W), 1)
    qm = jnp.zeros((n_pairs, GROUP_W), F32)
    for ti in range(t):
        for hh in range(N_HEADS):
            hit = (r_i == hh * t + ti) & (l_i >= hh * HEAD_DIM) & (l_i < (hh + 1) * HEAD_DIM)
            qm = jnp.where(hit, q[ti:ti + 1, :], qm)
    qm = jnp.concatenate([qm, jnp.zeros((lanes - n_pairs, GROUP_W), F32)], axis=0).astype(BF16)

    zpad = jnp.zeros((page - SUBLANES, GROUP_W), F32)
    k_own = jnp.concatenate([_rows_to_tile(kn_ref[...], SUBLANES), zpad], axis=0).astype(BF16)
    v_own = jnp.concatenate([_rows_to_tile(vn_ref[...], SUBLANES), zpad], axis=0).astype(BF16)

    b_i = lax.broadcasted_iota(jnp.int32, (SUBLANES, GROUP_W), 0)
    kmean = jnp.zeros((SUBLANES, GROUP_W), F32)
    for p in range(n_pages):
        kp = k_refs[p][...]
        s_ref[p * page:(p + 1) * page, :] = lax.dot_general(
            kp.astype(BF16), qm, _NT, preferred_element_type=F32) * scale
        ks = jnp.sum(kp, axis=0, keepdims=True) * (1.0 / MOBA_BLOCK)
        kmean = kmean + jnp.where(b_i == p // ppb, ks, 0.0)
    gate = lax.dot_general(kmean.astype(BF16), qm, _NT, preferred_element_type=F32)
    cnt = _beaten_counts([gate[m:m + 1, :] for m in range(nblk)], nblk)

    s_own = lax.dot_general(k_own, qm, _NT, preferred_element_type=F32) * scale
    key_i = lax.broadcasted_iota(jnp.int32, (page, lanes), 0)
    tok_i = lax.rem(lax.broadcasted_iota(jnp.int32, (page, lanes), 1), t)
    s_own = jnp.where((key_i <= tok_i) & (key_i < t), s_own, NEG)

    mx = jnp.max(s_own, axis=0, keepdims=True)
    for p in range(n_pages):
        s = s_ref[p * page:(p + 1) * page, :]
        s = jnp.where(jnp.broadcast_to(cnt[p // ppb], s.shape) < MOBA_TOPK, s, NEG)
        s_ref[p * page:(p + 1) * page, :] = s
        mx = jnp.maximum(mx, jnp.max(s, axis=0, keepdims=True))
    e_own = jnp.exp(s_own - mx)
    l = jnp.sum(e_own, axis=0, keepdims=True)
    for p in range(n_pages):
        e = jnp.exp(s_ref[p * page:(p + 1) * page, :] - mx)
        s_ref[p * page:(p + 1) * page, :] = e
        l = l + jnp.sum(e, axis=0, keepdims=True)
    inv = 1.0 / l
    acc = lax.dot_general((e_own * inv).astype(BF16), v_own, _TN,
                          preferred_element_type=F32)
    for p in range(n_pages):
        pn = (s_ref[p * page:(p + 1) * page, :] * inv).astype(BF16)
        acc = acc + lax.dot_general(pn, v_refs[p][...].astype(BF16), _TN,
                                    preferred_element_type=F32)
    for hh in range(N_HEADS):
        cols = slice(hh * HEAD_DIM, (hh + 1) * HEAD_DIM)
        o_ref[:, cols] = acc[hh * t:(hh + 1) * t, cols].astype(o_ref.dtype)


def moba_sample(mq, mk, mv, cache_k, cache_v, page_table, t):
    batch, n_pages = page_table.shape
    page = cache_k.shape[1]
    assert (n_pages * page) % MOBA_BLOCK == 0 and t <= SUBLANES and N_HEADS * t <= page
    assert n_pages * page // MOBA_BLOCK <= SUBLANES
    tok = pl.BlockSpec((None, t, GROUP_W), lambda b, pt: (b, 0, 0))

    def page_spec(p):
        return pl.BlockSpec((None, page, GROUP_W), lambda b, pt: (pt[b, p], 0, 0))

    specs = [tok, tok, tok] + [page_spec(p) for p in range(n_pages)] * 2
    grid_spec = pltpu.PrefetchScalarGridSpec(
        num_scalar_prefetch=1,
        grid=(batch,),
        in_specs=specs,
        out_specs=tok,
        scratch_shapes=[pltpu.VMEM((n_pages * page, page), F32)],
    )
    return pl.pallas_call(
        functools.partial(_moba_sample_kernel, n_pages=n_pages, page=page, t=t),
        grid_spec=grid_spec,
        out_shape=jax.ShapeDtypeStruct((batch, t, GROUP_W), BF16),
        compiler_params=_params("parallel"),
        name="moba_sample",
    )(page_table, mq, mk, mv, *([cache_k] * n_pages), *([cache_v] * n_pages))


def _out_proj_kernel(or_ref, om_ref, w_ref, x_ref, gate_ref, g_ref, b_ref, o_ref):
    y = _dot(or_ref[...], w_ref[:GROUP_W, :]) + _dot(om_ref[...], w_ref[GROUP_W:, :])
    v = ALPHA * x_ref[...] + (1.0 + gate_ref[...]) * y
    o_ref[...] = _layer_norm(v, g_ref[...], b_ref[...])


def out_proj(o_r, o_m, w_bf, x2d, gate, ln_g, ln_b, rows_per_group, tm):
    n, d = x2d.shape
    vec = pl.BlockSpec((1, d), lambda i: (0, 0))
    return pl.pallas_call(
        _out_proj_kernel,
        grid=(n // tm,),
        in_specs=[pl.BlockSpec((tm, GROUP_W), lambda i: (i, 0)),
                  pl.BlockSpec((tm, GROUP_W), lambda i: (i, 0)),
                  _resident(w_bf),
                  pl.BlockSpec((tm, d), lambda i: (i, 0)),
                  _mod_spec(gate, tm, rows_per_group), vec, vec],
        out_specs=pl.BlockSpec((tm, d), lambda i: (i, 0)),
        out_shape=jax.ShapeDtypeStruct((n, d), F32),
        compiler_params=_params("parallel"),
        name="out_proj",
    )(o_r, o_m, w_bf, x2d, gate, ln_g.reshape(1, d), ln_b.reshape(1, d))


def _ffn_kernel(*refs, d_ff, cw, period, carried):
    if carried:
        (x_ref, sh_ref, sc_ref, gate_ref, wup_ref, wdw_ref, bdw_ref, wdn_ref, g_ref, b_ref,
         o_ref, prev_ref, h_s, act_s, carry_s) = refs
    else:
        (x_ref, sh_ref, sc_ref, gate_ref, wup_ref, wdw_ref, bdw_ref, wdn_ref, g_ref, b_ref,
         e1_ref, e2_ref, o_ref, prev_ref, h_s, act_s) = refs
    tm = x_ref.shape[0]
    x = x_ref[...]
    h_s[...] = (x * (1.0 + sc_ref[...]) + sh_ref[...]).astype(BF16)
    row = lax.broadcasted_iota(jnp.int32, (tm, cw), 0)
    t_in = row if carried else row & (period - 1)

    if carried:
        @pl.when(pl.program_id(0) % (period // tm) == 0)
        def _():
            carry_s[...] = jnp.zeros_like(carry_s)

    for c in range(d_ff // cw):
        cu = slice(c * cw, (c + 1) * cw)
        cv = slice(d_ff + c * cw, d_ff + (c + 1) * cw)
        u = _dot(h_s[...], wup_ref[:, cu])
        v = _dot(h_s[...], wup_ref[:, cv])
        if carried:
            e1 = carry_s[1:2, cu]
            e2 = jnp.where(row == 0, carry_s[0:1, cu], carry_s[1:2, cu])
            carry_s[:, cu] = u[tm - 2:tm, :]
            prev_ref[:, cu] = u[tm - 2:tm, :]
        else:
            e1 = e1_ref[:, cu]
            e2 = e2_ref[:, cu]
            prev_ref[:, cu] = u
        s1 = jnp.where(t_in >= 1, pltpu.roll(u, 1, axis=0), e1)
        s2 = jnp.where(t_in >= 2, pltpu.roll(u, 2, axis=0), e2)
        uc = wdw_ref[0:1, cu] * s2 + wdw_ref[1:2, cu] * s1 + wdw_ref[2:3, cu] * u + bdw_ref[:, cu]
        act = 0.5 * uc * (1.0 + lax.erf(uc * (2.0 ** -0.5))) * v
        act_s[:, cu] = act.astype(BF16)
    fy = _dot(act_s[...], wdn_ref[...])
    vsum = ALPHA * x + (1.0 + gate_ref[...]) * fy
    o_ref[...] = _layer_norm(vsum, g_ref[...], b_ref[...])


def conv_ffn(x2d, shift, scale, gate, w_up_bf, w_dw, b_dw, w_dn_bf, ln_g, ln_b,
             rows_per_group, tm, period, prev=None):
    n, d = x2d.shape
    d_ff = w_dn_bf.shape[0]
    cw = 256
    carried = prev is None
    mod = _mod_spec(shift, tm, rows_per_group)
    vec = pl.BlockSpec((1, d), lambda i: (0, 0))
    b_dw2 = b_dw.reshape(1, d_ff)
    in_specs = [pl.BlockSpec((tm, d), lambda i: (i, 0)), mod, mod, mod,
                _resident(w_up_bf), _resident(w_dw), _resident(b_dw2), _resident(w_dn_bf), vec, vec]
    args = [x2d, shift, scale, gate, w_up_bf, w_dw, b_dw2, w_dn_bf,
            ln_g.reshape(1, d), ln_b.reshape(1, d)]
    scratch = [pltpu.VMEM((tm, d), BF16), pltpu.VMEM((tm, d_ff), BF16)]
    if carried:
        assert period % tm == 0
        prev_spec = pl.BlockSpec((None, 2, d_ff), lambda i: (i, 0, 0))
        prev_shape = jax.ShapeDtypeStruct((n // tm, 2, d_ff), F32)
        scratch.append(pltpu.VMEM((2, d_ff), F32))
    else:
        assert tm % period == 0 and period & (period - 1) == 0
        in_specs += [pl.BlockSpec((tm, d_ff), lambda i: (i, 0))] * 2
        args += list(prev)
        prev_spec = pl.BlockSpec((tm, d_ff), lambda i: (i, 0))
        prev_shape = jax.ShapeDtypeStruct((n, d_ff), F32)
    return pl.pallas_call(
        functools.partial(_ffn_kernel, d_ff=d_ff, cw=cw, period=period, carried=carried),
        grid=(n // tm,),
        in_specs=in_specs,
        out_specs=[pl.BlockSpec((tm, d), lambda i: (i, 0)), prev_spec],
        out_shape=[jax.ShapeDtypeStruct((n, d), F32), prev_shape],
        scratch_shapes=scratch,
        compiler_params=_params("arbitrary"),
        name="conv_ffn",
    )(*args)


def _glu_kernel(x_ref, sh_ref, sc_ref, w_ref, b_ref, o_ref, *, cw):
    h = (x_ref[...] * (1.0 + sc_ref[...]) + sh_ref[...]).astype(BF16)
    dc = o_ref.shape[1]
    for c in range(dc // cw):
        ca = slice(c * cw, (c + 1) * cw)
        cg = slice(dc + c * cw, dc + (c + 1) * cw)
        a = _dot(h, w_ref[:, ca]) + b_ref[:, ca]
        g = _dot(h, w_ref[:, cg]) + b_ref[:, cg]
        o_ref[:, ca] = a * jax.nn.sigmoid(g)


def conformer_glu(x2d, shift, scale, w1_bf, b1, rows_per_group, tm):
    n, d = x2d.shape
    dc = w1_bf.shape[1] // 2
    mod = _mod_spec(shift, tm, rows_per_group)
    b1r = b1.reshape(1, 2 * dc)
    return pl.pallas_call(
        functools.partial(_glu_kernel, cw=256),
        grid=(n // tm,),
        in_specs=[pl.BlockSpec((tm, d), lambda i: (i, 0)), mod, mod,
                  _resident(w1_bf), _resident(b1r)],
        out_specs=pl.BlockSpec((tm, dc), lambda i: (i, 0)),
        out_shape=jax.ShapeDtypeStruct((n, dc), F32),
        compiler_params=_params("parallel"),
        name="conformer_glu",
    )(x2d, shift, scale, w1_bf, b1r)


PAD_ROWS = 32


def _dwconv_prompt_kernel(x_ref, prev_ref, w_ref, b_ref, o_ref, win_ref, *, width, rs, cw):
    tm, dc = x_ref.shape
    j = pl.program_id(1)

    @pl.when(j == 0)
    def _():
        win_ref[0:PAD_ROWS, :] = jnp.zeros((PAD_ROWS, dc), F32)

    @pl.when(j > 0)
    def _():
        win_ref[0:PAD_ROWS, :] = prev_ref[...]

    win_ref[PAD_ROWS:, :] = x_ref[...]
    off = PAD_ROWS - (width - 1)
    for c in range(dc // cw):
        cols = slice(c * cw, (c + 1) * cw)
        for r0 in range(0, tm, rs):
            acc = jnp.zeros((rs, cw), F32) + b_ref[:, cols]
            for k in range(width):
                acc = acc + w_ref[k:k + 1, cols] * win_ref[off + r0 + k:off + r0 + k + rs, cols]
            o_ref[r0:r0 + rs, cols] = acc


def dwconv_prompt(glu, w_dw, b_dw, batch, seq, tm):
    n, dc = glu.shape
    width = w_dw.shape[0]
    assert width - 1 <= PAD_ROWS and tm % PAD_ROWS == 0
    nj = seq // tm
    per = tm // PAD_ROWS
    b2 = b_dw.reshape(1, dc)
    return pl.pallas_call(
        functools.partial(_dwconv_prompt_kernel, width=width, rs=64, cw=128),
        grid=(batch, nj),
        in_specs=[pl.BlockSpec((tm, dc), lambda b, j: (b * nj + j, 0)),
                  pl.BlockSpec((PAD_ROWS, dc), lambda b, j: (jnp.maximum((b * nj + j) * per - 1, 0), 0)),
                  pl.BlockSpec(w_dw.shape, lambda b, j: (0, 0)),
                  pl.BlockSpec(b2.shape, lambda b, j: (0, 0))],
        out_specs=pl.BlockSpec((tm, dc), lambda b, j: (b * nj + j, 0)),
        out_shape=jax.ShapeDtypeStruct((n, dc), F32),
        scratch_shapes=[pltpu.VMEM((PAD_ROWS + tm, dc), F32)],
        compiler_params=_params("parallel", "parallel"),
        name="dwconv_prompt",
    )(glu, glu, w_dw, b2)


def _dwconv_sample_kernel(x_ref, st_ref, w_ref, b_ref, o_ref, nst_ref, win_ref, *, width, bb, t):
    hist = width - 1

    def body(b, carry):
        win_ref[0:hist, :] = st_ref[b]
        win_ref[hist:hist + t, :] = x_ref[b]
        acc = jnp.zeros((t, x_ref.shape[2]), F32) + b_ref[...]
        for k in range(width):
            acc = acc + w_ref[k:k + 1, :] * win_ref[k:k + t, :]
        o_ref[b] = acc
        nst_ref[b] = win_ref[t:t + hist, :]
        return carry

    lax.fori_loop(0, bb, body, 0)


def dwconv_sample(glu, state, w_dw, b_dw):
    batch, t, dc = glu.shape
    width = w_dw.shape[0]
    bb = 8
    b2 = b_dw.reshape(1, dc)
    win_rows = -(-(width - 1 + t) // SUBLANES) * SUBLANES
    return pl.pallas_call(
        functools.partial(_dwconv_sample_kernel, width=width, bb=bb, t=t),
        grid=(batch // bb,),
        in_specs=[pl.BlockSpec((bb, t, dc), lambda i: (i, 0, 0)),
                  pl.BlockSpec((bb, width - 1, dc), lambda i: (i, 0, 0)),
                  pl.BlockSpec(w_dw.shape, lambda i: (0, 0)),
                  pl.BlockSpec(b2.shape, lambda i: (0, 0))],
        out_specs=[pl.BlockSpec((bb, t, dc), lambda i: (i, 0, 0)),
                   pl.BlockSpec((bb, width - 1, dc), lambda i: (i, 0, 0))],
        out_shape=[jax.ShapeDtypeStruct((batch, t, dc), F32),
                   jax.ShapeDtypeStruct(state.shape, F32)],
        scratch_shapes=[pltpu.VMEM((win_rows, dc), F32)],
        compiler_params=_params("parallel"),
        name="dwconv_sample",
    )(glu, state, w_dw, b2)


def _conf_tail_kernel(y_ref, gl_ref, bl_ref, w_ref, b2_ref, x_ref, gate_ref, g_ref, b_ref, o_ref):
    yn = _silu(_layer_norm(y_ref[...], gl_ref[...], bl_ref[...]))
    y = _dot(yn.astype(BF16), w_ref[...]) + b2_ref[...]
    v = ALPHA * x_ref[...] + (1.0 + gate_ref[...]) * y
    o_ref[...] = _layer_norm(v, g_ref[...], b_ref[...])


def conformer_tail(y, cf_g, cf_b, w2_bf, b2, x2d, gate, ln_g, ln_b, rows_per_group, tm):
    n, d = x2d.shape
    dc = y.shape[1]
    vecc = pl.BlockSpec((1, dc), lambda i: (0, 0))
    vec = pl.BlockSpec((1, d), lambda i: (0, 0))
    return pl.pallas_call(
        _conf_tail_kernel,
        grid=(n // tm,),
        in_specs=[pl.BlockSpec((tm, dc), lambda i: (i, 0)), vecc, vecc,
                  _resident(w2_bf), vec,
                  pl.BlockSpec((tm, d), lambda i: (i, 0)),
                  _mod_spec(gate, tm, rows_per_group), vec, vec],
        out_specs=pl.BlockSpec((tm, d), lambda i: (i, 0)),
        out_shape=jax.ShapeDtypeStruct((n, d), F32),
        compiler_params=_params("parallel"),
        name="conformer_tail",
    )(y, cf_g.reshape(1, dc), cf_b.reshape(1, dc), w2_bf, b2.reshape(1, d), x2d, gate,
      ln_g.reshape(1, d), ln_b.reshape(1, d))


def kernel(x_prompt, x_sample, cache_k, cache_v, state_ret, state_conv, state_ffn, page_table, c_prompt, c_sample, ab_w_in, ab_w_out, cf_w_pw1, cf_b_pw1, cf_w_dw, cf_b_dw, cf_ln_g, cf_ln_b, cf_w_pw2, cf_b_pw2, ffn_w_up, ffn_w_dw, ffn_b_dw, ffn_w_down, ada_w, ada_b, ln_g, ln_b):
    bp, seq, d = x_prompt.shape
    bs, ts, _ = x_sample.shape
    n_layers = ada_w.shape[0]
    d_ff = ffn_w_down.shape[1]
    past_len = page_table.shape[1] * cache_k.shape[2]
    np_, ns = bp * seq, bs * ts
    tm_p = min(256, seq)
    tm_s = min(128, ns)

    ada = adaln_all(jnp.concatenate([c_prompt, c_sample], 0), ada_w, ada_b)
    ada = ada.reshape(n_layers, bp + bs, 6, d)

    def mods(l, k):
        mp = ada[l, :bp, k][:, None, :]
        ms = jnp.repeat(ada[l, bp:, k], ts, axis=0)[None]
        return mp, ms

    xp = x_prompt.reshape(np_, d)
    xs = x_sample.reshape(ns, d)
    tabs_p = rotary_tables(jnp.arange(seq, dtype=jnp.int32))
    tabs_s = tuple(jnp.tile(tb, (tm_s // ts, 1)) for tb in
                   rotary_tables(past_len + jnp.arange(ts, dtype=jnp.int32)))

    outs = {k: [] for k in ("kp", "vp", "ks", "vs", "rp", "rs", "cp", "cs", "fp", "fs")}
    for l in range(n_layers):
        i = l // 2
        sh_p, sh_s = mods(l, 0)
        sc_p, sc_s = mods(l, 1)
        gt_p, gt_s = mods(l, 2)
        if l % 2 == 0:
            w_in = ab_w_in[i].astype(BF16)
            w_out = ab_w_out[i].astype(BF16)
            rq, rk, rv, rg, mq, mk, mv = in_proj(xp, sh_p, sc_p, w_in, tabs_p, seq, tm_p)
            o_r, s_p = retention_prompt(rq, rk, rv, rg, bp, seq)
            o_m = moba_prompt(mq, mk, mv, bp, seq)
            xp = out_proj(o_r, o_m, w_out, xp, gt_p, ln_g[l, 0], ln_b[l, 0], seq, tm_p)
            outs["kp"].append(mk.reshape(bp, seq, N_HEADS, HEAD_DIM))
            outs["vp"].append(mv.reshape(bp, seq, N_HEADS, HEAD_DIM))
            outs["rp"].append(s_p)

            rq, rk, rv, rg, mq, mk, mv = in_proj(xs, sh_s, sc_s, w_in, tabs_s, ns, tm_s)
            r3 = lambda a: a.reshape(bs, ts, GROUP_W)
            o_r, s_s = retention_sample(r3(rq), r3(rk), r3(rv), r3(rg), state_ret[i], ts)
            ck = cache_k[i].reshape(cache_k.shape[1], cache_k.shape[2], GROUP_W)
            cv = cache_v[i].reshape(cache_v.shape[1], cache_v.shape[2], GROUP_W)
            o_m = moba_sample(r3(mq), r3(mk), r3(mv), ck, cv, page_table, ts)
            xs = out_proj(o_r.reshape(ns, GROUP_W), o_m.reshape(ns, GROUP_W), w_out, xs, gt_s,
                          ln_g[l, 0], ln_b[l, 0], ns, tm_s)
            outs["ks"].append(mk.reshape(bs, ts, N_HEADS, HEAD_DIM))
            outs["vs"].append(mv.reshape(bs, ts, N_HEADS, HEAD_DIM))
            outs["rs"].append(s_s)
        else:
            w1 = cf_w_pw1[i].astype(BF16)
            w2 = cf_w_pw2[i].astype(BF16)
            hist = cf_w_dw.shape[1] - 1
            glu = conformer_glu(xp, sh_p, sc_p, w1, cf_b_pw1[i], seq, tm_p)
            y = dwconv_prompt(glu, cf_w_dw[i], cf_b_dw[i], bp, seq, tm_p)
            xp = conformer_tail(y, cf_ln_g[i], cf_ln_b[i], w2, cf_b_pw2[i], xp, gt_p,
                                ln_g[l, 0], ln_b[l, 0], seq, tm_p)
            outs["cp"].append(glu.reshape(bp, seq, -1)[:, seq - hist:])

            glu = conformer_glu(xs, sh_s, sc_s, w1, cf_b_pw1[i], ns, tm_s)
            y, nst = dwconv_sample(glu.reshape(bs, ts, -1), state_conv[i], cf_w_dw[i], cf_b_dw[i])
            xs = conformer_tail(y.reshape(ns, -1), cf_ln_g[i], cf_ln_b[i], w2, cf_b_pw2[i], xs, gt_s,
                                ln_g[l, 0], ln_b[l, 0], ns, tm_s)
            outs["cs"].append(nst)

        sh_p, sh_s = mods(l, 3)
        sc_p, sc_s = mods(l, 4)
        gt_p, gt_s = mods(l, 5)
        w_up = ffn_w_up[l].astype(BF16)
        w_dn = ffn_w_down[l].astype(BF16)
        xp, prev_p = conv_ffn(xp, sh_p, sc_p, gt_p, w_up, ffn_w_dw[l], ffn_b_dw[l], w_dn,
                              ln_g[l, 1], ln_b[l, 1], seq, tm_p, seq)
        outs["fp"].append(prev_p.reshape(bp, seq // tm_p, 2, d_ff)[:, -1])
        st = state_ffn[l]
        zero = jnp.zeros((bs, ts - 2, d_ff), F32)
        e1 = jnp.concatenate([st[:, 1:2], zero, zero[:, :1]], 1).reshape(ns, d_ff)
        e2 = jnp.concatenate([st, zero], 1).reshape(ns, d_ff)
        xs, u_s = conv_ffn(xs, sh_s, sc_s, gt_s, w_up, ffn_w_dw[l], ffn_b_dw[l], w_dn,
                           ln_g[l, 1], ln_b[l, 1], ns, tm_s, ts, prev=(e1, e2))
        outs["fs"].append(u_s.reshape(bs, ts, d_ff)[:, ts - 2:])

    st = lambda k: jnp.stack(outs[k])
    return (xp.reshape(bp, seq, d), xs.reshape(bs, ts, d), st("kp"), st("vp"), st("ks"), st("vs"),
            st("rp"), st("rs"), st("cp"), st("cs"), st("fp"), st("fs"))
```

```python
import functools

import jax
import jax.numpy as jnp
from jax import lax
from jax.experimental import pallas as pl
from jax.experimental.pallas import tpu as pltpu

HEAD_DIM = 128
N_HEADS = 4
GROUP_W = N_HEADS * HEAD_DIM
N_IN_COLS = 7
RET_CHUNK = 128
MOBA_BLOCK = 256
MOBA_TOPK = 3
ROPE_THETA = 10000.0
DEPTH = 2
ALPHA = (2 * DEPTH) ** 0.25
LN_EPS = 1e-5
GN_EPS = 1e-6
NEG = -1e30

F32 = jnp.float32
BF16 = jnp.bfloat16

_NT = (((1,), (1,)), ((), ()))
_TN = (((0,), (0,)), ((), ()))

VMEM_LIMIT = 56 * 1024 * 1024
SUBLANES = 8


def _params(*sem):
    return pltpu.CompilerParams(dimension_semantics=sem, vmem_limit_bytes=VMEM_LIMIT)


def _dot(a, b):
    return jnp.dot(a, b, preferred_element_type=F32)


def _layer_norm(v, g, b):
    mu = jnp.mean(v, axis=-1, keepdims=True)
    d = v - mu
    var = jnp.mean(d * d, axis=-1, keepdims=True)
    return d * lax.rsqrt(var + LN_EPS) * g + b


def _silu(x):
    return x * jax.nn.sigmoid(x)


def _mod_spec(mod, tm, rows_per_group):
    _, r, d = mod.shape
    if r == 1:
        return pl.BlockSpec((None, 1, d), lambda i: (i * tm // rows_per_group, 0, 0))
    return pl.BlockSpec((None, tm, d), lambda i: (0, i, 0))


def _resident(a):
    return pl.BlockSpec(a.shape, lambda *_: (0,) * a.ndim, pipeline_mode=pl.Buffered(1))


def _rows_to_tile(x, n_rows):
    t, w = x.shape
    r_i = lax.broadcasted_iota(jnp.int32, (n_rows, w), 0)
    out = jnp.zeros((n_rows, w), x.dtype)
    for ti in range(t):
        out = jnp.where(r_i == ti, x[ti:ti + 1, :], out)
    return out


def _adaln_kernel(c_ref, w_ref, b_ref, o_ref):
    a = _silu(c_ref[...]).astype(BF16)
    o_ref[...] = _dot(a, w_ref[...].astype(BF16)) + b_ref[...]


def adaln_all(c_all, ada_w, ada_b):
    n_layers, d, n_out = ada_w.shape
    m = c_all.shape[0]
    tn = 1536 if n_out % 1536 == 0 else n_out
    return pl.pallas_call(
        _adaln_kernel,
        grid=(n_layers, n_out // tn),
        in_specs=[
            pl.BlockSpec((m, d), lambda l, j: (0, 0)),
            pl.BlockSpec((None, d, tn), lambda l, j: (l, 0, j)),
            pl.BlockSpec((None, 1, tn), lambda l, j: (l, 0, j)),
        ],
        out_specs=pl.BlockSpec((None, m, tn), lambda l, j: (l, 0, j)),
        out_shape=jax.ShapeDtypeStruct((n_layers, m, n_out), F32),
        compiler_params=_params("parallel", "parallel"),
        name="adaln",
    )(c_all, ada_w, ada_b.reshape(n_layers, 1, n_out))


def _in_proj_kernel(x_ref, sh_ref, sc_ref, w_ref, cr_ref, sr_ref, cm_ref, sm_ref,
                    rq_ref, rk_ref, rv_ref, rg_ref, mq_ref, mk_ref, mv_ref):
    h = (x_ref[...] * (1.0 + sc_ref[...]) + sh_ref[...]).astype(BF16)
    cr, sr, cm, sm = cr_ref[...], sr_ref[...], cm_ref[...], sm_ref[...]
    even = (lax.broadcasted_iota(jnp.int32, cr.shape, 1) & 1) == 0
    outs = (rq_ref, rk_ref, rv_ref, rg_ref, mq_ref, mk_ref, mv_ref)
    for g, o_ref in enumerate(outs):
        z = _dot(h, w_ref[:, g * GROUP_W:(g + 1) * GROUP_W])
        for hh in range(N_HEADS):
            zs = z[:, hh * HEAD_DIM:(hh + 1) * HEAD_DIM]
            if g in (0, 1):
                nxt = pltpu.roll(zs, HEAD_DIM - 1, axis=1)
                prv = pltpu.roll(zs, 1, axis=1)
                zs = zs * cr + jnp.where(even, nxt, prv) * sr
                if g == 1:
                    zs = zs * (HEAD_DIM ** -0.5)
            elif g in (4, 5):
                zs = zs * cm + pltpu.roll(zs, HEAD_DIM // 2, axis=1) * sm
            o_ref[:, hh * HEAD_DIM:(hh + 1) * HEAD_DIM] = zs.astype(o_ref.dtype)


def in_proj(x2d, shift, scale, w_bf, tabs, rows_per_group, tm):
    n, d = x2d.shape
    p_tiles = tabs[0].shape[0] // tm
    mod_spec = _mod_spec(shift, tm, rows_per_group)
    tab_spec = pl.BlockSpec((tm, HEAD_DIM), lambda i: (i % p_tiles, 0))
    out_spec = pl.BlockSpec((tm, GROUP_W), lambda i: (i, 0))
    dts = (F32, F32, BF16, F32, BF16, F32, F32)
    return pl.pallas_call(
        _in_proj_kernel,
        grid=(n // tm,),
        in_specs=[pl.BlockSpec((tm, d), lambda i: (i, 0)), mod_spec, mod_spec,
                  _resident(w_bf), tab_spec, tab_spec, tab_spec, tab_spec],
        out_specs=[out_spec] * N_IN_COLS,
        out_shape=[jax.ShapeDtypeStruct((n, GROUP_W), dt) for dt in dts],
        compiler_params=_params("parallel"),
        name="in_proj",
    )(x2d, shift, scale, w_bf, *tabs)


def rotary_tables(pos):
    half = HEAD_DIM // 2
    posf = pos.astype(F32)[:, None]
    inv_r = 1.0 / (ROPE_THETA ** jnp.linspace(0.0, 1.0, half, dtype=F32))
    ang_r = posf * inv_r[None, :]
    cr = jnp.repeat(jnp.cos(ang_r), 2, axis=1)
    sr = jnp.stack([-jnp.sin(ang_r), jnp.sin(ang_r)], -1).reshape(-1, HEAD_DIM)
    inv_m = ROPE_THETA ** (-jnp.arange(0, HEAD_DIM, 2, dtype=F32) / HEAD_DIM)
    ang_m = posf * inv_m[None, :]
    cm = jnp.concatenate([jnp.cos(ang_m), jnp.cos(ang_m)], -1)
    sm = jnp.concatenate([-jnp.sin(ang_m), jnp.sin(ang_m)], -1)
    return cr, sr, cm, sm


def retention_tables(chunk, rows):
    log_g = jnp.log1p(-jnp.exp2(-5.0 - jnp.arange(N_HEADS, dtype=F32)))
    idx = jnp.arange(chunk, dtype=F32)
    diff = idx[:, None] - idx[None, :]
    decay_in = jnp.where(diff[None] >= 0,
                         jnp.exp(jnp.maximum(diff, 0.0)[None] * log_g[:, None, None]), 0.0)
    q_dec = jnp.exp((idx + 1.0)[None, :] * log_g[:, None])
    k_dec = jnp.exp((chunk - 1.0 - idx)[None, :] * log_g[:, None])
    c_dec = jnp.exp(chunk * log_g)
    pad = rows - chunk
    decay_in = jnp.pad(decay_in, ((0, 0), (0, pad), (0, rows - chunk)))
    bc = lambda t: jnp.broadcast_to(jnp.pad(t, ((0, 0), (0, pad)))[:, :, None], (N_HEADS, rows, HEAD_DIM))
    c_b = jnp.broadcast_to(c_dec[:, None, None], (N_HEADS, 1, HEAD_DIM))
    return decay_in, bc(q_dec), bc(k_dec), c_b


def _ret_chunk(qc, kc, vc, s, din, qd, kd, cd):
    att = lax.dot_general(qc.astype(BF16), kc.astype(BF16), _NT,
                          preferred_element_type=F32) * din
    o = _dot(att.astype(BF16), vc) + _dot((qc * qd).astype(BF16), s.astype(BF16))
    s_new = cd * s + lax.dot_general((kc * kd).astype(BF16), vc, _TN,
                                     preferred_element_type=F32)
    return o, s_new


def _group_norm_gate(o, g):
    o = o * lax.rsqrt(jnp.mean(o * o, axis=-1, keepdims=True) + GN_EPS)
    return o * _silu(g)


def _ret_prompt_kernel(q_ref, k_ref, v_ref, g_ref, din_ref, qd_ref, kd_ref, cd_ref,
                       o_ref, sout_ref, s_ref, *, n_sub):
    j = pl.program_id(1)

    @pl.when(j == 0)
    def _():
        s_ref[...] = jnp.zeros_like(s_ref)

    for hh in range(N_HEADS):
        cols = slice(hh * HEAD_DIM, (hh + 1) * HEAD_DIM)
        din, qd, kd, cd = din_ref[hh], qd_ref[hh], kd_ref[hh], cd_ref[hh]
        for c in range(n_sub):
            rows = slice(c * RET_CHUNK, (c + 1) * RET_CHUNK)
            o, s_new = _ret_chunk(q_ref[rows, cols], k_ref[rows, cols], v_ref[rows, cols],
                                  s_ref[hh], din, qd, kd, cd)
            s_ref[hh] = s_new
            o_ref[rows, cols] = _group_norm_gate(o, g_ref[rows, cols]).astype(o_ref.dtype)

    @pl.when(j == pl.num_programs(1) - 1)
    def _():
        sout_ref[...] = s_ref[...]


def retention_prompt(rq, rk, rv, rg, batch, seq):
    tc = min(4 * RET_CHUNK, seq)
    nj = seq // tc
    tabs = retention_tables(RET_CHUNK, RET_CHUNK)
    tok = pl.BlockSpec((tc, GROUP_W), lambda b, j: (b * nj + j, 0))
    tab = pl.BlockSpec((N_HEADS, RET_CHUNK, HEAD_DIM), lambda b, j: (0, 0, 0))
    return pl.pallas_call(
        functools.partial(_ret_prompt_kernel, n_sub=tc // RET_CHUNK),
        grid=(batch, nj),
        in_specs=[tok, tok, tok, tok, tab, tab, tab,
                  pl.BlockSpec((N_HEADS, 1, HEAD_DIM), lambda b, j: (0, 0, 0))],
        out_specs=[tok, pl.BlockSpec((None, N_HEADS, HEAD_DIM, HEAD_DIM), lambda b, j: (b, 0, 0, 0))],
        out_shape=[jax.ShapeDtypeStruct((batch * seq, GROUP_W), BF16),
                   jax.ShapeDtypeStruct((batch, N_HEADS, HEAD_DIM, HEAD_DIM), F32)],
        scratch_shapes=[pltpu.VMEM((N_HEADS, HEAD_DIM, HEAD_DIM), F32)],
        compiler_params=_params("parallel", "arbitrary"),
        name="retention_prompt",
    )(rq, rk, rv, rg, *tabs)


def _ret_sample_kernel(q_ref, k_ref, v_ref, g_ref, s0_ref, din_ref, qd_ref, kd_ref, cd_ref,
                       o_ref, sout_ref, *, bb, t):
    for hh in range(N_HEADS):
        cols = slice(hh * HEAD_DIM, (hh + 1) * HEAD_DIM)
        din, qd, kd, cd = din_ref[hh], qd_ref[hh], kd_ref[hh], cd_ref[hh]

        def body(b, carry):
            qc = _rows_to_tile(q_ref[b, :, cols], RET_CHUNK)
            kc = _rows_to_tile(k_ref[b, :, cols], RET_CHUNK)
            vc = _rows_to_tile(v_ref[b, :, cols].astype(F32), RET_CHUNK).astype(BF16)
            o, s_new = _ret_chunk(qc, kc, vc, s0_ref[b, hh], din, qd, kd, cd)
            sout_ref[b, hh] = s_new
            o_ref[b, :, cols] = _group_norm_gate(o[0:t, :], g_ref[b, :, cols]).astype(o_ref.dtype)
            return carry

        lax.fori_loop(0, bb, body, 0)


def retention_sample(rq, rk, rv, rg, states, layer, t):
    batch = states.shape[1]
    bb = 8
    tabs = retention_tables(t, RET_CHUNK)
    tok = pl.BlockSpec((bb, t, GROUP_W), lambda i: (i, 0, 0))
    st_in = pl.BlockSpec((None, bb, N_HEADS, HEAD_DIM, HEAD_DIM), lambda i: (layer, i, 0, 0, 0))
    st = pl.BlockSpec((bb, N_HEADS, HEAD_DIM, HEAD_DIM), lambda i: (i, 0, 0, 0))
    tab = pl.BlockSpec((N_HEADS, RET_CHUNK, HEAD_DIM), lambda i: (0, 0, 0))
    return pl.pallas_call(
        functools.partial(_ret_sample_kernel, bb=bb, t=t),
        grid=(batch // bb,),
        in_specs=[tok, tok, tok, tok, st_in, tab, tab, tab,
                  pl.BlockSpec((N_HEADS, 1, HEAD_DIM), lambda i: (0, 0, 0))],
        out_specs=[tok, st],
        out_shape=[jax.ShapeDtypeStruct((batch, t, GROUP_W), BF16),
                   jax.ShapeDtypeStruct(states.shape[1:], F32)],
        compiler_params=_params("parallel"),
        name="retention_sample",
    )(rq, rk, rv, rg, states, *tabs)


def _beaten_counts(rows, n_valid):
    counts = []
    for n in range(n_valid):
        cnt = jnp.zeros(rows[n].shape, jnp.int32)
        for m in range(n_valid):
            if m == n:
                continue
            beats = (rows[m] >= rows[n]) if m < n else (rows[m] > rows[n])
            cnt = cnt + beats.astype(jnp.int32)
        counts.append(cnt)
    return counts


def _moba_prompt_kernel(q_ref, k_ref, v_ref, o_ref, s_ref, *, nblk):
    blk = MOBA_BLOCK
    scale = HEAD_DIM ** -0.5
    kb = k_ref[...].astype(BF16)
    vt = v_ref[...].T.astype(BF16)
    row_id = lax.broadcasted_iota(jnp.int32, (SUBLANES, HEAD_DIM), 0)
    kmean = jnp.zeros((SUBLANES, HEAD_DIM), F32)
    for n in range(nblk):
        ks = jnp.sum(k_ref[n * blk:(n + 1) * blk, :], axis=0, keepdims=True) * (1.0 / blk)
        kmean = jnp.where(row_id == n, ks, kmean)
    kmean = kmean.astype(BF16)
    key_i = lax.broadcasted_iota(jnp.int32, (blk, blk), 0)
    qry_i = lax.broadcasted_iota(jnp.int32, (blk, blk), 1)
    causal = key_i <= qry_i

    for qi in range(nblk):
        qb = q_ref[qi * blk:(qi + 1) * blk, :]
        cnt = None
        if qi > MOBA_TOPK:
            gt = lax.dot_general(kmean, qb, _NT, preferred_element_type=F32)
            cnt = _beaten_counts([gt[m:m + 1, :] for m in range(qi)], qi)
        mx = jnp.full((1, blk), NEG, F32)
        for n in range(qi + 1):
            s = lax.dot_general(kb[n * blk:(n + 1) * blk], qb, _NT,
                                preferred_element_type=F32) * scale
            if n == qi:
                s = jnp.where(causal, s, NEG)
            elif cnt is not None:
                s = jnp.where(jnp.broadcast_to(cnt[n], s.shape) < MOBA_TOPK, s, NEG)
            s_ref[n * blk:(n + 1) * blk, :] = s
            mx = jnp.maximum(mx, jnp.max(s, axis=0, keepdims=True))
        l = jnp.zeros((1, blk), F32)
        ot = jnp.zeros((HEAD_DIM, blk), F32)
        for n in range(qi + 1):
            p = jnp.exp(s_ref[n * blk:(n + 1) * blk, :] - mx)
            l = l + jnp.sum(p, axis=0, keepdims=True)
            ot = ot + _dot(vt[:, n * blk:(n + 1) * blk], p.astype(BF16))
        ot = ot * (1.0 / l)
        o_ref[qi * blk:(qi + 1) * blk, :] = ot.T.astype(o_ref.dtype)


def moba_prompt(mq, mk, mv, batch, seq):
    nblk = seq // MOBA_BLOCK
    assert nblk <= SUBLANES
    spec = pl.BlockSpec((seq, HEAD_DIM), lambda b, h: (b, h))
    return pl.pallas_call(
        functools.partial(_moba_prompt_kernel, nblk=nblk),
        grid=(batch, N_HEADS),
        in_specs=[spec, spec, spec],
        out_specs=spec,
        out_shape=jax.ShapeDtypeStruct((batch * seq, GROUP_W), BF16),
        scratch_shapes=[pltpu.VMEM((seq, MOBA_BLOCK), F32)],
        compiler_params=_params("parallel", "parallel"),
        name="moba_prompt",
    )(mq, mk, mv)


def _moba_sample_kernel(pt_ref, q_ref, kn_ref, vn_ref, *rest, n_pages, page, t):
    k_refs = rest[:n_pages]
    v_refs = rest[n_pages:2 * n_pages]
    o_ref = rest[2 * n_pages]
    scale = HEAD_DIM ** -0.5
    ppb = MOBA_BLOCK // page
    nblk = n_pages // ppb
    n_pairs = N_HEADS * t

    q = q_ref[...].astype(F32)
    r_i = lax.broadcasted_iota(jnp.int32, (n_pairs, GROUP_W), 0)
    l_i = lax.broadcasted_iota(jnp.int32, (n_pairs, GROUP_W), 1)
    qm = jnp.zeros((n_pairs, GROUP_W), F32)
    for ti in range(t):
        for hh in range(N_HEADS):
            hit = (r_i == hh * t + ti) & (l_i >= hh * HEAD_DIM) & (l_i < (hh + 1) * HEAD_DIM)
            qm = jnp.where(hit, q[ti:ti + 1, :], qm)
    qm = qm.astype(BF16)

    zpad = jnp.zeros((page - SUBLANES, GROUP_W), F32)
    k_own = jnp.concatenate([_rows_to_tile(kn_ref[...], SUBLANES), zpad], axis=0).astype(BF16)
    v_own = jnp.concatenate([_rows_to_tile(vn_ref[...], SUBLANES), zpad], axis=0).astype(BF16)

    b_i = lax.broadcasted_iota(jnp.int32, (page, GROUP_W), 0)
    kmean = jnp.zeros((page, GROUP_W), F32)
    scores = []
    for p in range(n_pages):
        kp = k_refs[p][...]
        scores.append(lax.dot_general(qm, kp.astype(BF16), _NT,
                                      preferred_element_type=F32) * scale)
        ks = jnp.sum(kp, axis=0, keepdims=True) * (1.0 / MOBA_BLOCK)
        kmean = kmean + jnp.where(b_i == p // ppb, ks, 0.0)
    gate = lax.dot_general(qm, kmean.astype(BF16), _NT, preferred_element_type=F32)
    cnt = _beaten_counts([gate[:, m:m + 1] for m in range(nblk)], nblk)
    keep = [jnp.broadcast_to(c, (n_pairs, page)) < MOBA_TOPK for c in cnt]

    s_own = lax.dot_general(qm, k_own, _NT, preferred_element_type=F32) * scale
    key_i = lax.broadcasted_iota(jnp.int32, (n_pairs, page), 1)
    tok_i = lax.rem(lax.broadcasted_iota(jnp.int32, (n_pairs, page), 0), t)
    s_own = jnp.where((key_i <= tok_i) & (key_i < t), s_own, NEG)

    mx = s_own
    for p in range(n_pages):
        scores[p] = jnp.where(keep[p // ppb], scores[p], NEG)
        mx = jnp.maximum(mx, scores[p])
    mx = jnp.max(mx, axis=1, keepdims=True)
    e_own = jnp.exp(s_own - mx)
    l = e_own
    acc = _dot(e_own.astype(BF16), v_own)
    for p in range(n_pages):
        e = jnp.exp(scores[p] - mx)
        l = l + e
        acc = acc + _dot(e.astype(BF16), v_refs[p][...].astype(BF16))
    acc = acc * (1.0 / jnp.sum(l, axis=1, keepdims=True))
    for hh in range(N_HEADS):
        cols = slice(hh * HEAD_DIM, (hh + 1) * HEAD_DIM)
        o_ref[:, cols] = acc[hh * t:(hh + 1) * t, cols].astype(o_ref.dtype)


def moba_sample(mq, mk, mv, cache_k, cache_v, page_table, page_base, t):
    batch, n_pages = page_table.shape
    page = cache_k.shape[1]
    assert (n_pages * page) % MOBA_BLOCK == 0 and t <= SUBLANES
    assert n_pages * page // MOBA_BLOCK <= page
    tok = pl.BlockSpec((None, t, GROUP_W), lambda b, pt: (b, 0, 0))

    def page_spec(p):
        return pl.BlockSpec((None, page, GROUP_W), lambda b, pt: (page_base + pt[b, p], 0, 0))

    specs = [tok, tok, tok] + [page_spec(p) for p in range(n_pages)] * 2
    grid_spec = pltpu.PrefetchScalarGridSpec(
        num_scalar_prefetch=1,
        grid=(batch,),
        in_specs=specs,
        out_specs=tok,
    )
    return pl.pallas_call(
        functools.partial(_moba_sample_kernel, n_pages=n_pages, page=page, t=t),
        grid_spec=grid_spec,
        out_shape=jax.ShapeDtypeStruct((batch, t, GROUP_W), BF16),
        compiler_params=_params("parallel"),
        name="moba_sample",
    )(page_table, mq, mk, mv, *([cache_k] * n_pages), *([cache_v] * n_pages))


def _out_proj_kernel(or_ref, om_ref, w_ref, x_ref, gate_ref, g_ref, b_ref, o_ref):
    y = _dot(or_ref[...], w_ref[:GROUP_W, :]) + _dot(om_ref[...], w_ref[GROUP_W:, :])
    v = ALPHA * x_ref[...] + (1.0 + gate_ref[...]) * y
    o_ref[...] = _layer_norm(v, g_ref[...], b_ref[...])


def out_proj(o_r, o_m, w_bf, x2d, gate, ln_g, ln_b, rows_per_group, tm):
    n, d = x2d.shape
    vec = pl.BlockSpec((1, d), lambda i: (0, 0))
    return pl.pallas_call(
        _out_proj_kernel,
        grid=(n // tm,),
        in_specs=[pl.BlockSpec((tm, GROUP_W), lambda i: (i, 0)),
                  pl.BlockSpec((tm, GROUP_W), lambda i: (i, 0)),
                  _resident(w_bf),
                  pl.BlockSpec((tm, d), lambda i: (i, 0)),
                  _mod_spec(gate, tm, rows_per_group), vec, vec],
        out_specs=pl.BlockSpec((tm, d), lambda i: (i, 0)),
        out_shape=jax.ShapeDtypeStruct((n, d), F32),
        compiler_params=_params("parallel"),
        name="out_proj",
    )(o_r, o_m, w_bf, x2d, gate, ln_g.reshape(1, d), ln_b.reshape(1, d))


def _ffn_kernel(*refs, d_ff, cw, period, carried):
    if carried:
        (x_ref, sh_ref, sc_ref, gate_ref, wup_ref, wdw_ref, bdw_ref, wdn_ref, g_ref, b_ref,
         o_ref, prev_ref, h_s, act_s, carry_s) = refs
    else:
        (x_ref, sh_ref, sc_ref, gate_ref, wup_ref, wdw_ref, bdw_ref, wdn_ref, g_ref, b_ref,
         e1_ref, e2_ref, o_ref, prev_ref, h_s, act_s) = refs
    tm = x_ref.shape[0]
    x = x_ref[...]
    h_s[...] = (x * (1.0 + sc_ref[...]) + sh_ref[...]).astype(BF16)
    row = lax.broadcasted_iota(jnp.int32, (tm, cw), 0)
    t_in = row if carried else row & (period - 1)

    if carried:
        @pl.when(pl.program_id(0) % (period // tm) == 0)
        def _():
            carry_s[...] = jnp.zeros_like(carry_s)

    for c in range(d_ff // cw):
        cu = slice(c * cw, (c + 1) * cw)
        cv = slice(d_ff + c * cw, d_ff + (c + 1) * cw)
        u = _dot(h_s[...], wup_ref[:, cu])
        v = _dot(h_s[...], wup_ref[:, cv])
        if carried:
            e1 = carry_s[1:2, cu]
            e2 = jnp.where(row == 0, carry_s[0:1, cu], carry_s[1:2, cu])
            carry_s[:, cu] = u[tm - 2:tm, :]
            prev_ref[:, cu] = u[tm - 2:tm, :]
        else:
            e1 = e1_ref[:, cu]
            e2 = e2_ref[:, cu]
            prev_ref[:, cu] = u
        s1 = jnp.where(t_in >= 1, pltpu.roll(u, 1, axis=0), e1)
        s2 = jnp.where(t_in >= 2, pltpu.roll(u, 2, axis=0), e2)
        uc = wdw_ref[0:1, cu] * s2 + wdw_ref[1:2, cu] * s1 + wdw_ref[2:3, cu] * u + bdw_ref[:, cu]
        act = 0.5 * uc * (1.0 + lax.erf(uc * (2.0 ** -0.5))) * v
        act_s[:, cu] = act.astype(BF16)
    fy = _dot(act_s[...], wdn_ref[...])
    vsum = ALPHA * x + (1.0 + gate_ref[...]) * fy
    o_ref[...] = _layer_norm(vsum, g_ref[...], b_ref[...])


def conv_ffn(x2d, shift, scale, gate, w_up_bf, w_dw, b_dw, w_dn_bf, ln_g, ln_b,
             rows_per_group, tm, period, prev=None):
    n, d = x2d.shape
    d_ff = w_dn_bf.shape[0]
    cw = 256
    carried = prev is None
    mod = _mod_spec(shift, tm, rows_per_group)
    vec = pl.BlockSpec((1, d), lambda i: (0, 0))
    b_dw2 = b_dw.reshape(1, d_ff)
    in_specs = [pl.BlockSpec((tm, d), lambda i: (i, 0)), mod, mod, mod,
                _resident(w_up_bf), _resident(w_dw), _resident(b_dw2), _resident(w_dn_bf), vec, vec]
    args = [x2d, shift, scale, gate, w_up_bf, w_dw, b_dw2, w_dn_bf,
            ln_g.reshape(1, d), ln_b.reshape(1, d)]
    scratch = [pltpu.VMEM((tm, d), BF16), pltpu.VMEM((tm, d_ff), BF16)]
    if carried:
        assert period % tm == 0
        prev_spec = pl.BlockSpec((None, 2, d_ff), lambda i: (i, 0, 0))
        prev_shape = jax.ShapeDtypeStruct((n // tm, 2, d_ff), F32)
        scratch.append(pltpu.VMEM((2, d_ff), F32))
    else:
        assert tm % period == 0 and period & (period - 1) == 0
        in_specs += [pl.BlockSpec((tm, d_ff), lambda i: (i, 0))] * 2
        args += list(prev)
        prev_spec = pl.BlockSpec((tm, d_ff), lambda i: (i, 0))
        prev_shape = jax.ShapeDtypeStruct((n, d_ff), F32)
    return pl.pallas_call(
        functools.partial(_ffn_kernel, d_ff=d_ff, cw=cw, period=period, carried=carried),
        grid=(n // tm,),
        in_specs=in_specs,
        out_specs=[pl.BlockSpec((tm, d), lambda i: (i, 0)), prev_spec],
        out_shape=[jax.ShapeDtypeStruct((n, d), F32), prev_shape],
        scratch_shapes=scratch,
        compiler_params=_params("arbitrary"),
        name="conv_ffn",
    )(*args)


def _glu_kernel(x_ref, sh_ref, sc_ref, w_ref, b_ref, o_ref, *, cw):
    h = (x_ref[...] * (1.0 + sc_ref[...]) + sh_ref[...]).astype(BF16)
    dc = o_ref.shape[1]
    for c in range(dc // cw):
        ca = slice(c * cw, (c + 1) * cw)
        cg = slice(dc + c * cw, dc + (c + 1) * cw)
        a = _dot(h, w_ref[:, ca]) + b_ref[:, ca]
        g = _dot(h, w_ref[:, cg]) + b_ref[:, cg]
        o_ref[:, ca] = a * jax.nn.sigmoid(g)


def conformer_glu(x2d, shift, scale, w1_bf, b1, rows_per_group, tm):
    n, d = x2d.shape
    dc = w1_bf.shape[1] // 2
    mod = _mod_spec(shift, tm, rows_per_group)
    b1r = b1.reshape(1, 2 * dc)
    return pl.pallas_call(
        functools.partial(_glu_kernel, cw=256),
        grid=(n // tm,),
        in_specs=[pl.BlockSpec((tm, d), lambda i: (i, 0)), mod, mod,
                  _resident(w1_bf), _resident(b1r)],
        out_specs=pl.BlockSpec((tm, dc), lambda i: (i, 0)),
        out_shape=jax.ShapeDtypeStruct((n, dc), F32),
        compiler_params=_params("parallel"),
        name="conformer_glu",
    )(x2d, shift, scale, w1_bf, b1r)


PAD_ROWS = 32


def _dwconv_prompt_kernel(x_ref, prev_ref, w_ref, b_ref, o_ref, win_ref, *, width, rs, cw):
    tm, dc = x_ref.shape
    j = pl.program_id(1)

    @pl.when(j == 0)
    def _():
        win_ref[0:PAD_ROWS, :] = jnp.zeros((PAD_ROWS, dc), F32)

    @pl.when(j > 0)
    def _():
        win_ref[0:PAD_ROWS, :] = prev_ref[...]

    win_ref[PAD_ROWS:, :] = x_ref[...]
    off = PAD_ROWS - (width - 1)
    for c in range(dc // cw):
        cols = slice(c * cw, (c + 1) * cw)
        for r0 in range(0, tm, rs):
            acc = jnp.zeros((rs, cw), F32) + b_ref[:, cols]
            for k in range(width):
                acc = acc + w_ref[k:k + 1, cols] * win_ref[off + r0 + k:off + r0 + k + rs, cols]
            o_ref[r0:r0 + rs, cols] = acc


def dwconv_prompt(glu, w_dw, b_dw, batch, seq, tm):
    n, dc = glu.shape
    width = w_dw.shape[0]
    assert width - 1 <= PAD_ROWS and tm % PAD_ROWS == 0
    nj = seq // tm
    per = tm // PAD_ROWS
    b2 = b_dw.reshape(1, dc)
    return pl.pallas_call(
        functools.partial(_dwconv_prompt_kernel, width=width, rs=64, cw=128),
        grid=(batch, nj),
        in_specs=[pl.BlockSpec((tm, dc), lambda b, j: (b * nj + j, 0)),
                  pl.BlockSpec((PAD_ROWS, dc), lambda b, j: (jnp.maximum((b * nj + j) * per - 1, 0), 0)),
                  pl.BlockSpec(w_dw.shape, lambda b, j: (0, 0)),
                  pl.BlockSpec(b2.shape, lambda b, j: (0, 0))],
        out_specs=pl.BlockSpec((tm, dc), lambda b, j: (b * nj + j, 0)),
        out_shape=jax.ShapeDtypeStruct((n, dc), F32),
        scratch_shapes=[pltpu.VMEM((PAD_ROWS + tm, dc), F32)],
        compiler_params=_params("parallel", "parallel"),
        name="dwconv_prompt",
    )(glu, glu, w_dw, b2)


def _dwconv_sample_kernel(x_ref, st_ref, w_ref, b_ref, o_ref, nst_ref, win_ref, *, width, bb, t):
    hist = width - 1

    def body(b, carry):
        win_ref[0:hist, :] = st_ref[b]
        win_ref[hist:hist + t, :] = x_ref[b]
        acc = jnp.zeros((t, x_ref.shape[2]), F32) + b_ref[...]
        for k in range(width):
            acc = acc + w_ref[k:k + 1, :] * win_ref[k:k + t, :]
        o_ref[b] = acc
        nst_ref[b] = win_ref[t:t + hist, :]
        return carry

    lax.fori_loop(0, bb, body, 0)


def dwconv_sample(glu, states, layer, w_dw, b_dw):
    batch, t, dc = glu.shape
    width = w_dw.shape[0]
    bb = 8
    b2 = b_dw.reshape(1, dc)
    win_rows = -(-(width - 1 + t) // SUBLANES) * SUBLANES
    return pl.pallas_call(
        functools.partial(_dwconv_sample_kernel, width=width, bb=bb, t=t),
        grid=(batch // bb,),
        in_specs=[pl.BlockSpec((bb, t, dc), lambda i: (i, 0, 0)),
                  pl.BlockSpec((None, bb, width - 1, dc), lambda i: (layer, i, 0, 0)),
                  pl.BlockSpec(w_dw.shape, lambda i: (0, 0)),
                  pl.BlockSpec(b2.shape, lambda i: (0, 0))],
        out_specs=[pl.BlockSpec((bb, t, dc), lambda i: (i, 0, 0)),
                   pl.BlockSpec((bb, width - 1, dc), lambda i: (i, 0, 0))],
        out_shape=[jax.ShapeDtypeStruct((batch, t, dc), F32),
                   jax.ShapeDtypeStruct(states.shape[1:], F32)],
        scratch_shapes=[pltpu.VMEM((win_rows, dc), F32)],
        compiler_params=_params("parallel"),
        name="dwconv_sample",
    )(glu, states, w_dw, b2)


def _conf_tail_kernel(y_ref, gl_ref, bl_ref, w_ref, b2_ref, x_ref, gate_ref, g_ref, b_ref, o_ref):
    yn = _silu(_layer_norm(y_ref[...], gl_ref[...], bl_ref[...]))
    y = _dot(yn.astype(BF16), w_ref[...]) + b2_ref[...]
    v = ALPHA * x_ref[...] + (1.0 + gate_ref[...]) * y
    o_ref[...] = _layer_norm(v, g_ref[...], b_ref[...])


def conformer_tail(y, cf_g, cf_b, w2_bf, b2, x2d, gate, ln_g, ln_b, rows_per_group, tm):
    n, d = x2d.shape
    dc = y.shape[1]
    vecc = pl.BlockSpec((1, dc), lambda i: (0, 0))
    vec = pl.BlockSpec((1, d), lambda i: (0, 0))
    return pl.pallas_call(
        _conf_tail_kernel,
        grid=(n // tm,),
        in_specs=[pl.BlockSpec((tm, dc), lambda i: (i, 0)), vecc, vecc,
                  _resident(w2_bf), vec,
                  pl.BlockSpec((tm, d), lambda i: (i, 0)),
                  _mod_spec(gate, tm, rows_per_group), vec, vec],
        out_specs=pl.BlockSpec((tm, d), lambda i: (i, 0)),
        out_shape=jax.ShapeDtypeStruct((n, d), F32),
        compiler_params=_params("parallel"),
        name="conformer_tail",
    )(y, cf_g.reshape(1, dc), cf_b.reshape(1, dc), w2_bf, b2.reshape(1, d), x2d, gate,
      ln_g.reshape(1, d), ln_b.reshape(1, d))


def kernel(x_prompt, x_sample, cache_k, cache_v, state_ret, state_conv, state_ffn, page_table, c_prompt, c_sample, ab_w_in, ab_w_out, cf_w_pw1, cf_b_pw1, cf_w_dw, cf_b_dw, cf_ln_g, cf_ln_b, cf_w_pw2, cf_b_pw2, ffn_w_up, ffn_w_dw, ffn_b_dw, ffn_w_down, ada_w, ada_b, ln_g, ln_b):
    bp, seq, d = x_prompt.shape
    bs, ts, _ = x_sample.shape
    n_layers = ada_w.shape[0]
    d_ff = ffn_w_down.shape[1]
    past_len = page_table.shape[1] * cache_k.shape[2]
    np_, ns = bp * seq, bs * ts
    tm_p = min(256, seq)
    tm_s = min(128, ns)

    ada = adaln_all(jnp.concatenate([c_prompt, c_sample], 0), ada_w, ada_b)
    ada = ada.reshape(n_layers, bp + bs, 6, d)

    def mods(l, k):
        mp = ada[l, :bp, k][:, None, :]
        ms = jnp.repeat(ada[l, bp:, k], ts, axis=0)[None]
        return mp, ms

    xp = x_prompt.reshape(np_, d)
    xs = x_sample.reshape(ns, d)
    tabs_p = rotary_tables(jnp.arange(seq, dtype=jnp.int32))
    tabs_s = tuple(jnp.tile(tb, (tm_s // ts, 1)) for tb in
                   rotary_tables(past_len + jnp.arange(ts, dtype=jnp.int32)))

    outs = {k: [] for k in ("kp", "vp", "ks", "vs", "rp", "rs", "cp", "cs", "fp", "fs")}
    for l in range(n_layers):
        i = l // 2
        sh_p, sh_s = mods(l, 0)
        sc_p, sc_s = mods(l, 1)
        gt_p, gt_s = mods(l, 2)
        if l % 2 == 0:
            w_in = ab_w_in[i].astype(BF16)
            w_out = ab_w_out[i].astype(BF16)
            rq, rk, rv, rg, mq, mk, mv = in_proj(xp, sh_p, sc_p, w_in, tabs_p, seq, tm_p)
            o_r, s_p = retention_prompt(rq, rk, rv, rg, bp, seq)
            o_m = moba_prompt(mq, mk, mv, bp, seq)
            xp = out_proj(o_r, o_m, w_out, xp, gt_p, ln_g[l, 0], ln_b[l, 0], seq, tm_p)
            outs["kp"].append(mk.reshape(bp, seq, N_HEADS, HEAD_DIM))
            outs["vp"].append(mv.reshape(bp, seq, N_HEADS, HEAD_DIM))
            outs["rp"].append(s_p)

            rq, rk, rv, rg, mq, mk, mv = in_proj(xs, sh_s, sc_s, w_in, tabs_s, ns, tm_s)
            r3 = lambda a: a.reshape(bs, ts, GROUP_W)
            o_r, s_s = retention_sample(r3(rq), r3(rk), r3(rv), r3(rg), state_ret, i, ts)
            n_phys, page = cache_k.shape[1], cache_k.shape[2]
            ck = cache_k.reshape(-1, page, GROUP_W)
            cv = cache_v.reshape(-1, page, GROUP_W)
            o_m = moba_sample(r3(mq), r3(mk), r3(mv), ck, cv, page_table, i * n_phys, ts)
            xs = out_proj(o_r.reshape(ns, GROUP_W), o_m.reshape(ns, GROUP_W), w_out, xs, gt_s,
                          ln_g[l, 0], ln_b[l, 0], ns, tm_s)
            outs["ks"].append(mk.reshape(bs, ts, N_HEADS, HEAD_DIM))
            outs["vs"].append(mv.reshape(bs, ts, N_HEADS, HEAD_DIM))
            outs["rs"].append(s_s)
        else:
            w1 = cf_w_pw1[i].astype(BF16)
            w2 = cf_w_pw2[i].astype(BF16)
            hist = cf_w_dw.shape[1] - 1
            glu = conformer_glu(xp, sh_p, sc_p, w1, cf_b_pw1[i], seq, tm_p)
            y = dwconv_prompt(glu, cf_w_dw[i], cf_b_dw[i], bp, seq, tm_p)
            xp = conformer_tail(y, cf_ln_g[i], cf_ln_b[i], w2, cf_b_pw2[i], xp, gt_p,
                                ln_g[l, 0], ln_b[l, 0], seq, tm_p)
            outs["cp"].append(glu.reshape(bp, seq, -1)[:, seq - hist:])

            glu = conformer_glu(xs, sh_s, sc_s, w1, cf_b_pw1[i], ns, tm_s)
            y, nst = dwconv_sample(glu.reshape(bs, ts, -1), state_conv, i, cf_w_dw[i], cf_b_dw[i])
            xs = conformer_tail(y.reshape(ns, -1), cf_ln_g[i], cf_ln_b[i], w2, cf_b_pw2[i], xs, gt_s,
                                ln_g[l, 0], ln_b[l, 0], ns, tm_s)
            outs["cs"].append(nst)

        sh_p, sh_s = mods(l, 3)
        sc_p, sc_s = mods(l, 4)
        gt_p, gt_s = mods(l, 5)
        w_up = ffn_w_up[l].astype(BF16)
        w_dn = ffn_w_down[l].astype(BF16)
        xp, prev_p = conv_ffn(xp, sh_p, sc_p, gt_p, w_up, ffn_w_dw[l], ffn_b_dw[l], w_dn,
                              ln_g[l, 1], ln_b[l, 1], seq, tm_p, seq)
        outs["fp"].append(prev_p.reshape(bp, seq // tm_p, 2, d_ff)[:, -1])
        st = state_ffn[l]
        zero = jnp.zeros((bs, ts - 2, d_ff), F32)
        e1 = jnp.concatenate([st[:, 1:2], zero, zero[:, :1]], 1).reshape(ns, d_ff)
        e2 = jnp.concatenate([st, zero], 1).reshape(ns, d_ff)
        xs, u_s = conv_ffn(xs, sh_s, sc_s, gt_s, w_up, ffn_w_dw[l], ffn_b_dw[l], w_dn,
                           ln_g[l, 1], ln_b[l, 1], ns, tm_s, ts, prev=(e1, e2))
        outs["fs"].append(u_s.reshape(bs, ts, d_ff)[:, ts - 2:])

    st = lambda k: outs[k][0][None] if len(outs[k]) == 1 else jnp.stack(outs[k])
    return (xp.reshape(bp, seq, d), xs.reshape(bs, ts, d), st("kp"), st("vp"), st("ks"), st("vs"),
            st("rp"), st("rs"), st("cp"), st("cs"), st("fp"), st("fs"))
```

```python
import functools

import jax
import jax.numpy as jnp
from jax import lax
from jax.experimental import pallas as pl
from jax.experimental.pallas import tpu as pltpu

HEAD_DIM = 128
N_HEADS = 4
GROUP_W = N_HEADS * HEAD_DIM
N_IN_COLS = 7
RET_CHUNK = 128
MOBA_BLOCK = 256
MOBA_TOPK = 3
ROPE_THETA = 10000.0
DEPTH = 2
ALPHA = (2 * DEPTH) ** 0.25
LN_EPS = 1e-5
GN_EPS = 1e-6
NEG = -1e30

F32 = jnp.float32
BF16 = jnp.bfloat16

_NT = (((1,), (1,)), ((), ()))
_TN = (((0,), (0,)), ((), ()))

VMEM_LIMIT = 56 * 1024 * 1024
SUBLANES = 8


def _params(*sem):
    return pltpu.CompilerParams(dimension_semantics=sem, vmem_limit_bytes=VMEM_LIMIT)


def _dot(a, b):
    return jnp.dot(a, b, preferred_element_type=F32)


def _layer_norm(v, g, b):
    mu = jnp.mean(v, axis=-1, keepdims=True)
    d = v - mu
    var = jnp.mean(d * d, axis=-1, keepdims=True)
    return d * lax.rsqrt(var + LN_EPS) * g + b


def _silu(x):
    return x * jax.nn.sigmoid(x)


def _mod_spec(mod, tm, rows_per_group):
    _, r, d = mod.shape
    if r == 1:
        return pl.BlockSpec((None, 1, d), lambda i: (i * tm // rows_per_group, 0, 0))
    return pl.BlockSpec((None, tm, d), lambda i: (0, i, 0))


def _resident(a):
    return pl.BlockSpec(a.shape, lambda *_: (0,) * a.ndim, pipeline_mode=pl.Buffered(1))


def _rows_to_tile(x, n_rows):
    t, w = x.shape
    r_i = lax.broadcasted_iota(jnp.int32, (n_rows, w), 0)
    out = jnp.zeros((n_rows, w), x.dtype)
    for ti in range(t):
        out = jnp.where(r_i == ti, x[ti:ti + 1, :], out)
    return out


def _adaln_kernel(c_ref, w_ref, b_ref, o_ref):
    a = _silu(c_ref[...]).astype(BF16)
    o_ref[...] = _dot(a, w_ref[...].astype(BF16)) + b_ref[...]


def adaln_all(c_all, ada_w, ada_b):
    n_layers, d, n_out = ada_w.shape
    m = c_all.shape[0]
    tn = 1536 if n_out % 1536 == 0 else n_out
    return pl.pallas_call(
        _adaln_kernel,
        grid=(n_layers, n_out // tn),
        in_specs=[
            pl.BlockSpec((m, d), lambda l, j: (0, 0)),
            pl.BlockSpec((None, d, tn), lambda l, j: (l, 0, j)),
            pl.BlockSpec((None, 1, tn), lambda l, j: (l, 0, j)),
        ],
        out_specs=pl.BlockSpec((None, m, tn), lambda l, j: (l, 0, j)),
        out_shape=jax.ShapeDtypeStruct((n_layers, m, n_out), F32),
        compiler_params=_params("parallel", "parallel"),
        name="adaln",
    )(c_all, ada_w, ada_b.reshape(n_layers, 1, n_out))


def _in_proj_kernel(x_ref, sh_ref, sc_ref, w_ref, cr_ref, sr_ref, cm_ref, sm_ref,
                    rq_ref, rk_ref, rv_ref, rg_ref, mq_ref, mk_ref, mv_ref, kb_ref, vb_ref,
                    *km_refs):
    tm = x_ref.shape[0]
    h = (x_ref[...] * (1.0 + sc_ref[...]) + sh_ref[...]).astype(BF16)
    cr, sr, cm, sm = cr_ref[...], sr_ref[...], cm_ref[...], sm_ref[...]
    even = (lax.broadcasted_iota(jnp.int32, cr.shape, 1) & 1) == 0
    outs = (rq_ref, rk_ref, rv_ref, rg_ref, mq_ref, mk_ref, mv_ref)
    for g, o_ref in enumerate(outs):
        z = _dot(h, w_ref[:, g * GROUP_W:(g + 1) * GROUP_W])
        for hh in range(N_HEADS):
            cols = slice(hh * HEAD_DIM, (hh + 1) * HEAD_DIM)
            zs = z[:, cols]
            if g in (0, 1):
                nxt = pltpu.roll(zs, HEAD_DIM - 1, axis=1)
                prv = pltpu.roll(zs, 1, axis=1)
                zs = zs * cr + jnp.where(even, nxt, prv) * sr
                if g == 1:
                    zs = zs * (HEAD_DIM ** -0.5)
            elif g in (4, 5):
                zs = zs * cm + pltpu.roll(zs, HEAD_DIM // 2, axis=1) * sm
            if g in (5, 6):
                o_ref[pl.ds(hh, tm, stride=N_HEADS), :] = zs
                (kb_ref if g == 5 else vb_ref)[:, cols] = zs.astype(BF16)
                if g == 5 and km_refs:
                    for r in range(tm // MOBA_BLOCK):
                        blk = zs[r * MOBA_BLOCK:(r + 1) * MOBA_BLOCK, :]
                        km_refs[0][r:r + 1, cols] = jnp.sum(blk, axis=0, keepdims=True) * (1.0 / MOBA_BLOCK)
            else:
                o_ref[:, cols] = zs.astype(o_ref.dtype)


def in_proj(x2d, shift, scale, w_bf, tabs, rows_per_group, tm, block_means):
    n, d = x2d.shape
    p_tiles = tabs[0].shape[0] // tm
    mod_spec = _mod_spec(shift, tm, rows_per_group)
    tab_spec = pl.BlockSpec((tm, HEAD_DIM), lambda i: (i % p_tiles, 0))
    wide = pl.BlockSpec((tm, GROUP_W), lambda i: (i, 0))
    tall = pl.BlockSpec((tm * N_HEADS, HEAD_DIM), lambda i: (i, 0))
    wide_shape = lambda dt: jax.ShapeDtypeStruct((n, GROUP_W), dt)
    tall_shape = jax.ShapeDtypeStruct((n * N_HEADS, HEAD_DIM), F32)
    out_specs = [wide] * 5 + [tall, tall, wide, wide]
    out_shape = [wide_shape(F32), wide_shape(F32), wide_shape(BF16), wide_shape(F32), wide_shape(BF16),
                 tall_shape, tall_shape, wide_shape(BF16), wide_shape(BF16)]
    if block_means:
        assert tm % MOBA_BLOCK == 0
        per = tm // MOBA_BLOCK
        out_specs.append(pl.BlockSpec((None, per, GROUP_W), lambda i: (i, 0, 0)))
        out_shape.append(jax.ShapeDtypeStruct((n // tm, per, GROUP_W), F32))
    return pl.pallas_call(
        _in_proj_kernel,
        grid=(n // tm,),
        in_specs=[pl.BlockSpec((tm, d), lambda i: (i, 0)), mod_spec, mod_spec,
                  _resident(w_bf), tab_spec, tab_spec, tab_spec, tab_spec],
        out_specs=out_specs,
        out_shape=out_shape,
        compiler_params=_params("parallel"),
        name="in_proj",
    )(x2d, shift, scale, w_bf, *tabs)


def rotary_tables(pos):
    half = HEAD_DIM // 2
    posf = pos.astype(F32)[:, None]
    inv_r = 1.0 / (ROPE_THETA ** jnp.linspace(0.0, 1.0, half, dtype=F32))
    ang_r = posf * inv_r[None, :]
    cr = jnp.repeat(jnp.cos(ang_r), 2, axis=1)
    sr = jnp.stack([-jnp.sin(ang_r), jnp.sin(ang_r)], -1).reshape(-1, HEAD_DIM)
    inv_m = ROPE_THETA ** (-jnp.arange(0, HEAD_DIM, 2, dtype=F32) / HEAD_DIM)
    ang_m = posf * inv_m[None, :]
    cm = jnp.concatenate([jnp.cos(ang_m), jnp.cos(ang_m)], -1)
    sm = jnp.concatenate([-jnp.sin(ang_m), jnp.sin(ang_m)], -1)
    return cr, sr, cm, sm


def retention_tables(chunk, rows):
    log_g = jnp.log1p(-jnp.exp2(-5.0 - jnp.arange(N_HEADS, dtype=F32)))
    idx = jnp.arange(chunk, dtype=F32)
    diff = idx[:, None] - idx[None, :]
    decay_in = jnp.where(diff[None] >= 0,
                         jnp.exp(jnp.maximum(diff, 0.0)[None] * log_g[:, None, None]), 0.0)
    q_dec = jnp.exp((idx + 1.0)[None, :] * log_g[:, None])
    k_dec = jnp.exp((chunk - 1.0 - idx)[None, :] * log_g[:, None])
    c_dec = jnp.exp(chunk * log_g)
    pad = rows - chunk
    decay_in = jnp.pad(decay_in, ((0, 0), (0, pad), (0, rows - chunk)))
    bc = lambda t: jnp.broadcast_to(jnp.pad(t, ((0, 0), (0, pad)))[:, :, None], (N_HEADS, rows, HEAD_DIM))
    c_b = jnp.broadcast_to(c_dec[:, None, None], (N_HEADS, 1, HEAD_DIM))
    return decay_in, bc(q_dec), bc(k_dec), c_b


def _ret_chunk(qc, kc, vc, s, din, qd, kd, cd):
    att = lax.dot_general(qc.astype(BF16), kc.astype(BF16), _NT,
                          preferred_element_type=F32) * din
    o = _dot(att.astype(BF16), vc) + _dot((qc * qd).astype(BF16), s.astype(BF16))
    s_new = cd * s + lax.dot_general((kc * kd).astype(BF16), vc, _TN,
                                     preferred_element_type=F32)
    return o, s_new


def _group_norm_gate(o, g):
    o = o * lax.rsqrt(jnp.mean(o * o, axis=-1, keepdims=True) + GN_EPS)
    return o * _silu(g)


def _ret_prompt_kernel(q_ref, k_ref, v_ref, g_ref, din_ref, qd_ref, kd_ref, cd_ref,
                       o_ref, sout_ref, s_ref, *, n_sub):
    j = pl.program_id(1)

    @pl.when(j == 0)
    def _():
        s_ref[...] = jnp.zeros_like(s_ref)

    for hh in range(N_HEADS):
        cols = slice(hh * HEAD_DIM, (hh + 1) * HEAD_DIM)
        din, qd, kd, cd = din_ref[hh], qd_ref[hh], kd_ref[hh], cd_ref[hh]
        for c in range(n_sub):
            rows = slice(c * RET_CHUNK, (c + 1) * RET_CHUNK)
            o, s_new = _ret_chunk(q_ref[rows, cols], k_ref[rows, cols], v_ref[rows, cols],
                                  s_ref[hh], din, qd, kd, cd)
            s_ref[hh] = s_new
            o_ref[rows, cols] = _group_norm_gate(o, g_ref[rows, cols]).astype(o_ref.dtype)

    @pl.when(j == pl.num_programs(1) - 1)
    def _():
        sout_ref[...] = s_ref[...]


def retention_prompt(rq, rk, rv, rg, batch, seq):
    tc = min(4 * RET_CHUNK, seq)
    nj = seq // tc
    tabs = retention_tables(RET_CHUNK, RET_CHUNK)
    tok = pl.BlockSpec((tc, GROUP_W), lambda b, j: (b * nj + j, 0))
    tab = pl.BlockSpec((N_HEADS, RET_CHUNK, HEAD_DIM), lambda b, j: (0, 0, 0))
    return pl.pallas_call(
        functools.partial(_ret_prompt_kernel, n_sub=tc // RET_CHUNK),
        grid=(batch, nj),
        in_specs=[tok, tok, tok, tok, tab, tab, tab,
                  pl.BlockSpec((N_HEADS, 1, HEAD_DIM), lambda b, j: (0, 0, 0))],
        out_specs=[tok, pl.BlockSpec((None, N_HEADS, HEAD_DIM, HEAD_DIM), lambda b, j: (b, 0, 0, 0))],
        out_shape=[jax.ShapeDtypeStruct((batch * seq, GROUP_W), BF16),
                   jax.ShapeDtypeStruct((batch, N_HEADS, HEAD_DIM, HEAD_DIM), F32)],
        scratch_shapes=[pltpu.VMEM((N_HEADS, HEAD_DIM, HEAD_DIM), F32)],
        compiler_params=_params("parallel", "arbitrary"),
        name="retention_prompt",
    )(rq, rk, rv, rg, *tabs)


def _ret_sample_kernel(q_ref, k_ref, v_ref, g_ref, s0_ref, din_ref, qd_ref, kd_ref, cd_ref,
                       o_ref, sout_ref, *, bb, t):
    for hh in range(N_HEADS):
        cols = slice(hh * HEAD_DIM, (hh + 1) * HEAD_DIM)
        din, qd, kd, cd = din_ref[hh], qd_ref[hh], kd_ref[hh], cd_ref[hh]

        def body(b, carry):
            qc = _rows_to_tile(q_ref[b, :, cols], RET_CHUNK)
            kc = _rows_to_tile(k_ref[b, :, cols], RET_CHUNK)
            vc = _rows_to_tile(v_ref[b, :, cols].astype(F32), RET_CHUNK).astype(BF16)
            o, s_new = _ret_chunk(qc, kc, vc, s0_ref[b, hh], din, qd, kd, cd)
            sout_ref[b, hh] = s_new
            o_ref[b, :, cols] = _group_norm_gate(o[0:t, :], g_ref[b, :, cols]).astype(o_ref.dtype)
            return carry

        lax.fori_loop(0, bb, body, 0)


def retention_sample(rq, rk, rv, rg, states, layer, t):
    batch = states.shape[1]
    bb = 8
    tabs = retention_tables(t, RET_CHUNK)
    tok = pl.BlockSpec((bb, t, GROUP_W), lambda i: (i, 0, 0))
    st_in = pl.BlockSpec((None, bb, N_HEADS, HEAD_DIM, HEAD_DIM), lambda i: (layer, i, 0, 0, 0))
    st = pl.BlockSpec((bb, N_HEADS, HEAD_DIM, HEAD_DIM), lambda i: (i, 0, 0, 0))
    tab = pl.BlockSpec((N_HEADS, RET_CHUNK, HEAD_DIM), lambda i: (0, 0, 0))
    return pl.pallas_call(
        functools.partial(_ret_sample_kernel, bb=bb, t=t),
        grid=(batch // bb,),
        in_specs=[tok, tok, tok, tok, st_in, tab, tab, tab,
                  pl.BlockSpec((N_HEADS, 1, HEAD_DIM), lambda i: (0, 0, 0))],
        out_specs=[tok, st],
        out_shape=[jax.ShapeDtypeStruct((batch, t, GROUP_W), BF16),
                   jax.ShapeDtypeStruct(states.shape[1:], F32)],
        compiler_params=_params("parallel"),
        name="retention_sample",
    )(rq, rk, rv, rg, states, *tabs)


def _beaten_counts(rows, n_valid):
    counts = []
    for n in range(n_valid):
        cnt = jnp.zeros(rows[n].shape, jnp.int32)
        for m in range(n_valid):
            if m == n:
                continue
            beats = (rows[m] >= rows[n]) if m < n else (rows[m] > rows[n])
            cnt = cnt + beats.astype(jnp.int32)
        counts.append(cnt)
    return counts


def _moba_prompt_kernel(q_ref, k_ref, v_ref, km_ref, o_ref, s_ref, *, nblk):
    blk = MOBA_BLOCK
    scale = HEAD_DIM ** -0.5
    kb = k_ref[...]
    vt = v_ref[...].astype(F32).T.astype(BF16)
    kmean = km_ref[...].astype(BF16)
    key_i = lax.broadcasted_iota(jnp.int32, (blk, blk), 0)
    qry_i = lax.broadcasted_iota(jnp.int32, (blk, blk), 1)
    causal = key_i <= qry_i

    for qi in range(nblk):
        qb = q_ref[qi * blk:(qi + 1) * blk, :]
        cnt = None
        if qi > MOBA_TOPK:
            gt = lax.dot_general(kmean, qb, _NT, preferred_element_type=F32)
            cnt = _beaten_counts([gt[m:m + 1, :] for m in range(qi)], qi)
        mx = jnp.full((1, blk), NEG, F32)
        for n in range(qi + 1):
            s = lax.dot_general(kb[n * blk:(n + 1) * blk], qb, _NT,
                                preferred_element_type=F32) * scale
            if n == qi:
                s = jnp.where(causal, s, NEG)
            elif cnt is not None:
                s = jnp.where(jnp.broadcast_to(cnt[n], s.shape) < MOBA_TOPK, s, NEG)
            s_ref[n * blk:(n + 1) * blk, :] = s
            mx = jnp.maximum(mx, jnp.max(s, axis=0, keepdims=True))
        l = jnp.zeros((1, blk), F32)
        ot = jnp.zeros((HEAD_DIM, blk), F32)
        for n in range(qi + 1):
            p = jnp.exp(s_ref[n * blk:(n + 1) * blk, :] - mx)
            l = l + jnp.sum(p, axis=0, keepdims=True)
            ot = ot + _dot(vt[:, n * blk:(n + 1) * blk], p.astype(BF16))
        ot = ot * (1.0 / l)
        o_ref[qi * blk:(qi + 1) * blk, :] = ot.T.astype(o_ref.dtype)


def moba_prompt(mq, mk, mv, kmean, batch, seq):
    nblk = seq // MOBA_BLOCK
    assert nblk % SUBLANES == 0
    spec = pl.BlockSpec((seq, HEAD_DIM), lambda b, h: (b, h))
    return pl.pallas_call(
        functools.partial(_moba_prompt_kernel, nblk=nblk),
        grid=(batch, N_HEADS),
        in_specs=[spec, spec, spec, pl.BlockSpec((nblk, HEAD_DIM), lambda b, h: (b, h))],
        out_specs=spec,
        out_shape=jax.ShapeDtypeStruct((batch * seq, GROUP_W), BF16),
        scratch_shapes=[pltpu.VMEM((seq, MOBA_BLOCK), F32)],
        compiler_params=_params("parallel", "parallel"),
        name="moba_prompt",
    )(mq, mk, mv, kmean)


def _page_rows(ref, page):
    return jnp.concatenate([ref[pl.ds(hh, page, stride=N_HEADS), :] for hh in range(N_HEADS)], axis=1)


def _moba_sample_kernel(pt_ref, q_ref, kn_ref, vn_ref, *rest, n_pages, page, t):
    k_refs = rest[:n_pages]
    v_refs = rest[n_pages:2 * n_pages]
    o_ref = rest[2 * n_pages]
    scale = HEAD_DIM ** -0.5
    ppb = MOBA_BLOCK // page
    nblk = n_pages // ppb
    n_pairs = N_HEADS * t

    q = q_ref[...].astype(F32)
    r_i = lax.broadcasted_iota(jnp.int32, (n_pairs, GROUP_W), 0)
    l_i = lax.broadcasted_iota(jnp.int32, (n_pairs, GROUP_W), 1)
    qm = jnp.zeros((n_pairs, GROUP_W), F32)
    for ti in range(t):
        for hh in range(N_HEADS):
            hit = (r_i == hh * t + ti) & (l_i >= hh * HEAD_DIM) & (l_i < (hh + 1) * HEAD_DIM)
            qm = jnp.where(hit, q[ti:ti + 1, :], qm)
    qm = qm.astype(BF16)

    zpad = jnp.zeros((page - SUBLANES, GROUP_W), F32)
    k_own = jnp.concatenate([_rows_to_tile(kn_ref[...].astype(F32), SUBLANES), zpad], axis=0).astype(BF16)
    v_own = jnp.concatenate([_rows_to_tile(vn_ref[...].astype(F32), SUBLANES), zpad], axis=0).astype(BF16)

    b_i = lax.broadcasted_iota(jnp.int32, (page, GROUP_W), 0)
    kmean = jnp.zeros((page, GROUP_W), F32)
    scores = []
    for p in range(n_pages):
        kp = _page_rows(k_refs[p], page)
        scores.append(lax.dot_general(qm, kp.astype(BF16), _NT,
                                      preferred_element_type=F32) * scale)
        ks = jnp.sum(kp, axis=0, keepdims=True) * (1.0 / MOBA_BLOCK)
        kmean = kmean + jnp.where(b_i == p // ppb, ks, 0.0)
    gate = lax.dot_general(qm, kmean.astype(BF16), _NT, preferred_element_type=F32)
    cnt = _beaten_counts([gate[:, m:m + 1] for m in range(nblk)], nblk)
    keep = [jnp.broadcast_to(c, (n_pairs, page)) < MOBA_TOPK for c in cnt]

    s_own = lax.dot_general(qm, k_own, _NT, preferred_element_type=F32) * scale
    key_i = lax.broadcasted_iota(jnp.int32, (n_pairs, page), 1)
    tok_i = lax.rem(lax.broadcasted_iota(jnp.int32, (n_pairs, page), 0), t)
    s_own = jnp.where((key_i <= tok_i) & (key_i < t), s_own, NEG)

    mx = s_own
    for p in range(n_pages):
        scores[p] = jnp.where(keep[p // ppb], scores[p], NEG)
        mx = jnp.maximum(mx, scores[p])
    mx = jnp.max(mx, axis=1, keepdims=True)
    e_own = jnp.exp(s_own - mx)
    l = e_own
    acc = _dot(e_own.astype(BF16), v_own)
    for p in range(n_pages):
        e = jnp.exp(scores[p] - mx)
        l = l + e
        acc = acc + _dot(e.astype(BF16), _page_rows(v_refs[p], page).astype(BF16))
    acc = acc * (1.0 / jnp.sum(l, axis=1, keepdims=True))
    for hh in range(N_HEADS):
        cols = slice(hh * HEAD_DIM, (hh + 1) * HEAD_DIM)
        o_ref[:, cols] = acc[hh * t:(hh + 1) * t, cols].astype(o_ref.dtype)


def moba_sample(mq, mk, mv, cache_k, cache_v, page_table, page_base, page, t):
    batch, n_pages = page_table.shape
    assert (n_pages * page) % MOBA_BLOCK == 0 and t <= SUBLANES
    assert n_pages * page // MOBA_BLOCK <= page
    tok = pl.BlockSpec((None, t, GROUP_W), lambda b, pt: (b, 0, 0))

    def page_spec(p):
        return pl.BlockSpec((page * N_HEADS, HEAD_DIM), lambda b, pt: (page_base + pt[b, p], 0))

    specs = [tok, tok, tok] + [page_spec(p) for p in range(n_pages)] * 2
    grid_spec = pltpu.PrefetchScalarGridSpec(
        num_scalar_prefetch=1,
        grid=(batch,),
        in_specs=specs,
        out_specs=tok,
    )
    return pl.pallas_call(
        functools.partial(_moba_sample_kernel, n_pages=n_pages, page=page, t=t),
        grid_spec=grid_spec,
        out_shape=jax.ShapeDtypeStruct((batch, t, GROUP_W), BF16),
        compiler_params=_params("parallel"),
        name="moba_sample",
    )(page_table, mq, mk, mv, *([cache_k] * n_pages), *([cache_v] * n_pages))


def _out_proj_kernel(or_ref, om_ref, w_ref, x_ref, gate_ref, g_ref, b_ref, o_ref):
    y = _dot(or_ref[...], w_ref[:GROUP_W, :]) + _dot(om_ref[...], w_ref[GROUP_W:, :])
    v = ALPHA * x_ref[...] + (1.0 + gate_ref[...]) * y
    o_ref[...] = _layer_norm(v, g_ref[...], b_ref[...])


def out_proj(o_r, o_m, w_bf, x2d, gate, ln_g, ln_b, rows_per_group, tm):
    n, d = x2d.shape
    vec = pl.BlockSpec((1, d), lambda i: (0, 0))
    return pl.pallas_call(
        _out_proj_kernel,
        grid=(n // tm,),
        in_specs=[pl.BlockSpec((tm, GROUP_W), lambda i: (i, 0)),
                  pl.BlockSpec((tm, GROUP_W), lambda i: (i, 0)),
                  _resident(w_bf),
                  pl.BlockSpec((tm, d), lambda i: (i, 0)),
                  _mod_spec(gate, tm, rows_per_group), vec, vec],
        out_specs=pl.BlockSpec((tm, d), lambda i: (i, 0)),
        out_shape=jax.ShapeDtypeStruct((n, d), F32),
        compiler_params=_params("parallel"),
        name="out_proj",
    )(o_r, o_m, w_bf, x2d, gate, ln_g.reshape(1, d), ln_b.reshape(1, d))


def _ffn_kernel(*refs, d_ff, cw, period, carried):
    if carried:
        (x_ref, sh_ref, sc_ref, gate_ref, wup_ref, wdw_ref, bdw_ref, wdn_ref, g_ref, b_ref,
         o_ref, prev_ref, h_s, act_s, carry_s) = refs
    else:
        (x_ref, sh_ref, sc_ref, gate_ref, wup_ref, wdw_ref, bdw_ref, wdn_ref, g_ref, b_ref,
         e1_ref, e2_ref, o_ref, prev_ref, h_s, act_s) = refs
    tm = x_ref.shape[0]
    x = x_ref[...]
    h_s[...] = (x * (1.0 + sc_ref[...]) + sh_ref[...]).astype(BF16)
    row = lax.broadcasted_iota(jnp.int32, (tm, cw), 0)
    t_in = row if carried else row & (period - 1)

    if carried:
        @pl.when(pl.program_id(0) % (period // tm) == 0)
        def _():
            carry_s[...] = jnp.zeros_like(carry_s)

    for c in range(d_ff // cw):
        cu = slice(c * cw, (c + 1) * cw)
        cv = slice(d_ff + c * cw, d_ff + (c + 1) * cw)
        u = _dot(h_s[...], wup_ref[:, cu])
        v = _dot(h_s[...], wup_ref[:, cv])
        if carried:
            e1 = carry_s[1:2, cu]
            e2 = jnp.where(row == 0, carry_s[0:1, cu], carry_s[1:2, cu])
            carry_s[:, cu] = u[tm - 2:tm, :]
            prev_ref[:, cu] = u[tm - 2:tm, :]
        else:
            e1 = e1_ref[:, cu]
            e2 = e2_ref[:, cu]
            prev_ref[:, cu] = u
        s1 = jnp.where(t_in >= 1, pltpu.roll(u, 1, axis=0), e1)
        s2 = jnp.where(t_in >= 2, pltpu.roll(u, 2, axis=0), e2)
        uc = wdw_ref[0:1, cu] * s2 + wdw_ref[1:2, cu] * s1 + wdw_ref[2:3, cu] * u + bdw_ref[:, cu]
        act = 0.5 * uc * (1.0 + lax.erf(uc * (2.0 ** -0.5))) * v
        act_s[:, cu] = act.astype(BF16)
    fy = _dot(act_s[...], wdn_ref[...])
    vsum = ALPHA * x + (1.0 + gate_ref[...]) * fy
    o_ref[...] = _layer_norm(vsum, g_ref[...], b_ref[...])


def conv_ffn(x2d, shift, scale, gate, w_up_bf, w_dw, b_dw, w_dn_bf, ln_g, ln_b,
             rows_per_group, tm, period, prev=None):
    n, d = x2d.shape
    d_ff = w_dn_bf.shape[0]
    cw = 256
    carried = prev is None
    mod = _mod_spec(shift, tm, rows_per_group)
    vec = pl.BlockSpec((1, d), lambda i: (0, 0))
    b_dw2 = b_dw.reshape(1, d_ff)
    in_specs = [pl.BlockSpec((tm, d), lambda i: (i, 0)), mod, mod, mod,
                _resident(w_up_bf), _resident(w_dw), _resident(b_dw2), _resident(w_dn_bf), vec, vec]
    args = [x2d, shift, scale, gate, w_up_bf, w_dw, b_dw2, w_dn_bf,
            ln_g.reshape(1, d), ln_b.reshape(1, d)]
    scratch = [pltpu.VMEM((tm, d), BF16), pltpu.VMEM((tm, d_ff), BF16)]
    if carried:
        assert period % tm == 0
        prev_spec = pl.BlockSpec((None, 2, d_ff), lambda i: (i, 0, 0))
        prev_shape = jax.ShapeDtypeStruct((n // tm, 2, d_ff), F32)
        scratch.append(pltpu.VMEM((2, d_ff), F32))
    else:
        assert tm % period == 0 and period & (period - 1) == 0
        in_specs += [pl.BlockSpec((tm, d_ff), lambda i: (i, 0))] * 2
        args += list(prev)
        prev_spec = pl.BlockSpec((tm, d_ff), lambda i: (i, 0))
        prev_shape = jax.ShapeDtypeStruct((n, d_ff), F32)
    return pl.pallas_call(
        functools.partial(_ffn_kernel, d_ff=d_ff, cw=cw, period=period, carried=carried),
        grid=(n // tm,),
        in_specs=in_specs,
        out_specs=[pl.BlockSpec((tm, d), lambda i: (i, 0)), prev_spec],
        out_shape=[jax.ShapeDtypeStruct((n, d), F32), prev_shape],
        scratch_shapes=scratch,
        compiler_params=_params("arbitrary"),
        name="conv_ffn",
    )(*args)


def _glu_kernel(x_ref, sh_ref, sc_ref, w_ref, b_ref, o_ref, *, cw):
    h = (x_ref[...] * (1.0 + sc_ref[...]) + sh_ref[...]).astype(BF16)
    dc = o_ref.shape[1]
    for c in range(dc // cw):
        ca = slice(c * cw, (c + 1) * cw)
        cg = slice(dc + c * cw, dc + (c + 1) * cw)
        a = _dot(h, w_ref[:, ca]) + b_ref[:, ca]
        g = _dot(h, w_ref[:, cg]) + b_ref[:, cg]
        o_ref[:, ca] = a * jax.nn.sigmoid(g)


def conformer_glu(x2d, shift, scale, w1_bf, b1, rows_per_group, tm):
    n, d = x2d.shape
    dc = w1_bf.shape[1] // 2
    mod = _mod_spec(shift, tm, rows_per_group)
    b1r = b1.reshape(1, 2 * dc)
    return pl.pallas_call(
        functools.partial(_glu_kernel, cw=256),
        grid=(n // tm,),
        in_specs=[pl.BlockSpec((tm, d), lambda i: (i, 0)), mod, mod,
                  _resident(w1_bf), _resident(b1r)],
        out_specs=pl.BlockSpec((tm, dc), lambda i: (i, 0)),
        out_shape=jax.ShapeDtypeStruct((n, dc), F32),
        compiler_params=_params("parallel"),
        name="conformer_glu",
    )(x2d, shift, scale, w1_bf, b1r)


PAD_ROWS = 32


def _dwconv_prompt_kernel(x_ref, prev_ref, w_ref, b_ref, o_ref, win_ref, *, width, rs, cw):
    tm, dc = x_ref.shape
    j = pl.program_id(1)

    @pl.when(j == 0)
    def _():
        win_ref[0:PAD_ROWS, :] = jnp.zeros((PAD_ROWS, dc), F32)

    @pl.when(j > 0)
    def _():
        win_ref[0:PAD_ROWS, :] = prev_ref[...]

    win_ref[PAD_ROWS:, :] = x_ref[...]
    off = PAD_ROWS - (width - 1)
    for c in range(dc // cw):
        cols = slice(c * cw, (c + 1) * cw)
        for r0 in range(0, tm, rs):
            acc = jnp.zeros((rs, cw), F32) + b_ref[:, cols]
            for k in range(width):
                acc = acc + w_ref[k:k + 1, cols] * win_ref[off + r0 + k:off + r0 + k + rs, cols]
            o_ref[r0:r0 + rs, cols] = acc


def dwconv_prompt(glu, w_dw, b_dw, batch, seq, tm):
    n, dc = glu.shape
    width = w_dw.shape[0]
    assert width - 1 <= PAD_ROWS and tm % PAD_ROWS == 0
    nj = seq // tm
    per = tm // PAD_ROWS
    b2 = b_dw.reshape(1, dc)
    return pl.pallas_call(
        functools.partial(_dwconv_prompt_kernel, width=width, rs=64, cw=128),
        grid=(batch, nj),
        in_specs=[pl.BlockSpec((tm, dc), lambda b, j: (b * nj + j, 0)),
                  pl.BlockSpec((PAD_ROWS, dc), lambda b, j: (jnp.maximum((b * nj + j) * per - 1, 0), 0)),
                  pl.BlockSpec(w_dw.shape, lambda b, j: (0, 0)),
                  pl.BlockSpec(b2.shape, lambda b, j: (0, 0))],
        out_specs=pl.BlockSpec((tm, dc), lambda b, j: (b * nj + j, 0)),
        out_shape=jax.ShapeDtypeStruct((n, dc), F32),
        scratch_shapes=[pltpu.VMEM((PAD_ROWS + tm, dc), F32)],
        compiler_params=_params("parallel", "parallel"),
        name="dwconv_prompt",
    )(glu, glu, w_dw, b2)


def _dwconv_sample_kernel(x_ref, st_ref, w_ref, b_ref, o_ref, nst_ref, win_ref, *, width, bb, t):
    hist = width - 1

    def body(b, carry):
        win_ref[0:hist, :] = st_ref[b]
        win_ref[hist:hist + t, :] = x_ref[b]
        acc = jnp.zeros((t, x_ref.shape[2]), F32) + b_ref[...]
        for k in range(width):
            acc = acc + w_ref[k:k + 1, :] * win_ref[k:k + t, :]
        o_ref[b] = acc
        nst_ref[b] = win_ref[t:t + hist, :]
        return carry

    lax.fori_loop(0, bb, body, 0)


def dwconv_sample(glu, states, layer, w_dw, b_dw):
    batch, t, dc = glu.shape
    width = w_dw.shape[0]
    bb = 8
    b2 = b_dw.reshape(1, dc)
    win_rows = -(-(width - 1 + t) // SUBLANES) * SUBLANES
    return pl.pallas_call(
        functools.partial(_dwconv_sample_kernel, width=width, bb=bb, t=t),
        grid=(batch // bb,),
        in_specs=[pl.BlockSpec((bb, t, dc), lambda i: (i, 0, 0)),
                  pl.BlockSpec((None, bb, width - 1, dc), lambda i: (layer, i, 0, 0)),
                  pl.BlockSpec(w_dw.shape, lambda i: (0, 0)),
                  pl.BlockSpec(b2.shape, lambda i: (0, 0))],
        out_specs=[pl.BlockSpec((bb, t, dc), lambda i: (i, 0, 0)),
                   pl.BlockSpec((bb, width - 1, dc), lambda i: (i, 0, 0))],
        out_shape=[jax.ShapeDtypeStruct((batch, t, dc), F32),
                   jax.ShapeDtypeStruct(states.shape[1:], F32)],
        scratch_shapes=[pltpu.VMEM((win_rows, dc), F32)],
        compiler_params=_params("parallel"),
        name="dwconv_sample",
    )(glu, states, w_dw, b2)


def _conf_tail_kernel(y_ref, gl_ref, bl_ref, w_ref, b2_ref, x_ref, gate_ref, g_ref, b_ref, o_ref):
    yn = _silu(_layer_norm(y_ref[...], gl_ref[...], bl_ref[...]))
    y = _dot(yn.astype(BF16), w_ref[...]) + b2_ref[...]
    v = ALPHA * x_ref[...] + (1.0 + gate_ref[...]) * y
    o_ref[...] = _layer_norm(v, g_ref[...], b_ref[...])


def conformer_tail(y, cf_g, cf_b, w2_bf, b2, x2d, gate, ln_g, ln_b, rows_per_group, tm):
    n, d = x2d.shape
    dc = y.shape[1]
    vecc = pl.BlockSpec((1, dc), lambda i: (0, 0))
    vec = pl.BlockSpec((1, d), lambda i: (0, 0))
    return pl.pallas_call(
        _conf_tail_kernel,
        grid=(n // tm,),
        in_specs=[pl.BlockSpec((tm, dc), lambda i: (i, 0)), vecc, vecc,
                  _resident(w2_bf), vec,
                  pl.BlockSpec((tm, d), lambda i: (i, 0)),
                  _mod_spec(gate, tm, rows_per_group), vec, vec],
        out_specs=pl.BlockSpec((tm, d), lambda i: (i, 0)),
        out_shape=jax.ShapeDtypeStruct((n, d), F32),
        compiler_params=_params("parallel"),
        name="conformer_tail",
    )(y, cf_g.reshape(1, dc), cf_b.reshape(1, dc), w2_bf, b2.reshape(1, d), x2d, gate,
      ln_g.reshape(1, d), ln_b.reshape(1, d))


def kernel(x_prompt, x_sample, cache_k, cache_v, state_ret, state_conv, state_ffn, page_table, c_prompt, c_sample, ab_w_in, ab_w_out, cf_w_pw1, cf_b_pw1, cf_w_dw, cf_b_dw, cf_ln_g, cf_ln_b, cf_w_pw2, cf_b_pw2, ffn_w_up, ffn_w_dw, ffn_b_dw, ffn_w_down, ada_w, ada_b, ln_g, ln_b):
    bp, seq, d = x_prompt.shape
    bs, ts, _ = x_sample.shape
    n_layers = ada_w.shape[0]
    d_ff = ffn_w_down.shape[1]
    past_len = page_table.shape[1] * cache_k.shape[2]
    np_, ns = bp * seq, bs * ts
    tm_p = min(256, seq)
    tm_s = min(128, ns)

    ada = adaln_all(jnp.concatenate([c_prompt, c_sample], 0), ada_w, ada_b)
    ada = ada.reshape(n_layers, bp + bs, 6, d)

    def mods(l, k):
        mp = ada[l, :bp, k][:, None, :]
        ms = jnp.repeat(ada[l, bp:, k], ts, axis=0)[None]
        return mp, ms

    xp = x_prompt.reshape(np_, d)
    xs = x_sample.reshape(ns, d)
    tabs_p = rotary_tables(jnp.arange(seq, dtype=jnp.int32))
    tabs_s = tuple(jnp.tile(tb, (tm_s // ts, 1)) for tb in
                   rotary_tables(past_len + jnp.arange(ts, dtype=jnp.int32)))

    outs = {k: [] for k in ("kp", "vp", "ks", "vs", "rp", "rs", "cp", "cs", "fp", "fs")}
    for l in range(n_layers):
        i = l // 2
        sh_p, sh_s = mods(l, 0)
        sc_p, sc_s = mods(l, 1)
        gt_p, gt_s = mods(l, 2)
        if l % 2 == 0:
            w_in = ab_w_in[i].astype(BF16)
            w_out = ab_w_out[i].astype(BF16)
            rq, rk, rv, rg, mq, mk, mv, kb, vb, kmean = in_proj(
                xp, sh_p, sc_p, w_in, tabs_p, seq, tm_p, True)
            o_r, s_p = retention_prompt(rq, rk, rv, rg, bp, seq)
            o_m = moba_prompt(mq, kb, vb, kmean.reshape(-1, GROUP_W), bp, seq)
            xp = out_proj(o_r, o_m, w_out, xp, gt_p, ln_g[l, 0], ln_b[l, 0], seq, tm_p)
            outs["kp"].append(mk.reshape(bp, seq, N_HEADS, HEAD_DIM))
            outs["vp"].append(mv.reshape(bp, seq, N_HEADS, HEAD_DIM))
            outs["rp"].append(s_p)

            rq, rk, rv, rg, mq, mk, mv, kb, vb = in_proj(xs, sh_s, sc_s, w_in, tabs_s, ns, tm_s, False)
            r3 = lambda a: a.reshape(bs, ts, GROUP_W)
            o_r, s_s = retention_sample(r3(rq), r3(rk), r3(rv), r3(rg), state_ret, i, ts)
            n_phys, page = cache_k.shape[1], cache_k.shape[2]
            ck = cache_k.reshape(-1, HEAD_DIM)
            cv = cache_v.reshape(-1, HEAD_DIM)
            o_m = moba_sample(r3(mq), r3(kb), r3(vb), ck, cv, page_table, i * n_phys, page, ts)
            xs = out_proj(o_r.reshape(ns, GROUP_W), o_m.reshape(ns, GROUP_W), w_out, xs, gt_s,
                          ln_g[l, 0], ln_b[l, 0], ns, tm_s)
            outs["ks"].append(mk.reshape(bs, ts, N_HEADS, HEAD_DIM))
            outs["vs"].append(mv.reshape(bs, ts, N_HEADS, HEAD_DIM))
            outs["rs"].append(s_s)
        else:
            w1 = cf_w_pw1[i].astype(BF16)
            w2 = cf_w_pw2[i].astype(BF16)
            hist = cf_w_dw.shape[1] - 1
            glu = conformer_glu(xp, sh_p, sc_p, w1, cf_b_pw1[i], seq, tm_p)
            y = dwconv_prompt(glu, cf_w_dw[i], cf_b_dw[i], bp, seq, tm_p)
            xp = conformer_tail(y, cf_ln_g[i], cf_ln_b[i], w2, cf_b_pw2[i], xp, gt_p,
                                ln_g[l, 0], ln_b[l, 0], seq, tm_p)
            outs["cp"].append(glu.reshape(bp, seq, -1)[:, seq - hist:])

            glu = conformer_glu(xs, sh_s, sc_s, w1, cf_b_pw1[i], ns, tm_s)
            y, nst = dwconv_sample(glu.reshape(bs, ts, -1), state_conv, i, cf_w_dw[i], cf_b_dw[i])
            xs = conformer_tail(y.reshape(ns, -1), cf_ln_g[i], cf_ln_b[i], w2, cf_b_pw2[i], xs, gt_s,
                                ln_g[l, 0], ln_b[l, 0], ns, tm_s)
            outs["cs"].append(nst)

        sh_p, sh_s = mods(l, 3)
        sc_p, sc_s = mods(l, 4)
        gt_p, gt_s = mods(l, 5)
        w_up = ffn_w_up[l].astype(BF16)
        w_dn = ffn_w_down[l].astype(BF16)
        xp, prev_p = conv_ffn(xp, sh_p, sc_p, gt_p, w_up, ffn_w_dw[l], ffn_b_dw[l], w_dn,
                              ln_g[l, 1], ln_b[l, 1], seq, tm_p, seq)
        outs["fp"].append(prev_p.reshape(bp, seq // tm_p, 2, d_ff)[:, -1])
        st = state_ffn[l]
        zero = jnp.zeros((bs, ts - 2, d_ff), F32)
        e1 = jnp.concatenate([st[:, 1:2], zero, zero[:, :1]], 1).reshape(ns, d_ff)
        e2 = jnp.concatenate([st, zero], 1).reshape(ns, d_ff)
        xs, u_s = conv_ffn(xs, sh_s, sc_s, gt_s, w_up, ffn_w_dw[l], ffn_b_dw[l], w_dn,
                           ln_g[l, 1], ln_b[l, 1], ns, tm_s, ts, prev=(e1, e2))
        outs["fs"].append(u_s.reshape(bs, ts, d_ff)[:, ts - 2:])

    st = lambda k: outs[k][0][None] if len(outs[k]) == 1 else jnp.stack(outs[k])
    return (xp.reshape(bp, seq, d), xs.reshape(bs, ts, d), st("kp"), st("vp"), st("ks"), st("vs"),
            st("rp"), st("rs"), st("cp"), st("cs"), st("fp"), st("fs"))
```

```python
import functools

import jax
import jax.numpy as jnp
from jax import lax
from jax.experimental import pallas as pl
from jax.experimental.pallas import tpu as pltpu

HEAD_DIM = 128
N_HEADS = 4
GROUP_W = N_HEADS * HEAD_DIM
N_IN_COLS = 7
RET_CHUNK = 128
MOBA_BLOCK = 256
MOBA_TOPK = 3
ROPE_THETA = 10000.0
DEPTH = 2
ALPHA = (2 * DEPTH) ** 0.25
LN_EPS = 1e-5
GN_EPS = 1e-6
NEG = -1e30

F32 = jnp.float32
BF16 = jnp.bfloat16

_NT = (((1,), (1,)), ((), ()))
_TN = (((0,), (0,)), ((), ()))

VMEM_LIMIT = 56 * 1024 * 1024
SUBLANES = 8


def _params(*sem):
    return pltpu.CompilerParams(dimension_semantics=sem, vmem_limit_bytes=VMEM_LIMIT)


def _dot(a, b):
    return jnp.dot(a, b, preferred_element_type=F32)


def _layer_norm(v, g, b):
    mu = jnp.mean(v, axis=-1, keepdims=True)
    d = v - mu
    var = jnp.mean(d * d, axis=-1, keepdims=True)
    return d * lax.rsqrt(var + LN_EPS) * g + b


def _silu(x):
    return x * jax.nn.sigmoid(x)


def _mod_spec(mod, tm, rows_per_group):
    _, r, d = mod.shape
    if r == 1:
        return pl.BlockSpec((None, 1, d), lambda i: (i * tm // rows_per_group, 0, 0))
    return pl.BlockSpec((None, tm, d), lambda i: (0, i, 0))


def _resident(a, layer=None):
    if layer is None:
        return pl.BlockSpec(a.shape, lambda *_: (0,) * a.ndim, pipeline_mode=pl.Buffered(1))
    return pl.BlockSpec((None,) + a.shape[1:], lambda *_: (layer,) + (0,) * (a.ndim - 1),
                        pipeline_mode=pl.Buffered(1))


def _rows_to_tile(x, n_rows):
    t, w = x.shape
    r_i = lax.broadcasted_iota(jnp.int32, (n_rows, w), 0)
    out = jnp.zeros((n_rows, w), x.dtype)
    for ti in range(t):
        out = jnp.where(r_i == ti, x[ti:ti + 1, :], out)
    return out


def _adaln_kernel(c_ref, w_ref, b_ref, o_ref):
    a = _silu(c_ref[...]).astype(BF16)
    o_ref[...] = _dot(a, w_ref[...].astype(BF16)) + b_ref[...]


def adaln_all(c_all, ada_w, ada_b):
    n_layers, d, n_out = ada_w.shape
    m = c_all.shape[0]
    return pl.pallas_call(
        _adaln_kernel,
        grid=(n_layers, n_out // d),
        in_specs=[
            pl.BlockSpec((m, d), lambda l, j: (0, 0)),
            pl.BlockSpec((None, d, d), lambda l, j: (l, 0, j)),
            pl.BlockSpec((None, 1, d), lambda l, j: (l, 0, j)),
        ],
        out_specs=pl.BlockSpec((None, None, m, d), lambda l, j: (l, j, 0, 0)),
        out_shape=jax.ShapeDtypeStruct((n_layers, n_out // d, m, d), F32),
        compiler_params=_params("parallel", "parallel"),
        name="adaln",
    )(c_all, ada_w, ada_b.reshape(n_layers, 1, n_out))


def _in_proj_kernel(x_ref, sh_ref, sc_ref, w_ref, cr_ref, sr_ref, cm_ref, sm_ref,
                    rq_ref, rk_ref, rv_ref, rg_ref, mq_ref, mk_ref, mv_ref, kb_ref, vb_ref,
                    *km_refs):
    tm = x_ref.shape[0]
    h = (x_ref[...] * (1.0 + sc_ref[...]) + sh_ref[...]).astype(BF16)
    cr, sr, cm, sm = cr_ref[...], sr_ref[...], cm_ref[...], sm_ref[...]
    even = (lax.broadcasted_iota(jnp.int32, cr.shape, 1) & 1) == 0
    outs = (rq_ref, rk_ref, rv_ref, rg_ref, mq_ref, mk_ref, mv_ref)
    for g, o_ref in enumerate(outs):
        z = _dot(h, w_ref[:, g * GROUP_W:(g + 1) * GROUP_W])
        for hh in range(N_HEADS):
            cols = slice(hh * HEAD_DIM, (hh + 1) * HEAD_DIM)
            zs = z[:, cols]
            if g in (0, 1):
                nxt = pltpu.roll(zs, HEAD_DIM - 1, axis=1)
                prv = pltpu.roll(zs, 1, axis=1)
                zs = zs * cr + jnp.where(even, nxt, prv) * sr
                if g == 1:
                    zs = zs * (HEAD_DIM ** -0.5)
            elif g in (4, 5):
                zs = zs * cm + pltpu.roll(zs, HEAD_DIM // 2, axis=1) * sm
            if g in (5, 6):
                o_ref[pl.ds(hh, tm, stride=N_HEADS), :] = zs
                (kb_ref if g == 5 else vb_ref)[:, cols] = zs.astype(BF16)
                if g == 5 and km_refs:
                    for r in range(tm // MOBA_BLOCK):
                        blk = zs[r * MOBA_BLOCK:(r + 1) * MOBA_BLOCK, :]
                        km_refs[0][r:r + 1, cols] = jnp.sum(blk, axis=0, keepdims=True) * (1.0 / MOBA_BLOCK)
            else:
                o_ref[:, cols] = zs.astype(o_ref.dtype)


def in_proj(x2d, shift, scale, w_bf, tabs, rows_per_group, tm, block_means):
    n, d = x2d.shape
    p_tiles = tabs[0].shape[0] // tm
    mod_spec = _mod_spec(shift, tm, rows_per_group)
    tab_spec = pl.BlockSpec((tm, HEAD_DIM), lambda i: (i % p_tiles, 0))
    wide = pl.BlockSpec((tm, GROUP_W), lambda i: (i, 0))
    tall = pl.BlockSpec((tm * N_HEADS, HEAD_DIM), lambda i: (i, 0))
    wide_shape = lambda dt: jax.ShapeDtypeStruct((n, GROUP_W), dt)
    tall_shape = jax.ShapeDtypeStruct((n * N_HEADS, HEAD_DIM), F32)
    out_specs = [wide] * 5 + [tall, tall, wide, wide]
    out_shape = [wide_shape(F32), wide_shape(F32), wide_shape(BF16), wide_shape(F32), wide_shape(BF16),
                 tall_shape, tall_shape, wide_shape(BF16), wide_shape(BF16)]
    if block_means:
        assert tm % MOBA_BLOCK == 0
        per = tm // MOBA_BLOCK
        out_specs.append(pl.BlockSpec((None, per, GROUP_W), lambda i: (i, 0, 0)))
        out_shape.append(jax.ShapeDtypeStruct((n // tm, per, GROUP_W), F32))
    return pl.pallas_call(
        _in_proj_kernel,
        grid=(n // tm,),
        in_specs=[pl.BlockSpec((tm, d), lambda i: (i, 0)), mod_spec, mod_spec,
                  _resident(w_bf), tab_spec, tab_spec, tab_spec, tab_spec],
        out_specs=out_specs,
        out_shape=out_shape,
        compiler_params=_params("parallel"),
        name="in_proj",
    )(x2d, shift, scale, w_bf, *tabs)


def rotary_tables(pos):
    half = HEAD_DIM // 2
    posf = pos.astype(F32)[:, None]
    inv_r = 1.0 / (ROPE_THETA ** jnp.linspace(0.0, 1.0, half, dtype=F32))
    ang_r = posf * inv_r[None, :]
    cr = jnp.repeat(jnp.cos(ang_r), 2, axis=1)
    sr = jnp.stack([-jnp.sin(ang_r), jnp.sin(ang_r)], -1).reshape(-1, HEAD_DIM)
    inv_m = ROPE_THETA ** (-jnp.arange(0, HEAD_DIM, 2, dtype=F32) / HEAD_DIM)
    ang_m = posf * inv_m[None, :]
    cm = jnp.concatenate([jnp.cos(ang_m), jnp.cos(ang_m)], -1)
    sm = jnp.concatenate([-jnp.sin(ang_m), jnp.sin(ang_m)], -1)
    return cr, sr, cm, sm


def retention_tables(chunk, rows):
    log_g = jnp.log1p(-jnp.exp2(-5.0 - jnp.arange(N_HEADS, dtype=F32)))
    idx = jnp.arange(chunk, dtype=F32)
    diff = idx[:, None] - idx[None, :]
    decay_in = jnp.where(diff[None] >= 0,
                         jnp.exp(jnp.maximum(diff, 0.0)[None] * log_g[:, None, None]), 0.0)
    q_dec = jnp.exp((idx + 1.0)[None, :] * log_g[:, None])
    k_dec = jnp.exp((chunk - 1.0 - idx)[None, :] * log_g[:, None])
    c_dec = jnp.exp(chunk * log_g)
    pad = rows - chunk
    decay_in = jnp.pad(decay_in, ((0, 0), (0, pad), (0, rows - chunk)))
    bc = lambda t: jnp.broadcast_to(jnp.pad(t, ((0, 0), (0, pad)))[:, :, None], (N_HEADS, rows, HEAD_DIM))
    c_b = jnp.broadcast_to(c_dec[:, None, None], (N_HEADS, 1, HEAD_DIM))
    return decay_in, bc(q_dec), bc(k_dec), c_b


def _ret_chunk(qc, kc, vc, s, din, qd, kd, cd):
    att = lax.dot_general(qc.astype(BF16), kc.astype(BF16), _NT,
                          preferred_element_type=F32) * din
    o = _dot(att.astype(BF16), vc) + _dot((qc * qd).astype(BF16), s.astype(BF16))
    s_new = cd * s + lax.dot_general((kc * kd).astype(BF16), vc, _TN,
                                     preferred_element_type=F32)
    return o, s_new


def _group_norm_gate(o, g):
    o = o * lax.rsqrt(jnp.mean(o * o, axis=-1, keepdims=True) + GN_EPS)
    return o * _silu(g)


def _ret_prompt_kernel(q_ref, k_ref, v_ref, g_ref, din_ref, qd_ref, kd_ref, cd_ref,
                       o_ref, sout_ref, s_ref, *, n_sub):
    j = pl.program_id(1)

    @pl.when(j == 0)
    def _():
        s_ref[...] = jnp.zeros_like(s_ref)

    for hh in range(N_HEADS):
        cols = slice(hh * HEAD_DIM, (hh + 1) * HEAD_DIM)
        din, qd, kd, cd = din_ref[hh], qd_ref[hh], kd_ref[hh], cd_ref[hh]
        for c in range(n_sub):
            rows = slice(c * RET_CHUNK, (c + 1) * RET_CHUNK)
            o, s_new = _ret_chunk(q_ref[rows, cols], k_ref[rows, cols], v_ref[rows, cols],
                                  s_ref[hh], din, qd, kd, cd)
            s_ref[hh] = s_new
            o_ref[rows, cols] = _group_norm_gate(o, g_ref[rows, cols]).astype(o_ref.dtype)

    @pl.when(j == pl.num_programs(1) - 1)
    def _():
        sout_ref[...] = s_ref[...]


def retention_prompt(rq, rk, rv, rg, batch, seq):
    tc = min(4 * RET_CHUNK, seq)
    nj = seq // tc
    tabs = retention_tables(RET_CHUNK, RET_CHUNK)
    tok = pl.BlockSpec((tc, GROUP_W), lambda b, j: (b * nj + j, 0))
    tab = pl.BlockSpec((N_HEADS, RET_CHUNK, HEAD_DIM), lambda b, j: (0, 0, 0))
    return pl.pallas_call(
        functools.partial(_ret_prompt_kernel, n_sub=tc // RET_CHUNK),
        grid=(batch, nj),
        in_specs=[tok, tok, tok, tok, tab, tab, tab,
                  pl.BlockSpec((N_HEADS, 1, HEAD_DIM), lambda b, j: (0, 0, 0))],
        out_specs=[tok, pl.BlockSpec((None, N_HEADS, HEAD_DIM, HEAD_DIM), lambda b, j: (b, 0, 0, 0))],
        out_shape=[jax.ShapeDtypeStruct((batch * seq, GROUP_W), BF16),
                   jax.ShapeDtypeStruct((batch, N_HEADS, HEAD_DIM, HEAD_DIM), F32)],
        scratch_shapes=[pltpu.VMEM((N_HEADS, HEAD_DIM, HEAD_DIM), F32)],
        compiler_params=_params("parallel", "arbitrary"),
        name="retention_prompt",
    )(rq, rk, rv, rg, *tabs)


def _ret_sample_kernel(q_ref, k_ref, v_ref, g_ref, s0_ref, din_ref, qd_ref, kd_ref, cd_ref,
                       o_ref, sout_ref, *, bb, t):
    for hh in range(N_HEADS):
        cols = slice(hh * HEAD_DIM, (hh + 1) * HEAD_DIM)
        din, qd, kd, cd = din_ref[hh], qd_ref[hh], kd_ref[hh], cd_ref[hh]

        def body(b, carry):
            qc = _rows_to_tile(q_ref[b, :, cols], RET_CHUNK)
            kc = _rows_to_tile(k_ref[b, :, cols], RET_CHUNK)
            vc = _rows_to_tile(v_ref[b, :, cols].astype(F32), RET_CHUNK).astype(BF16)
            o, s_new = _ret_chunk(qc, kc, vc, s0_ref[b, hh], din, qd, kd, cd)
            sout_ref[b, hh] = s_new
            o_ref[b, :, cols] = _group_norm_gate(o[0:t, :], g_ref[b, :, cols]).astype(o_ref.dtype)
            return carry

        lax.fori_loop(0, bb, body, 0)


def retention_sample(rq, rk, rv, rg, states, layer, t):
    batch = states.shape[1]
    bb = 8
    tabs = retention_tables(t, RET_CHUNK)
    tok = pl.BlockSpec((bb, t, GROUP_W), lambda i: (i, 0, 0))
    st_in = pl.BlockSpec((None, bb, N_HEADS, HEAD_DIM, HEAD_DIM), lambda i: (layer, i, 0, 0, 0))
    st = pl.BlockSpec((bb, N_HEADS, HEAD_DIM, HEAD_DIM), lambda i: (i, 0, 0, 0))
    tab = pl.BlockSpec((N_HEADS, RET_CHUNK, HEAD_DIM), lambda i: (0, 0, 0))
    return pl.pallas_call(
        functools.partial(_ret_sample_kernel, bb=bb, t=t),
        grid=(batch // bb,),
        in_specs=[tok, tok, tok, tok, st_in, tab, tab, tab,
                  pl.BlockSpec((N_HEADS, 1, HEAD_DIM), lambda i: (0, 0, 0))],
        out_specs=[tok, st],
        out_shape=[jax.ShapeDtypeStruct((batch, t, GROUP_W), BF16),
                   jax.ShapeDtypeStruct(states.shape[1:], F32)],
        compiler_params=_params("parallel"),
        name="retention_sample",
    )(rq, rk, rv, rg, states, *tabs)


def _beaten_counts(rows, n_valid):
    counts = []
    for n in range(n_valid):
        cnt = jnp.zeros(rows[n].shape, jnp.int32)
        for m in range(n_valid):
            if m == n:
                continue
            beats = (rows[m] >= rows[n]) if m < n else (rows[m] > rows[n])
            cnt = cnt + beats.astype(jnp.int32)
        counts.append(cnt)
    return counts


def _moba_prompt_kernel(q_ref, k_ref, v_ref, km_ref, o_ref, s_ref, *, nblk):
    blk = MOBA_BLOCK
    scale = HEAD_DIM ** -0.5
    kb = k_ref[...]
    vt = v_ref[...].astype(F32).T.astype(BF16)
    kmean = km_ref[...].astype(BF16)
    key_i = lax.broadcasted_iota(jnp.int32, (blk, blk), 0)
    qry_i = lax.broadcasted_iota(jnp.int32, (blk, blk), 1)
    causal = key_i <= qry_i

    for qi in range(nblk):
        qb = q_ref[qi * blk:(qi + 1) * blk, :]
        cnt = None
        if qi > MOBA_TOPK:
            gt = lax.dot_general(kmean, qb, _NT, preferred_element_type=F32)
            cnt = _beaten_counts([gt[m:m + 1, :] for m in range(qi)], qi)
        mx = jnp.full((1, blk), NEG, F32)
        for n in range(qi + 1):
            s = lax.dot_general(kb[n * blk:(n + 1) * blk], qb, _NT,
                                preferred_element_type=F32) * scale
            if n == qi:
                s = jnp.where(causal, s, NEG)
            elif cnt is not None:
                s = jnp.where(jnp.broadcast_to(cnt[n], s.shape) < MOBA_TOPK, s, NEG)
            s_ref[n * blk:(n + 1) * blk, :] = s
            mx = jnp.maximum(mx, jnp.max(s, axis=0, keepdims=True))
        l = jnp.zeros((1, blk), F32)
        ot = jnp.zeros((HEAD_DIM, blk), F32)
        for n in range(qi + 1):
            p = jnp.exp(s_ref[n * blk:(n + 1) * blk, :] - mx)
            l = l + jnp.sum(p, axis=0, keepdims=True)
            ot = ot + _dot(vt[:, n * blk:(n + 1) * blk], p.astype(BF16))
        ot = ot * (1.0 / l)
        o_ref[qi * blk:(qi + 1) * blk, :] = ot.T.astype(o_ref.dtype)


def moba_prompt(mq, mk, mv, kmean, batch, seq):
    nblk = seq // MOBA_BLOCK
    assert nblk % SUBLANES == 0
    spec = pl.BlockSpec((seq, HEAD_DIM), lambda b, h: (b, h))
    return pl.pallas_call(
        functools.partial(_moba_prompt_kernel, nblk=nblk),
        grid=(batch, N_HEADS),
        in_specs=[spec, spec, spec, pl.BlockSpec((nblk, HEAD_DIM), lambda b, h: (b, h))],
        out_specs=spec,
        out_shape=jax.ShapeDtypeStruct((batch * seq, GROUP_W), BF16),
        scratch_shapes=[pltpu.VMEM((seq, MOBA_BLOCK), F32)],
        compiler_params=_params("parallel", "parallel"),
        name="moba_prompt",
    )(mq, mk, mv, kmean)


def _page_rows(ref, page):
    return jnp.concatenate([ref[pl.ds(hh, page, stride=N_HEADS), :] for hh in range(N_HEADS)], axis=1)


def _moba_sample_kernel(pt_ref, q_ref, kn_ref, vn_ref, *rest, n_pages, page, t):
    k_refs = rest[:n_pages]
    v_refs = rest[n_pages:2 * n_pages]
    o_ref = rest[2 * n_pages]
    scale = HEAD_DIM ** -0.5
    ppb = MOBA_BLOCK // page
    nblk = n_pages // ppb
    n_pairs = N_HEADS * t

    q = q_ref[...].astype(F32)
    r_i = lax.broadcasted_iota(jnp.int32, (n_pairs, GROUP_W), 0)
    l_i = lax.broadcasted_iota(jnp.int32, (n_pairs, GROUP_W), 1)
    qm = jnp.zeros((n_pairs, GROUP_W), F32)
    for ti in range(t):
        for hh in range(N_HEADS):
            hit = (r_i == hh * t + ti) & (l_i >= hh * HEAD_DIM) & (l_i < (hh + 1) * HEAD_DIM)
            qm = jnp.where(hit, q[ti:ti + 1, :], qm)
    qm = qm.astype(BF16)

    zpad = jnp.zeros((page - SUBLANES, GROUP_W), F32)
    k_own = jnp.concatenate([_rows_to_tile(kn_ref[...].astype(F32), SUBLANES), zpad], axis=0).astype(BF16)
    v_own = jnp.concatenate([_rows_to_tile(vn_ref[...].astype(F32), SUBLANES), zpad], axis=0).astype(BF16)

    b_i = lax.broadcasted_iota(jnp.int32, (page, GROUP_W), 0)
    kmean = jnp.zeros((page, GROUP_W), F32)
    scores = []
    for p in range(n_pages):
        kp = _page_rows(k_refs[p], page)
        scores.append(lax.dot_general(qm, kp.astype(BF16), _NT,
                                      preferred_element_type=F32) * scale)
        ks = jnp.sum(kp, axis=0, keepdims=True) * (1.0 / MOBA_BLOCK)
        kmean = kmean + jnp.where(b_i == p // ppb, ks, 0.0)
    gate = lax.dot_general(qm, kmean.astype(BF16), _NT, preferred_element_type=F32)
    cnt = _beaten_counts([gate[:, m:m + 1] for m in range(nblk)], nblk)
    keep = [jnp.broadcast_to(c, (n_pairs, page)) < MOBA_TOPK for c in cnt]

    s_own = lax.dot_general(qm, k_own, _NT, preferred_element_type=F32) * scale
    key_i = lax.broadcasted_iota(jnp.int32, (n_pairs, page), 1)
    tok_i = lax.rem(lax.broadcasted_iota(jnp.int32, (n_pairs, page), 0), t)
    s_own = jnp.where((key_i <= tok_i) & (key_i < t), s_own, NEG)

    mx = s_own
    for p in range(n_pages):
        scores[p] = jnp.where(keep[p // ppb], scores[p], NEG)
        mx = jnp.maximum(mx, scores[p])
    mx = jnp.max(mx, axis=1, keepdims=True)
    e_own = jnp.exp(s_own - mx)
    l = e_own
    acc = _dot(e_own.astype(BF16), v_own)
    for p in range(n_pages):
        e = jnp.exp(scores[p] - mx)
        l = l + e
        acc = acc + _dot(e.astype(BF16), _page_rows(v_refs[p], page).astype(BF16))
    acc = acc * (1.0 / jnp.sum(l, axis=1, keepdims=True))
    for hh in range(N_HEADS):
        cols = slice(hh * HEAD_DIM, (hh + 1) * HEAD_DIM)
        o_ref[:, cols] = acc[hh * t:(hh + 1) * t, cols].astype(o_ref.dtype)


def moba_sample(mq, mk, mv, cache_k, cache_v, page_table, page_base, page, t):
    batch, n_pages = page_table.shape
    assert (n_pages * page) % MOBA_BLOCK == 0 and t <= SUBLANES
    assert n_pages * page // MOBA_BLOCK <= page
    tok = pl.BlockSpec((None, t, GROUP_W), lambda b, pt: (b, 0, 0))

    def page_spec(p):
        return pl.BlockSpec((page * N_HEADS, HEAD_DIM), lambda b, pt: (page_base + pt[b, p], 0))

    specs = [tok, tok, tok] + [page_spec(p) for p in range(n_pages)] * 2
    grid_spec = pltpu.PrefetchScalarGridSpec(
        num_scalar_prefetch=1,
        grid=(batch,),
        in_specs=specs,
        out_specs=tok,
    )
    return pl.pallas_call(
        functools.partial(_moba_sample_kernel, n_pages=n_pages, page=page, t=t),
        grid_spec=grid_spec,
        out_shape=jax.ShapeDtypeStruct((batch, t, GROUP_W), BF16),
        compiler_params=_params("parallel"),
        name="moba_sample",
    )(page_table, mq, mk, mv, *([cache_k] * n_pages), *([cache_v] * n_pages))


def _out_proj_kernel(or_ref, om_ref, w_ref, x_ref, gate_ref, g_ref, b_ref, o_ref):
    y = _dot(or_ref[...], w_ref[:GROUP_W, :]) + _dot(om_ref[...], w_ref[GROUP_W:, :])
    v = ALPHA * x_ref[...] + (1.0 + gate_ref[...]) * y
    o_ref[...] = _layer_norm(v, g_ref[...], b_ref[...])


def out_proj(o_r, o_m, w_bf, x2d, gate, ln_g, ln_b, rows_per_group, tm):
    n, d = x2d.shape
    vec = pl.BlockSpec((1, d), lambda i: (0, 0))
    return pl.pallas_call(
        _out_proj_kernel,
        grid=(n // tm,),
        in_specs=[pl.BlockSpec((tm, GROUP_W), lambda i: (i, 0)),
                  pl.BlockSpec((tm, GROUP_W), lambda i: (i, 0)),
                  _resident(w_bf),
                  pl.BlockSpec((tm, d), lambda i: (i, 0)),
                  _mod_spec(gate, tm, rows_per_group), vec, vec],
        out_specs=pl.BlockSpec((tm, d), lambda i: (i, 0)),
        out_shape=jax.ShapeDtypeStruct((n, d), F32),
        compiler_params=_params("parallel"),
        name="out_proj",
    )(o_r, o_m, w_bf, x2d, gate, ln_g.reshape(1, d), ln_b.reshape(1, d))


def _ffn_kernel(*refs, d_ff, cw, period, carried):
    if carried:
        (x_ref, sh_ref, sc_ref, gate_ref, wup_ref, wdw_ref, bdw_ref, wdn_ref, g_ref, b_ref,
         o_ref, prev_ref, h_s, act_s, carry_s) = refs
    else:
        (x_ref, sh_ref, sc_ref, gate_ref, wup_ref, wdw_ref, bdw_ref, wdn_ref, g_ref, b_ref,
         e1_ref, e2_ref, o_ref, prev_ref, h_s, act_s) = refs
    tm = x_ref.shape[0]
    x = x_ref[...]
    h_s[...] = (x * (1.0 + sc_ref[...]) + sh_ref[...]).astype(BF16)
    row = lax.broadcasted_iota(jnp.int32, (tm, cw), 0)
    t_in = row if carried else row & (period - 1)

    if carried:
        @pl.when(pl.program_id(0) % (period // tm) == 0)
        def _():
            carry_s[...] = jnp.zeros_like(carry_s)

    for c in range(d_ff // cw):
        cu = slice(c * cw, (c + 1) * cw)
        cv = slice(d_ff + c * cw, d_ff + (c + 1) * cw)
        u = _dot(h_s[...], wup_ref[:, cu])
        v = _dot(h_s[...], wup_ref[:, cv])
        if carried:
            e1 = carry_s[1:2, cu]
            e2 = jnp.where(row == 0, carry_s[0:1, cu], carry_s[1:2, cu])
            carry_s[:, cu] = u[tm - 2:tm, :]
            prev_ref[:, cu] = u[tm - 2:tm, :]
        else:
            e1 = e1_ref[:, cu]
            e2 = e2_ref[:, cu]
            prev_ref[:, cu] = u
        s1 = jnp.where(t_in >= 1, pltpu.roll(u, 1, axis=0), e1)
        s2 = jnp.where(t_in >= 2, pltpu.roll(u, 2, axis=0), e2)
        uc = wdw_ref[0:1, cu] * s2 + wdw_ref[1:2, cu] * s1 + wdw_ref[2:3, cu] * u + bdw_ref[:, cu]
        act = 0.5 * uc * (1.0 + lax.erf(uc * (2.0 ** -0.5))) * v
        act_s[:, cu] = act.astype(BF16)
    fy = _dot(act_s[...], wdn_ref[...])
    vsum = ALPHA * x + (1.0 + gate_ref[...]) * fy
    o_ref[...] = _layer_norm(vsum, g_ref[...], b_ref[...])


def conv_ffn(x2d, shift, scale, gate, w_up_bf, w_dw, b_dw, w_dn_bf, layer, ln_g, ln_b,
             rows_per_group, tm, period, prev=None):
    n, d = x2d.shape
    d_ff = w_dn_bf.shape[1]
    cw = 256
    carried = prev is None
    mod = _mod_spec(shift, tm, rows_per_group)
    vec = pl.BlockSpec((1, d), lambda i: (0, 0))
    b_dw2 = b_dw.reshape(1, d_ff)
    in_specs = [pl.BlockSpec((tm, d), lambda i: (i, 0)), mod, mod, mod,
                _resident(w_up_bf, layer), _resident(w_dw), _resident(b_dw2),
                _resident(w_dn_bf, layer), vec, vec]
    args = [x2d, shift, scale, gate, w_up_bf, w_dw, b_dw2, w_dn_bf,
            ln_g.reshape(1, d), ln_b.reshape(1, d)]
    scratch = [pltpu.VMEM((tm, d), BF16), pltpu.VMEM((tm, d_ff), BF16)]
    if carried:
        assert period % tm == 0
        prev_spec = pl.BlockSpec((None, 2, d_ff), lambda i: (i, 0, 0))
        prev_shape = jax.ShapeDtypeStruct((n // tm, 2, d_ff), F32)
        scratch.append(pltpu.VMEM((2, d_ff), F32))
    else:
        assert tm % period == 0 and period & (period - 1) == 0
        in_specs += [pl.BlockSpec((tm, d_ff), lambda i: (i, 0))] * 2
        args += list(prev)
        prev_spec = pl.BlockSpec((tm, d_ff), lambda i: (i, 0))
        prev_shape = jax.ShapeDtypeStruct((n, d_ff), F32)
    return pl.pallas_call(
        functools.partial(_ffn_kernel, d_ff=d_ff, cw=cw, period=period, carried=carried),
        grid=(n // tm,),
        in_specs=in_specs,
        out_specs=[pl.BlockSpec((tm, d), lambda i: (i, 0)), prev_spec],
        out_shape=[jax.ShapeDtypeStruct((n, d), F32), prev_shape],
        scratch_shapes=scratch,
        compiler_params=_params("arbitrary"),
        name="conv_ffn",
    )(*args)


def _glu_kernel(x_ref, sh_ref, sc_ref, w_ref, b_ref, o_ref, *, cw):
    h = (x_ref[...] * (1.0 + sc_ref[...]) + sh_ref[...]).astype(BF16)
    dc = o_ref.shape[1]
    for c in range(dc // cw):
        ca = slice(c * cw, (c + 1) * cw)
        cg = slice(dc + c * cw, dc + (c + 1) * cw)
        a = _dot(h, w_ref[:, ca]) + b_ref[:, ca]
        g = _dot(h, w_ref[:, cg]) + b_ref[:, cg]
        o_ref[:, ca] = a * jax.nn.sigmoid(g)


def conformer_glu(x2d, shift, scale, w1_bf, b1, rows_per_group, tm):
    n, d = x2d.shape
    dc = w1_bf.shape[1] // 2
    mod = _mod_spec(shift, tm, rows_per_group)
    b1r = b1.reshape(1, 2 * dc)
    return pl.pallas_call(
        functools.partial(_glu_kernel, cw=256),
        grid=(n // tm,),
        in_specs=[pl.BlockSpec((tm, d), lambda i: (i, 0)), mod, mod,
                  _resident(w1_bf), _resident(b1r)],
        out_specs=pl.BlockSpec((tm, dc), lambda i: (i, 0)),
        out_shape=jax.ShapeDtypeStruct((n, dc), F32),
        compiler_params=_params("parallel"),
        name="conformer_glu",
    )(x2d, shift, scale, w1_bf, b1r)


PAD_ROWS = 32


def _dwconv_rows(win_ref, w_ref, bias, r0, rs, cols, width):
    off = PAD_ROWS - (width - 1)
    y = bias
    for r in range(SUBLANES):
        z = None
        for m in range((off + width - 1) // SUBLANES + 1):
            k = SUBLANES * m + r - off
            if 0 <= k < width:
                lo = r0 + SUBLANES * m
                term = w_ref[k:k + 1, cols] * win_ref[lo:lo + rs + SUBLANES, cols]
                z = term if z is None else z + term
        if z is not None:
            y = y + z[r:r + rs, :]
    return y


def _conformer_prompt_kernel(x_ref, sh_ref, sc_ref, gate_ref, w1_ref, b1_ref, wdw_ref, bdw_ref,
                             gl_ref, bl_ref, w2_ref, b2_ref, g_ref, b_ref,
                             o_ref, tail_ref, win_ref, y_ref, *, width, rs, cw):
    tm, dc = y_ref.shape

    @pl.when(pl.program_id(1) == 0)
    def _():
        win_ref[0:PAD_ROWS, :] = jnp.zeros((PAD_ROWS, dc), F32)
        win_ref[PAD_ROWS + tm:, :] = jnp.zeros((SUBLANES, dc), F32)

    x = x_ref[...]
    h = (x * (1.0 + sc_ref[...]) + sh_ref[...]).astype(BF16)
    for c in range(dc // cw):
        ca = slice(c * cw, (c + 1) * cw)
        cg = slice(dc + c * cw, dc + (c + 1) * cw)
        a = _dot(h, w1_ref[:, ca]) + b1_ref[:, ca]
        g = _dot(h, w1_ref[:, cg]) + b1_ref[:, cg]
        win_ref[PAD_ROWS:PAD_ROWS + tm, ca] = a * jax.nn.sigmoid(g)
    tail_ref[...] = win_ref[tm:tm + PAD_ROWS, :]
    for c in range(dc // HEAD_DIM):
        cols = slice(c * HEAD_DIM, (c + 1) * HEAD_DIM)
        bias = bdw_ref[:, cols]
        for r0 in range(0, tm, rs):
            y_ref[r0:r0 + rs, cols] = _dwconv_rows(win_ref, wdw_ref, bias, r0, rs, cols, width)
    win_ref[0:PAD_ROWS, :] = win_ref[tm:tm + PAD_ROWS, :]
    yn = _silu(_layer_norm(y_ref[...], gl_ref[...], bl_ref[...]))
    y = _dot(yn.astype(BF16), w2_ref[...]) + b2_ref[...]
    v = ALPHA * x + (1.0 + gate_ref[...]) * y
    o_ref[...] = _layer_norm(v, g_ref[...], b_ref[...])


def conformer_prompt(x2d, shift, scale, gate, w1_bf, b1, w_dw, b_dw, cf_g, cf_b, w2_bf, b2,
                     ln_g, ln_b, batch, seq, tm):
    n, d = x2d.shape
    dc = w2_bf.shape[0]
    width = w_dw.shape[0]
    assert width - 1 <= PAD_ROWS and tm % PAD_ROWS == 0
    nj = seq // tm
    row = lambda a: a.reshape(1, -1)
    tile = lambda w: pl.BlockSpec((tm, w), lambda b, j: (b * nj + j, 0))
    mod = pl.BlockSpec((None, 1, d), lambda b, j: (b, 0, 0))
    consts = [w1_bf, row(b1), w_dw, row(b_dw), row(cf_g), row(cf_b), w2_bf, row(b2), row(ln_g), row(ln_b)]
    return pl.pallas_call(
        functools.partial(_conformer_prompt_kernel, width=width, rs=64, cw=256),
        grid=(batch, nj),
        in_specs=[tile(d), mod, mod, mod] + [_resident(a) for a in consts],
        out_specs=[tile(d), pl.BlockSpec((None, PAD_ROWS, dc), lambda b, j: (b, 0, 0))],
        out_shape=[jax.ShapeDtypeStruct((n, d), F32),
                   jax.ShapeDtypeStruct((batch, PAD_ROWS, dc), F32)],
        scratch_shapes=[pltpu.VMEM((PAD_ROWS + tm + SUBLANES, dc), F32), pltpu.VMEM((tm, dc), F32)],
        compiler_params=_params("parallel", "arbitrary"),
        name="conformer_prompt",
    )(x2d, shift, scale, gate, *consts)


def _dwconv_sample_kernel(x_ref, st_ref, w_ref, b_ref, o_ref, nst_ref, win_ref, *, width, bb, t):
    hist = width - 1

    def body(b, carry):
        win_ref[0:hist, :] = st_ref[b]
        win_ref[hist:hist + t, :] = x_ref[b]
        acc = jnp.zeros((t, x_ref.shape[2]), F32) + b_ref[...]
        for k in range(width):
            acc = acc + w_ref[k:k + 1, :] * win_ref[k:k + t, :]
        o_ref[b] = acc
        nst_ref[b] = win_ref[t:t + hist, :]
        return carry

    lax.fori_loop(0, bb, body, 0)


def dwconv_sample(glu, states, layer, w_dw, b_dw):
    batch, t, dc = glu.shape
    width = w_dw.shape[0]
    bb = 8
    b2 = b_dw.reshape(1, dc)
    win_rows = -(-(width - 1 + t) // SUBLANES) * SUBLANES
    return pl.pallas_call(
        functools.partial(_dwconv_sample_kernel, width=width, bb=bb, t=t),
        grid=(batch // bb,),
        in_specs=[pl.BlockSpec((bb, t, dc), lambda i: (i, 0, 0)),
                  pl.BlockSpec((None, bb, width - 1, dc), lambda i: (layer, i, 0, 0)),
                  pl.BlockSpec(w_dw.shape, lambda i: (0, 0)),
                  pl.BlockSpec(b2.shape, lambda i: (0, 0))],
        out_specs=[pl.BlockSpec((bb, t, dc), lambda i: (i, 0, 0)),
                   pl.BlockSpec((bb, width - 1, dc), lambda i: (i, 0, 0))],
        out_shape=[jax.ShapeDtypeStruct((batch, t, dc), F32),
                   jax.ShapeDtypeStruct(states.shape[1:], F32)],
        scratch_shapes=[pltpu.VMEM((win_rows, dc), F32)],
        compiler_params=_params("parallel"),
        name="dwconv_sample",
    )(glu, states, w_dw, b2)


def _conf_tail_kernel(y_ref, gl_ref, bl_ref, w_ref, b2_ref, x_ref, gate_ref, g_ref, b_ref, o_ref):
    yn = _silu(_layer_norm(y_ref[...], gl_ref[...], bl_ref[...]))
    y = _dot(yn.astype(BF16), w_ref[...]) + b2_ref[...]
    v = ALPHA * x_ref[...] + (1.0 + gate_ref[...]) * y
    o_ref[...] = _layer_norm(v, g_ref[...], b_ref[...])


def conformer_tail(y, cf_g, cf_b, w2_bf, b2, x2d, gate, ln_g, ln_b, rows_per_group, tm):
    n, d = x2d.shape
    dc = y.shape[1]
    vecc = pl.BlockSpec((1, dc), lambda i: (0, 0))
    vec = pl.BlockSpec((1, d), lambda i: (0, 0))
    return pl.pallas_call(
        _conf_tail_kernel,
        grid=(n // tm,),
        in_specs=[pl.BlockSpec((tm, dc), lambda i: (i, 0)), vecc, vecc,
                  _resident(w2_bf), vec,
                  pl.BlockSpec((tm, d), lambda i: (i, 0)),
                  _mod_spec(gate, tm, rows_per_group), vec, vec],
        out_specs=pl.BlockSpec((tm, d), lambda i: (i, 0)),
        out_shape=jax.ShapeDtypeStruct((n, d), F32),
        compiler_params=_params("parallel"),
        name="conformer_tail",
    )(y, cf_g.reshape(1, dc), cf_b.reshape(1, dc), w2_bf, b2.reshape(1, d), x2d, gate,
      ln_g.reshape(1, d), ln_b.reshape(1, d))


def kernel(x_prompt, x_sample, cache_k, cache_v, state_ret, state_conv, state_ffn, page_table, c_prompt, c_sample, ab_w_in, ab_w_out, cf_w_pw1, cf_b_pw1, cf_w_dw, cf_b_dw, cf_ln_g, cf_ln_b, cf_w_pw2, cf_b_pw2, ffn_w_up, ffn_w_dw, ffn_b_dw, ffn_w_down, ada_w, ada_b, ln_g, ln_b):
    bp, seq, d = x_prompt.shape
    bs, ts, _ = x_sample.shape
    n_layers = ada_w.shape[0]
    d_ff = ffn_w_down.shape[1]
    past_len = page_table.shape[1] * cache_k.shape[2]
    np_, ns = bp * seq, bs * ts
    tm_p = min(256, seq)
    tm_s = min(128, ns)

    ada = adaln_all(jnp.concatenate([c_prompt, c_sample], 0), ada_w, ada_b)

    def mods(l, k):
        mp = ada[l, k, :bp][:, None, :]
        ms = jnp.repeat(ada[l, k, bp:], ts, axis=0)[None]
        return mp, ms

    xp = x_prompt.reshape(np_, d)
    xs = x_sample.reshape(ns, d)
    tabs_p = rotary_tables(jnp.arange(seq, dtype=jnp.int32))
    tabs_s = tuple(jnp.tile(tb, (tm_s // ts, 1)) for tb in
                   rotary_tables(past_len + jnp.arange(ts, dtype=jnp.int32)))

    w_up = ffn_w_up.astype(BF16)
    w_dn = ffn_w_down.astype(BF16)
    outs = {k: [] for k in ("kp", "vp", "ks", "vs", "rp", "rs", "cp", "cs", "fp", "fs")}
    for l in range(n_layers):
        i = l // 2
        sh_p, sh_s = mods(l, 0)
        sc_p, sc_s = mods(l, 1)
        gt_p, gt_s = mods(l, 2)
        if l % 2 == 0:
            w_in = ab_w_in[i].astype(BF16)
            w_out = ab_w_out[i].astype(BF16)
            rq, rk, rv, rg, mq, mk, mv, kb, vb, kmean = in_proj(
                xp, sh_p, sc_p, w_in, tabs_p, seq, tm_p, True)
            o_r, s_p = retention_prompt(rq, rk, rv, rg, bp, seq)
            o_m = moba_prompt(mq, kb, vb, kmean.reshape(-1, GROUP_W), bp, seq)
            xp = out_proj(o_r, o_m, w_out, xp, gt_p, ln_g[l, 0], ln_b[l, 0], seq, tm_p)
            outs["kp"].append(mk.reshape(bp, seq, N_HEADS, HEAD_DIM))
            outs["vp"].append(mv.reshape(bp, seq, N_HEADS, HEAD_DIM))
            outs["rp"].append(s_p)

            rq, rk, rv, rg, mq, mk, mv, kb, vb = in_proj(xs, sh_s, sc_s, w_in, tabs_s, ns, tm_s, False)
            r3 = lambda a: a.reshape(bs, ts, GROUP_W)
            o_r, s_s = retention_sample(r3(rq), r3(rk), r3(rv), r3(rg), state_ret, i, ts)
            n_phys, page = cache_k.shape[1], cache_k.shape[2]
            ck = cache_k.reshape(-1, HEAD_DIM)
            cv = cache_v.reshape(-1, HEAD_DIM)
            o_m = moba_sample(r3(mq), r3(kb), r3(vb), ck, cv, page_table, i * n_phys, page, ts)
            xs = out_proj(o_r.reshape(ns, GROUP_W), o_m.reshape(ns, GROUP_W), w_out, xs, gt_s,
                          ln_g[l, 0], ln_b[l, 0], ns, tm_s)
            outs["ks"].append(mk.reshape(bs, ts, N_HEADS, HEAD_DIM))
            outs["vs"].append(mv.reshape(bs, ts, N_HEADS, HEAD_DIM))
            outs["rs"].append(s_s)
        else:
            w1 = cf_w_pw1[i].astype(BF16)
            w2 = cf_w_pw2[i].astype(BF16)
            hist = cf_w_dw.shape[1] - 1
            xp, tail = conformer_prompt(xp, sh_p, sc_p, gt_p, w1, cf_b_pw1[i], cf_w_dw[i], cf_b_dw[i],
                                        cf_ln_g[i], cf_ln_b[i], w2, cf_b_pw2[i],
                                        ln_g[l, 0], ln_b[l, 0], bp, seq, tm_p)
            outs["cp"].append(tail[:, PAD_ROWS - hist:])

            glu = conformer_glu(xs, sh_s, sc_s, w1, cf_b_pw1[i], ns, tm_s)
            y, nst = dwconv_sample(glu.reshape(bs, ts, -1), state_conv, i, cf_w_dw[i], cf_b_dw[i])
            xs = conformer_tail(y.reshape(ns, -1), cf_ln_g[i], cf_ln_b[i], w2, cf_b_pw2[i], xs, gt_s,
                                ln_g[l, 0], ln_b[l, 0], ns, tm_s)
            outs["cs"].append(nst)

        sh_p, sh_s = mods(l, 3)
        sc_p, sc_s = mods(l, 4)
        gt_p, gt_s = mods(l, 5)
        xp, prev_p = conv_ffn(xp, sh_p, sc_p, gt_p, w_up, ffn_w_dw[l], ffn_b_dw[l], w_dn, l,
                              ln_g[l, 1], ln_b[l, 1], seq, tm_p, seq)
        outs["fp"].append(prev_p.reshape(bp, seq // tm_p, 2, d_ff)[:, -1])
        st = state_ffn[l]
        zero = jnp.zeros((bs, ts - 2, d_ff), F32)
        e1 = jnp.concatenate([st[:, 1:2], zero, zero[:, :1]], 1).reshape(ns, d_ff)
        e2 = jnp.concatenate([st, zero], 1).reshape(ns, d_ff)
        xs, u_s = conv_ffn(xs, sh_s, sc_s, gt_s, w_up, ffn_w_dw[l], ffn_b_dw[l], w_dn, l,
                           ln_g[l, 1], ln_b[l, 1], ns, tm_s, ts, prev=(e1, e2))
        outs["fs"].append(u_s.reshape(bs, ts, d_ff)[:, ts - 2:])

    st = lambda k: outs[k][0][None] if len(outs[k]) == 1 else jnp.stack(outs[k])
    return (xp.reshape(bp, seq, d), xs.reshape(bs, ts, d), st("kp"), st("vp"), st("ks"), st("vs"),
            st("rp"), st("rs"), st("cp"), st("cs"), st("fp"), st("fs"))
```

```python
import functools

import jax
import jax.numpy as jnp
from jax import lax
from jax.experimental import pallas as pl
from jax.experimental.pallas import tpu as pltpu

HEAD_DIM = 128
N_HEADS = 4
GROUP_W = N_HEADS * HEAD_DIM
N_IN_COLS = 7
RET_CHUNK = 128
MOBA_BLOCK = 256
MOBA_TOPK = 3
ROPE_THETA = 10000.0
DEPTH = 2
ALPHA = (2 * DEPTH) ** 0.25
LN_EPS = 1e-5
GN_EPS = 1e-6
NEG = -1e30
LOG2_E = 1.4426950408889634

F32 = jnp.float32
BF16 = jnp.bfloat16

_NT = (((1,), (1,)), ((), ()))
_TN = (((0,), (0,)), ((), ()))

VMEM_LIMIT = 56 * 1024 * 1024
SUBLANES = 8


def _params(*sem):
    return pltpu.CompilerParams(dimension_semantics=sem, vmem_limit_bytes=VMEM_LIMIT)


def _dot(a, b):
    return jnp.dot(a, b, preferred_element_type=F32)


def _layer_norm(v, g, b):
    mu = jnp.mean(v, axis=-1, keepdims=True)
    d = v - mu
    var = jnp.mean(d * d, axis=-1, keepdims=True)
    return d * lax.rsqrt(var + LN_EPS) * g + b


def _silu(x):
    return x * jax.nn.sigmoid(x)


def _mod_spec(mod, tm, rows_per_group):
    _, r, d = mod.shape
    if r == 1:
        return pl.BlockSpec((None, 1, d), lambda i: (i * tm // rows_per_group, 0, 0))
    assert r % tm == 0
    return pl.BlockSpec((None, tm, d), lambda i: (0, i % (r // tm), 0))


def _resident(a, layer=None):
    if layer is None:
        return pl.BlockSpec(a.shape, lambda *_: (0,) * a.ndim, pipeline_mode=pl.Buffered(1))
    return pl.BlockSpec((None,) + a.shape[1:], lambda *_: (layer,) + (0,) * (a.ndim - 1),
                        pipeline_mode=pl.Buffered(1))


def _adaln_kernel(c_ref, w_ref, b_ref, o_ref):
    a = _silu(c_ref[...]).astype(BF16)
    o_ref[...] = _dot(a, w_ref[...].astype(BF16)) + b_ref[...]


def adaln_all(c_all, ada_w, ada_b):
    n_layers, d, n_out = ada_w.shape
    m = c_all.shape[0]
    return pl.pallas_call(
        _adaln_kernel,
        grid=(n_layers, n_out // d),
        in_specs=[
            pl.BlockSpec((m, d), lambda l, j: (0, 0)),
            pl.BlockSpec((None, d, d), lambda l, j: (l, 0, j)),
            pl.BlockSpec((None, 1, d), lambda l, j: (l, 0, j)),
        ],
        out_specs=pl.BlockSpec((None, None, m, d), lambda l, j: (l, j, 0, 0)),
        out_shape=jax.ShapeDtypeStruct((n_layers, n_out // d, m, d), F32),
        compiler_params=_params("parallel", "parallel"),
        name="adaln",
    )(c_all, ada_w, ada_b.reshape(n_layers, 1, n_out))


def _in_proj_kernel(x_ref, sh_ref, sc_ref, w_ref, cr_ref, sr_ref, cm_ref, sm_ref,
                    rq_ref, rk_ref, rv_ref, rg_ref, mq_ref, mk_ref, mv_ref, kb_ref, vb_ref,
                    *km_refs):
    tm = x_ref.shape[0]
    h = (x_ref[...] * (1.0 + sc_ref[...]) + sh_ref[...]).astype(BF16)
    cr, sr, cm, sm = cr_ref[...], sr_ref[...], cm_ref[...], sm_ref[...]
    even = (lax.broadcasted_iota(jnp.int32, cr.shape, 1) & 1) == 0
    outs = (rq_ref, rk_ref, rv_ref, rg_ref, mq_ref, mk_ref, mv_ref)
    for g, o_ref in enumerate(outs):
        z = _dot(h, w_ref[:, g * GROUP_W:(g + 1) * GROUP_W])
        for hh in range(N_HEADS):
            cols = slice(hh * HEAD_DIM, (hh + 1) * HEAD_DIM)
            zs = z[:, cols]
            if g in (0, 1):
                nxt = pltpu.roll(zs, HEAD_DIM - 1, axis=1)
                prv = pltpu.roll(zs, 1, axis=1)
                zs = zs * cr + jnp.where(even, nxt, prv) * sr
                if g == 1:
                    zs = zs * (HEAD_DIM ** -0.5)
            elif g in (4, 5):
                zs = zs * cm + pltpu.roll(zs, HEAD_DIM // 2, axis=1) * sm
            if g in (5, 6):
                o_ref[pl.ds(hh, tm, stride=N_HEADS), :] = zs
                head_major = kb_ref if g == 5 else vb_ref
                head_major[:, cols] = zs.astype(head_major.dtype)
                if g == 5 and km_refs:
                    for r in range(tm // MOBA_BLOCK):
                        blk = zs[r * MOBA_BLOCK:(r + 1) * MOBA_BLOCK, :]
                        km_refs[0][r:r + 1, cols] = jnp.sum(blk, axis=0, keepdims=True) * (1.0 / MOBA_BLOCK)
            else:
                o_ref[:, cols] = zs.astype(o_ref.dtype)


def in_proj(x2d, shift, scale, w_bf, tabs, rows_per_group, tm, block_means, mxu_dtype):
    n, d = x2d.shape
    p_tiles = tabs[0].shape[0] // tm
    mod_spec = _mod_spec(shift, tm, rows_per_group)
    tab_spec = pl.BlockSpec((tm, HEAD_DIM), lambda i: (i % p_tiles, 0))
    wide = pl.BlockSpec((tm, GROUP_W), lambda i: (i, 0))
    tall = pl.BlockSpec((tm * N_HEADS, HEAD_DIM), lambda i: (i, 0))
    wide_shape = lambda dt: jax.ShapeDtypeStruct((n, GROUP_W), dt)
    tall_shape = jax.ShapeDtypeStruct((n * N_HEADS, HEAD_DIM), F32)
    out_specs = [wide] * 5 + [tall, tall, wide, wide]
    out_shape = [wide_shape(F32), wide_shape(F32), wide_shape(mxu_dtype), wide_shape(F32),
                 wide_shape(mxu_dtype), tall_shape, tall_shape, wide_shape(mxu_dtype), wide_shape(mxu_dtype)]
    if block_means:
        assert tm % MOBA_BLOCK == 0
        per = tm // MOBA_BLOCK
        out_specs.append(pl.BlockSpec((None, per, GROUP_W), lambda i: (i, 0, 0)))
        out_shape.append(jax.ShapeDtypeStruct((n // tm, per, GROUP_W), F32))
    return pl.pallas_call(
        _in_proj_kernel,
        grid=(n // tm,),
        in_specs=[pl.BlockSpec((tm, d), lambda i: (i, 0)), mod_spec, mod_spec,
                  _resident(w_bf), tab_spec, tab_spec, tab_spec, tab_spec],
        out_specs=out_specs,
        out_shape=out_shape,
        compiler_params=_params("parallel"),
        name="in_proj",
    )(x2d, shift, scale, w_bf, *tabs)


def rotary_tables(pos):
    half = HEAD_DIM // 2
    posf = pos.astype(F32)[:, None]
    inv_r = 1.0 / (ROPE_THETA ** jnp.linspace(0.0, 1.0, half, dtype=F32))
    ang_r = posf * inv_r[None, :]
    cr = jnp.repeat(jnp.cos(ang_r), 2, axis=1)
    sr = jnp.stack([-jnp.sin(ang_r), jnp.sin(ang_r)], -1).reshape(-1, HEAD_DIM)
    inv_m = ROPE_THETA ** (-jnp.arange(0, HEAD_DIM, 2, dtype=F32) / HEAD_DIM)
    ang_m = posf * inv_m[None, :]
    cm = jnp.concatenate([jnp.cos(ang_m), jnp.cos(ang_m)], -1)
    sm = jnp.concatenate([-jnp.sin(ang_m), jnp.sin(ang_m)], -1)
    return cr, sr, cm, sm


def retention_tables(chunk, rows):
    log_g = jnp.log1p(-jnp.exp2(-5.0 - jnp.arange(N_HEADS, dtype=F32)))
    idx = jnp.arange(chunk, dtype=F32)
    diff = idx[:, None] - idx[None, :]
    decay_in = jnp.where(diff[None] >= 0,
                         jnp.exp(jnp.maximum(diff, 0.0)[None] * log_g[:, None, None]), 0.0)
    q_dec = jnp.exp((idx + 1.0)[None, :] * log_g[:, None])
    k_dec = jnp.exp((chunk - 1.0 - idx)[None, :] * log_g[:, None])
    c_dec = jnp.exp(chunk * log_g)
    pad = rows - chunk
    decay_in = jnp.pad(decay_in, ((0, 0), (0, pad), (0, rows - chunk)))
    bc = lambda t: jnp.broadcast_to(jnp.pad(t, ((0, 0), (0, pad)))[:, :, None], (N_HEADS, rows, HEAD_DIM))
    c_b = jnp.broadcast_to(c_dec[:, None, None], (N_HEADS, 1, HEAD_DIM))
    return decay_in, bc(q_dec), bc(k_dec), c_b


def _ret_chunk(qc, kc, vc, s, din, qd, kd, cd):
    att = lax.dot_general(qc.astype(BF16), kc.astype(BF16), _NT,
                          preferred_element_type=F32) * din
    o = _dot(att.astype(BF16), vc) + _dot((qc * qd).astype(BF16), s.astype(BF16))
    s_new = cd * s + lax.dot_general((kc * kd).astype(BF16), vc, _TN,
                                     preferred_element_type=F32)
    return o, s_new


def _group_norm_gate(o, g):
    o = o * lax.rsqrt(jnp.mean(o * o, axis=-1, keepdims=True) + GN_EPS)
    return o * _silu(g)


def _ret_prompt_kernel(q_ref, k_ref, v_ref, g_ref, din_ref, qd_ref, kd_ref, cd_ref,
                       o_ref, sout_ref, s_ref, *, n_sub):
    j = pl.program_id(1)

    @pl.when(j == 0)
    def _():
        s_ref[...] = jnp.zeros_like(s_ref)

    for hh in range(N_HEADS):
        cols = slice(hh * HEAD_DIM, (hh + 1) * HEAD_DIM)
        din, qd, kd, cd = din_ref[hh], qd_ref[hh], kd_ref[hh], cd_ref[hh]
        for c in range(n_sub):
            rows = slice(c * RET_CHUNK, (c + 1) * RET_CHUNK)
            o, s_new = _ret_chunk(q_ref[rows, cols], k_ref[rows, cols], v_ref[rows, cols],
                                  s_ref[hh], din, qd, kd, cd)
            s_ref[hh] = s_new
            o_ref[rows, cols] = _group_norm_gate(o, g_ref[rows, cols]).astype(o_ref.dtype)

    @pl.when(j == pl.num_programs(1) - 1)
    def _():
        sout_ref[...] = s_ref[...]


def retention_prompt(rq, rk, rv, rg, batch, seq):
    tc = min(4 * RET_CHUNK, seq)
    nj = seq // tc
    tabs = retention_tables(RET_CHUNK, RET_CHUNK)
    tok = pl.BlockSpec((tc, GROUP_W), lambda b, j: (b * nj + j, 0))
    tab = pl.BlockSpec((N_HEADS, RET_CHUNK, HEAD_DIM), lambda b, j: (0, 0, 0))
    return pl.pallas_call(
        functools.partial(_ret_prompt_kernel, n_sub=tc // RET_CHUNK),
        grid=(batch, nj),
        in_specs=[tok, tok, tok, tok, tab, tab, tab,
                  pl.BlockSpec((N_HEADS, 1, HEAD_DIM), lambda b, j: (0, 0, 0))],
        out_specs=[tok, pl.BlockSpec((None, N_HEADS, HEAD_DIM, HEAD_DIM), lambda b, j: (b, 0, 0, 0))],
        out_shape=[jax.ShapeDtypeStruct((batch * seq, GROUP_W), BF16),
                   jax.ShapeDtypeStruct((batch, N_HEADS, HEAD_DIM, HEAD_DIM), F32)],
        scratch_shapes=[pltpu.VMEM((N_HEADS, HEAD_DIM, HEAD_DIM), F32)],
        compiler_params=_params("parallel", "arbitrary"),
        name="retention_prompt",
    )(rq, rk, rv, rg, *tabs)


def _group_rows(ref, group, batch, t, cols):
    return [ref[pl.ds(pl.multiple_of(ti * batch + group * SUBLANES, SUBLANES), SUBLANES), cols]
            for ti in range(t)]


def _seq_tile(token_rows, row, n_rows):
    width = token_rows[0].shape[1]
    r_i = lax.broadcasted_iota(jnp.int32, (n_rows, width), 0)
    g_i = lax.broadcasted_iota(jnp.int32, token_rows[0].shape, 0)
    out = jnp.zeros((n_rows, width), F32)
    for ti, rows in enumerate(token_rows):
        if isinstance(row, int):
            picked = rows[row:row + 1, :]
        else:
            picked = jnp.sum(jnp.where(g_i == row, rows, 0.0), axis=0, keepdims=True)
        out = jnp.where(r_i == ti, picked, out)
    return out


def _ret_sample_kernel(q_ref, k_ref, v_ref, g_ref, s0_ref, din_ref, qd_ref, kd_ref, cd_ref,
                       o_ref, sout_ref, *, t, batch):
    group = pl.program_id(0)
    r_i = lax.broadcasted_iota(jnp.int32, (SUBLANES, HEAD_DIM), 0)
    for hh in range(N_HEADS):
        cols = slice(hh * HEAD_DIM, (hh + 1) * HEAD_DIM)
        din, qd, kd, cd = din_ref[hh], qd_ref[hh], kd_ref[hh], cd_ref[hh]
        q_t, k_t, v_t, g_t = (_group_rows(r, group, batch, t, cols) for r in (q_ref, k_ref, v_ref, g_ref))
        out_t = [jnp.zeros((SUBLANES, HEAD_DIM), F32) for _ in range(t)]
        for bl in range(SUBLANES):
            qc = _seq_tile(q_t, bl, RET_CHUNK)
            kc = _seq_tile(k_t, bl, RET_CHUNK)
            vc = _seq_tile(v_t, bl, RET_CHUNK).astype(BF16)
            o, s_new = _ret_chunk(qc, kc, vc, s0_ref[bl, hh], din, qd, kd, cd)
            sout_ref[bl, hh] = s_new
            gated = _group_norm_gate(o[0:SUBLANES, :], _seq_tile(g_t, bl, SUBLANES))
            for ti in range(t):
                out_t[ti] = jnp.where(r_i == bl, gated[ti:ti + 1, :], out_t[ti])
        for ti in range(t):
            o_ref[pl.ds(pl.multiple_of(ti * batch + group * SUBLANES, SUBLANES), SUBLANES), cols] = out_t[ti]


def retention_sample(rq, rk, rv, rg, states, layer, t):
    batch = states.shape[1]
    bb = SUBLANES
    assert batch % bb == 0
    tabs = retention_tables(t, RET_CHUNK)
    tok = pl.BlockSpec(rq.shape, lambda i: (0, 0))
    st_in = pl.BlockSpec((None, bb, N_HEADS, HEAD_DIM, HEAD_DIM), lambda i: (layer, i, 0, 0, 0))
    st = pl.BlockSpec((bb, N_HEADS, HEAD_DIM, HEAD_DIM), lambda i: (i, 0, 0, 0))
    tab = pl.BlockSpec((N_HEADS, RET_CHUNK, HEAD_DIM), lambda i: (0, 0, 0))
    return pl.pallas_call(
        functools.partial(_ret_sample_kernel, t=t, batch=batch),
        grid=(batch // bb,),
        in_specs=[tok, tok, tok, tok, st_in, tab, tab, tab,
                  pl.BlockSpec((N_HEADS, 1, HEAD_DIM), lambda i: (0, 0, 0))],
        out_specs=[tok, st],
        out_shape=[jax.ShapeDtypeStruct(rq.shape, F32),
                   jax.ShapeDtypeStruct(states.shape[1:], F32)],
        compiler_params=_params("arbitrary"),
        name="retention_sample",
    )(rq, rk, rv, rg, states, *tabs)


def _beaten_counts(rows, n_valid):
    counts = []
    for n in range(n_valid):
        cnt = jnp.zeros(rows[n].shape, jnp.int32)
        for m in range(n_valid):
            if m == n:
                continue
            beats = (rows[m] >= rows[n]) if m < n else (rows[m] > rows[n])
            cnt = cnt + beats.astype(jnp.int32)
        counts.append(cnt)
    return counts


def _moba_prompt_kernel(q_ref, k_ref, v_ref, km_ref, o_ref, s_ref, *, nblk):
    blk = MOBA_BLOCK
    scale = HEAD_DIM ** -0.5
    kb = k_ref[...]
    vt = v_ref[...].astype(F32).T.astype(BF16)
    kmean = km_ref[...].astype(BF16)
    key_i = lax.broadcasted_iota(jnp.int32, (blk, blk), 0)
    qry_i = lax.broadcasted_iota(jnp.int32, (blk, blk), 1)
    causal = key_i <= qry_i

    for qi in range(nblk):
        qb = q_ref[qi * blk:(qi + 1) * blk, :]
        cnt = None
        if qi > MOBA_TOPK:
            gt = lax.dot_general(kmean, qb, _NT, preferred_element_type=F32)
            cnt = _beaten_counts([gt[m:m + 1, :] for m in range(qi)], qi)
        mx = jnp.full((1, blk), NEG, F32)
        for n in range(qi + 1):
            s = lax.dot_general(kb[n * blk:(n + 1) * blk], qb, _NT,
                                preferred_element_type=F32)
            if n == qi:
                s = jnp.where(causal, s, NEG)
            elif cnt is not None:
                s = jnp.where(jnp.broadcast_to(cnt[n], s.shape) < MOBA_TOPK, s, NEG)
            s_ref[n * blk:(n + 1) * blk, :] = s
            mx = jnp.maximum(mx, jnp.max(s, axis=0, keepdims=True))
        l = jnp.zeros((1, blk), F32)
        ot = jnp.zeros((HEAD_DIM, blk), F32)
        for n in range(qi + 1):
            p = jnp.exp2((s_ref[n * blk:(n + 1) * blk, :] - mx) * (scale * LOG2_E))
            l = l + jnp.sum(p, axis=0, keepdims=True)
            ot = ot + _dot(vt[:, n * blk:(n + 1) * blk], p.astype(BF16))
        ot = ot * (1.0 / l)
        o_ref[qi * blk:(qi + 1) * blk, :] = ot.T.astype(o_ref.dtype)


def moba_prompt(mq, mk, mv, kmean, batch, seq):
    nblk = seq // MOBA_BLOCK
    assert nblk % SUBLANES == 0
    spec = pl.BlockSpec((seq, HEAD_DIM), lambda b, h: (b, h))
    return pl.pallas_call(
        functools.partial(_moba_prompt_kernel, nblk=nblk),
        grid=(batch, N_HEADS),
        in_specs=[spec, spec, spec, pl.BlockSpec((nblk, HEAD_DIM), lambda b, h: (b, h))],
        out_specs=spec,
        out_shape=jax.ShapeDtypeStruct((batch * seq, GROUP_W), BF16),
        scratch_shapes=[pltpu.VMEM((seq, MOBA_BLOCK), F32)],
        compiler_params=_params("parallel", "parallel"),
        name="moba_prompt",
    )(mq, mk, mv, kmean)


def _page_rows(ref, page):
    return jnp.concatenate([ref[pl.ds(hh, page, stride=N_HEADS), :] for hh in range(N_HEADS)], axis=1)


def _moba_sample_kernel(pt_ref, q_ref, kn_ref, vn_ref, *rest, n_pages, page, t):
    k_refs = rest[:n_pages]
    v_refs = rest[n_pages:2 * n_pages]
    o_ref = rest[2 * n_pages]
    scale = HEAD_DIM ** -0.5
    ppb = MOBA_BLOCK // page
    nblk = n_pages // ppb
    n_pairs = N_HEADS * t

    b = pl.program_id(0)
    batch = q_ref.shape[0] // t
    all_cols = slice(0, GROUP_W)
    group = b // SUBLANES
    row = b % SUBLANES
    q = _seq_tile(_group_rows(q_ref, group, batch, t, all_cols), row, SUBLANES)

    r_i = lax.broadcasted_iota(jnp.int32, (n_pairs, GROUP_W), 0)
    l_i = lax.broadcasted_iota(jnp.int32, (n_pairs, GROUP_W), 1)
    qm = jnp.zeros((n_pairs, GROUP_W), F32)
    for ti in range(t):
        for hh in range(N_HEADS):
            hit = (r_i == hh * t + ti) & (l_i >= hh * HEAD_DIM) & (l_i < (hh + 1) * HEAD_DIM)
            qm = jnp.where(hit, q[ti:ti + 1, :], qm)
    qm = qm.astype(BF16)

    zpad = jnp.zeros((page - SUBLANES, GROUP_W), F32)
    own = lambda ref: jnp.concatenate(
        [_seq_tile(_group_rows(ref, group, batch, t, all_cols), row, SUBLANES), zpad], axis=0).astype(BF16)
    k_own, v_own = own(kn_ref), own(vn_ref)

    b_i = lax.broadcasted_iota(jnp.int32, (page, GROUP_W), 0)
    kmean = jnp.zeros((page, GROUP_W), F32)
    scores = []
    for p in range(n_pages):
        kp = _page_rows(k_refs[p], page)
        scores.append(lax.dot_general(qm, kp.astype(BF16), _NT,
                                      preferred_element_type=F32) * scale)
        ks = jnp.sum(kp, axis=0, keepdims=True) * (1.0 / MOBA_BLOCK)
        kmean = kmean + jnp.where(b_i == p // ppb, ks, 0.0)
    gate = lax.dot_general(qm, kmean.astype(BF16), _NT, preferred_element_type=F32)
    cnt = _beaten_counts([gate[:, m:m + 1] for m in range(nblk)], nblk)
    keep = [jnp.broadcast_to(c, (n_pairs, page)) < MOBA_TOPK for c in cnt]

    s_own = lax.dot_general(qm, k_own, _NT, preferred_element_type=F32) * scale
    key_i = lax.broadcasted_iota(jnp.int32, (n_pairs, page), 1)
    tok_i = lax.rem(lax.broadcasted_iota(jnp.int32, (n_pairs, page), 0), t)
    s_own = jnp.where((key_i <= tok_i) & (key_i < t), s_own, NEG)

    mx = s_own
    for p in range(n_pages):
        scores[p] = jnp.where(keep[p // ppb], scores[p], NEG)
        mx = jnp.maximum(mx, scores[p])
    mx = jnp.max(mx, axis=1, keepdims=True)
    e_own = jnp.exp(s_own - mx)
    l = e_own
    acc = _dot(e_own.astype(BF16), v_own)
    for p in range(n_pages):
        e = jnp.exp(scores[p] - mx)
        l = l + e
        acc = acc + _dot(e.astype(BF16), _page_rows(v_refs[p], page).astype(BF16))
    acc = acc * (1.0 / jnp.sum(l, axis=1, keepdims=True))
    o_ref[...] = jnp.zeros_like(o_ref)
    for hh in range(N_HEADS):
        cols = slice(hh * HEAD_DIM, (hh + 1) * HEAD_DIM)
        o_ref[0:t, cols] = acc[hh * t:(hh + 1) * t, cols]


def moba_sample(mq, mk, mv, cache_k, cache_v, page_table, page_base, page, t):
    batch, n_pages = page_table.shape
    assert (n_pages * page) % MOBA_BLOCK == 0 and t <= SUBLANES
    assert n_pages * page // MOBA_BLOCK <= page
    tok = pl.BlockSpec(mq.shape, lambda b, pt: (0, 0))

    def page_spec(p):
        return pl.BlockSpec((page * N_HEADS, HEAD_DIM), lambda b, pt: (page_base + pt[b, p], 0))

    specs = [tok, tok, tok] + [page_spec(p) for p in range(n_pages)] * 2
    grid_spec = pltpu.PrefetchScalarGridSpec(
        num_scalar_prefetch=1,
        grid=(batch,),
        in_specs=specs,
        out_specs=pl.BlockSpec((None, SUBLANES, GROUP_W), lambda b, pt: (b, 0, 0)),
    )
    return pl.pallas_call(
        functools.partial(_moba_sample_kernel, n_pages=n_pages, page=page, t=t),
        grid_spec=grid_spec,
        out_shape=jax.ShapeDtypeStruct((batch, SUBLANES, GROUP_W), F32),
        compiler_params=_params("parallel"),
        name="moba_sample",
    )(page_table, mq, mk, mv, *([cache_k] * n_pages), *([cache_v] * n_pages))


def _out_proj_kernel(or_ref, om_ref, w_ref, x_ref, gate_ref, g_ref, b_ref, o_ref, *, sub):
    for r0 in range(0, x_ref.shape[0], sub):
        rows = slice(r0, r0 + sub)
        y = (_dot(or_ref[rows, :].astype(BF16), w_ref[:GROUP_W, :])
             + _dot(om_ref[rows, :].astype(BF16), w_ref[GROUP_W:, :]))
        gate = gate_ref[...] if gate_ref.shape[0] == 1 else gate_ref[rows, :]
        v = ALPHA * x_ref[rows, :] + (1.0 + gate) * y
        o_ref[rows, :] = _layer_norm(v, g_ref[...], b_ref[...])


def out_proj(o_r, o_m, w_bf, x2d, gate, ln_g, ln_b, rows_per_group, tm):
    n, d = x2d.shape
    vec = pl.BlockSpec((1, d), lambda i: (0, 0))
    return pl.pallas_call(
        functools.partial(_out_proj_kernel, sub=min(128, tm)),
        grid=(n // tm,),
        in_specs=[pl.BlockSpec((tm, GROUP_W), lambda i: (i, 0)),
                  pl.BlockSpec((tm, GROUP_W), lambda i: (i, 0)),
                  _resident(w_bf),
                  pl.BlockSpec((tm, d), lambda i: (i, 0)),
                  _mod_spec(gate, tm, rows_per_group), vec, vec],
        out_specs=pl.BlockSpec((tm, d), lambda i: (i, 0)),
        out_shape=jax.ShapeDtypeStruct((n, d), F32),
        compiler_params=_params("parallel"),
        name="out_proj",
    )(o_r, o_m, w_bf, x2d, gate, ln_g.reshape(1, d), ln_b.reshape(1, d))


def _ffn_kernel(*refs, d_ff, cw, tiles_per_seq, planes):
    if planes:
        (x_ref, sh_ref, sc_ref, gate_ref, wup_ref, wdw_ref, bdw_ref, wdn_ref, g_ref, b_ref,
         p0_ref, p1_ref, o_ref, prev_ref, h_s, act_s, hist_s) = refs
    else:
        (x_ref, sh_ref, sc_ref, gate_ref, wup_ref, wdw_ref, bdw_ref, wdn_ref, g_ref, b_ref,
         o_ref, prev_ref, h_s, act_s, hist_s) = refs
    tm = x_ref.shape[0]
    x = x_ref[...]
    h_s[...] = (x * (1.0 + sc_ref[...]) + sh_ref[...]).astype(BF16)
    row = lax.broadcasted_iota(jnp.int32, (tm, cw), 0)

    @pl.when(pl.program_id(0) % tiles_per_seq == 0)
    def _():
        if planes:
            hist_s[0] = p0_ref[...]
            hist_s[1] = p1_ref[...]
        else:
            hist_s[...] = jnp.zeros_like(hist_s)

    for c in range(d_ff // cw):
        cu = slice(c * cw, (c + 1) * cw)
        cv = slice(d_ff + c * cw, d_ff + (c + 1) * cw)
        u = _dot(h_s[...], wup_ref[:, cu])
        v = _dot(h_s[...], wup_ref[:, cv])
        if planes:
            s2 = hist_s[0, :, cu]
            s1 = hist_s[1, :, cu]
            hist_s[0, :, cu] = s1
            hist_s[1, :, cu] = u
            prev_ref[:, cu] = u
        else:
            e1 = hist_s[1:2, cu]
            e2 = jnp.where(row == 0, hist_s[0:1, cu], hist_s[1:2, cu])
            hist_s[:, cu] = u[tm - 2:tm, :]
            prev_ref[:, cu] = u[tm - 2:tm, :]
            s1 = jnp.where(row >= 1, pltpu.roll(u, 1, axis=0), e1)
            s2 = jnp.where(row >= 2, pltpu.roll(u, 2, axis=0), e2)
        uc = wdw_ref[0:1, cu] * s2 + wdw_ref[1:2, cu] * s1 + wdw_ref[2:3, cu] * u + bdw_ref[:, cu]
        act = 0.5 * uc * (1.0 + lax.erf(uc * (2.0 ** -0.5))) * v
        act_s[:, cu] = act.astype(BF16)
    fy = _dot(act_s[...], wdn_ref[...])
    vsum = ALPHA * x + (1.0 + gate_ref[...]) * fy
    o_ref[...] = _layer_norm(vsum, g_ref[...], b_ref[...])


def conv_ffn(x2d, shift, scale, gate, w_up_bf, w_dw, b_dw, w_dn_bf, layer, ln_g, ln_b,
             rows_per_group, tm, tiles_per_seq, history=None):
    n, d = x2d.shape
    d_ff = w_dn_bf.shape[1]
    cw = 256
    planes = history is not None
    mod = _mod_spec(shift, tm, rows_per_group)
    vec = pl.BlockSpec((1, d), lambda i: (0, 0))
    b_dw2 = b_dw.reshape(1, d_ff)
    in_specs = [pl.BlockSpec((tm, d), lambda i: (i, 0)), mod, mod, mod,
                _resident(w_up_bf, layer), _resident(w_dw), _resident(b_dw2),
                _resident(w_dn_bf, layer), vec, vec]
    args = [x2d, shift, scale, gate, w_up_bf, w_dw, b_dw2, w_dn_bf,
            ln_g.reshape(1, d), ln_b.reshape(1, d)]
    scratch = [pltpu.VMEM((tm, d), BF16), pltpu.VMEM((tm, d_ff), BF16)]
    if planes:
        in_specs += [_resident(history[0]), _resident(history[1])]
        args += list(history)
        prev_spec = pl.BlockSpec((tm, d_ff), lambda i: (i, 0))
        prev_shape = jax.ShapeDtypeStruct((n, d_ff), F32)
        scratch.append(pltpu.VMEM((2, tm, d_ff), F32))
    else:
        prev_spec = pl.BlockSpec((None, 2, d_ff), lambda i: (i, 0, 0))
        prev_shape = jax.ShapeDtypeStruct((n // tm, 2, d_ff), F32)
        scratch.append(pltpu.VMEM((2, d_ff), F32))
    return pl.pallas_call(
        functools.partial(_ffn_kernel, d_ff=d_ff, cw=cw, tiles_per_seq=tiles_per_seq, planes=planes),
        grid=(n // tm,),
        in_specs=in_specs,
        out_specs=[pl.BlockSpec((tm, d), lambda i: (i, 0)), prev_spec],
        out_shape=[jax.ShapeDtypeStruct((n, d), F32), prev_shape],
        scratch_shapes=scratch,
        compiler_params=_params("arbitrary"),
        name="conv_ffn",
    )(*args)


def _glu_kernel(x_ref, sh_ref, sc_ref, w_ref, b_ref, o_ref, *, cw):
    h = (x_ref[...] * (1.0 + sc_ref[...]) + sh_ref[...]).astype(BF16)
    dc = o_ref.shape[1]
    for c in range(dc // cw):
        ca = slice(c * cw, (c + 1) * cw)
        cg = slice(dc + c * cw, dc + (c + 1) * cw)
        a = _dot(h, w_ref[:, ca]) + b_ref[:, ca]
        g = _dot(h, w_ref[:, cg]) + b_ref[:, cg]
        o_ref[:, ca] = a * jax.nn.sigmoid(g)


def conformer_glu(x2d, shift, scale, w1_bf, b1, rows_per_group, tm):
    n, d = x2d.shape
    dc = w1_bf.shape[1] // 2
    mod = _mod_spec(shift, tm, rows_per_group)
    b1r = b1.reshape(1, 2 * dc)
    return pl.pallas_call(
        functools.partial(_glu_kernel, cw=256),
        grid=(n // tm,),
        in_specs=[pl.BlockSpec((tm, d), lambda i: (i, 0)), mod, mod,
                  _resident(w1_bf), _resident(b1r)],
        out_specs=pl.BlockSpec((tm, dc), lambda i: (i, 0)),
        out_shape=jax.ShapeDtypeStruct((n, dc), F32),
        compiler_params=_params("parallel"),
        name="conformer_glu",
    )(x2d, shift, scale, w1_bf, b1r)


PAD_ROWS = 32


def _dwconv_rows(win_ref, w_ref, bias, r0, rs, cols, width):
    off = PAD_ROWS - (width - 1)
    y = bias
    for r in range(SUBLANES):
        z = None
        for m in range((off + width - 1) // SUBLANES + 1):
            k = SUBLANES * m + r - off
            if 0 <= k < width:
                lo = r0 + SUBLANES * m
                term = w_ref[k:k + 1, cols] * win_ref[lo:lo + rs + SUBLANES, cols]
                z = term if z is None else z + term
        if z is not None:
            y = y + z[r:r + rs, :]
    return y


def _conformer_prompt_kernel(x_ref, sh_ref, sc_ref, gate_ref, w1_ref, b1_ref, wdw_ref, bdw_ref,
                             gl_ref, bl_ref, w2_ref, b2_ref, g_ref, b_ref,
                             o_ref, tail_ref, win_ref, y_ref, *, width, rs, cw):
    tm, dc = y_ref.shape

    @pl.when(pl.program_id(1) == 0)
    def _():
        win_ref[0:PAD_ROWS, :] = jnp.zeros((PAD_ROWS, dc), F32)
        win_ref[PAD_ROWS + tm:, :] = jnp.zeros((SUBLANES, dc), F32)

    x = x_ref[...]
    h = (x * (1.0 + sc_ref[...]) + sh_ref[...]).astype(BF16)
    for c in range(dc // cw):
        ca = slice(c * cw, (c + 1) * cw)
        cg = slice(dc + c * cw, dc + (c + 1) * cw)
        a = _dot(h, w1_ref[:, ca]) + b1_ref[:, ca]
        g = _dot(h, w1_ref[:, cg]) + b1_ref[:, cg]
        win_ref[PAD_ROWS:PAD_ROWS + tm, ca] = a * jax.nn.sigmoid(g)
    tail_ref[...] = win_ref[tm:tm + PAD_ROWS, :]
    for c in range(dc // HEAD_DIM):
        cols = slice(c * HEAD_DIM, (c + 1) * HEAD_DIM)
        bias = bdw_ref[:, cols]
        for r0 in range(0, tm, rs):
            y_ref[r0:r0 + rs, cols] = _dwconv_rows(win_ref, wdw_ref, bias, r0, rs, cols, width)
    win_ref[0:PAD_ROWS, :] = win_ref[tm:tm + PAD_ROWS, :]
    yn = _silu(_layer_norm(y_ref[...], gl_ref[...], bl_ref[...]))
    y = _dot(yn.astype(BF16), w2_ref[...]) + b2_ref[...]
    v = ALPHA * x + (1.0 + gate_ref[...]) * y
    o_ref[...] = _layer_norm(v, g_ref[...], b_ref[...])


def conformer_prompt(x2d, shift, scale, gate, w1_bf, b1, w_dw, b_dw, cf_g, cf_b, w2_bf, b2,
                     ln_g, ln_b, batch, seq, tm):
    n, d = x2d.shape
    dc = w2_bf.shape[0]
    width = w_dw.shape[0]
    assert width - 1 <= PAD_ROWS and tm % PAD_ROWS == 0
    nj = seq // tm
    row = lambda a: a.reshape(1, -1)
    tile = lambda w: pl.BlockSpec((tm, w), lambda b, j: (b * nj + j, 0))
    mod = pl.BlockSpec((None, 1, d), lambda b, j: (b, 0, 0))
    consts = [w1_bf, row(b1), w_dw, row(b_dw), row(cf_g), row(cf_b), w2_bf, row(b2), row(ln_g), row(ln_b)]
    return pl.pallas_call(
        functools.partial(_conformer_prompt_kernel, width=width, rs=64, cw=256),
        grid=(batch, nj),
        in_specs=[tile(d), mod, mod, mod] + [_resident(a) for a in consts],
        out_specs=[tile(d), pl.BlockSpec((None, PAD_ROWS, dc), lambda b, j: (b, 0, 0))],
        out_shape=[jax.ShapeDtypeStruct((n, d), F32),
                   jax.ShapeDtypeStruct((batch, PAD_ROWS, dc), F32)],
        scratch_shapes=[pltpu.VMEM((PAD_ROWS + tm + SUBLANES, dc), F32), pltpu.VMEM((tm, dc), F32)],
        compiler_params=_params("parallel", "arbitrary"),
        name="conformer_prompt",
    )(x2d, shift, scale, gate, *consts)


def _dwconv_sample_kernel(x_ref, st_ref, w_ref, b_ref, o_ref, nst_ref, *, width, t):
    hist = width - 1
    bb, dc = x_ref.shape[1], x_ref.shape[2]
    plane = lambda j, cols: st_ref[j, :, cols] if j < hist else x_ref[j - hist, :, cols]
    for c in range(dc // HEAD_DIM):
        cols = slice(c * HEAD_DIM, (c + 1) * HEAD_DIM)
        acc = [jnp.zeros((bb, HEAD_DIM), F32) + b_ref[:, cols] for _ in range(t)]
        for j in range(hist + t):
            p = plane(j, cols)
            for ti in range(t):
                if 0 <= j - ti < width:
                    acc[ti] = acc[ti] + w_ref[j - ti:j - ti + 1, cols] * p
            if j >= t:
                nst_ref[j - t, :, cols] = p
        for ti in range(t):
            o_ref[ti, :, cols] = acc[ti]


def dwconv_sample(glu, states, layer, w_dw, b_dw):
    t, batch, dc = glu.shape
    width = w_dw.shape[0]
    hist = width - 1
    bb = 32
    b2 = b_dw.reshape(1, dc)
    return pl.pallas_call(
        functools.partial(_dwconv_sample_kernel, width=width, t=t),
        grid=(batch // bb,),
        in_specs=[pl.BlockSpec((t, bb, dc), lambda i: (0, i, 0)),
                  pl.BlockSpec((None, hist, bb, dc), lambda i: (layer, 0, i, 0)),
                  pl.BlockSpec(w_dw.shape, lambda i: (0, 0)),
                  pl.BlockSpec(b2.shape, lambda i: (0, 0))],
        out_specs=[pl.BlockSpec((t, bb, dc), lambda i: (0, i, 0)),
                   pl.BlockSpec((hist, bb, dc), lambda i: (0, i, 0))],
        out_shape=[jax.ShapeDtypeStruct((t, batch, dc), F32),
                   jax.ShapeDtypeStruct((hist, batch, dc), F32)],
        compiler_params=_params("parallel"),
        name="dwconv_sample",
    )(glu, states, w_dw, b2)


def _conf_tail_kernel(y_ref, gl_ref, bl_ref, w_ref, b2_ref, x_ref, gate_ref, g_ref, b_ref, o_ref):
    yn = _silu(_layer_norm(y_ref[...], gl_ref[...], bl_ref[...]))
    y = _dot(yn.astype(BF16), w_ref[...]) + b2_ref[...]
    v = ALPHA * x_ref[...] + (1.0 + gate_ref[...]) * y
    o_ref[...] = _layer_norm(v, g_ref[...], b_ref[...])


def conformer_tail(y, cf_g, cf_b, w2_bf, b2, x2d, gate, ln_g, ln_b, rows_per_group, tm):
    n, d = x2d.shape
    dc = y.shape[1]
    vecc = pl.BlockSpec((1, dc), lambda i: (0, 0))
    vec = pl.BlockSpec((1, d), lambda i: (0, 0))
    return pl.pallas_call(
        _conf_tail_kernel,
        grid=(n // tm,),
        in_specs=[pl.BlockSpec((tm, dc), lambda i: (i, 0)), vecc, vecc,
                  _resident(w2_bf), vec,
                  pl.BlockSpec((tm, d), lambda i: (i, 0)),
                  _mod_spec(gate, tm, rows_per_group), vec, vec],
        out_specs=pl.BlockSpec((tm, d), lambda i: (i, 0)),
        out_shape=jax.ShapeDtypeStruct((n, d), F32),
        compiler_params=_params("parallel"),
        name="conformer_tail",
    )(y, cf_g.reshape(1, dc), cf_b.reshape(1, dc), w2_bf, b2.reshape(1, d), x2d, gate,
      ln_g.reshape(1, d), ln_b.reshape(1, d))


def kernel(x_prompt, x_sample, cache_k, cache_v, state_ret, state_conv, state_ffn, page_table, c_prompt, c_sample, ab_w_in, ab_w_out, cf_w_pw1, cf_b_pw1, cf_w_dw, cf_b_dw, cf_ln_g, cf_ln_b, cf_w_pw2, cf_b_pw2, ffn_w_up, ffn_w_dw, ffn_b_dw, ffn_w_down, ada_w, ada_b, ln_g, ln_b):
    bp, seq, d = x_prompt.shape
    bs, ts, _ = x_sample.shape
    n_layers = ada_w.shape[0]
    d_ff = ffn_w_down.shape[1]
    past_len = page_table.shape[1] * cache_k.shape[2]
    np_, ns = bp * seq, bs * ts
    tm_p = min(256, seq)
    tm_wide = min(512, seq)
    tm_s = bs

    ada = adaln_all(jnp.concatenate([c_prompt, c_sample], 0), ada_w, ada_b)

    def mods(l, k):
        return ada[l, k, :bp][:, None, :], ada[l, k, bp:][None]

    token_major = lambda a: jnp.swapaxes(a, 0, 1)
    xp = x_prompt.reshape(np_, d)
    xs = token_major(x_sample).reshape(ns, d)
    tabs_p = rotary_tables(jnp.arange(seq, dtype=jnp.int32))
    tabs_s = tuple(jnp.repeat(tb, bs, axis=0) for tb in
                   rotary_tables(past_len + jnp.arange(ts, dtype=jnp.int32)))

    w_up = ffn_w_up.astype(BF16)
    w_dn = ffn_w_down.astype(BF16)
    outs = {k: [] for k in ("kp", "vp", "ks", "vs", "rp", "rs", "cp", "cs", "fp", "fs")}
    for l in range(n_layers):
        i = l // 2
        sh_p, sh_s = mods(l, 0)
        sc_p, sc_s = mods(l, 1)
        gt_p, gt_s = mods(l, 2)
        if l % 2 == 0:
            w_in = ab_w_in[i].astype(BF16)
            w_out = ab_w_out[i].astype(BF16)
            rq, rk, rv, rg, mq, mk, mv, kb, vb, kmean = in_proj(
                xp, sh_p, sc_p, w_in, tabs_p, seq, tm_p, True, BF16)
            o_r, s_p = retention_prompt(rq, rk, rv, rg, bp, seq)
            o_m = moba_prompt(mq, kb, vb, kmean.reshape(-1, GROUP_W), bp, seq)
            xp = out_proj(o_r, o_m, w_out, xp, gt_p, ln_g[l, 0], ln_b[l, 0], seq, tm_wide)
            outs["kp"].append(mk.reshape(bp, seq, N_HEADS, HEAD_DIM))
            outs["vp"].append(mv.reshape(bp, seq, N_HEADS, HEAD_DIM))
            outs["rp"].append(s_p)

            rq, rk, rv, rg, mq, mk, mv, kb, vb = in_proj(xs, sh_s, sc_s, w_in, tabs_s, ns, tm_s, False, F32)
            o_r, s_s = retention_sample(rq, rk, rv, rg, state_ret, i, ts)
            n_phys, page = cache_k.shape[1], cache_k.shape[2]
            ck = cache_k.reshape(-1, HEAD_DIM)
            cv = cache_v.reshape(-1, HEAD_DIM)
            o_m = moba_sample(mq, kb, vb, ck, cv, page_table, i * n_phys, page, ts)
            o_m = token_major(o_m[:, :ts]).reshape(ns, GROUP_W)
            xs = out_proj(o_r, o_m, w_out, xs, gt_s, ln_g[l, 0], ln_b[l, 0], ns, tm_s)
            outs["ks"].append(token_major(mk.reshape(ts, bs, N_HEADS, HEAD_DIM)))
            outs["vs"].append(token_major(mv.reshape(ts, bs, N_HEADS, HEAD_DIM)))
            outs["rs"].append(s_s)
        else:
            w1 = cf_w_pw1[i].astype(BF16)
            w2 = cf_w_pw2[i].astype(BF16)
            hist = cf_w_dw.shape[1] - 1
            xp, tail = conformer_prompt(xp, sh_p, sc_p, gt_p, w1, cf_b_pw1[i], cf_w_dw[i], cf_b_dw[i],
                                        cf_ln_g[i], cf_ln_b[i], w2, cf_b_pw2[i],
                                        ln_g[l, 0], ln_b[l, 0], bp, seq, tm_p)
            outs["cp"].append(tail[:, PAD_ROWS - hist:])

            glu = conformer_glu(xs, sh_s, sc_s, w1, cf_b_pw1[i], ns, tm_s)
            y, nst = dwconv_sample(glu.reshape(ts, bs, -1), jnp.swapaxes(state_conv, 1, 2), i,
                                   cf_w_dw[i], cf_b_dw[i])
            xs = conformer_tail(y.reshape(ns, -1), cf_ln_g[i], cf_ln_b[i], w2, cf_b_pw2[i], xs, gt_s,
                                ln_g[l, 0], ln_b[l, 0], ns, tm_s)
            outs["cs"].append(token_major(nst))

        sh_p, sh_s = mods(l, 3)
        sc_p, sc_s = mods(l, 4)
        gt_p, gt_s = mods(l, 5)
        xp, prev_p = conv_ffn(xp, sh_p, sc_p, gt_p, w_up, ffn_w_dw[l], ffn_b_dw[l], w_dn, l,
                              ln_g[l, 1], ln_b[l, 1], seq, tm_p, seq // tm_p)
        outs["fp"].append(prev_p.reshape(bp, seq // tm_p, 2, d_ff)[:, -1])
        xs, u_s = conv_ffn(xs, sh_s, sc_s, gt_s, w_up, ffn_w_dw[l], ffn_b_dw[l], w_dn, l,
                           ln_g[l, 1], ln_b[l, 1], ns, tm_s, ts,
                           history=(state_ffn[l, :, 0], state_ffn[l, :, 1]))
        outs["fs"].append(token_major(u_s.reshape(ts, bs, d_ff)[ts - 2:]))

    st = lambda k: outs[k][0][None] if len(outs[k]) == 1 else jnp.stack(outs[k])
    return (xp.reshape(bp, seq, d), token_major(xs.reshape(ts, bs, d)), st("kp"), st("vp"), st("ks"), st("vs"),
            st("rp"), st("rs"), st("cp"), st("cs"), st("fp"), st("fs"))
```

```python
import functools

import jax
import jax.numpy as jnp
from jax import lax
from jax.experimental import pallas as pl
from jax.experimental.pallas import tpu as pltpu

HEAD_DIM = 128
N_HEADS = 4
GROUP_W = N_HEADS * HEAD_DIM
N_IN_COLS = 7
RET_CHUNK = 128
MOBA_BLOCK = 256
MOBA_TOPK = 3
ROPE_THETA = 10000.0
DEPTH = 2
ALPHA = (2 * DEPTH) ** 0.25
LN_EPS = 1e-5
GN_EPS = 1e-6
NEG = -1e30
LOG2_E = 1.4426950408889634

F32 = jnp.float32
BF16 = jnp.bfloat16

_NT = (((1,), (1,)), ((), ()))
_TN = (((0,), (0,)), ((), ()))

VMEM_LIMIT = 56 * 1024 * 1024
SUBLANES = 8


def _params(*sem):
    return pltpu.CompilerParams(dimension_semantics=sem, vmem_limit_bytes=VMEM_LIMIT)


def _dot(a, b):
    return jnp.dot(a, b, preferred_element_type=F32)


def _layer_norm(v, g, b):
    mu = jnp.mean(v, axis=-1, keepdims=True)
    d = v - mu
    var = jnp.mean(d * d, axis=-1, keepdims=True)
    return d * lax.rsqrt(var + LN_EPS) * g + b


def _silu(x):
    return x * jax.nn.sigmoid(x)


def _mod_spec(mod, tm, rows_per_group):
    _, r, d = mod.shape
    if r == 1:
        return pl.BlockSpec((None, 1, d), lambda i: (i * tm // rows_per_group, 0, 0))
    assert r % tm == 0
    return pl.BlockSpec((None, tm, d), lambda i: (0, i % (r // tm), 0))


def _resident(a, layer=None):
    if layer is None:
        return pl.BlockSpec(a.shape, lambda *_: (0,) * a.ndim, pipeline_mode=pl.Buffered(1))
    return pl.BlockSpec((None,) + a.shape[1:], lambda *_: (layer,) + (0,) * (a.ndim - 1),
                        pipeline_mode=pl.Buffered(1))


def _adaln_kernel(c_ref, w_ref, b_ref, o_ref):
    a = _silu(c_ref[...]).astype(BF16)
    o_ref[...] = _dot(a, w_ref[...].astype(BF16)) + b_ref[...]


def adaln_all(c_all, ada_w, ada_b):
    n_layers, d, n_out = ada_w.shape
    m = c_all.shape[0]
    return pl.pallas_call(
        _adaln_kernel,
        grid=(n_layers, n_out // d),
        in_specs=[
            pl.BlockSpec((m, d), lambda l, j: (0, 0)),
            pl.BlockSpec((None, d, d), lambda l, j: (l, 0, j)),
            pl.BlockSpec((None, 1, d), lambda l, j: (l, 0, j)),
        ],
        out_specs=pl.BlockSpec((None, None, m, d), lambda l, j: (l, j, 0, 0)),
        out_shape=jax.ShapeDtypeStruct((n_layers, n_out // d, m, d), F32),
        compiler_params=_params("parallel", "parallel"),
        name="adaln",
    )(c_all, ada_w, ada_b.reshape(n_layers, 1, n_out))


def _in_proj_kernel(x_ref, sh_ref, sc_ref, w_ref, cr_ref, sr_ref, cm_ref, sm_ref,
                    rq_ref, rk_ref, rv_ref, rg_ref, mq_ref, mk_ref, mv_ref, kb_ref, vb_ref,
                    *km_refs):
    tm = x_ref.shape[0]
    h = (x_ref[...] * (1.0 + sc_ref[...]) + sh_ref[...]).astype(BF16)
    cr, sr, cm, sm = cr_ref[...], sr_ref[...], cm_ref[...], sm_ref[...]
    even = (lax.broadcasted_iota(jnp.int32, cr.shape, 1) & 1) == 0
    outs = (rq_ref, rk_ref, rv_ref, rg_ref, mq_ref, mk_ref, mv_ref)
    for g, o_ref in enumerate(outs):
        z = _dot(h, w_ref[:, g * GROUP_W:(g + 1) * GROUP_W])
        for hh in range(N_HEADS):
            cols = slice(hh * HEAD_DIM, (hh + 1) * HEAD_DIM)
            zs = z[:, cols]
            if g in (0, 1):
                nxt = pltpu.roll(zs, HEAD_DIM - 1, axis=1)
                prv = pltpu.roll(zs, 1, axis=1)
                zs = zs * cr + jnp.where(even, nxt, prv) * sr
                if g == 1:
                    zs = zs * (HEAD_DIM ** -0.5)
            elif g in (4, 5):
                zs = zs * cm + pltpu.roll(zs, HEAD_DIM // 2, axis=1) * sm
            if g in (5, 6):
                o_ref[pl.ds(hh, tm, stride=N_HEADS), :] = zs
                head_major = kb_ref if g == 5 else vb_ref
                head_major[:, cols] = zs.astype(head_major.dtype)
                if g == 5 and km_refs:
                    for r in range(tm // MOBA_BLOCK):
                        blk = zs[r * MOBA_BLOCK:(r + 1) * MOBA_BLOCK, :]
                        km_refs[0][r:r + 1, cols] = jnp.sum(blk, axis=0, keepdims=True) * (1.0 / MOBA_BLOCK)
            else:
                o_ref[:, cols] = zs.astype(o_ref.dtype)


def in_proj(x2d, shift, scale, w_bf, tabs, rows_per_group, tm, block_means, mxu_dtype):
    n, d = x2d.shape
    p_tiles = tabs[0].shape[0] // tm
    mod_spec = _mod_spec(shift, tm, rows_per_group)
    tab_spec = pl.BlockSpec((tm, HEAD_DIM), lambda i: (i % p_tiles, 0))
    wide = pl.BlockSpec((tm, GROUP_W), lambda i: (i, 0))
    tall = pl.BlockSpec((tm * N_HEADS, HEAD_DIM), lambda i: (i, 0))
    wide_shape = lambda dt: jax.ShapeDtypeStruct((n, GROUP_W), dt)
    tall_shape = jax.ShapeDtypeStruct((n * N_HEADS, HEAD_DIM), F32)
    out_specs = [wide] * 5 + [tall, tall, wide, wide]
    out_shape = [wide_shape(F32), wide_shape(F32), wide_shape(mxu_dtype), wide_shape(F32),
                 wide_shape(mxu_dtype), tall_shape, tall_shape, wide_shape(mxu_dtype), wide_shape(mxu_dtype)]
    if block_means:
        assert tm % MOBA_BLOCK == 0
        per = tm // MOBA_BLOCK
        out_specs.append(pl.BlockSpec((None, per, GROUP_W), lambda i: (i, 0, 0)))
        out_shape.append(jax.ShapeDtypeStruct((n // tm, per, GROUP_W), F32))
    return pl.pallas_call(
        _in_proj_kernel,
        grid=(n // tm,),
        in_specs=[pl.BlockSpec((tm, d), lambda i: (i, 0)), mod_spec, mod_spec,
                  _resident(w_bf), tab_spec, tab_spec, tab_spec, tab_spec],
        out_specs=out_specs,
        out_shape=out_shape,
        compiler_params=_params("parallel"),
        name="in_proj",
    )(x2d, shift, scale, w_bf, *tabs)


def rotary_tables(pos):
    half = HEAD_DIM // 2
    posf = pos.astype(F32)[:, None]
    inv_r = 1.0 / (ROPE_THETA ** jnp.linspace(0.0, 1.0, half, dtype=F32))
    ang_r = posf * inv_r[None, :]
    cr = jnp.repeat(jnp.cos(ang_r), 2, axis=1)
    sr = jnp.stack([-jnp.sin(ang_r), jnp.sin(ang_r)], -1).reshape(-1, HEAD_DIM)
    inv_m = ROPE_THETA ** (-jnp.arange(0, HEAD_DIM, 2, dtype=F32) / HEAD_DIM)
    ang_m = posf * inv_m[None, :]
    cm = jnp.concatenate([jnp.cos(ang_m), jnp.cos(ang_m)], -1)
    sm = jnp.concatenate([-jnp.sin(ang_m), jnp.sin(ang_m)], -1)
    return cr, sr, cm, sm


def retention_tables(chunk, rows):
    log_g = jnp.log1p(-jnp.exp2(-5.0 - jnp.arange(N_HEADS, dtype=F32)))
    idx = jnp.arange(chunk, dtype=F32)
    diff = idx[:, None] - idx[None, :]
    decay_in = jnp.where(diff[None] >= 0,
                         jnp.exp(jnp.maximum(diff, 0.0)[None] * log_g[:, None, None]), 0.0)
    q_dec = jnp.exp((idx + 1.0)[None, :] * log_g[:, None])
    k_dec = jnp.exp((chunk - 1.0 - idx)[None, :] * log_g[:, None])
    c_dec = jnp.exp(chunk * log_g)
    pad = rows - chunk
    decay_in = jnp.pad(decay_in, ((0, 0), (0, pad), (0, rows - chunk)))
    bc = lambda t: jnp.broadcast_to(jnp.pad(t, ((0, 0), (0, pad)))[:, :, None], (N_HEADS, rows, HEAD_DIM))
    c_b = jnp.broadcast_to(c_dec[:, None, None], (N_HEADS, 1, HEAD_DIM))
    return decay_in, bc(q_dec), bc(k_dec), c_b


def _ret_chunk(qc, kc, vc, s, din, qd, kd, cd):
    att = lax.dot_general(qc.astype(BF16), kc.astype(BF16), _NT,
                          preferred_element_type=F32) * din
    o = _dot(att.astype(BF16), vc) + _dot((qc * qd).astype(BF16), s.astype(BF16))
    s_new = cd * s + lax.dot_general((kc * kd).astype(BF16), vc, _TN,
                                     preferred_element_type=F32)
    return o, s_new


def _group_norm_gate(o, g):
    o = o * lax.rsqrt(jnp.mean(o * o, axis=-1, keepdims=True) + GN_EPS)
    return o * _silu(g)


def _ret_prompt_kernel(q_ref, k_ref, v_ref, g_ref, din_ref, qd_ref, kd_ref, cd_ref,
                       o_ref, sout_ref, s_ref, *, n_sub):
    j = pl.program_id(1)

    @pl.when(j == 0)
    def _():
        s_ref[...] = jnp.zeros_like(s_ref)

    for hh in range(N_HEADS):
        cols = slice(hh * HEAD_DIM, (hh + 1) * HEAD_DIM)
        din, qd, kd, cd = din_ref[hh], qd_ref[hh], kd_ref[hh], cd_ref[hh]
        for c in range(n_sub):
            rows = slice(c * RET_CHUNK, (c + 1) * RET_CHUNK)
            o, s_new = _ret_chunk(q_ref[rows, cols], k_ref[rows, cols], v_ref[rows, cols],
                                  s_ref[hh], din, qd, kd, cd)
            s_ref[hh] = s_new
            o_ref[rows, cols] = _group_norm_gate(o, g_ref[rows, cols]).astype(o_ref.dtype)

    @pl.when(j == pl.num_programs(1) - 1)
    def _():
        sout_ref[...] = s_ref[...]


def retention_prompt(rq, rk, rv, rg, batch, seq):
    tc = min(4 * RET_CHUNK, seq)
    nj = seq // tc
    tabs = retention_tables(RET_CHUNK, RET_CHUNK)
    tok = pl.BlockSpec((tc, GROUP_W), lambda b, j: (b * nj + j, 0))
    tab = pl.BlockSpec((N_HEADS, RET_CHUNK, HEAD_DIM), lambda b, j: (0, 0, 0))
    return pl.pallas_call(
        functools.partial(_ret_prompt_kernel, n_sub=tc // RET_CHUNK),
        grid=(batch, nj),
        in_specs=[tok, tok, tok, tok, tab, tab, tab,
                  pl.BlockSpec((N_HEADS, 1, HEAD_DIM), lambda b, j: (0, 0, 0))],
        out_specs=[tok, pl.BlockSpec((None, N_HEADS, HEAD_DIM, HEAD_DIM), lambda b, j: (b, 0, 0, 0))],
        out_shape=[jax.ShapeDtypeStruct((batch * seq, GROUP_W), BF16),
                   jax.ShapeDtypeStruct((batch, N_HEADS, HEAD_DIM, HEAD_DIM), F32)],
        scratch_shapes=[pltpu.VMEM((N_HEADS, HEAD_DIM, HEAD_DIM), F32)],
        compiler_params=_params("parallel", "arbitrary"),
        name="retention_prompt",
    )(rq, rk, rv, rg, *tabs)


def _group_rows(ref, group, batch, t, cols):
    return [ref[pl.ds(pl.multiple_of(ti * batch + group * SUBLANES, SUBLANES), SUBLANES), cols]
            for ti in range(t)]


def _seq_tile(token_rows, row, n_rows):
    width = token_rows[0].shape[1]
    r_i = lax.broadcasted_iota(jnp.int32, (n_rows, width), 0)
    g_i = lax.broadcasted_iota(jnp.int32, token_rows[0].shape, 0)
    out = jnp.zeros((n_rows, width), F32)
    for ti, rows in enumerate(token_rows):
        if isinstance(row, int):
            picked = rows[row:row + 1, :]
        else:
            picked = jnp.sum(jnp.where(g_i == row, rows, 0.0), axis=0, keepdims=True)
        out = jnp.where(r_i == ti, picked, out)
    return out


def _ret_sample_kernel(q_ref, k_ref, v_ref, g_ref, s0_ref, din_ref, qd_ref, kd_ref, cd_ref,
                       o_ref, sout_ref, *, t, batch):
    group = pl.program_id(0)
    r_i = lax.broadcasted_iota(jnp.int32, (SUBLANES, HEAD_DIM), 0)
    for hh in range(N_HEADS):
        cols = slice(hh * HEAD_DIM, (hh + 1) * HEAD_DIM)
        din, qd, kd, cd = din_ref[hh], qd_ref[hh], kd_ref[hh], cd_ref[hh]
        q_t, k_t, v_t, g_t = (_group_rows(r, group, batch, t, cols) for r in (q_ref, k_ref, v_ref, g_ref))
        out_t = [jnp.zeros((SUBLANES, HEAD_DIM), F32) for _ in range(t)]
        for bl in range(SUBLANES):
            qc = _seq_tile(q_t, bl, RET_CHUNK)
            kc = _seq_tile(k_t, bl, RET_CHUNK)
            vc = _seq_tile(v_t, bl, RET_CHUNK).astype(BF16)
            o, s_new = _ret_chunk(qc, kc, vc, s0_ref[bl, hh], din, qd, kd, cd)
            sout_ref[bl, hh] = s_new
            gated = _group_norm_gate(o[0:SUBLANES, :], _seq_tile(g_t, bl, SUBLANES))
            for ti in range(t):
                out_t[ti] = jnp.where(r_i == bl, gated[ti:ti + 1, :], out_t[ti])
        for ti in range(t):
            o_ref[pl.ds(pl.multiple_of(ti * batch + group * SUBLANES, SUBLANES), SUBLANES), cols] = out_t[ti]


def retention_sample(rq, rk, rv, rg, states, layer, t):
    batch = states.shape[1]
    bb = SUBLANES
    assert batch % bb == 0
    tabs = retention_tables(t, RET_CHUNK)
    tok = pl.BlockSpec(rq.shape, lambda i: (0, 0))
    st_in = pl.BlockSpec((None, bb, N_HEADS, HEAD_DIM, HEAD_DIM), lambda i: (layer, i, 0, 0, 0))
    st = pl.BlockSpec((bb, N_HEADS, HEAD_DIM, HEAD_DIM), lambda i: (i, 0, 0, 0))
    tab = pl.BlockSpec((N_HEADS, RET_CHUNK, HEAD_DIM), lambda i: (0, 0, 0))
    return pl.pallas_call(
        functools.partial(_ret_sample_kernel, t=t, batch=batch),
        grid=(batch // bb,),
        in_specs=[tok, tok, tok, tok, st_in, tab, tab, tab,
                  pl.BlockSpec((N_HEADS, 1, HEAD_DIM), lambda i: (0, 0, 0))],
        out_specs=[tok, st],
        out_shape=[jax.ShapeDtypeStruct(rq.shape, F32),
                   jax.ShapeDtypeStruct(states.shape[1:], F32)],
        compiler_params=_params("arbitrary"),
        name="retention_sample",
    )(rq, rk, rv, rg, states, *tabs)


def _beaten_counts(rows, n_valid):
    counts = []
    for n in range(n_valid):
        cnt = jnp.zeros(rows[n].shape, jnp.int32)
        for m in range(n_valid):
            if m == n:
                continue
            beats = (rows[m] >= rows[n]) if m < n else (rows[m] > rows[n])
            cnt = cnt + beats.astype(jnp.int32)
        counts.append(cnt)
    return counts


def _moba_prompt_kernel(q_ref, k_ref, v_ref, km_ref, o_ref, s_ref, *, nblk):
    blk = MOBA_BLOCK
    scale = HEAD_DIM ** -0.5
    kb = k_ref[...]
    vt = v_ref[...].astype(F32).T.astype(BF16)
    kmean = km_ref[...].astype(BF16)
    key_i = lax.broadcasted_iota(jnp.int32, (blk, blk), 0)
    qry_i = lax.broadcasted_iota(jnp.int32, (blk, blk), 1)
    causal = key_i <= qry_i

    for qi in range(nblk):
        qb = q_ref[qi * blk:(qi + 1) * blk, :]
        cnt = None
        if qi > MOBA_TOPK:
            gt = lax.dot_general(kmean, qb, _NT, preferred_element_type=F32)
            cnt = _beaten_counts([gt[m:m + 1, :] for m in range(qi)], qi)
        mx = jnp.full((1, blk), NEG, F32)
        for n in range(qi + 1):
            s = lax.dot_general(kb[n * blk:(n + 1) * blk], qb, _NT,
                                preferred_element_type=F32)
            if n == qi:
                s = jnp.where(causal, s, NEG)
            elif cnt is not None:
                s = jnp.where(jnp.broadcast_to(cnt[n], s.shape) < MOBA_TOPK, s, NEG)
            s_ref[n * blk:(n + 1) * blk, :] = s
            mx = jnp.maximum(mx, jnp.max(s, axis=0, keepdims=True))
        l = jnp.zeros((1, blk), F32)
        ot = jnp.zeros((HEAD_DIM, blk), F32)
        for n in range(qi + 1):
            p = jnp.exp2((s_ref[n * blk:(n + 1) * blk, :] - mx) * (scale * LOG2_E))
            l = l + jnp.sum(p, axis=0, keepdims=True)
            ot = ot + _dot(vt[:, n * blk:(n + 1) * blk], p.astype(BF16))
        ot = ot * (1.0 / l)
        o_ref[qi * blk:(qi + 1) * blk, :] = ot.T.astype(o_ref.dtype)


def moba_prompt(mq, mk, mv, kmean, batch, seq):
    nblk = seq // MOBA_BLOCK
    assert nblk % SUBLANES == 0
    spec = pl.BlockSpec((seq, HEAD_DIM), lambda b, h: (b, h))
    return pl.pallas_call(
        functools.partial(_moba_prompt_kernel, nblk=nblk),
        grid=(batch, N_HEADS),
        in_specs=[spec, spec, spec, pl.BlockSpec((nblk, HEAD_DIM), lambda b, h: (b, h))],
        out_specs=spec,
        out_shape=jax.ShapeDtypeStruct((batch * seq, GROUP_W), BF16),
        scratch_shapes=[pltpu.VMEM((seq, MOBA_BLOCK), F32)],
        compiler_params=_params("parallel", "parallel"),
        name="moba_prompt",
    )(mq, mk, mv, kmean)


def _page_rows(ref, page):
    return jnp.concatenate([ref[pl.ds(hh, page, stride=N_HEADS), :] for hh in range(N_HEADS)], axis=1)


def _moba_sample_kernel(pt_ref, q_ref, kn_ref, vn_ref, *rest, n_pages, page, t):
    k_refs = rest[:n_pages]
    v_refs = rest[n_pages:2 * n_pages]
    o_ref = rest[2 * n_pages]
    scale = HEAD_DIM ** -0.5
    ppb = MOBA_BLOCK // page
    nblk = n_pages // ppb
    n_pairs = N_HEADS * t

    b = pl.program_id(0)
    batch = q_ref.shape[0] // t
    all_cols = slice(0, GROUP_W)
    group = b // SUBLANES
    row = b % SUBLANES
    q = _seq_tile(_group_rows(q_ref, group, batch, t, all_cols), row, SUBLANES)

    r_i = lax.broadcasted_iota(jnp.int32, (n_pairs, GROUP_W), 0)
    l_i = lax.broadcasted_iota(jnp.int32, (n_pairs, GROUP_W), 1)
    qm = jnp.zeros((n_pairs, GROUP_W), F32)
    for ti in range(t):
        for hh in range(N_HEADS):
            hit = (r_i == hh * t + ti) & (l_i >= hh * HEAD_DIM) & (l_i < (hh + 1) * HEAD_DIM)
            qm = jnp.where(hit, q[ti:ti + 1, :], qm)
    qm = qm.astype(BF16)

    zpad = jnp.zeros((page - SUBLANES, GROUP_W), F32)
    own = lambda ref: jnp.concatenate(
        [_seq_tile(_group_rows(ref, group, batch, t, all_cols), row, SUBLANES), zpad], axis=0).astype(BF16)
    k_own, v_own = own(kn_ref), own(vn_ref)

    b_i = lax.broadcasted_iota(jnp.int32, (page, GROUP_W), 0)
    kmean = jnp.zeros((page, GROUP_W), F32)
    scores = []
    for p in range(n_pages):
        kp = _page_rows(k_refs[p], page)
        scores.append(lax.dot_general(qm, kp.astype(BF16), _NT,
                                      preferred_element_type=F32) * scale)
        ks = jnp.sum(kp, axis=0, keepdims=True) * (1.0 / MOBA_BLOCK)
        kmean = kmean + jnp.where(b_i == p // ppb, ks, 0.0)
    gate = lax.dot_general(qm, kmean.astype(BF16), _NT, preferred_element_type=F32)
    cnt = _beaten_counts([gate[:, m:m + 1] for m in range(nblk)], nblk)
    keep = [jnp.broadcast_to(c, (n_pairs, page)) < MOBA_TOPK for c in cnt]

    s_own = lax.dot_general(qm, k_own, _NT, preferred_element_type=F32) * scale
    key_i = lax.broadcasted_iota(jnp.int32, (n_pairs, page), 1)
    tok_i = lax.rem(lax.broadcasted_iota(jnp.int32, (n_pairs, page), 0), t)
    s_own = jnp.where((key_i <= tok_i) & (key_i < t), s_own, NEG)

    mx = s_own
    for p in range(n_pages):
        scores[p] = jnp.where(keep[p // ppb], scores[p], NEG)
        mx = jnp.maximum(mx, scores[p])
    mx = jnp.max(mx, axis=1, keepdims=True)
    e_own = jnp.exp(s_own - mx)
    l = e_own
    acc = _dot(e_own.astype(BF16), v_own)
    for p in range(n_pages):
        e = jnp.exp(scores[p] - mx)
        l = l + e
        acc = acc + _dot(e.astype(BF16), _page_rows(v_refs[p], page).astype(BF16))
    acc = acc * (1.0 / jnp.sum(l, axis=1, keepdims=True))
    o_ref[...] = jnp.zeros_like(o_ref)
    for hh in range(N_HEADS):
        cols = slice(hh * HEAD_DIM, (hh + 1) * HEAD_DIM)
        o_ref[0:t, cols] = acc[hh * t:(hh + 1) * t, cols]


def moba_sample(mq, mk, mv, cache_k, cache_v, page_table, page_base, page, t):
    batch, n_pages = page_table.shape
    assert (n_pages * page) % MOBA_BLOCK == 0 and t <= SUBLANES
    assert n_pages * page // MOBA_BLOCK <= page
    tok = pl.BlockSpec(mq.shape, lambda b, pt: (0, 0))

    def page_spec(p):
        return pl.BlockSpec((page * N_HEADS, HEAD_DIM), lambda b, pt: (page_base + pt[b, p], 0))

    specs = [tok, tok, tok] + [page_spec(p) for p in range(n_pages)] * 2
    grid_spec = pltpu.PrefetchScalarGridSpec(
        num_scalar_prefetch=1,
        grid=(batch,),
        in_specs=specs,
        out_specs=pl.BlockSpec((None, SUBLANES, GROUP_W), lambda b, pt: (b, 0, 0)),
    )
    return pl.pallas_call(
        functools.partial(_moba_sample_kernel, n_pages=n_pages, page=page, t=t),
        grid_spec=grid_spec,
        out_shape=jax.ShapeDtypeStruct((batch, SUBLANES, GROUP_W), F32),
        compiler_params=_params("parallel"),
        name="moba_sample",
    )(page_table, mq, mk, mv, *([cache_k] * n_pages), *([cache_v] * n_pages))


def _out_proj_kernel(or_ref, om_ref, w_ref, x_ref, gate_ref, g_ref, b_ref, o_ref, *, sub):
    for r0 in range(0, x_ref.shape[0], sub):
        rows = slice(r0, r0 + sub)
        y = (_dot(or_ref[rows, :].astype(BF16), w_ref[:GROUP_W, :])
             + _dot(om_ref[rows, :].astype(BF16), w_ref[GROUP_W:, :]))
        gate = gate_ref[...] if gate_ref.shape[0] == 1 else gate_ref[rows, :]
        v = ALPHA * x_ref[rows, :] + (1.0 + gate) * y
        o_ref[rows, :] = _layer_norm(v, g_ref[...], b_ref[...])


def out_proj(o_r, o_m, w_bf, x2d, gate, ln_g, ln_b, rows_per_group, tm):
    n, d = x2d.shape
    vec = pl.BlockSpec((1, d), lambda i: (0, 0))
    return pl.pallas_call(
        functools.partial(_out_proj_kernel, sub=min(128, tm)),
        grid=(n // tm,),
        in_specs=[pl.BlockSpec((tm, GROUP_W), lambda i: (i, 0)),
                  pl.BlockSpec((tm, GROUP_W), lambda i: (i, 0)),
                  _resident(w_bf),
                  pl.BlockSpec((tm, d), lambda i: (i, 0)),
                  _mod_spec(gate, tm, rows_per_group), vec, vec],
        out_specs=pl.BlockSpec((tm, d), lambda i: (i, 0)),
        out_shape=jax.ShapeDtypeStruct((n, d), F32),
        compiler_params=_params("parallel"),
        name="out_proj",
    )(o_r, o_m, w_bf, x2d, gate, ln_g.reshape(1, d), ln_b.reshape(1, d))


def _run(phases):
    for _ in phases:
        pass


def _ffn_phases(x, sh_ref, sc_ref, gate_ref, wup_ref, wdw_ref, bdw_ref, wdn_ref, g_ref, b_ref,
                o_ref, prev_ref, h_s, act_s, hist_s, *, d_ff, cw, planes):
    tm = x.shape[0]
    h_s[...] = (x * (1.0 + sc_ref[...]) + sh_ref[...]).astype(BF16)
    row = lax.broadcasted_iota(jnp.int32, (tm, cw), 0)
    for c in range(d_ff // cw):
        cu = slice(c * cw, (c + 1) * cw)
        cv = slice(d_ff + c * cw, d_ff + (c + 1) * cw)
        u = _dot(h_s[...], wup_ref[:, cu])
        v = _dot(h_s[...], wup_ref[:, cv])
        if planes:
            s2 = hist_s[0, :, cu]
            s1 = hist_s[1, :, cu]
            hist_s[0, :, cu] = s1
            hist_s[1, :, cu] = u
            prev_ref[:, cu] = u
        else:
            e1 = hist_s[1:2, cu]
            e2 = jnp.where(row == 0, hist_s[0:1, cu], hist_s[1:2, cu])
            hist_s[:, cu] = u[tm - 2:tm, :]
            prev_ref[:, cu] = u[tm - 2:tm, :]
            s1 = jnp.where(row >= 1, pltpu.roll(u, 1, axis=0), e1)
            s2 = jnp.where(row >= 2, pltpu.roll(u, 2, axis=0), e2)
        uc = wdw_ref[0:1, cu] * s2 + wdw_ref[1:2, cu] * s1 + wdw_ref[2:3, cu] * u + bdw_ref[:, cu]
        act = 0.5 * uc * (1.0 + lax.erf(uc * (2.0 ** -0.5))) * v
        act_s[:, cu] = act.astype(BF16)
        yield act[0:1, 0:HEAD_DIM]
    fy = _dot(act_s[...], wdn_ref[...])
    vsum = ALPHA * x + (1.0 + gate_ref[...]) * fy
    o_ref[...] = _layer_norm(vsum, g_ref[...], b_ref[...])
    yield


def _ffn_kernel(*refs, d_ff, cw, tiles_per_seq, planes):
    if planes:
        (x_ref, sh_ref, sc_ref, gate_ref, wup_ref, wdw_ref, bdw_ref, wdn_ref, g_ref, b_ref,
         p0_ref, p1_ref, o_ref, prev_ref, h_s, act_s, hist_s) = refs
    else:
        (x_ref, sh_ref, sc_ref, gate_ref, wup_ref, wdw_ref, bdw_ref, wdn_ref, g_ref, b_ref,
         o_ref, prev_ref, h_s, act_s, hist_s) = refs

    @pl.when(pl.program_id(0) % tiles_per_seq == 0)
    def _():
        if planes:
            hist_s[0] = p0_ref[...]
            hist_s[1] = p1_ref[...]
        else:
            hist_s[...] = jnp.zeros_like(hist_s)

    _run(_ffn_phases(x_ref[...], sh_ref, sc_ref, gate_ref, wup_ref, wdw_ref, bdw_ref, wdn_ref,
                     g_ref, b_ref, o_ref, prev_ref, h_s, act_s, hist_s, d_ff=d_ff, cw=cw, planes=planes))


def conv_ffn(x2d, shift, scale, gate, w_up_bf, w_dw, b_dw, w_dn_bf, layer, ln_g, ln_b,
             rows_per_group, tm, tiles_per_seq, history=None):
    n, d = x2d.shape
    d_ff = w_dn_bf.shape[1]
    cw = 256
    planes = history is not None
    mod = _mod_spec(shift, tm, rows_per_group)
    vec = pl.BlockSpec((1, d), lambda i: (0, 0))
    b_dw2 = b_dw.reshape(1, d_ff)
    in_specs = [pl.BlockSpec((tm, d), lambda i: (i, 0)), mod, mod, mod,
                _resident(w_up_bf, layer), _resident(w_dw), _resident(b_dw2),
                _resident(w_dn_bf, layer), vec, vec]
    args = [x2d, shift, scale, gate, w_up_bf, w_dw, b_dw2, w_dn_bf,
            ln_g.reshape(1, d), ln_b.reshape(1, d)]
    scratch = [pltpu.VMEM((tm, d), BF16), pltpu.VMEM((tm, d_ff), BF16)]
    if planes:
        in_specs += [_resident(history[0]), _resident(history[1])]
        args += list(history)
        prev_spec = pl.BlockSpec((tm, d_ff), lambda i: (i, 0))
        prev_shape = jax.ShapeDtypeStruct((n, d_ff), F32)
        scratch.append(pltpu.VMEM((2, tm, d_ff), F32))
    else:
        prev_spec = pl.BlockSpec((None, 2, d_ff), lambda i: (i, 0, 0))
        prev_shape = jax.ShapeDtypeStruct((n // tm, 2, d_ff), F32)
        scratch.append(pltpu.VMEM((2, d_ff), F32))
    return pl.pallas_call(
        functools.partial(_ffn_kernel, d_ff=d_ff, cw=cw, tiles_per_seq=tiles_per_seq, planes=planes),
        grid=(n // tm,),
        in_specs=in_specs,
        out_specs=[pl.BlockSpec((tm, d), lambda i: (i, 0)), prev_spec],
        out_shape=[jax.ShapeDtypeStruct((n, d), F32), prev_shape],
        scratch_shapes=scratch,
        compiler_params=_params("arbitrary"),
        name="conv_ffn",
    )(*args)


def _glu_kernel(x_ref, sh_ref, sc_ref, w_ref, b_ref, o_ref, *, cw):
    h = (x_ref[...] * (1.0 + sc_ref[...]) + sh_ref[...]).astype(BF16)
    dc = o_ref.shape[1]
    for c in range(dc // cw):
        ca = slice(c * cw, (c + 1) * cw)
        cg = slice(dc + c * cw, dc + (c + 1) * cw)
        a = _dot(h, w_ref[:, ca]) + b_ref[:, ca]
        g = _dot(h, w_ref[:, cg]) + b_ref[:, cg]
        o_ref[:, ca] = a * jax.nn.sigmoid(g)


def conformer_glu(x2d, shift, scale, w1_bf, b1, rows_per_group, tm):
    n, d = x2d.shape
    dc = w1_bf.shape[1] // 2
    mod = _mod_spec(shift, tm, rows_per_group)
    b1r = b1.reshape(1, 2 * dc)
    return pl.pallas_call(
        functools.partial(_glu_kernel, cw=256),
        grid=(n // tm,),
        in_specs=[pl.BlockSpec((tm, d), lambda i: (i, 0)), mod, mod,
                  _resident(w1_bf), _resident(b1r)],
        out_specs=pl.BlockSpec((tm, dc), lambda i: (i, 0)),
        out_shape=jax.ShapeDtypeStruct((n, dc), F32),
        compiler_params=_params("parallel"),
        name="conformer_glu",
    )(x2d, shift, scale, w1_bf, b1r)


PAD_ROWS = 32


def _dwconv_rows(win_ref, w_ref, bias, r0, rs, cols, width):
    off = PAD_ROWS - (width - 1)
    y = bias
    for r in range(SUBLANES):
        z = None
        for m in range((off + width - 1) // SUBLANES + 1):
            k = SUBLANES * m + r - off
            if 0 <= k < width:
                lo = r0 + SUBLANES * m
                term = w_ref[k:k + 1, cols] * win_ref[lo:lo + rs + SUBLANES, cols]
                z = term if z is None else z + term
        if z is not None:
            y = y + z[r:r + rs, :]
    return y


def _ordered_after(x, token):
    zero = lax.shift_right_logical(lax.shift_right_logical(
        lax.bitcast_convert_type(token, jnp.uint32), jnp.uint32(16)), jnp.uint32(16))
    return lax.bitcast_convert_type(lax.bitcast_convert_type(x, jnp.uint32) | zero, F32)


def _conformer_phases(x, sh_ref, sc_ref, gate_ref, w1_ref, b1_ref, wdw_ref, bdw_ref,
                      gl_ref, bl_ref, w2_ref, b2_ref, g_ref, b_ref, o_ref, tail_ref, win_ref, y_ref,
                      *, width, rs, cw, after=None):
    tm, dc = y_ref.shape
    h = (x * (1.0 + sc_ref[...]) + sh_ref[...]).astype(BF16)
    for c in range(dc // cw):
        ca = slice(c * cw, (c + 1) * cw)
        cg = slice(dc + c * cw, dc + (c + 1) * cw)
        a = _dot(h, w1_ref[:, ca]) + b1_ref[:, ca]
        g = _dot(h, w1_ref[:, cg]) + b1_ref[:, cg]
        win_ref[PAD_ROWS:PAD_ROWS + tm, ca] = a * jax.nn.sigmoid(g)
    tail_ref[...] = win_ref[tm:tm + PAD_ROWS, :]
    yield
    for c in range(dc // HEAD_DIM):
        cols = slice(c * HEAD_DIM, (c + 1) * HEAD_DIM)
        for r0 in range(0, tm, rs):
            bias = bdw_ref[:, cols]
            if after is not None and after[0] is not None:
                bias = _ordered_after(bias, after[0])
            y_ref[r0:r0 + rs, cols] = _dwconv_rows(win_ref, wdw_ref, bias, r0, rs, cols, width)
            yield
    win_ref[0:PAD_ROWS, :] = win_ref[tm:tm + PAD_ROWS, :]
    yn = _silu(_layer_norm(y_ref[...], gl_ref[...], bl_ref[...]))
    y = _dot(yn.astype(BF16), w2_ref[...]) + b2_ref[...]
    v = ALPHA * x + (1.0 + gate_ref[...]) * y
    o_ref[...] = _layer_norm(v, g_ref[...], b_ref[...])
    yield


N_CONF_CONSTS = 10
N_FFN_CONSTS = 6


def _conformer_ffn_kernel(*refs, tiles_per_seq, width, rs, cw, d_ff):
    x_ref, shc_ref, scc_ref, gtc_ref, shf_ref, scf_ref, gtf_ref = refs[:7]
    conf = refs[7:7 + N_CONF_CONSTS]
    ffn = refs[7 + N_CONF_CONSTS:7 + N_CONF_CONSTS + N_FFN_CONSTS]
    o_ref, tail_ref, prev_ref, win_ref, y_ref, x1_s, h_s, act_s, hist_s = refs[7 + N_CONF_CONSTS + N_FFN_CONSTS:]
    i = pl.program_id(0)
    n_tiles = pl.num_programs(0) - 1
    tm, dc = y_ref.shape

    @pl.when(i == 0)
    def _():
        x1_s[...] = jnp.zeros_like(x1_s)
        win_ref[PAD_ROWS + tm:, :] = jnp.zeros((SUBLANES, dc), F32)

    @pl.when(jnp.minimum(i, n_tiles - 1) % tiles_per_seq == 0)
    def _():
        win_ref[0:PAD_ROWS, :] = jnp.zeros((PAD_ROWS, dc), F32)

    @pl.when(jnp.maximum(i - 1, 0) % tiles_per_seq == 0)
    def _():
        hist_s[...] = jnp.zeros_like(hist_s)

    ffn_ph = _ffn_phases(x1_s[...], shf_ref, scf_ref, gtf_ref, *ffn, o_ref, prev_ref, h_s, act_s, hist_s,
                         d_ff=d_ff, cw=cw, planes=False)
    token = [None]
    conf_ph = _conformer_phases(x_ref[...], shc_ref, scc_ref, gtc_ref, *conf, x1_s, tail_ref, win_ref, y_ref,
                                width=width, rs=rs, cw=cw, after=token)
    next(conf_ph)
    n_ffn = d_ff // cw
    n_conv = (dc // HEAD_DIM) * (tm // rs)
    per_chunk = -(-n_conv // n_ffn)
    for _ in range(n_ffn):
        token[0] = next(ffn_ph)
        for _ in range(per_chunk):
            next(conf_ph, None)
    _run(ffn_ph)
    _run(conf_ph)


def conformer_ffn_prompt(x2d, mods_conf, mods_ffn, w1_bf, b1, w_dw, b_dw, cf_g, cf_b, w2_bf, b2,
                         ln_g0, ln_b0, w_up_bf, ffn_w_dw, ffn_b_dw, w_dn_bf, layer, ln_g1, ln_b1,
                         batch, seq, tm):
    n, d = x2d.shape
    dc = w2_bf.shape[0]
    d_ff = w_dn_bf.shape[1]
    width = w_dw.shape[0]
    assert width - 1 <= PAD_ROWS and tm % PAD_ROWS == 0
    nj = seq // tm
    n_tiles = n // tm
    row = lambda a: a.reshape(1, -1)
    conf_tile = lambda i: jnp.minimum(i, n_tiles - 1)
    ffn_tile = lambda i: jnp.maximum(i - 1, 0)
    mod_c = pl.BlockSpec((None, 1, d), lambda i: (conf_tile(i) // nj, 0, 0))
    mod_f = pl.BlockSpec((None, 1, d), lambda i: (ffn_tile(i) // nj, 0, 0))
    conf = [w1_bf, row(b1), w_dw, row(b_dw), row(cf_g), row(cf_b), w2_bf, row(b2), row(ln_g0), row(ln_b0)]
    ffn = [ffn_w_dw, row(ffn_b_dw), row(ln_g1), row(ln_b1)]
    assert len(conf) == N_CONF_CONSTS and len(ffn) + 2 == N_FFN_CONSTS
    ffn_specs = [_resident(w_up_bf, layer), _resident(ffn[0]), _resident(ffn[1]),
                 _resident(w_dn_bf, layer), _resident(ffn[2]), _resident(ffn[3])]
    ffn_args = [w_up_bf, ffn[0], ffn[1], w_dn_bf, ffn[2], ffn[3]]
    return pl.pallas_call(
        functools.partial(_conformer_ffn_kernel, tiles_per_seq=nj, width=width, rs=64, cw=256, d_ff=d_ff),
        grid=(n_tiles + 1,),
        in_specs=[pl.BlockSpec((tm, d), lambda i: (conf_tile(i), 0)), mod_c, mod_c, mod_c,
                  mod_f, mod_f, mod_f] + [_resident(a) for a in conf] + ffn_specs,
        out_specs=[pl.BlockSpec((tm, d), lambda i: (ffn_tile(i), 0)),
                   pl.BlockSpec((None, PAD_ROWS, dc), lambda i: (conf_tile(i) // nj, 0, 0)),
                   pl.BlockSpec((None, 2, d_ff), lambda i: (ffn_tile(i), 0, 0))],
        out_shape=[jax.ShapeDtypeStruct((n, d), F32),
                   jax.ShapeDtypeStruct((batch, PAD_ROWS, dc), F32),
                   jax.ShapeDtypeStruct((n_tiles, 2, d_ff), F32)],
        scratch_shapes=[pltpu.VMEM((PAD_ROWS + tm + SUBLANES, dc), F32), pltpu.VMEM((tm, dc), F32),
                        pltpu.VMEM((tm, d), F32), pltpu.VMEM((tm, d), BF16),
                        pltpu.VMEM((tm, d_ff), BF16), pltpu.VMEM((2, d_ff), F32)],
        compiler_params=_params("arbitrary"),
        name="conformer_ffn_prompt",
    )(x2d, *mods_conf, *mods_ffn, *conf, *ffn_args)


def _dwconv_sample_kernel(x_ref, st_ref, w_ref, b_ref, o_ref, nst_ref, *, width, t):
    hist = width - 1
    bb, dc = x_ref.shape[1], x_ref.shape[2]
    plane = lambda j, cols: st_ref[j, :, cols] if j < hist else x_ref[j - hist, :, cols]
    for c in range(dc // HEAD_DIM):
        cols = slice(c * HEAD_DIM, (c + 1) * HEAD_DIM)
        acc = [jnp.zeros((bb, HEAD_DIM), F32) + b_ref[:, cols] for _ in range(t)]
        for j in range(hist + t):
            p = plane(j, cols)
            for ti in range(t):
                if 0 <= j - ti < width:
                    acc[ti] = acc[ti] + w_ref[j - ti:j - ti + 1, cols] * p
            if j >= t:
                nst_ref[j - t, :, cols] = p
        for ti in range(t):
            o_ref[ti, :, cols] = acc[ti]


def dwconv_sample(glu, states, layer, w_dw, b_dw):
    t, batch, dc = glu.shape
    width = w_dw.shape[0]
    hist = width - 1
    bb = 32
    b2 = b_dw.reshape(1, dc)
    return pl.pallas_call(
        functools.partial(_dwconv_sample_kernel, width=width, t=t),
        grid=(batch // bb,),
        in_specs=[pl.BlockSpec((t, bb, dc), lambda i: (0, i, 0)),
                  pl.BlockSpec((None, hist, bb, dc), lambda i: (layer, 0, i, 0)),
                  pl.BlockSpec(w_dw.shape, lambda i: (0, 0)),
                  pl.BlockSpec(b2.shape, lambda i: (0, 0))],
        out_specs=[pl.BlockSpec((t, bb, dc), lambda i: (0, i, 0)),
                   pl.BlockSpec((hist, bb, dc), lambda i: (0, i, 0))],
        out_shape=[jax.ShapeDtypeStruct((t, batch, dc), F32),
                   jax.ShapeDtypeStruct((hist, batch, dc), F32)],
        compiler_params=_params("parallel"),
        name="dwconv_sample",
    )(glu, states, w_dw, b2)


def _conf_tail_kernel(y_ref, gl_ref, bl_ref, w_ref, b2_ref, x_ref, gate_ref, g_ref, b_ref, o_ref):
    yn = _silu(_layer_norm(y_ref[...], gl_ref[...], bl_ref[...]))
    y = _dot(yn.astype(BF16), w_ref[...]) + b2_ref[...]
    v = ALPHA * x_ref[...] + (1.0 + gate_ref[...]) * y
    o_ref[...] = _layer_norm(v, g_ref[...], b_ref[...])


def conformer_tail(y, cf_g, cf_b, w2_bf, b2, x2d, gate, ln_g, ln_b, rows_per_group, tm):
    n, d = x2d.shape
    dc = y.shape[1]
    vecc = pl.BlockSpec((1, dc), lambda i: (0, 0))
    vec = pl.BlockSpec((1, d), lambda i: (0, 0))
    return pl.pallas_call(
        _conf_tail_kernel,
        grid=(n // tm,),
        in_specs=[pl.BlockSpec((tm, dc), lambda i: (i, 0)), vecc, vecc,
                  _resident(w2_bf), vec,
                  pl.BlockSpec((tm, d), lambda i: (i, 0)),
                  _mod_spec(gate, tm, rows_per_group), vec, vec],
        out_specs=pl.BlockSpec((tm, d), lambda i: (i, 0)),
        out_shape=jax.ShapeDtypeStruct((n, d), F32),
        compiler_params=_params("parallel"),
        name="conformer_tail",
    )(y, cf_g.reshape(1, dc), cf_b.reshape(1, dc), w2_bf, b2.reshape(1, d), x2d, gate,
      ln_g.reshape(1, d), ln_b.reshape(1, d))


def kernel(x_prompt, x_sample, cache_k, cache_v, state_ret, state_conv, state_ffn, page_table, c_prompt, c_sample, ab_w_in, ab_w_out, cf_w_pw1, cf_b_pw1, cf_w_dw, cf_b_dw, cf_ln_g, cf_ln_b, cf_w_pw2, cf_b_pw2, ffn_w_up, ffn_w_dw, ffn_b_dw, ffn_w_down, ada_w, ada_b, ln_g, ln_b):
    bp, seq, d = x_prompt.shape
    bs, ts, _ = x_sample.shape
    n_layers = ada_w.shape[0]
    d_ff = ffn_w_down.shape[1]
    past_len = page_table.shape[1] * cache_k.shape[2]
    np_, ns = bp * seq, bs * ts
    tm_p = min(256, seq)
    tm_wide = min(512, seq)
    tm_s = bs

    ada = adaln_all(jnp.concatenate([c_prompt, c_sample], 0), ada_w, ada_b)

    def mods(l, k):
        return ada[l, k, :bp][:, None, :], ada[l, k, bp:][None]

    token_major = lambda a: jnp.swapaxes(a, 0, 1)
    xp = x_prompt.reshape(np_, d)
    xs = token_major(x_sample).reshape(ns, d)
    tabs_p = rotary_tables(jnp.arange(seq, dtype=jnp.int32))
    tabs_s = tuple(jnp.repeat(tb, bs, axis=0) for tb in
                   rotary_tables(past_len + jnp.arange(ts, dtype=jnp.int32)))

    w_up = ffn_w_up.astype(BF16)
    w_dn = ffn_w_down.astype(BF16)
    outs = {k: [] for k in ("kp", "vp", "ks", "vs", "rp", "rs", "cp", "cs", "fp", "fs")}
    for l in range(n_layers):
        i = l // 2
        sh_p, sh_s = mods(l, 0)
        sc_p, sc_s = mods(l, 1)
        gt_p, gt_s = mods(l, 2)
        if l % 2 == 0:
            w_in = ab_w_in[i].astype(BF16)
            w_out = ab_w_out[i].astype(BF16)
            rq, rk, rv, rg, mq, mk, mv, kb, vb, kmean = in_proj(
                xp, sh_p, sc_p, w_in, tabs_p, seq, tm_p, True, BF16)
            o_r, s_p = retention_prompt(rq, rk, rv, rg, bp, seq)
            o_m = moba_prompt(mq, kb, vb, kmean.reshape(-1, GROUP_W), bp, seq)
            xp = out_proj(o_r, o_m, w_out, xp, gt_p, ln_g[l, 0], ln_b[l, 0], seq, tm_wide)
            outs["kp"].append(mk.reshape(bp, seq, N_HEADS, HEAD_DIM))
            outs["vp"].append(mv.reshape(bp, seq, N_HEADS, HEAD_DIM))
            outs["rp"].append(s_p)

            rq, rk, rv, rg, mq, mk, mv, kb, vb = in_proj(xs, sh_s, sc_s, w_in, tabs_s, ns, tm_s, False, F32)
            o_r, s_s = retention_sample(rq, rk, rv, rg, state_ret, i, ts)
            n_phys, page = cache_k.shape[1], cache_k.shape[2]
            ck = cache_k.reshape(-1, HEAD_DIM)
            cv = cache_v.reshape(-1, HEAD_DIM)
            o_m = moba_sample(mq, kb, vb, ck, cv, page_table, i * n_phys, page, ts)
            o_m = token_major(o_m[:, :ts]).reshape(ns, GROUP_W)
            xs = out_proj(o_r, o_m, w_out, xs, gt_s, ln_g[l, 0], ln_b[l, 0], ns, tm_s)
            outs["ks"].append(token_major(mk.reshape(ts, bs, N_HEADS, HEAD_DIM)))
            outs["vs"].append(token_major(mv.reshape(ts, bs, N_HEADS, HEAD_DIM)))
            outs["rs"].append(s_s)
        else:
            w1 = cf_w_pw1[i].astype(BF16)
            w2 = cf_w_pw2[i].astype(BF16)
            hist = cf_w_dw.shape[1] - 1
            mods_ffn_p = tuple(mods(l, k)[0] for k in (3, 4, 5))
            xp, tail, prev_p = conformer_ffn_prompt(
                xp, (sh_p, sc_p, gt_p), mods_ffn_p, w1, cf_b_pw1[i], cf_w_dw[i], cf_b_dw[i],
                cf_ln_g[i], cf_ln_b[i], w2, cf_b_pw2[i], ln_g[l, 0], ln_b[l, 0],
                w_up, ffn_w_dw[l], ffn_b_dw[l], w_dn, l, ln_g[l, 1], ln_b[l, 1], bp, seq, tm_p)
            outs["cp"].append(tail[:, PAD_ROWS - hist:])

            glu = conformer_glu(xs, sh_s, sc_s, w1, cf_b_pw1[i], ns, tm_s)
            y, nst = dwconv_sample(glu.reshape(ts, bs, -1), jnp.swapaxes(state_conv, 1, 2), i,
                                   cf_w_dw[i], cf_b_dw[i])
            xs = conformer_tail(y.reshape(ns, -1), cf_ln_g[i], cf_ln_b[i], w2, cf_b_pw2[i], xs, gt_s,
                                ln_g[l, 0], ln_b[l, 0], ns, tm_s)
            outs["cs"].append(token_major(nst))

        sh_p, sh_s = mods(l, 3)
        sc_p, sc_s = mods(l, 4)
        gt_p, gt_s = mods(l, 5)
        if l % 2 == 0:
            xp, prev_p = conv_ffn(xp, sh_p, sc_p, gt_p, w_up, ffn_w_dw[l], ffn_b_dw[l], w_dn, l,
                                  ln_g[l, 1], ln_b[l, 1], seq, tm_p, seq // tm_p)
        outs["fp"].append(prev_p.reshape(bp, seq // tm_p, 2, d_ff)[:, -1])
        xs, u_s = conv_ffn(xs, sh_s, sc_s, gt_s, w_up, ffn_w_dw[l], ffn_b_dw[l], w_dn, l,
                           ln_g[l, 1], ln_b[l, 1], ns, tm_s, ts,
                           history=(state_ffn[l, :, 0], state_ffn[l, :, 1]))
        outs["fs"].append(token_major(u_s.reshape(ts, bs, d_ff)[ts - 2:]))

    st = lambda k: outs[k][0][None] if len(outs[k]) == 1 else jnp.stack(outs[k])
    return (xp.reshape(bp, seq, d), token_major(xs.reshape(ts, bs, d)), st("kp"), st("vp"), st("ks"), st("vs"),
            st("rp"), st("rs"), st("cp"), st("cs"), st("fp"), st("fs"))
```

```python
import functools

import jax
import jax.numpy as jnp
from jax import lax
from jax.experimental import pallas as pl
from jax.experimental.pallas import tpu as pltpu

HEAD_DIM = 128
N_HEADS = 4
GROUP_W = N_HEADS * HEAD_DIM
N_IN_COLS = 7
RET_CHUNK = 128
MOBA_BLOCK = 256
MOBA_TOPK = 3
ROPE_THETA = 10000.0
DEPTH = 2
ALPHA = (2 * DEPTH) ** 0.25
LN_EPS = 1e-5
GN_EPS = 1e-6
NEG = -1e30
LOG2_E = 1.4426950408889634

F32 = jnp.float32
BF16 = jnp.bfloat16

_NT = (((1,), (1,)), ((), ()))
_TN = (((0,), (0,)), ((), ()))

VMEM_LIMIT = 56 * 1024 * 1024
SUBLANES = 8


def _params(*sem):
    return pltpu.CompilerParams(dimension_semantics=sem, vmem_limit_bytes=VMEM_LIMIT)


def _dot(a, b):
    return jnp.dot(a, b, preferred_element_type=F32)


def _layer_norm(v, g, b):
    mu = jnp.mean(v, axis=-1, keepdims=True)
    d = v - mu
    var = jnp.mean(d * d, axis=-1, keepdims=True)
    return d * lax.rsqrt(var + LN_EPS) * g + b


def _silu(x):
    return x * jax.nn.sigmoid(x)


def _mod_spec(mod, tm, rows_per_group):
    _, r, d = mod.shape
    if r == 1:
        return pl.BlockSpec((None, 1, d), lambda i: (i * tm // rows_per_group, 0, 0))
    assert r % tm == 0
    return pl.BlockSpec((None, tm, d), lambda i: (0, i % (r // tm), 0))


def _resident(a, layer=None):
    if layer is None:
        return pl.BlockSpec(a.shape, lambda *_: (0,) * a.ndim, pipeline_mode=pl.Buffered(1))
    return pl.BlockSpec((None,) + a.shape[1:], lambda *_: (layer,) + (0,) * (a.ndim - 1),
                        pipeline_mode=pl.Buffered(1))


def _adaln_kernel(c_ref, w_ref, b_ref, o_ref):
    a = _silu(c_ref[...]).astype(BF16)
    o_ref[...] = _dot(a, w_ref[...].astype(BF16)) + b_ref[...]


def adaln_all(c_all, ada_w, ada_b):
    n_layers, d, n_out = ada_w.shape
    m = c_all.shape[0]
    return pl.pallas_call(
        _adaln_kernel,
        grid=(n_layers, n_out // d),
        in_specs=[
            pl.BlockSpec((m, d), lambda l, j: (0, 0)),
            pl.BlockSpec((None, d, d), lambda l, j: (l, 0, j)),
            pl.BlockSpec((None, 1, d), lambda l, j: (l, 0, j)),
        ],
        out_specs=pl.BlockSpec((None, None, m, d), lambda l, j: (l, j, 0, 0)),
        out_shape=jax.ShapeDtypeStruct((n_layers, n_out // d, m, d), F32),
        compiler_params=_params("parallel", "parallel"),
        name="adaln",
    )(c_all, ada_w, ada_b.reshape(n_layers, 1, n_out))


def _in_proj_kernel(x_ref, sh_ref, sc_ref, w_ref, cr_ref, sr_ref, cm_ref, sm_ref,
                    rq_ref, rk_ref, rv_ref, rg_ref, mq_ref, mk_ref, mv_ref, kb_ref, vb_ref,
                    *km_refs):
    tm = x_ref.shape[0]
    h = (x_ref[...] * (1.0 + sc_ref[...]) + sh_ref[...]).astype(BF16)
    cr, sr, cm, sm = cr_ref[...], sr_ref[...], cm_ref[...], sm_ref[...]
    even = (lax.broadcasted_iota(jnp.int32, cr.shape, 1) & 1) == 0
    outs = (rq_ref, rk_ref, rv_ref, rg_ref, mq_ref, mk_ref, mv_ref)
    for g, o_ref in enumerate(outs):
        z = _dot(h, w_ref[:, g * GROUP_W:(g + 1) * GROUP_W])
        for hh in range(N_HEADS):
            cols = slice(hh * HEAD_DIM, (hh + 1) * HEAD_DIM)
            zs = z[:, cols]
            if g in (0, 1):
                nxt = pltpu.roll(zs, HEAD_DIM - 1, axis=1)
                prv = pltpu.roll(zs, 1, axis=1)
                zs = zs * cr + jnp.where(even, nxt, prv) * sr
                if g == 1:
                    zs = zs * (HEAD_DIM ** -0.5)
            elif g in (4, 5):
                zs = zs * cm + pltpu.roll(zs, HEAD_DIM // 2, axis=1) * sm
            if g in (5, 6):
                o_ref[pl.ds(hh, tm, stride=N_HEADS), :] = zs
                head_major = kb_ref if g == 5 else vb_ref
                head_major[:, cols] = zs.astype(head_major.dtype)
                if g == 5 and km_refs:
                    for r in range(tm // MOBA_BLOCK):
                        blk = zs[r * MOBA_BLOCK:(r + 1) * MOBA_BLOCK, :]
                        km_refs[0][r:r + 1, cols] = jnp.sum(blk, axis=0, keepdims=True) * (1.0 / MOBA_BLOCK)
            else:
                o_ref[:, cols] = zs.astype(o_ref.dtype)


def in_proj(x2d, shift, scale, w_bf, tabs, rows_per_group, tm, block_means, mxu_dtype):
    n, d = x2d.shape
    p_tiles = tabs[0].shape[0] // tm
    mod_spec = _mod_spec(shift, tm, rows_per_group)
    tab_spec = pl.BlockSpec((tm, HEAD_DIM), lambda i: (i % p_tiles, 0))
    wide = pl.BlockSpec((tm, GROUP_W), lambda i: (i, 0))
    tall = pl.BlockSpec((tm * N_HEADS, HEAD_DIM), lambda i: (i, 0))
    wide_shape = lambda dt: jax.ShapeDtypeStruct((n, GROUP_W), dt)
    tall_shape = jax.ShapeDtypeStruct((n * N_HEADS, HEAD_DIM), F32)
    out_specs = [wide] * 5 + [tall, tall, wide, wide]
    out_shape = [wide_shape(F32), wide_shape(F32), wide_shape(mxu_dtype), wide_shape(F32),
                 wide_shape(mxu_dtype), tall_shape, tall_shape, wide_shape(mxu_dtype), wide_shape(mxu_dtype)]
    if block_means:
        assert tm % MOBA_BLOCK == 0
        per = tm // MOBA_BLOCK
        out_specs.append(pl.BlockSpec((None, per, GROUP_W), lambda i: (i, 0, 0)))
        out_shape.append(jax.ShapeDtypeStruct((n // tm, per, GROUP_W), F32))
    return pl.pallas_call(
        _in_proj_kernel,
        grid=(n // tm,),
        in_specs=[pl.BlockSpec((tm, d), lambda i: (i, 0)), mod_spec, mod_spec,
                  _resident(w_bf), tab_spec, tab_spec, tab_spec, tab_spec],
        out_specs=out_specs,
        out_shape=out_shape,
        compiler_params=_params("parallel"),
        name="in_proj",
    )(x2d, shift, scale, w_bf, *tabs)


def rotary_tables(pos):
    half = HEAD_DIM // 2
    posf = pos.astype(F32)[:, None]
    inv_r = 1.0 / (ROPE_THETA ** jnp.linspace(0.0, 1.0, half, dtype=F32))
    ang_r = posf * inv_r[None, :]
    cr = jnp.repeat(jnp.cos(ang_r), 2, axis=1)
    sr = jnp.stack([-jnp.sin(ang_r), jnp.sin(ang_r)], -1).reshape(-1, HEAD_DIM)
    inv_m = ROPE_THETA ** (-jnp.arange(0, HEAD_DIM, 2, dtype=F32) / HEAD_DIM)
    ang_m = posf * inv_m[None, :]
    cm = jnp.concatenate([jnp.cos(ang_m), jnp.cos(ang_m)], -1)
    sm = jnp.concatenate([-jnp.sin(ang_m), jnp.sin(ang_m)], -1)
    return cr, sr, cm, sm


def retention_tables(chunk, rows):
    log_g = jnp.log1p(-jnp.exp2(-5.0 - jnp.arange(N_HEADS, dtype=F32)))
    idx = jnp.arange(chunk, dtype=F32)
    diff = idx[:, None] - idx[None, :]
    decay_in = jnp.where(diff[None] >= 0,
                         jnp.exp(jnp.maximum(diff, 0.0)[None] * log_g[:, None, None]), 0.0)
    q_dec = jnp.exp((idx + 1.0)[None, :] * log_g[:, None])
    k_dec = jnp.exp((chunk - 1.0 - idx)[None, :] * log_g[:, None])
    c_dec = jnp.exp(chunk * log_g)
    pad = rows - chunk
    decay_in = jnp.pad(decay_in, ((0, 0), (0, pad), (0, rows - chunk)))
    bc = lambda t: jnp.broadcast_to(jnp.pad(t, ((0, 0), (0, pad)))[:, :, None], (N_HEADS, rows, HEAD_DIM))
    c_b = jnp.broadcast_to(c_dec[:, None, None], (N_HEADS, 1, HEAD_DIM))
    return decay_in, bc(q_dec), bc(k_dec), c_b


def _ret_chunk(qc, kc, vc, s, din, qd, kd, cd):
    att = lax.dot_general(qc.astype(BF16), kc.astype(BF16), _NT,
                          preferred_element_type=F32) * din
    o = _dot(att.astype(BF16), vc) + _dot((qc * qd).astype(BF16), s.astype(BF16))
    s_new = cd * s + lax.dot_general((kc * kd).astype(BF16), vc, _TN,
                                     preferred_element_type=F32)
    return o, s_new


def _group_norm_gate(o, g):
    o = o * lax.rsqrt(jnp.mean(o * o, axis=-1, keepdims=True) + GN_EPS)
    return o * _silu(g)


def _ret_prompt_kernel(q_ref, k_ref, v_ref, g_ref, din_ref, qd_ref, kd_ref, cd_ref,
                       o_ref, sout_ref, s_ref, *, n_sub):
    j = pl.program_id(1)

    @pl.when(j == 0)
    def _():
        s_ref[...] = jnp.zeros_like(s_ref)

    for hh in range(N_HEADS):
        cols = slice(hh * HEAD_DIM, (hh + 1) * HEAD_DIM)
        din, qd, kd, cd = din_ref[hh], qd_ref[hh], kd_ref[hh], cd_ref[hh]
        for c in range(n_sub):
            rows = slice(c * RET_CHUNK, (c + 1) * RET_CHUNK)
            o, s_new = _ret_chunk(q_ref[rows, cols], k_ref[rows, cols], v_ref[rows, cols],
                                  s_ref[hh], din, qd, kd, cd)
            s_ref[hh] = s_new
            o_ref[rows, cols] = _group_norm_gate(o, g_ref[rows, cols]).astype(o_ref.dtype)

    @pl.when(j == pl.num_programs(1) - 1)
    def _():
        sout_ref[...] = s_ref[...]


def retention_prompt(rq, rk, rv, rg, batch, seq):
    tc = min(4 * RET_CHUNK, seq)
    nj = seq // tc
    tabs = retention_tables(RET_CHUNK, RET_CHUNK)
    tok = pl.BlockSpec((tc, GROUP_W), lambda b, j: (b * nj + j, 0))
    tab = pl.BlockSpec((N_HEADS, RET_CHUNK, HEAD_DIM), lambda b, j: (0, 0, 0))
    return pl.pallas_call(
        functools.partial(_ret_prompt_kernel, n_sub=tc // RET_CHUNK),
        grid=(batch, nj),
        in_specs=[tok, tok, tok, tok, tab, tab, tab,
                  pl.BlockSpec((N_HEADS, 1, HEAD_DIM), lambda b, j: (0, 0, 0))],
        out_specs=[tok, pl.BlockSpec((None, N_HEADS, HEAD_DIM, HEAD_DIM), lambda b, j: (b, 0, 0, 0))],
        out_shape=[jax.ShapeDtypeStruct((batch * seq, GROUP_W), BF16),
                   jax.ShapeDtypeStruct((batch, N_HEADS, HEAD_DIM, HEAD_DIM), F32)],
        scratch_shapes=[pltpu.VMEM((N_HEADS, HEAD_DIM, HEAD_DIM), F32)],
        compiler_params=_params("parallel", "arbitrary"),
        name="retention_prompt",
    )(rq, rk, rv, rg, *tabs)


def _group_rows(ref, group, batch, t, cols):
    return [ref[pl.ds(pl.multiple_of(ti * batch + group * SUBLANES, SUBLANES), SUBLANES), cols]
            for ti in range(t)]


def _seq_tile(token_rows, row, n_rows):
    width = token_rows[0].shape[1]
    r_i = lax.broadcasted_iota(jnp.int32, (n_rows, width), 0)
    g_i = lax.broadcasted_iota(jnp.int32, token_rows[0].shape, 0)
    out = jnp.zeros((n_rows, width), F32)
    for ti, rows in enumerate(token_rows):
        if isinstance(row, int):
            picked = rows[row:row + 1, :]
        else:
            picked = jnp.sum(jnp.where(g_i == row, rows, 0.0), axis=0, keepdims=True)
        out = jnp.where(r_i == ti, picked, out)
    return out


def _ret_sample_kernel(q_ref, k_ref, v_ref, g_ref, s0_ref, din_ref, qd_ref, kd_ref, cd_ref,
                       o_ref, sout_ref, *, t, batch):
    group = pl.program_id(0)
    r_i = lax.broadcasted_iota(jnp.int32, (SUBLANES, HEAD_DIM), 0)
    for hh in range(N_HEADS):
        cols = slice(hh * HEAD_DIM, (hh + 1) * HEAD_DIM)
        din, qd, kd, cd = din_ref[hh], qd_ref[hh], kd_ref[hh], cd_ref[hh]
        q_t, k_t, v_t, g_t = (_group_rows(r, group, batch, t, cols) for r in (q_ref, k_ref, v_ref, g_ref))
        out_t = [jnp.zeros((SUBLANES, HEAD_DIM), F32) for _ in range(t)]
        for bl in range(SUBLANES):
            qc = _seq_tile(q_t, bl, RET_CHUNK)
            kc = _seq_tile(k_t, bl, RET_CHUNK)
            vc = _seq_tile(v_t, bl, RET_CHUNK).astype(BF16)
            o, s_new = _ret_chunk(qc, kc, vc, s0_ref[bl, hh], din, qd, kd, cd)
            sout_ref[bl, hh] = s_new
            gated = _group_norm_gate(o[0:SUBLANES, :], _seq_tile(g_t, bl, SUBLANES))
            for ti in range(t):
                out_t[ti] = jnp.where(r_i == bl, gated[ti:ti + 1, :], out_t[ti])
        for ti in range(t):
            o_ref[pl.ds(pl.multiple_of(ti * batch + group * SUBLANES, SUBLANES), SUBLANES), cols] = out_t[ti]


def retention_sample(rq, rk, rv, rg, states, layer, t):
    batch = states.shape[1]
    bb = SUBLANES
    assert batch % bb == 0
    tabs = retention_tables(t, RET_CHUNK)
    tok = pl.BlockSpec(rq.shape, lambda i: (0, 0))
    st_in = pl.BlockSpec((None, bb, N_HEADS, HEAD_DIM, HEAD_DIM), lambda i: (layer, i, 0, 0, 0))
    st = pl.BlockSpec((bb, N_HEADS, HEAD_DIM, HEAD_DIM), lambda i: (i, 0, 0, 0))
    tab = pl.BlockSpec((N_HEADS, RET_CHUNK, HEAD_DIM), lambda i: (0, 0, 0))
    return pl.pallas_call(
        functools.partial(_ret_sample_kernel, t=t, batch=batch),
        grid=(batch // bb,),
        in_specs=[tok, tok, tok, tok, st_in, tab, tab, tab,
                  pl.BlockSpec((N_HEADS, 1, HEAD_DIM), lambda i: (0, 0, 0))],
        out_specs=[tok, st],
        out_shape=[jax.ShapeDtypeStruct(rq.shape, F32),
                   jax.ShapeDtypeStruct(states.shape[1:], F32)],
        compiler_params=_params("arbitrary"),
        name="retention_sample",
    )(rq, rk, rv, rg, states, *tabs)


def _beaten_counts(rows, n_valid):
    counts = []
    for n in range(n_valid):
        cnt = jnp.zeros(rows[n].shape, jnp.int32)
        for m in range(n_valid):
            if m == n:
                continue
            beats = (rows[m] >= rows[n]) if m < n else (rows[m] > rows[n])
            cnt = cnt + beats.astype(jnp.int32)
        counts.append(cnt)
    return counts


def _moba_prompt_kernel(q_ref, k_ref, v_ref, km_ref, o_ref, s_ref, *, nblk):
    blk = MOBA_BLOCK
    scale = HEAD_DIM ** -0.5
    kb = k_ref[...]
    vt = v_ref[...].astype(F32).T.astype(BF16)
    kmean = km_ref[...].astype(BF16)
    key_i = lax.broadcasted_iota(jnp.int32, (blk, blk), 0)
    qry_i = lax.broadcasted_iota(jnp.int32, (blk, blk), 1)
    causal = key_i <= qry_i

    for qi in range(nblk):
        qb = q_ref[qi * blk:(qi + 1) * blk, :]
        cnt = None
        if qi > MOBA_TOPK:
            gt = lax.dot_general(kmean, qb, _NT, preferred_element_type=F32)
            cnt = _beaten_counts([gt[m:m + 1, :] for m in range(qi)], qi)
        mx = jnp.full((1, blk), NEG, F32)
        for n in range(qi + 1):
            s = lax.dot_general(kb[n * blk:(n + 1) * blk], qb, _NT,
                                preferred_element_type=F32)
            if n == qi:
                s = jnp.where(causal, s, NEG)
            elif cnt is not None:
                s = jnp.where(jnp.broadcast_to(cnt[n], s.shape) < MOBA_TOPK, s, NEG)
            s_ref[n * blk:(n + 1) * blk, :] = s
            mx = jnp.maximum(mx, jnp.max(s, axis=0, keepdims=True))
        l = jnp.zeros((1, blk), F32)
        ot = jnp.zeros((HEAD_DIM, blk), F32)
        for n in range(qi + 1):
            p = jnp.exp2((s_ref[n * blk:(n + 1) * blk, :] - mx) * (scale * LOG2_E))
            l = l + jnp.sum(p, axis=0, keepdims=True)
            ot = ot + _dot(vt[:, n * blk:(n + 1) * blk], p.astype(BF16))
        ot = ot * (1.0 / l)
        o_ref[qi * blk:(qi + 1) * blk, :] = ot.T.astype(o_ref.dtype)


def moba_prompt(mq, mk, mv, kmean, batch, seq):
    nblk = seq // MOBA_BLOCK
    assert nblk % SUBLANES == 0
    spec = pl.BlockSpec((seq, HEAD_DIM), lambda b, h: (b, h))
    return pl.pallas_call(
        functools.partial(_moba_prompt_kernel, nblk=nblk),
        grid=(batch, N_HEADS),
        in_specs=[spec, spec, spec, pl.BlockSpec((nblk, HEAD_DIM), lambda b, h: (b, h))],
        out_specs=spec,
        out_shape=jax.ShapeDtypeStruct((batch * seq, GROUP_W), BF16),
        scratch_shapes=[pltpu.VMEM((seq, MOBA_BLOCK), F32)],
        compiler_params=_params("parallel", "parallel"),
        name="moba_prompt",
    )(mq, mk, mv, kmean)


def _page_rows(ref, page):
    return jnp.concatenate([ref[pl.ds(hh, page, stride=N_HEADS), :] for hh in range(N_HEADS)], axis=1)


def _moba_sample_phases(b, q_ref, kn_ref, vn_ref, k_refs, v_refs, o_ref, *, page, t):
    n_pages = len(k_refs)
    scale = HEAD_DIM ** -0.5
    ppb = MOBA_BLOCK // page
    nblk = n_pages // ppb
    n_pairs = N_HEADS * t
    batch = q_ref.shape[0] // t
    all_cols = slice(0, GROUP_W)
    group = b // SUBLANES
    row = b % SUBLANES
    q = _seq_tile(_group_rows(q_ref, group, batch, t, all_cols), row, SUBLANES)

    r_i = lax.broadcasted_iota(jnp.int32, (n_pairs, GROUP_W), 0)
    l_i = lax.broadcasted_iota(jnp.int32, (n_pairs, GROUP_W), 1)
    qm = jnp.zeros((n_pairs, GROUP_W), F32)
    for ti in range(t):
        for hh in range(N_HEADS):
            hit = (r_i == hh * t + ti) & (l_i >= hh * HEAD_DIM) & (l_i < (hh + 1) * HEAD_DIM)
            qm = jnp.where(hit, q[ti:ti + 1, :], qm)
    qm = qm.astype(BF16)

    zpad = jnp.zeros((page - SUBLANES, GROUP_W), F32)
    own = lambda ref: jnp.concatenate(
        [_seq_tile(_group_rows(ref, group, batch, t, all_cols), row, SUBLANES), zpad], axis=0).astype(BF16)
    k_own, v_own = own(kn_ref), own(vn_ref)

    b_i = lax.broadcasted_iota(jnp.int32, (page, GROUP_W), 0)
    kmean = jnp.zeros((page, GROUP_W), F32)
    scores = []
    for p in range(n_pages):
        kp = _page_rows(k_refs[p], page)
        scores.append(lax.dot_general(qm, kp.astype(BF16), _NT,
                                      preferred_element_type=F32) * scale)
        ks = jnp.sum(kp, axis=0, keepdims=True) * (1.0 / MOBA_BLOCK)
        kmean = kmean + jnp.where(b_i == p // ppb, ks, 0.0)
        yield
    gate = lax.dot_general(qm, kmean.astype(BF16), _NT, preferred_element_type=F32)
    cnt = _beaten_counts([gate[:, m:m + 1] for m in range(nblk)], nblk)
    keep = [jnp.broadcast_to(c, (n_pairs, page)) < MOBA_TOPK for c in cnt]

    s_own = lax.dot_general(qm, k_own, _NT, preferred_element_type=F32) * scale
    key_i = lax.broadcasted_iota(jnp.int32, (n_pairs, page), 1)
    tok_i = lax.rem(lax.broadcasted_iota(jnp.int32, (n_pairs, page), 0), t)
    s_own = jnp.where((key_i <= tok_i) & (key_i < t), s_own, NEG)

    mx = s_own
    for p in range(n_pages):
        scores[p] = jnp.where(keep[p // ppb], scores[p], NEG)
        mx = jnp.maximum(mx, scores[p])
    mx = jnp.max(mx, axis=1, keepdims=True)
    e_own = jnp.exp(s_own - mx)
    l = e_own
    probs = []
    for p in range(n_pages):
        e = jnp.exp(scores[p] - mx)
        l = l + e
        probs.append(e.astype(BF16))
    inv_l = 1.0 / jnp.sum(l, axis=1, keepdims=True)
    yield
    acc = _dot(e_own.astype(BF16), v_own)
    for p in range(n_pages):
        acc = acc + _dot(probs[p], _page_rows(v_refs[p], page).astype(BF16))
        yield
    acc = acc * inv_l
    o_ref[...] = jnp.zeros_like(o_ref)
    for hh in range(N_HEADS):
        cols = slice(hh * HEAD_DIM, (hh + 1) * HEAD_DIM)
        o_ref[0:t, cols] = acc[hh * t:(hh + 1) * t, cols]
    yield


def _out_proj_kernel(or_ref, om_ref, w_ref, x_ref, gate_ref, g_ref, b_ref, o_ref, *, sub):
    for r0 in range(0, x_ref.shape[0], sub):
        rows = slice(r0, r0 + sub)
        y = (_dot(or_ref[rows, :].astype(BF16), w_ref[:GROUP_W, :])
             + _dot(om_ref[rows, :].astype(BF16), w_ref[GROUP_W:, :]))
        gate = gate_ref[...] if gate_ref.shape[0] == 1 else gate_ref[rows, :]
        v = ALPHA * x_ref[rows, :] + (1.0 + gate) * y
        o_ref[rows, :] = _layer_norm(v, g_ref[...], b_ref[...])


def out_proj(o_r, o_m, w_bf, x2d, gate, ln_g, ln_b, rows_per_group, tm):
    n, d = x2d.shape
    vec = pl.BlockSpec((1, d), lambda i: (0, 0))
    return pl.pallas_call(
        functools.partial(_out_proj_kernel, sub=min(128, tm)),
        grid=(n // tm,),
        in_specs=[pl.BlockSpec((tm, GROUP_W), lambda i: (i, 0)),
                  pl.BlockSpec((tm, GROUP_W), lambda i: (i, 0)),
                  _resident(w_bf),
                  pl.BlockSpec((tm, d), lambda i: (i, 0)),
                  _mod_spec(gate, tm, rows_per_group), vec, vec],
        out_specs=pl.BlockSpec((tm, d), lambda i: (i, 0)),
        out_shape=jax.ShapeDtypeStruct((n, d), F32),
        compiler_params=_params("parallel"),
        name="out_proj",
    )(o_r, o_m, w_bf, x2d, gate, ln_g.reshape(1, d), ln_b.reshape(1, d))


def _run(phases):
    for _ in phases:
        pass


def _ffn_phases(x, sh_ref, sc_ref, gate_ref, wup_ref, wdw_ref, bdw_ref, wdn_ref, g_ref, b_ref,
                o_ref, prev_ref, h_s, act_s, hist_s, *, d_ff, cw, planes, chunks=None):
    tm = x.shape[0]
    n_chunks = d_ff // cw
    chunks = range(n_chunks) if chunks is None else chunks
    if chunks.start == 0:
        h_s[...] = (x * (1.0 + sc_ref[...]) + sh_ref[...]).astype(BF16)
    row = lax.broadcasted_iota(jnp.int32, (tm, cw), 0)
    for c in chunks:
        cu = slice(c * cw, (c + 1) * cw)
        cv = slice(d_ff + c * cw, d_ff + (c + 1) * cw)
        u = _dot(h_s[...], wup_ref[:, cu])
        v = _dot(h_s[...], wup_ref[:, cv])
        if planes:
            s2 = hist_s[0, :, cu]
            s1 = hist_s[1, :, cu]
            hist_s[0, :, cu] = s1
            hist_s[1, :, cu] = u
            prev_ref[:, cu] = u
        else:
            e1 = hist_s[1:2, cu]
            e2 = jnp.where(row == 0, hist_s[0:1, cu], hist_s[1:2, cu])
            hist_s[:, cu] = u[tm - 2:tm, :]
            prev_ref[:, cu] = u[tm - 2:tm, :]
            s1 = jnp.where(row >= 1, pltpu.roll(u, 1, axis=0), e1)
            s2 = jnp.where(row >= 2, pltpu.roll(u, 2, axis=0), e2)
        uc = wdw_ref[0:1, cu] * s2 + wdw_ref[1:2, cu] * s1 + wdw_ref[2:3, cu] * u + bdw_ref[:, cu]
        act = 0.5 * uc * (1.0 + lax.erf(uc * (2.0 ** -0.5))) * v
        act_s[:, cu] = act.astype(BF16)
        yield act[0:1, 0:HEAD_DIM]
    if chunks.stop != n_chunks:
        return
    fy = _dot(act_s[...], wdn_ref[...])
    vsum = ALPHA * x + (1.0 + gate_ref[...]) * fy
    o_ref[...] = _layer_norm(vsum, g_ref[...], b_ref[...])
    yield


def _ffn_kernel(*refs, d_ff, cw, tiles_per_seq, planes):
    if planes:
        (x_ref, sh_ref, sc_ref, gate_ref, wup_ref, wdw_ref, bdw_ref, wdn_ref, g_ref, b_ref,
         p0_ref, p1_ref, o_ref, prev_ref, h_s, act_s, hist_s) = refs
    else:
        (x_ref, sh_ref, sc_ref, gate_ref, wup_ref, wdw_ref, bdw_ref, wdn_ref, g_ref, b_ref,
         o_ref, prev_ref, h_s, act_s, hist_s) = refs

    @pl.when(pl.program_id(0) % tiles_per_seq == 0)
    def _():
        if planes:
            hist_s[0] = p0_ref[...]
            hist_s[1] = p1_ref[...]
        else:
            hist_s[...] = jnp.zeros_like(hist_s)

    _run(_ffn_phases(x_ref[...], sh_ref, sc_ref, gate_ref, wup_ref, wdw_ref, bdw_ref, wdn_ref,
                     g_ref, b_ref, o_ref, prev_ref, h_s, act_s, hist_s, d_ff=d_ff, cw=cw, planes=planes))


def conv_ffn(x2d, shift, scale, gate, w_up_bf, w_dw, b_dw, w_dn_bf, layer, ln_g, ln_b,
             rows_per_group, tm, tiles_per_seq, history=None):
    n, d = x2d.shape
    d_ff = w_dn_bf.shape[1]
    cw = 256
    planes = history is not None
    mod = _mod_spec(shift, tm, rows_per_group)
    vec = pl.BlockSpec((1, d), lambda i: (0, 0))
    b_dw2 = b_dw.reshape(1, d_ff)
    in_specs = [pl.BlockSpec((tm, d), lambda i: (i, 0)), mod, mod, mod,
                _resident(w_up_bf, layer), _resident(w_dw), _resident(b_dw2),
                _resident(w_dn_bf, layer), vec, vec]
    args = [x2d, shift, scale, gate, w_up_bf, w_dw, b_dw2, w_dn_bf,
            ln_g.reshape(1, d), ln_b.reshape(1, d)]
    scratch = [pltpu.VMEM((tm, d), BF16), pltpu.VMEM((tm, d_ff), BF16)]
    if planes:
        in_specs += [_resident(history[0]), _resident(history[1])]
        args += list(history)
        prev_spec = pl.BlockSpec((tm, d_ff), lambda i: (i, 0))
        prev_shape = jax.ShapeDtypeStruct((n, d_ff), F32)
        scratch.append(pltpu.VMEM((2, tm, d_ff), F32))
    else:
        prev_spec = pl.BlockSpec((None, 2, d_ff), lambda i: (i, 0, 0))
        prev_shape = jax.ShapeDtypeStruct((n // tm, 2, d_ff), F32)
        scratch.append(pltpu.VMEM((2, d_ff), F32))
    return pl.pallas_call(
        functools.partial(_ffn_kernel, d_ff=d_ff, cw=cw, tiles_per_seq=tiles_per_seq, planes=planes),
        grid=(n // tm,),
        in_specs=in_specs,
        out_specs=[pl.BlockSpec((tm, d), lambda i: (i, 0)), prev_spec],
        out_shape=[jax.ShapeDtypeStruct((n, d), F32), prev_shape],
        scratch_shapes=scratch,
        compiler_params=_params("arbitrary"),
        name="conv_ffn",
    )(*args)


def _moba_ffn_kernel(pt_ref, q_ref, kn_ref, vn_ref, *rest, n_pages, page, t, d_ff, cw, parts, tiles_per_seq):
    k_refs = rest[:n_pages]
    v_refs = rest[n_pages:2 * n_pages]
    (x_ref, sh_ref, sc_ref, gate_ref, wup_ref, wdw_ref, bdw_ref, wdn_ref, g_ref, b_ref,
     om_ref, o_ref, prev_ref, h_s, act_s, hist_s) = rest[2 * n_pages:]
    i = pl.program_id(0)
    n_chunks = d_ff // cw
    per_part = -(-n_chunks // parts)

    @pl.when(i % (parts * tiles_per_seq) == 0)
    def _():
        hist_s[...] = jnp.zeros_like(hist_s)

    def step(part):
        chunks = range(part * per_part, min((part + 1) * per_part, n_chunks))
        ffn_ph = _ffn_phases(x_ref[...], sh_ref, sc_ref, gate_ref, wup_ref, wdw_ref, bdw_ref, wdn_ref,
                             g_ref, b_ref, o_ref, prev_ref, h_s, act_s, hist_s,
                             d_ff=d_ff, cw=cw, planes=False, chunks=chunks)
        moba_ph = _moba_sample_phases(i, q_ref, kn_ref, vn_ref, k_refs, v_refs, om_ref, page=page, t=t)
        for _ in range(n_pages):
            next(moba_ph)
        next(ffn_ph)
        next(moba_ph)
        for _ in range(len(chunks) - 1):
            next(ffn_ph)
        for _ in range(n_pages + 1):
            next(moba_ph)
        _run(ffn_ph)

    for part in range(parts):
        pl.when(i % parts == part)(functools.partial(step, part))


def moba_sample_ffn_prompt(mq, mk, mv, cache_k, cache_v, page_table, page_base, page, t,
                           x2d, shift, scale, gate, w_up_bf, w_dw, b_dw, w_dn_bf, layer, ln_g, ln_b,
                           seq, tm):
    batch, n_pages = page_table.shape
    n, d = x2d.shape
    d_ff = w_dn_bf.shape[1]
    assert (n_pages * page) % MOBA_BLOCK == 0 and t <= SUBLANES
    assert n_pages * page // MOBA_BLOCK <= page
    n_tiles = n // tm
    assert batch % n_tiles == 0
    parts = batch // n_tiles
    tok = pl.BlockSpec(mq.shape, lambda i, pt: (0, 0))

    def page_spec(p):
        return pl.BlockSpec((page * N_HEADS, HEAD_DIM), lambda i, pt: (page_base + pt[i, p], 0))

    const = lambda a, lead=None: pl.BlockSpec(
        a.shape if lead is None else (None,) + a.shape[1:],
        (lambda i, pt: (0,) * a.ndim) if lead is None else (lambda i, pt: (lead,) + (0,) * (a.ndim - 1)),
        pipeline_mode=pl.Buffered(1))
    mod = pl.BlockSpec((None, 1, d), lambda i, pt: ((i // parts) * tm // seq, 0, 0))
    b_dw2, g2, b2 = b_dw.reshape(1, d_ff), ln_g.reshape(1, d), ln_b.reshape(1, d)
    specs = ([tok, tok, tok] + [page_spec(p) for p in range(n_pages)] * 2
             + [pl.BlockSpec((tm, d), lambda i, pt: (i // parts, 0)), mod, mod, mod,
                const(w_up_bf, layer), const(w_dw), const(b_dw2), const(w_dn_bf, layer), const(g2), const(b2)])
    grid_spec = pltpu.PrefetchScalarGridSpec(
        num_scalar_prefetch=1,
        grid=(batch,),
        in_specs=specs,
        out_specs=[pl.BlockSpec((None, SUBLANES, GROUP_W), lambda i, pt: (i, 0, 0)),
                   pl.BlockSpec((tm, d), lambda i, pt: (i // parts, 0)),
                   pl.BlockSpec((None, 2, d_ff), lambda i, pt: (i // parts, 0, 0))],
        scratch_shapes=[pltpu.VMEM((tm, d), BF16), pltpu.VMEM((tm, d_ff), BF16), pltpu.VMEM((2, d_ff), F32)],
    )
    return pl.pallas_call(
        functools.partial(_moba_ffn_kernel, n_pages=n_pages, page=page, t=t, d_ff=d_ff, cw=256,
                          parts=parts, tiles_per_seq=seq // tm),
        grid_spec=grid_spec,
        out_shape=[jax.ShapeDtypeStruct((batch, SUBLANES, GROUP_W), F32),
                   jax.ShapeDtypeStruct((n, d), F32),
                   jax.ShapeDtypeStruct((n_tiles, 2, d_ff), F32)],
        compiler_params=_params("arbitrary"),
        name="moba_sample_ffn_prompt",
    )(page_table, mq, mk, mv, *([cache_k] * n_pages), *([cache_v] * n_pages),
      x2d, shift, scale, gate, w_up_bf, w_dw, b_dw2, w_dn_bf, g2, b2)


def _glu_kernel(x_ref, sh_ref, sc_ref, w_ref, b_ref, o_ref, *, cw):
    h = (x_ref[...] * (1.0 + sc_ref[...]) + sh_ref[...]).astype(BF16)
    dc = o_ref.shape[1]
    for c in range(dc // cw):
        ca = slice(c * cw, (c + 1) * cw)
        cg = slice(dc + c * cw, dc + (c + 1) * cw)
        a = _dot(h, w_ref[:, ca]) + b_ref[:, ca]
        g = _dot(h, w_ref[:, cg]) + b_ref[:, cg]
        o_ref[:, ca] = a * jax.nn.sigmoid(g)


def conformer_glu(x2d, shift, scale, w1_bf, b1, rows_per_group, tm):
    n, d = x2d.shape
    dc = w1_bf.shape[1] // 2
    mod = _mod_spec(shift, tm, rows_per_group)
    b1r = b1.reshape(1, 2 * dc)
    return pl.pallas_call(
        functools.partial(_glu_kernel, cw=256),
        grid=(n // tm,),
        in_specs=[pl.BlockSpec((tm, d), lambda i: (i, 0)), mod, mod,
                  _resident(w1_bf), _resident(b1r)],
        out_specs=pl.BlockSpec((tm, dc), lambda i: (i, 0)),
        out_shape=jax.ShapeDtypeStruct((n, dc), F32),
        compiler_params=_params("parallel"),
        name="conformer_glu",
    )(x2d, shift, scale, w1_bf, b1r)


PAD_ROWS = 32


def _dwconv_rows(win_ref, w_ref, bias, r0, rs, cols, width):
    off = PAD_ROWS - (width - 1)
    y = bias
    for r in range(SUBLANES):
        z = None
        for m in range((off + width - 1) // SUBLANES + 1):
            k = SUBLANES * m + r - off
            if 0 <= k < width:
                lo = r0 + SUBLANES * m
                term = w_ref[k:k + 1, cols] * win_ref[lo:lo + rs + SUBLANES, cols]
                z = term if z is None else z + term
        if z is not None:
            y = y + z[r:r + rs, :]
    return y


def _ordered_after(x, token):
    zero = lax.shift_right_logical(lax.shift_right_logical(
        lax.bitcast_convert_type(token, jnp.uint32), jnp.uint32(16)), jnp.uint32(16))
    return lax.bitcast_convert_type(lax.bitcast_convert_type(x, jnp.uint32) | zero, F32)


def _conformer_phases(x, sh_ref, sc_ref, gate_ref, w1_ref, b1_ref, wdw_ref, bdw_ref,
                      gl_ref, bl_ref, w2_ref, b2_ref, g_ref, b_ref, o_ref, tail_ref, win_ref, y_ref,
                      *, width, rs, cw, after=None):
    tm, dc = y_ref.shape
    h = (x * (1.0 + sc_ref[...]) + sh_ref[...]).astype(BF16)
    for c in range(dc // cw):
        ca = slice(c * cw, (c + 1) * cw)
        cg = slice(dc + c * cw, dc + (c + 1) * cw)
        a = _dot(h, w1_ref[:, ca]) + b1_ref[:, ca]
        g = _dot(h, w1_ref[:, cg]) + b1_ref[:, cg]
        win_ref[PAD_ROWS:PAD_ROWS + tm, ca] = a * jax.nn.sigmoid(g)
    tail_ref[...] = win_ref[tm:tm + PAD_ROWS, :]
    yield
    for c in range(dc // HEAD_DIM):
        cols = slice(c * HEAD_DIM, (c + 1) * HEAD_DIM)
        for r0 in range(0, tm, rs):
            bias = bdw_ref[:, cols]
            if after is not None and after[0] is not None:
                bias = _ordered_after(bias, after[0])
            y_ref[r0:r0 + rs, cols] = _dwconv_rows(win_ref, wdw_ref, bias, r0, rs, cols, width)
            yield
    win_ref[0:PAD_ROWS, :] = win_ref[tm:tm + PAD_ROWS, :]
    yn = _silu(_layer_norm(y_ref[...], gl_ref[...], bl_ref[...]))
    y = _dot(yn.astype(BF16), w2_ref[...]) + b2_ref[...]
    v = ALPHA * x + (1.0 + gate_ref[...]) * y
    o_ref[...] = _layer_norm(v, g_ref[...], b_ref[...])
    yield


N_CONF_CONSTS = 10
N_FFN_CONSTS = 6


def _conformer_ffn_kernel(*refs, tiles_per_seq, width, rs, cw, d_ff):
    x_ref, shc_ref, scc_ref, gtc_ref, shf_ref, scf_ref, gtf_ref = refs[:7]
    conf = refs[7:7 + N_CONF_CONSTS]
    ffn = refs[7 + N_CONF_CONSTS:7 + N_CONF_CONSTS + N_FFN_CONSTS]
    o_ref, tail_ref, prev_ref, win_ref, y_ref, x1_s, h_s, act_s, hist_s = refs[7 + N_CONF_CONSTS + N_FFN_CONSTS:]
    i = pl.program_id(0)
    n_tiles = pl.num_programs(0) - 1
    tm, dc = y_ref.shape

    @pl.when(i == 0)
    def _():
        x1_s[...] = jnp.zeros_like(x1_s)
        win_ref[PAD_ROWS + tm:, :] = jnp.zeros((SUBLANES, dc), F32)

    @pl.when(jnp.minimum(i, n_tiles - 1) % tiles_per_seq == 0)
    def _():
        win_ref[0:PAD_ROWS, :] = jnp.zeros((PAD_ROWS, dc), F32)

    @pl.when(jnp.maximum(i - 1, 0) % tiles_per_seq == 0)
    def _():
        hist_s[...] = jnp.zeros_like(hist_s)

    ffn_ph = _ffn_phases(x1_s[...], shf_ref, scf_ref, gtf_ref, *ffn, o_ref, prev_ref, h_s, act_s, hist_s,
                         d_ff=d_ff, cw=cw, planes=False)
    token = [None]
    conf_ph = _conformer_phases(x_ref[...], shc_ref, scc_ref, gtc_ref, *conf, x1_s, tail_ref, win_ref, y_ref,
                                width=width, rs=rs, cw=cw, after=token)
    next(conf_ph)
    n_ffn = d_ff // cw
    n_conv = (dc // HEAD_DIM) * (tm // rs)
    per_chunk = -(-n_conv // n_ffn)
    for _ in range(n_ffn):
        token[0] = next(ffn_ph)
        for _ in range(per_chunk):
            next(conf_ph, None)
    _run(ffn_ph)
    _run(conf_ph)


def conformer_ffn_prompt(x2d, mods_conf, mods_ffn, w1_bf, b1, w_dw, b_dw, cf_g, cf_b, w2_bf, b2,
                         ln_g0, ln_b0, w_up_bf, ffn_w_dw, ffn_b_dw, w_dn_bf, layer, ln_g1, ln_b1,
                         batch, seq, tm):
    n, d = x2d.shape
    dc = w2_bf.shape[0]
    d_ff = w_dn_bf.shape[1]
    width = w_dw.shape[0]
    assert width - 1 <= PAD_ROWS and tm % PAD_ROWS == 0
    nj = seq // tm
    n_tiles = n // tm
    row = lambda a: a.reshape(1, -1)
    conf_tile = lambda i: jnp.minimum(i, n_tiles - 1)
    ffn_tile = lambda i: jnp.maximum(i - 1, 0)
    mod_c = pl.BlockSpec((None, 1, d), lambda i: (conf_tile(i) // nj, 0, 0))
    mod_f = pl.BlockSpec((None, 1, d), lambda i: (ffn_tile(i) // nj, 0, 0))
    conf = [w1_bf, row(b1), w_dw, row(b_dw), row(cf_g), row(cf_b), w2_bf, row(b2), row(ln_g0), row(ln_b0)]
    ffn = [ffn_w_dw, row(ffn_b_dw), row(ln_g1), row(ln_b1)]
    assert len(conf) == N_CONF_CONSTS and len(ffn) + 2 == N_FFN_CONSTS
    ffn_specs = [_resident(w_up_bf, layer), _resident(ffn[0]), _resident(ffn[1]),
                 _resident(w_dn_bf, layer), _resident(ffn[2]), _resident(ffn[3])]
    ffn_args = [w_up_bf, ffn[0], ffn[1], w_dn_bf, ffn[2], ffn[3]]
    return pl.pallas_call(
        functools.partial(_conformer_ffn_kernel, tiles_per_seq=nj, width=width, rs=64, cw=256, d_ff=d_ff),
        grid=(n_tiles + 1,),
        in_specs=[pl.BlockSpec((tm, d), lambda i: (conf_tile(i), 0)), mod_c, mod_c, mod_c,
                  mod_f, mod_f, mod_f] + [_resident(a) for a in conf] + ffn_specs,
        out_specs=[pl.BlockSpec((tm, d), lambda i: (ffn_tile(i), 0)),
                   pl.BlockSpec((None, PAD_ROWS, dc), lambda i: (conf_tile(i) // nj, 0, 0)),
                   pl.BlockSpec((None, 2, d_ff), lambda i: (ffn_tile(i), 0, 0))],
        out_shape=[jax.ShapeDtypeStruct((n, d), F32),
                   jax.ShapeDtypeStruct((batch, PAD_ROWS, dc), F32),
                   jax.ShapeDtypeStruct((n_tiles, 2, d_ff), F32)],
        scratch_shapes=[pltpu.VMEM((PAD_ROWS + tm + SUBLANES, dc), F32), pltpu.VMEM((tm, dc), F32),
                        pltpu.VMEM((tm, d), F32), pltpu.VMEM((tm, d), BF16),
                        pltpu.VMEM((tm, d_ff), BF16), pltpu.VMEM((2, d_ff), F32)],
        compiler_params=_params("arbitrary"),
        name="conformer_ffn_prompt",
    )(x2d, *mods_conf, *mods_ffn, *conf, *ffn_args)


def _dwconv_sample_kernel(x_ref, st_ref, w_ref, b_ref, o_ref, nst_ref, *, width, t):
    hist = width - 1
    bb, dc = x_ref.shape[1], x_ref.shape[2]
    plane = lambda j, cols: st_ref[j, :, cols] if j < hist else x_ref[j - hist, :, cols]
    for c in range(dc // HEAD_DIM):
        cols = slice(c * HEAD_DIM, (c + 1) * HEAD_DIM)
        acc = [jnp.zeros((bb, HEAD_DIM), F32) + b_ref[:, cols] for _ in range(t)]
        for j in range(hist + t):
            p = plane(j, cols)
            for ti in range(t):
                if 0 <= j - ti < width:
                    acc[ti] = acc[ti] + w_ref[j - ti:j - ti + 1, cols] * p
            if j >= t:
                nst_ref[j - t, :, cols] = p
        for ti in range(t):
            o_ref[ti, :, cols] = acc[ti]


def dwconv_sample(glu, states, layer, w_dw, b_dw):
    t, batch, dc = glu.shape
    width = w_dw.shape[0]
    hist = width - 1
    bb = 32
    b2 = b_dw.reshape(1, dc)
    return pl.pallas_call(
        functools.partial(_dwconv_sample_kernel, width=width, t=t),
        grid=(batch // bb,),
        in_specs=[pl.BlockSpec((t, bb, dc), lambda i: (0, i, 0)),
                  pl.BlockSpec((None, hist, bb, dc), lambda i: (layer, 0, i, 0)),
                  pl.BlockSpec(w_dw.shape, lambda i: (0, 0)),
                  pl.BlockSpec(b2.shape, lambda i: (0, 0))],
        out_specs=[pl.BlockSpec((t, bb, dc), lambda i: (0, i, 0)),
                   pl.BlockSpec((hist, bb, dc), lambda i: (0, i, 0))],
        out_shape=[jax.ShapeDtypeStruct((t, batch, dc), F32),
                   jax.ShapeDtypeStruct((hist, batch, dc), F32)],
        compiler_params=_params("parallel"),
        name="dwconv_sample",
    )(glu, states, w_dw, b2)


def _conf_tail_kernel(y_ref, gl_ref, bl_ref, w_ref, b2_ref, x_ref, gate_ref, g_ref, b_ref, o_ref):
    yn = _silu(_layer_norm(y_ref[...], gl_ref[...], bl_ref[...]))
    y = _dot(yn.astype(BF16), w_ref[...]) + b2_ref[...]
    v = ALPHA * x_ref[...] + (1.0 + gate_ref[...]) * y
    o_ref[...] = _layer_norm(v, g_ref[...], b_ref[...])


def conformer_tail(y, cf_g, cf_b, w2_bf, b2, x2d, gate, ln_g, ln_b, rows_per_group, tm):
    n, d = x2d.shape
    dc = y.shape[1]
    vecc = pl.BlockSpec((1, dc), lambda i: (0, 0))
    vec = pl.BlockSpec((1, d), lambda i: (0, 0))
    return pl.pallas_call(
        _conf_tail_kernel,
        grid=(n // tm,),
        in_specs=[pl.BlockSpec((tm, dc), lambda i: (i, 0)), vecc, vecc,
                  _resident(w2_bf), vec,
                  pl.BlockSpec((tm, d), lambda i: (i, 0)),
                  _mod_spec(gate, tm, rows_per_group), vec, vec],
        out_specs=pl.BlockSpec((tm, d), lambda i: (i, 0)),
        out_shape=jax.ShapeDtypeStruct((n, d), F32),
        compiler_params=_params("parallel"),
        name="conformer_tail",
    )(y, cf_g.reshape(1, dc), cf_b.reshape(1, dc), w2_bf, b2.reshape(1, d), x2d, gate,
      ln_g.reshape(1, d), ln_b.reshape(1, d))


def kernel(x_prompt, x_sample, cache_k, cache_v, state_ret, state_conv, state_ffn, page_table, c_prompt, c_sample, ab_w_in, ab_w_out, cf_w_pw1, cf_b_pw1, cf_w_dw, cf_b_dw, cf_ln_g, cf_ln_b, cf_w_pw2, cf_b_pw2, ffn_w_up, ffn_w_dw, ffn_b_dw, ffn_w_down, ada_w, ada_b, ln_g, ln_b):
    bp, seq, d = x_prompt.shape
    bs, ts, _ = x_sample.shape
    n_layers = ada_w.shape[0]
    d_ff = ffn_w_down.shape[1]
    past_len = page_table.shape[1] * cache_k.shape[2]
    np_, ns = bp * seq, bs * ts
    tm_p = min(256, seq)
    tm_wide = min(512, seq)
    tm_s = bs

    ada = adaln_all(jnp.concatenate([c_prompt, c_sample], 0), ada_w, ada_b)

    def mods(l, k):
        return ada[l, k, :bp][:, None, :], ada[l, k, bp:][None]

    token_major = lambda a: jnp.swapaxes(a, 0, 1)
    xp = x_prompt.reshape(np_, d)
    xs = token_major(x_sample).reshape(ns, d)
    tabs_p = rotary_tables(jnp.arange(seq, dtype=jnp.int32))
    tabs_s = tuple(jnp.repeat(tb, bs, axis=0) for tb in
                   rotary_tables(past_len + jnp.arange(ts, dtype=jnp.int32)))

    w_up = ffn_w_up.astype(BF16)
    w_dn = ffn_w_down.astype(BF16)
    outs = {k: [] for k in ("kp", "vp", "ks", "vs", "rp", "rs", "cp", "cs", "fp", "fs")}
    for l in range(n_layers):
        i = l // 2
        sh_p, sh_s = mods(l, 0)
        sc_p, sc_s = mods(l, 1)
        gt_p, gt_s = mods(l, 2)
        if l % 2 == 0:
            w_in = ab_w_in[i].astype(BF16)
            w_out = ab_w_out[i].astype(BF16)
            rq, rk, rv, rg, mq, mk, mv, kb, vb, kmean = in_proj(
                xp, sh_p, sc_p, w_in, tabs_p, seq, tm_p, True, BF16)
            o_r, s_p = retention_prompt(rq, rk, rv, rg, bp, seq)
            o_m = moba_prompt(mq, kb, vb, kmean.reshape(-1, GROUP_W), bp, seq)
            xp = out_proj(o_r, o_m, w_out, xp, gt_p, ln_g[l, 0], ln_b[l, 0], seq, tm_wide)
            outs["kp"].append(mk.reshape(bp, seq, N_HEADS, HEAD_DIM))
            outs["vp"].append(mv.reshape(bp, seq, N_HEADS, HEAD_DIM))
            outs["rp"].append(s_p)

            rq, rk, rv, rg, mq, mk, mv, kb, vb = in_proj(xs, sh_s, sc_s, w_in, tabs_s, ns, tm_s, False, F32)
            o_r, s_s = retention_sample(rq, rk, rv, rg, state_ret, i, ts)
            n_phys, page = cache_k.shape[1], cache_k.shape[2]
            ck = cache_k.reshape(-1, HEAD_DIM)
            cv = cache_v.reshape(-1, HEAD_DIM)
            mods_ffn_p = tuple(mods(l, k)[0] for k in (3, 4, 5))
            o_m, xp, prev_p = moba_sample_ffn_prompt(
                mq, kb, vb, ck, cv, page_table, i * n_phys, page, ts,
                xp, *mods_ffn_p, w_up, ffn_w_dw[l], ffn_b_dw[l], w_dn, l, ln_g[l, 1], ln_b[l, 1], seq, tm_p)
            o_m = token_major(o_m[:, :ts]).reshape(ns, GROUP_W)
            xs = out_proj(o_r, o_m, w_out, xs, gt_s, ln_g[l, 0], ln_b[l, 0], ns, tm_s)
            outs["ks"].append(token_major(mk.reshape(ts, bs, N_HEADS, HEAD_DIM)))
            outs["vs"].append(token_major(mv.reshape(ts, bs, N_HEADS, HEAD_DIM)))
            outs["rs"].append(s_s)
        else:
            w1 = cf_w_pw1[i].astype(BF16)
            w2 = cf_w_pw2[i].astype(BF16)
            hist = cf_w_dw.shape[1] - 1
            mods_ffn_p = tuple(mods(l, k)[0] for k in (3, 4, 5))
            xp, tail, prev_p = conformer_ffn_prompt(
                xp, (sh_p, sc_p, gt_p), mods_ffn_p, w1, cf_b_pw1[i], cf_w_dw[i], cf_b_dw[i],
                cf_ln_g[i], cf_ln_b[i], w2, cf_b_pw2[i], ln_g[l, 0], ln_b[l, 0],
                w_up, ffn_w_dw[l], ffn_b_dw[l], w_dn, l, ln_g[l, 1], ln_b[l, 1], bp, seq, tm_p)
            outs["cp"].append(tail[:, PAD_ROWS - hist:])

            glu = conformer_glu(xs, sh_s, sc_s, w1, cf_b_pw1[i], ns, tm_s)
            y, nst = dwconv_sample(glu.reshape(ts, bs, -1), jnp.swapaxes(state_conv, 1, 2), i,
                                   cf_w_dw[i], cf_b_dw[i])
            xs = conformer_tail(y.reshape(ns, -1), cf_ln_g[i], cf_ln_b[i], w2, cf_b_pw2[i], xs, gt_s,
                                ln_g[l, 0], ln_b[l, 0], ns, tm_s)
            outs["cs"].append(token_major(nst))

        sh_p, sh_s = mods(l, 3)
        sc_p, sc_s = mods(l, 4)
        gt_p, gt_s = mods(l, 5)
        outs["fp"].append(prev_p.reshape(bp, seq // tm_p, 2, d_ff)[:, -1])
        xs, u_s = conv_ffn(xs, sh_s, sc_s, gt_s, w_up, ffn_w_dw[l], ffn_b_dw[l], w_dn, l,
                           ln_g[l, 1], ln_b[l, 1], ns, tm_s, ts,
                           history=(state_ffn[l, :, 0], state_ffn[l, :, 1]))
        outs["fs"].append(token_major(u_s.reshape(ts, bs, d_ff)[ts - 2:]))

    st = lambda k: outs[k][0][None] if len(outs[k]) == 1 else jnp.stack(outs[k])
    return (xp.reshape(bp, seq, d), token_major(xs.reshape(ts, bs, d)), st("kp"), st("vp"), st("ks"), st("vs"),
            st("rp"), st("rs"), st("cp"), st("cs"), st("fp"), st("fs"))
```

```python
import functools

import jax
import jax.numpy as jnp
from jax import lax
from jax.experimental import pallas as pl
from jax.experimental.pallas import tpu as pltpu

HEAD_DIM = 128
N_HEADS = 4
GROUP_W = N_HEADS * HEAD_DIM
N_IN_COLS = 7
RET_CHUNK = 128
MOBA_BLOCK = 256
MOBA_TOPK = 3
ROPE_THETA = 10000.0
DEPTH = 2
ALPHA = (2 * DEPTH) ** 0.25
LN_EPS = 1e-5
GN_EPS = 1e-6
NEG = -1e30
LOG2_E = 1.4426950408889634

F32 = jnp.float32
BF16 = jnp.bfloat16

_NT = (((1,), (1,)), ((), ()))
_TN = (((0,), (0,)), ((), ()))

VMEM_LIMIT = 56 * 1024 * 1024
SUBLANES = 8


def _params(*sem):
    return pltpu.CompilerParams(dimension_semantics=sem, vmem_limit_bytes=VMEM_LIMIT)


def _dot(a, b):
    return jnp.dot(a, b, preferred_element_type=F32)


def _layer_norm(v, g, b):
    mu = jnp.mean(v, axis=-1, keepdims=True)
    d = v - mu
    var = jnp.mean(d * d, axis=-1, keepdims=True)
    return d * lax.rsqrt(var + LN_EPS) * g + b


def _silu(x):
    return x * jax.nn.sigmoid(x)


def _mod_spec(mod, tm, rows_per_group):
    _, r, d = mod.shape
    if r == 1:
        return pl.BlockSpec((None, 1, d), lambda i: (i * tm // rows_per_group, 0, 0))
    assert r % tm == 0
    return pl.BlockSpec((None, tm, d), lambda i: (0, i % (r // tm), 0))


def _resident(a, layer=None):
    if layer is None:
        return pl.BlockSpec(a.shape, lambda *_: (0,) * a.ndim, pipeline_mode=pl.Buffered(1))
    return pl.BlockSpec((None,) + a.shape[1:], lambda *_: (layer,) + (0,) * (a.ndim - 1),
                        pipeline_mode=pl.Buffered(1))


def _adaln_kernel(c_ref, w_ref, b_ref, o_ref):
    a = _silu(c_ref[...]).astype(BF16)
    o_ref[...] = _dot(a, w_ref[...].astype(BF16)) + b_ref[...]


def adaln_all(c_all, ada_w, ada_b):
    n_layers, d, n_out = ada_w.shape
    m = c_all.shape[0]
    return pl.pallas_call(
        _adaln_kernel,
        grid=(n_layers, n_out // d),
        in_specs=[
            pl.BlockSpec((m, d), lambda l, j: (0, 0)),
            pl.BlockSpec((None, d, d), lambda l, j: (l, 0, j)),
            pl.BlockSpec((None, 1, d), lambda l, j: (l, 0, j)),
        ],
        out_specs=pl.BlockSpec((None, None, m, d), lambda l, j: (l, j, 0, 0)),
        out_shape=jax.ShapeDtypeStruct((n_layers, n_out // d, m, d), F32),
        compiler_params=_params("parallel", "parallel"),
        name="adaln",
    )(c_all, ada_w, ada_b.reshape(n_layers, 1, n_out))


def _in_proj_kernel(x_ref, sh_ref, sc_ref, w_ref, cr_ref, sr_ref, cm_ref, sm_ref,
                    rq_ref, rk_ref, rv_ref, rg_ref, mq_ref, mk_ref, mv_ref, kb_ref, vb_ref,
                    *km_refs):
    tm = x_ref.shape[0]
    h = (x_ref[...] * (1.0 + sc_ref[...]) + sh_ref[...]).astype(BF16)
    cr, sr, cm, sm = cr_ref[...], sr_ref[...], cm_ref[...], sm_ref[...]
    even = (lax.broadcasted_iota(jnp.int32, cr.shape, 1) & 1) == 0
    outs = (rq_ref, rk_ref, rv_ref, rg_ref, mq_ref, mk_ref, mv_ref)
    for g, o_ref in enumerate(outs):
        z = _dot(h, w_ref[:, g * GROUP_W:(g + 1) * GROUP_W])
        for hh in range(N_HEADS):
            cols = slice(hh * HEAD_DIM, (hh + 1) * HEAD_DIM)
            zs = z[:, cols]
            if g in (0, 1):
                nxt = pltpu.roll(zs, HEAD_DIM - 1, axis=1)
                prv = pltpu.roll(zs, 1, axis=1)
                zs = zs * cr + jnp.where(even, nxt, prv) * sr
                if g == 1:
                    zs = zs * (HEAD_DIM ** -0.5)
            elif g in (4, 5):
                zs = zs * cm + pltpu.roll(zs, HEAD_DIM // 2, axis=1) * sm
            if g in (5, 6):
                o_ref[pl.ds(hh, tm, stride=N_HEADS), :] = zs
                head_major = kb_ref if g == 5 else vb_ref
                head_major[:, cols] = zs.astype(head_major.dtype)
                if g == 5 and km_refs:
                    for r in range(tm // MOBA_BLOCK):
                        blk = zs[r * MOBA_BLOCK:(r + 1) * MOBA_BLOCK, :]
                        km_refs[0][r:r + 1, cols] = jnp.sum(blk, axis=0, keepdims=True) * (1.0 / MOBA_BLOCK)
            else:
                o_ref[:, cols] = zs.astype(o_ref.dtype)


def in_proj(x2d, shift, scale, w_bf, tabs, rows_per_group, tm, block_means, mxu_dtype):
    n, d = x2d.shape
    p_tiles = tabs[0].shape[0] // tm
    mod_spec = _mod_spec(shift, tm, rows_per_group)
    tab_spec = pl.BlockSpec((tm, HEAD_DIM), lambda i: (i % p_tiles, 0))
    wide = pl.BlockSpec((tm, GROUP_W), lambda i: (i, 0))
    tall = pl.BlockSpec((tm * N_HEADS, HEAD_DIM), lambda i: (i, 0))
    wide_shape = lambda dt: jax.ShapeDtypeStruct((n, GROUP_W), dt)
    tall_shape = jax.ShapeDtypeStruct((n * N_HEADS, HEAD_DIM), F32)
    out_specs = [wide] * 5 + [tall, tall, wide, wide]
    out_shape = [wide_shape(F32), wide_shape(F32), wide_shape(mxu_dtype), wide_shape(F32),
                 wide_shape(mxu_dtype), tall_shape, tall_shape, wide_shape(mxu_dtype), wide_shape(mxu_dtype)]
    if block_means:
        assert tm % MOBA_BLOCK == 0
        per = tm // MOBA_BLOCK
        out_specs.append(pl.BlockSpec((None, per, GROUP_W), lambda i: (i, 0, 0)))
        out_shape.append(jax.ShapeDtypeStruct((n // tm, per, GROUP_W), F32))
    return pl.pallas_call(
        _in_proj_kernel,
        grid=(n // tm,),
        in_specs=[pl.BlockSpec((tm, d), lambda i: (i, 0)), mod_spec, mod_spec,
                  _resident(w_bf), tab_spec, tab_spec, tab_spec, tab_spec],
        out_specs=out_specs,
        out_shape=out_shape,
        compiler_params=_params("parallel"),
        name="in_proj",
    )(x2d, shift, scale, w_bf, *tabs)


def rotary_tables(pos):
    half = HEAD_DIM // 2
    posf = pos.astype(F32)[:, None]
    inv_r = 1.0 / (ROPE_THETA ** jnp.linspace(0.0, 1.0, half, dtype=F32))
    ang_r = posf * inv_r[None, :]
    cr = jnp.repeat(jnp.cos(ang_r), 2, axis=1)
    sr = jnp.stack([-jnp.sin(ang_r), jnp.sin(ang_r)], -1).reshape(-1, HEAD_DIM)
    inv_m = ROPE_THETA ** (-jnp.arange(0, HEAD_DIM, 2, dtype=F32) / HEAD_DIM)
    ang_m = posf * inv_m[None, :]
    cm = jnp.concatenate([jnp.cos(ang_m), jnp.cos(ang_m)], -1)
    sm = jnp.concatenate([-jnp.sin(ang_m), jnp.sin(ang_m)], -1)
    return cr, sr, cm, sm


def retention_tables(chunk, rows):
    log_g = jnp.log1p(-jnp.exp2(-5.0 - jnp.arange(N_HEADS, dtype=F32)))
    idx = jnp.arange(chunk, dtype=F32)
    diff = idx[:, None] - idx[None, :]
    decay_in = jnp.where(diff[None] >= 0,
                         jnp.exp(jnp.maximum(diff, 0.0)[None] * log_g[:, None, None]), 0.0)
    q_dec = jnp.exp((idx + 1.0)[None, :] * log_g[:, None])
    k_dec = jnp.exp((chunk - 1.0 - idx)[None, :] * log_g[:, None])
    c_dec = jnp.exp(chunk * log_g)
    pad = rows - chunk
    decay_in = jnp.pad(decay_in, ((0, 0), (0, pad), (0, rows - chunk)))
    bc = lambda t: jnp.broadcast_to(jnp.pad(t, ((0, 0), (0, pad)))[:, :, None], (N_HEADS, rows, HEAD_DIM))
    c_b = jnp.broadcast_to(c_dec[:, None, None], (N_HEADS, 1, HEAD_DIM))
    return decay_in, bc(q_dec), bc(k_dec), c_b


def _ret_chunk(qc, kc, vc, s, din, qd, kd, cd):
    att = lax.dot_general(qc.astype(BF16), kc.astype(BF16), _NT,
                          preferred_element_type=F32) * din
    o = _dot(att.astype(BF16), vc) + _dot((qc * qd).astype(BF16), s.astype(BF16))
    s_new = cd * s + lax.dot_general((kc * kd).astype(BF16), vc, _TN,
                                     preferred_element_type=F32)
    return o, s_new


def _group_norm_gate(o, g):
    o = o * lax.rsqrt(jnp.mean(o * o, axis=-1, keepdims=True) + GN_EPS)
    return o * _silu(g)


def _ret_prompt_kernel(q_ref, k_ref, v_ref, g_ref, din_ref, qd_ref, kd_ref, cd_ref,
                       o_ref, sout_ref, s_ref, *, n_sub):
    j = pl.program_id(1)
    chunk = din_ref.shape[1]

    @pl.when(j == 0)
    def _():
        s_ref[...] = jnp.zeros_like(s_ref)

    for hh in range(N_HEADS):
        cols = slice(hh * HEAD_DIM, (hh + 1) * HEAD_DIM)
        din, qd, kd, cd = din_ref[hh], qd_ref[hh], kd_ref[hh], cd_ref[hh]
        for c in range(n_sub):
            rows = slice(c * chunk, (c + 1) * chunk)
            o, s_new = _ret_chunk(q_ref[rows, cols], k_ref[rows, cols], v_ref[rows, cols],
                                  s_ref[hh], din, qd, kd, cd)
            s_ref[hh] = s_new
            o_ref[rows, cols] = _group_norm_gate(o, g_ref[rows, cols]).astype(o_ref.dtype)

    @pl.when(j == pl.num_programs(1) - 1)
    def _():
        sout_ref[...] = s_ref[...]


def retention_prompt(rq, rk, rv, rg, batch, seq):
    chunk = min(2 * RET_CHUNK, seq)
    tc = min(2 * chunk, seq)
    nj = seq // tc
    tabs = retention_tables(chunk, chunk)
    tok = pl.BlockSpec((tc, GROUP_W), lambda b, j: (b * nj + j, 0))
    tab = pl.BlockSpec((N_HEADS, chunk, HEAD_DIM), lambda b, j: (0, 0, 0))
    return pl.pallas_call(
        functools.partial(_ret_prompt_kernel, n_sub=tc // chunk),
        grid=(batch, nj),
        in_specs=[tok, tok, tok, tok, pl.BlockSpec((N_HEADS, chunk, chunk), lambda b, j: (0, 0, 0)), tab, tab,
                  pl.BlockSpec((N_HEADS, 1, HEAD_DIM), lambda b, j: (0, 0, 0))],
        out_specs=[tok, pl.BlockSpec((None, N_HEADS, HEAD_DIM, HEAD_DIM), lambda b, j: (b, 0, 0, 0))],
        out_shape=[jax.ShapeDtypeStruct((batch * seq, GROUP_W), BF16),
                   jax.ShapeDtypeStruct((batch, N_HEADS, HEAD_DIM, HEAD_DIM), F32)],
        scratch_shapes=[pltpu.VMEM((N_HEADS, HEAD_DIM, HEAD_DIM), F32)],
        compiler_params=_params("parallel", "arbitrary"),
        name="retention_prompt",
    )(rq, rk, rv, rg, *tabs)


def _group_rows(ref, group, batch, t, cols):
    return [ref[pl.ds(pl.multiple_of(ti * batch + group * SUBLANES, SUBLANES), SUBLANES), cols]
            for ti in range(t)]


def _seq_tile(token_rows, row, n_rows):
    width = token_rows[0].shape[1]
    r_i = lax.broadcasted_iota(jnp.int32, (n_rows, width), 0)
    g_i = lax.broadcasted_iota(jnp.int32, token_rows[0].shape, 0)
    out = jnp.zeros((n_rows, width), F32)
    for ti, rows in enumerate(token_rows):
        if isinstance(row, int):
            picked = rows[row:row + 1, :]
        else:
            picked = jnp.sum(jnp.where(g_i == row, rows, 0.0), axis=0, keepdims=True)
        out = jnp.where(r_i == ti, picked, out)
    return out


def _ret_sample_kernel(q_ref, k_ref, v_ref, g_ref, s0_ref, din_ref, qd_ref, kd_ref, cd_ref,
                       o_ref, sout_ref, *, t, batch):
    group = pl.program_id(0)
    r_i = lax.broadcasted_iota(jnp.int32, (SUBLANES, HEAD_DIM), 0)
    for hh in range(N_HEADS):
        cols = slice(hh * HEAD_DIM, (hh + 1) * HEAD_DIM)
        din, qd, kd, cd = din_ref[hh], qd_ref[hh], kd_ref[hh], cd_ref[hh]
        q_t, k_t, v_t, g_t = (_group_rows(r, group, batch, t, cols) for r in (q_ref, k_ref, v_ref, g_ref))
        out_t = [jnp.zeros((SUBLANES, HEAD_DIM), F32) for _ in range(t)]
        for bl in range(SUBLANES):
            qc = _seq_tile(q_t, bl, RET_CHUNK)
            kc = _seq_tile(k_t, bl, RET_CHUNK)
            vc = _seq_tile(v_t, bl, RET_CHUNK).astype(BF16)
            o, s_new = _ret_chunk(qc, kc, vc, s0_ref[bl, hh], din, qd, kd, cd)
            sout_ref[bl, hh] = s_new
            gated = _group_norm_gate(o[0:SUBLANES, :], _seq_tile(g_t, bl, SUBLANES))
            for ti in range(t):
                out_t[ti] = jnp.where(r_i == bl, gated[ti:ti + 1, :], out_t[ti])
        for ti in range(t):
            o_ref[pl.ds(pl.multiple_of(ti * batch + group * SUBLANES, SUBLANES), SUBLANES), cols] = out_t[ti]


def retention_sample(rq, rk, rv, rg, states, layer, t):
    batch = states.shape[1]
    bb = SUBLANES
    assert batch % bb == 0
    tabs = retention_tables(t, RET_CHUNK)
    tok = pl.BlockSpec(rq.shape, lambda i: (0, 0))
    st_in = pl.BlockSpec((None, bb, N_HEADS, HEAD_DIM, HEAD_DIM), lambda i: (layer, i, 0, 0, 0))
    st = pl.BlockSpec((bb, N_HEADS, HEAD_DIM, HEAD_DIM), lambda i: (i, 0, 0, 0))
    tab = pl.BlockSpec((N_HEADS, RET_CHUNK, HEAD_DIM), lambda i: (0, 0, 0))
    return pl.pallas_call(
        functools.partial(_ret_sample_kernel, t=t, batch=batch),
        grid=(batch // bb,),
        in_specs=[tok, tok, tok, tok, st_in, tab, tab, tab,
                  pl.BlockSpec((N_HEADS, 1, HEAD_DIM), lambda i: (0, 0, 0))],
        out_specs=[tok, st],
        out_shape=[jax.ShapeDtypeStruct(rq.shape, F32),
                   jax.ShapeDtypeStruct(states.shape[1:], F32)],
        compiler_params=_params("arbitrary"),
        name="retention_sample",
    )(rq, rk, rv, rg, states, *tabs)


def _beaten_counts(rows, n_valid):
    counts = []
    for n in range(n_valid):
        cnt = jnp.zeros(rows[n].shape, jnp.int32)
        for m in range(n_valid):
            if m == n:
                continue
            beats = (rows[m] >= rows[n]) if m < n else (rows[m] > rows[n])
            cnt = cnt + beats.astype(jnp.int32)
        counts.append(cnt)
    return counts


def _moba_prompt_kernel(q_ref, k_ref, v_ref, km_ref, o_ref, s_ref, *, nblk):
    blk = MOBA_BLOCK
    scale = HEAD_DIM ** -0.5
    kb = k_ref[...]
    vt = v_ref[...].astype(F32).T.astype(BF16)
    kmean = km_ref[...].astype(BF16)
    key_i = lax.broadcasted_iota(jnp.int32, (blk, blk), 0)
    qry_i = lax.broadcasted_iota(jnp.int32, (blk, blk), 1)
    causal = key_i <= qry_i

    for qi in range(nblk):
        qb = q_ref[qi * blk:(qi + 1) * blk, :]
        cnt = None
        if qi > MOBA_TOPK:
            gt = lax.dot_general(kmean, qb, _NT, preferred_element_type=F32)
            cnt = _beaten_counts([gt[m:m + 1, :] for m in range(qi)], qi)
        mx = jnp.full((1, blk), NEG, F32)
        for n in range(qi + 1):
            s = lax.dot_general(kb[n * blk:(n + 1) * blk], qb, _NT,
                                preferred_element_type=F32)
            if n == qi:
                s = jnp.where(causal, s, NEG)
            elif cnt is not None:
                s = jnp.where(jnp.broadcast_to(cnt[n], s.shape) < MOBA_TOPK, s, NEG)
            s_ref[n * blk:(n + 1) * blk, :] = s
            mx = jnp.maximum(mx, jnp.max(s, axis=0, keepdims=True))
        l = jnp.zeros((1, blk), F32)
        ot = jnp.zeros((HEAD_DIM, blk), F32)
        for n in range(qi + 1):
            p = jnp.exp2((s_ref[n * blk:(n + 1) * blk, :] - mx) * (scale * LOG2_E))
            l = l + jnp.sum(p, axis=0, keepdims=True)
            ot = ot + _dot(vt[:, n * blk:(n + 1) * blk], p.astype(BF16))
        ot = ot * (1.0 / l)
        o_ref[qi * blk:(qi + 1) * blk, :] = ot.T.astype(o_ref.dtype)


def moba_prompt(mq, mk, mv, kmean, batch, seq):
    nblk = seq // MOBA_BLOCK
    assert nblk % SUBLANES == 0
    spec = pl.BlockSpec((seq, HEAD_DIM), lambda b, h: (b, h))
    return pl.pallas_call(
        functools.partial(_moba_prompt_kernel, nblk=nblk),
        grid=(batch, N_HEADS),
        in_specs=[spec, spec, spec, pl.BlockSpec((nblk, HEAD_DIM), lambda b, h: (b, h))],
        out_specs=spec,
        out_shape=jax.ShapeDtypeStruct((batch * seq, GROUP_W), BF16),
        scratch_shapes=[pltpu.VMEM((seq, MOBA_BLOCK), F32)],
        compiler_params=_params("parallel", "parallel"),
        name="moba_prompt",
    )(mq, mk, mv, kmean)


def _page_rows(ref, page):
    return jnp.concatenate([ref[pl.ds(hh, page, stride=N_HEADS), :] for hh in range(N_HEADS)], axis=1)


def _moba_sample_phases(b, q_ref, kn_ref, vn_ref, k_refs, v_refs, o_ref, *, page, t):
    n_pages = len(k_refs)
    scale = HEAD_DIM ** -0.5
    ppb = MOBA_BLOCK // page
    nblk = n_pages // ppb
    n_pairs = N_HEADS * t
    batch = q_ref.shape[0] // t
    all_cols = slice(0, GROUP_W)
    group = b // SUBLANES
    row = b % SUBLANES
    q = _seq_tile(_group_rows(q_ref, group, batch, t, all_cols), row, SUBLANES)

    r_i = lax.broadcasted_iota(jnp.int32, (n_pairs, GROUP_W), 0)
    l_i = lax.broadcasted_iota(jnp.int32, (n_pairs, GROUP_W), 1)
    qm = jnp.zeros((n_pairs, GROUP_W), F32)
    for ti in range(t):
        for hh in range(N_HEADS):
            hit = (r_i == hh * t + ti) & (l_i >= hh * HEAD_DIM) & (l_i < (hh + 1) * HEAD_DIM)
            qm = jnp.where(hit, q[ti:ti + 1, :], qm)
    qm = qm.astype(BF16)

    zpad = jnp.zeros((page - SUBLANES, GROUP_W), F32)
    own = lambda ref: jnp.concatenate(
        [_seq_tile(_group_rows(ref, group, batch, t, all_cols), row, SUBLANES), zpad], axis=0).astype(BF16)
    k_own, v_own = own(kn_ref), own(vn_ref)

    b_i = lax.broadcasted_iota(jnp.int32, (page, GROUP_W), 0)
    kmean = jnp.zeros((page, GROUP_W), F32)
    scores = []
    for p in range(n_pages):
        kp = _page_rows(k_refs[p], page)
        scores.append(lax.dot_general(qm, kp.astype(BF16), _NT,
                                      preferred_element_type=F32) * scale)
        ks = jnp.sum(kp, axis=0, keepdims=True) * (1.0 / MOBA_BLOCK)
        kmean = kmean + jnp.where(b_i == p // ppb, ks, 0.0)
        yield
    gate = lax.dot_general(qm, kmean.astype(BF16), _NT, preferred_element_type=F32)
    cnt = _beaten_counts([gate[:, m:m + 1] for m in range(nblk)], nblk)
    keep = [jnp.broadcast_to(c, (n_pairs, page)) < MOBA_TOPK for c in cnt]

    s_own = lax.dot_general(qm, k_own, _NT, preferred_element_type=F32) * scale
    key_i = lax.broadcasted_iota(jnp.int32, (n_pairs, page), 1)
    tok_i = lax.rem(lax.broadcasted_iota(jnp.int32, (n_pairs, page), 0), t)
    s_own = jnp.where((key_i <= tok_i) & (key_i < t), s_own, NEG)

    mx = s_own
    for p in range(n_pages):
        scores[p] = jnp.where(keep[p // ppb], scores[p], NEG)
        mx = jnp.maximum(mx, scores[p])
    mx = jnp.max(mx, axis=1, keepdims=True)
    e_own = jnp.exp(s_own - mx)
    l = e_own
    probs = []
    for p in range(n_pages):
        e = jnp.exp(scores[p] - mx)
        l = l + e
        probs.append(e.astype(BF16))
    inv_l = 1.0 / jnp.sum(l, axis=1, keepdims=True)
    yield
    acc = _dot(e_own.astype(BF16), v_own)
    for p in range(n_pages):
        acc = acc + _dot(probs[p], _page_rows(v_refs[p], page).astype(BF16))
        yield
    acc = acc * inv_l
    o_ref[...] = jnp.zeros_like(o_ref)
    for hh in range(N_HEADS):
        cols = slice(hh * HEAD_DIM, (hh + 1) * HEAD_DIM)
        o_ref[0:t, cols] = acc[hh * t:(hh + 1) * t, cols]
    yield


def _mixer_out(or_ref, om_ref, w_ref, x_ref, gate_ref, g_ref, b_ref, o_ref, sub):
    for r0 in range(0, x_ref.shape[0], sub):
        rows = slice(r0, r0 + sub)
        y = (_dot(or_ref[rows, :].astype(BF16), w_ref[:GROUP_W, :])
             + _dot(om_ref[rows, :].astype(BF16), w_ref[GROUP_W:, :]))
        gate = gate_ref[...] if gate_ref.shape[0] == 1 else gate_ref[rows, :]
        v = ALPHA * x_ref[rows, :] + (1.0 + gate) * y
        o_ref[rows, :] = _layer_norm(v, g_ref[...], b_ref[...])


def _out_proj_kernel(or_ref, om_ref, w_ref, x_ref, gate_ref, g_ref, b_ref, o_ref, *, sub):
    _mixer_out(or_ref, om_ref, w_ref, x_ref, gate_ref, g_ref, b_ref, o_ref, sub)


def out_proj(o_r, o_m, w_bf, x2d, gate, ln_g, ln_b, rows_per_group, tm):
    n, d = x2d.shape
    vec = pl.BlockSpec((1, d), lambda i: (0, 0))
    return pl.pallas_call(
        functools.partial(_out_proj_kernel, sub=min(128, tm)),
        grid=(n // tm,),
        in_specs=[pl.BlockSpec((tm, GROUP_W), lambda i: (i, 0)),
                  pl.BlockSpec((tm, GROUP_W), lambda i: (i, 0)),
                  _resident(w_bf),
                  pl.BlockSpec((tm, d), lambda i: (i, 0)),
                  _mod_spec(gate, tm, rows_per_group), vec, vec],
        out_specs=pl.BlockSpec((tm, d), lambda i: (i, 0)),
        out_shape=jax.ShapeDtypeStruct((n, d), F32),
        compiler_params=_params("parallel"),
        name="out_proj",
    )(o_r, o_m, w_bf, x2d, gate, ln_g.reshape(1, d), ln_b.reshape(1, d))


def _run(phases):
    for _ in phases:
        pass


def _ffn_phases(x, sh_ref, sc_ref, gate_ref, wup_ref, wdw_ref, bdw_ref, wdn_ref, g_ref, b_ref,
                o_ref, prev_ref, h_s, act_s, hist_s, *, d_ff, cw, planes, chunks=None):
    tm = x.shape[0]
    n_chunks = d_ff // cw
    chunks = range(n_chunks) if chunks is None else chunks
    if chunks.start == 0:
        h_s[...] = (x * (1.0 + sc_ref[...]) + sh_ref[...]).astype(BF16)
    row = lax.broadcasted_iota(jnp.int32, (tm, cw), 0)
    for c in chunks:
        cu = slice(c * cw, (c + 1) * cw)
        cv = slice(d_ff + c * cw, d_ff + (c + 1) * cw)
        u = _dot(h_s[...], wup_ref[:, cu])
        v = _dot(h_s[...], wup_ref[:, cv])
        if planes:
            s2 = hist_s[0, :, cu]
            s1 = hist_s[1, :, cu]
            hist_s[0, :, cu] = s1
            hist_s[1, :, cu] = u
            prev_ref[:, cu] = u
        else:
            e1 = hist_s[1:2, cu]
            e2 = jnp.where(row == 0, hist_s[0:1, cu], hist_s[1:2, cu])
            hist_s[:, cu] = u[tm - 2:tm, :]
            prev_ref[:, cu] = u[tm - 2:tm, :]
            s1 = jnp.where(row >= 1, pltpu.roll(u, 1, axis=0), e1)
            s2 = jnp.where(row >= 2, pltpu.roll(u, 2, axis=0), e2)
        uc = wdw_ref[0:1, cu] * s2 + wdw_ref[1:2, cu] * s1 + wdw_ref[2:3, cu] * u + bdw_ref[:, cu]
        act = 0.5 * uc * (1.0 + lax.erf(uc * (2.0 ** -0.5))) * v
        act_s[:, cu] = act.astype(BF16)
        yield act[0:1, 0:HEAD_DIM]
    if chunks.stop != n_chunks:
        return
    fy = _dot(act_s[...], wdn_ref[...])
    vsum = ALPHA * x + (1.0 + gate_ref[...]) * fy
    o_ref[...] = _layer_norm(vsum, g_ref[...], b_ref[...])
    yield


def _ffn_kernel(*refs, d_ff, cw, tiles_per_seq, planes):
    if planes:
        (x_ref, sh_ref, sc_ref, gate_ref, wup_ref, wdw_ref, bdw_ref, wdn_ref, g_ref, b_ref,
         p0_ref, p1_ref, o_ref, prev_ref, h_s, act_s, hist_s) = refs
    else:
        (x_ref, sh_ref, sc_ref, gate_ref, wup_ref, wdw_ref, bdw_ref, wdn_ref, g_ref, b_ref,
         o_ref, prev_ref, h_s, act_s, hist_s) = refs

    @pl.when(pl.program_id(0) % tiles_per_seq == 0)
    def _():
        if planes:
            hist_s[0] = p0_ref[...]
            hist_s[1] = p1_ref[...]
        else:
            hist_s[...] = jnp.zeros_like(hist_s)

    _run(_ffn_phases(x_ref[...], sh_ref, sc_ref, gate_ref, wup_ref, wdw_ref, bdw_ref, wdn_ref,
                     g_ref, b_ref, o_ref, prev_ref, h_s, act_s, hist_s, d_ff=d_ff, cw=cw, planes=planes))


def conv_ffn(x2d, shift, scale, gate, w_up_bf, w_dw, b_dw, w_dn_bf, layer, ln_g, ln_b,
             rows_per_group, tm, tiles_per_seq, history=None):
    n, d = x2d.shape
    d_ff = w_dn_bf.shape[1]
    cw = 256
    planes = history is not None
    mod = _mod_spec(shift, tm, rows_per_group)
    vec = pl.BlockSpec((1, d), lambda i: (0, 0))
    b_dw2 = b_dw.reshape(1, d_ff)
    in_specs = [pl.BlockSpec((tm, d), lambda i: (i, 0)), mod, mod, mod,
                _resident(w_up_bf, layer), _resident(w_dw), _resident(b_dw2),
                _resident(w_dn_bf, layer), vec, vec]
    args = [x2d, shift, scale, gate, w_up_bf, w_dw, b_dw2, w_dn_bf,
            ln_g.reshape(1, d), ln_b.reshape(1, d)]
    scratch = [pltpu.VMEM((tm, d), BF16), pltpu.VMEM((tm, d_ff), BF16)]
    if planes:
        in_specs += [_resident(history[0]), _resident(history[1])]
        args += list(history)
        prev_spec = pl.BlockSpec((tm, d_ff), lambda i: (i, 0))
        prev_shape = jax.ShapeDtypeStruct((n, d_ff), F32)
        scratch.append(pltpu.VMEM((2, tm, d_ff), F32))
    else:
        prev_spec = pl.BlockSpec((None, 2, d_ff), lambda i: (i, 0, 0))
        prev_shape = jax.ShapeDtypeStruct((n // tm, 2, d_ff), F32)
        scratch.append(pltpu.VMEM((2, d_ff), F32))
    return pl.pallas_call(
        functools.partial(_ffn_kernel, d_ff=d_ff, cw=cw, tiles_per_seq=tiles_per_seq, planes=planes),
        grid=(n // tm,),
        in_specs=in_specs,
        out_specs=[pl.BlockSpec((tm, d), lambda i: (i, 0)), prev_spec],
        out_shape=[jax.ShapeDtypeStruct((n, d), F32), prev_shape],
        scratch_shapes=scratch,
        compiler_params=_params("arbitrary"),
        name="conv_ffn",
    )(*args)


def _moba_ffn_kernel(pt_ref, q_ref, kn_ref, vn_ref, *rest, n_pages, page, t, d_ff, cw, parts, tiles_per_seq):
    k_refs = rest[:n_pages]
    v_refs = rest[n_pages:2 * n_pages]
    (or_ref, omp_ref, wout_ref, x_ref, gate0_ref, g0_ref, b0_ref,
     sh_ref, sc_ref, gate_ref, wup_ref, wdw_ref, bdw_ref, wdn_ref, g_ref, b_ref,
     om_ref, o_ref, prev_ref, x1_s, h_s, act_s, hist_s) = rest[2 * n_pages:]
    i = pl.program_id(0)
    n_chunks = d_ff // cw
    per_part = -(-n_chunks // parts)

    @pl.when(i % (parts * tiles_per_seq) == 0)
    def _():
        hist_s[...] = jnp.zeros_like(hist_s)

    def step(part):
        chunks = range(part * per_part, min((part + 1) * per_part, n_chunks))
        if part == 0:
            _mixer_out(or_ref, omp_ref, wout_ref, x_ref, gate0_ref, g0_ref, b0_ref, x1_s, sub=128)
        ffn_ph = _ffn_phases(x1_s[...], sh_ref, sc_ref, gate_ref, wup_ref, wdw_ref, bdw_ref, wdn_ref,
                             g_ref, b_ref, o_ref, prev_ref, h_s, act_s, hist_s,
                             d_ff=d_ff, cw=cw, planes=False, chunks=chunks)
        moba_ph = _moba_sample_phases(i, q_ref, kn_ref, vn_ref, k_refs, v_refs, om_ref, page=page, t=t)
        for _ in range(n_pages):
            next(moba_ph)
        next(ffn_ph)
        next(moba_ph)
        for _ in range(len(chunks) - 1):
            next(ffn_ph)
        for _ in range(n_pages + 1):
            next(moba_ph)
        _run(ffn_ph)

    for part in range(parts):
        pl.when(i % parts == part)(functools.partial(step, part))


def moba_sample_ffn_prompt(mq, mk, mv, cache_k, cache_v, page_table, page_base, page, t,
                           o_r, o_m, w_out_bf, x2d, gate0, ln_g0, ln_b0,
                           shift, scale, gate, w_up_bf, w_dw, b_dw, w_dn_bf, layer, ln_g, ln_b,
                           seq, tm):
    batch, n_pages = page_table.shape
    n, d = x2d.shape
    d_ff = w_dn_bf.shape[1]
    assert (n_pages * page) % MOBA_BLOCK == 0 and t <= SUBLANES
    assert n_pages * page // MOBA_BLOCK <= page
    n_tiles = n // tm
    assert batch % n_tiles == 0
    parts = batch // n_tiles
    tok = pl.BlockSpec(mq.shape, lambda i, pt: (0, 0))

    def page_spec(p):
        return pl.BlockSpec((page * N_HEADS, HEAD_DIM), lambda i, pt: (page_base + pt[i, p], 0))

    const = lambda a, lead=None: pl.BlockSpec(
        a.shape if lead is None else (None,) + a.shape[1:],
        (lambda i, pt: (0,) * a.ndim) if lead is None else (lambda i, pt: (lead,) + (0,) * (a.ndim - 1)),
        pipeline_mode=pl.Buffered(1))
    mod = pl.BlockSpec((None, 1, d), lambda i, pt: ((i // parts) * tm // seq, 0, 0))
    b_dw2, g2, b2 = b_dw.reshape(1, d_ff), ln_g.reshape(1, d), ln_b.reshape(1, d)
    g0, b0 = ln_g0.reshape(1, d), ln_b0.reshape(1, d)
    tile = lambda w: pl.BlockSpec((tm, w), lambda i, pt: (i // parts, 0))
    specs = ([tok, tok, tok] + [page_spec(p) for p in range(n_pages)] * 2
             + [tile(GROUP_W), tile(GROUP_W), const(w_out_bf), tile(d), mod, const(g0), const(b0), mod, mod, mod,
                const(w_up_bf, layer), const(w_dw), const(b_dw2), const(w_dn_bf, layer), const(g2), const(b2)])
    grid_spec = pltpu.PrefetchScalarGridSpec(
        num_scalar_prefetch=1,
        grid=(batch,),
        in_specs=specs,
        out_specs=[pl.BlockSpec((None, SUBLANES, GROUP_W), lambda i, pt: (i, 0, 0)),
                   pl.BlockSpec((tm, d), lambda i, pt: (i // parts, 0)),
                   pl.BlockSpec((None, 2, d_ff), lambda i, pt: (i // parts, 0, 0))],
        scratch_shapes=[pltpu.VMEM((tm, d), F32), pltpu.VMEM((tm, d), BF16), pltpu.VMEM((tm, d_ff), BF16),
                        pltpu.VMEM((2, d_ff), F32)],
    )
    return pl.pallas_call(
        functools.partial(_moba_ffn_kernel, n_pages=n_pages, page=page, t=t, d_ff=d_ff, cw=256,
                          parts=parts, tiles_per_seq=seq // tm),
        grid_spec=grid_spec,
        out_shape=[jax.ShapeDtypeStruct((batch, SUBLANES, GROUP_W), F32),
                   jax.ShapeDtypeStruct((n, d), F32),
                   jax.ShapeDtypeStruct((n_tiles, 2, d_ff), F32)],
        compiler_params=_params("arbitrary"),
        name="moba_sample_ffn_prompt",
    )(page_table, mq, mk, mv, *([cache_k] * n_pages), *([cache_v] * n_pages),
      o_r, o_m, w_out_bf, x2d, gate0, g0, b0, shift, scale, gate, w_up_bf, w_dw, b_dw2, w_dn_bf, g2, b2)


def _glu_kernel(x_ref, sh_ref, sc_ref, w_ref, b_ref, o_ref, *, cw):
    h = (x_ref[...] * (1.0 + sc_ref[...]) + sh_ref[...]).astype(BF16)
    dc = o_ref.shape[1]
    for c in range(dc // cw):
        ca = slice(c * cw, (c + 1) * cw)
        cg = slice(dc + c * cw, dc + (c + 1) * cw)
        a = _dot(h, w_ref[:, ca]) + b_ref[:, ca]
        g = _dot(h, w_ref[:, cg]) + b_ref[:, cg]
        o_ref[:, ca] = a * jax.nn.sigmoid(g)


def conformer_glu(x2d, shift, scale, w1_bf, b1, rows_per_group, tm):
    n, d = x2d.shape
    dc = w1_bf.shape[1] // 2
    mod = _mod_spec(shift, tm, rows_per_group)
    b1r = b1.reshape(1, 2 * dc)
    return pl.pallas_call(
        functools.partial(_glu_kernel, cw=256),
        grid=(n // tm,),
        in_specs=[pl.BlockSpec((tm, d), lambda i: (i, 0)), mod, mod,
                  _resident(w1_bf), _resident(b1r)],
        out_specs=pl.BlockSpec((tm, dc), lambda i: (i, 0)),
        out_shape=jax.ShapeDtypeStruct((n, dc), F32),
        compiler_params=_params("parallel"),
        name="conformer_glu",
    )(x2d, shift, scale, w1_bf, b1r)


PAD_ROWS = 32


def _dwconv_rows(win_ref, w_ref, bias, r0, rs, cols, width):
    off = PAD_ROWS - (width - 1)
    y = bias
    for r in range(SUBLANES):
        z = None
        for m in range((off + width - 1) // SUBLANES + 1):
            k = SUBLANES * m + r - off
            if 0 <= k < width:
                lo = r0 + SUBLANES * m
                term = w_ref[k:k + 1, cols] * win_ref[lo:lo + rs + SUBLANES, cols]
                z = term if z is None else z + term
        if z is not None:
            y = y + z[r:r + rs, :]
    return y


def _ordered_after(x, token):
    zero = lax.shift_right_logical(lax.shift_right_logical(
        lax.bitcast_convert_type(token, jnp.uint32), jnp.uint32(16)), jnp.uint32(16))
    return lax.bitcast_convert_type(lax.bitcast_convert_type(x, jnp.uint32) | zero, F32)


def _conformer_phases(x, sh_ref, sc_ref, gate_ref, w1_ref, b1_ref, wdw_ref, bdw_ref,
                      gl_ref, bl_ref, w2_ref, b2_ref, g_ref, b_ref, o_ref, tail_ref, win_ref, y_ref,
                      *, width, rs, cw, after=None):
    tm, dc = y_ref.shape
    h = (x * (1.0 + sc_ref[...]) + sh_ref[...]).astype(BF16)
    for c in range(dc // cw):
        ca = slice(c * cw, (c + 1) * cw)
        cg = slice(dc + c * cw, dc + (c + 1) * cw)
        a = _dot(h, w1_ref[:, ca]) + b1_ref[:, ca]
        g = _dot(h, w1_ref[:, cg]) + b1_ref[:, cg]
        win_ref[PAD_ROWS:PAD_ROWS + tm, ca] = a * jax.nn.sigmoid(g)
    tail_ref[...] = win_ref[tm:tm + PAD_ROWS, :]
    yield
    for c in range(dc // HEAD_DIM):
        cols = slice(c * HEAD_DIM, (c + 1) * HEAD_DIM)
        for r0 in range(0, tm, rs):
            bias = bdw_ref[:, cols]
            if after is not None and after[0] is not None:
                bias = _ordered_after(bias, after[0])
            y_ref[r0:r0 + rs, cols] = _dwconv_rows(win_ref, wdw_ref, bias, r0, rs, cols, width)
            yield
    win_ref[0:PAD_ROWS, :] = win_ref[tm:tm + PAD_ROWS, :]
    yn = _silu(_layer_norm(y_ref[...], gl_ref[...], bl_ref[...]))
    y = _dot(yn.astype(BF16), w2_ref[...]) + b2_ref[...]
    v = ALPHA * x + (1.0 + gate_ref[...]) * y
    o_ref[...] = _layer_norm(v, g_ref[...], b_ref[...])
    yield


N_CONF_CONSTS = 10
N_FFN_CONSTS = 6


def _conformer_ffn_kernel(*refs, tiles_per_seq, width, rs, cw, d_ff):
    x_ref, shc_ref, scc_ref, gtc_ref, shf_ref, scf_ref, gtf_ref = refs[:7]
    conf = refs[7:7 + N_CONF_CONSTS]
    ffn = refs[7 + N_CONF_CONSTS:7 + N_CONF_CONSTS + N_FFN_CONSTS]
    o_ref, tail_ref, prev_ref, win_ref, y_ref, x1_s, h_s, act_s, hist_s = refs[7 + N_CONF_CONSTS + N_FFN_CONSTS:]
    i = pl.program_id(0)
    n_tiles = pl.num_programs(0) - 1
    tm, dc = y_ref.shape

    @pl.when(i == 0)
    def _():
        x1_s[...] = jnp.zeros_like(x1_s)
        win_ref[PAD_ROWS + tm:, :] = jnp.zeros((SUBLANES, dc), F32)

    @pl.when(jnp.minimum(i, n_tiles - 1) % tiles_per_seq == 0)
    def _():
        win_ref[0:PAD_ROWS, :] = jnp.zeros((PAD_ROWS, dc), F32)

    @pl.when(jnp.maximum(i - 1, 0) % tiles_per_seq == 0)
    def _():
        hist_s[...] = jnp.zeros_like(hist_s)

    ffn_ph = _ffn_phases(x1_s[...], shf_ref, scf_ref, gtf_ref, *ffn, o_ref, prev_ref, h_s, act_s, hist_s,
                         d_ff=d_ff, cw=cw, planes=False)
    token = [None]
    conf_ph = _conformer_phases(x_ref[...], shc_ref, scc_ref, gtc_ref, *conf, x1_s, tail_ref, win_ref, y_ref,
                                width=width, rs=rs, cw=cw, after=token)
    next(conf_ph)
    n_ffn = d_ff // cw
    n_conv = (dc // HEAD_DIM) * (tm // rs)
    per_chunk = -(-n_conv // n_ffn)
    for _ in range(n_ffn):
        token[0] = next(ffn_ph)
        for _ in range(per_chunk):
            next(conf_ph, None)
    _run(ffn_ph)
    _run(conf_ph)


def conformer_ffn_prompt(x2d, mods_conf, mods_ffn, w1_bf, b1, w_dw, b_dw, cf_g, cf_b, w2_bf, b2,
                         ln_g0, ln_b0, w_up_bf, ffn_w_dw, ffn_b_dw, w_dn_bf, layer, ln_g1, ln_b1,
                         batch, seq, tm):
    n, d = x2d.shape
    dc = w2_bf.shape[0]
    d_ff = w_dn_bf.shape[1]
    width = w_dw.shape[0]
    assert width - 1 <= PAD_ROWS and tm % PAD_ROWS == 0
    nj = seq // tm
    n_tiles = n // tm
    row = lambda a: a.reshape(1, -1)
    conf_tile = lambda i: jnp.minimum(i, n_tiles - 1)
    ffn_tile = lambda i: jnp.maximum(i - 1, 0)
    mod_c = pl.BlockSpec((None, 1, d), lambda i: (conf_tile(i) // nj, 0, 0))
    mod_f = pl.BlockSpec((None, 1, d), lambda i: (ffn_tile(i) // nj, 0, 0))
    conf = [w1_bf, row(b1), w_dw, row(b_dw), row(cf_g), row(cf_b), w2_bf, row(b2), row(ln_g0), row(ln_b0)]
    ffn = [ffn_w_dw, row(ffn_b_dw), row(ln_g1), row(ln_b1)]
    assert len(conf) == N_CONF_CONSTS and len(ffn) + 2 == N_FFN_CONSTS
    ffn_specs = [_resident(w_up_bf, layer), _resident(ffn[0]), _resident(ffn[1]),
                 _resident(w_dn_bf, layer), _resident(ffn[2]), _resident(ffn[3])]
    ffn_args = [w_up_bf, ffn[0], ffn[1], w_dn_bf, ffn[2], ffn[3]]
    return pl.pallas_call(
        functools.partial(_conformer_ffn_kernel, tiles_per_seq=nj, width=width, rs=64, cw=256, d_ff=d_ff),
        grid=(n_tiles + 1,),
        in_specs=[pl.BlockSpec((tm, d), lambda i: (conf_tile(i), 0)), mod_c, mod_c, mod_c,
                  mod_f, mod_f, mod_f] + [_resident(a) for a in conf] + ffn_specs,
        out_specs=[pl.BlockSpec((tm, d), lambda i: (ffn_tile(i), 0)),
                   pl.BlockSpec((None, PAD_ROWS, dc), lambda i: (conf_tile(i) // nj, 0, 0)),
                   pl.BlockSpec((None, 2, d_ff), lambda i: (ffn_tile(i), 0, 0))],
        out_shape=[jax.ShapeDtypeStruct((n, d), F32),
                   jax.ShapeDtypeStruct((batch, PAD_ROWS, dc), F32),
                   jax.ShapeDtypeStruct((n_tiles, 2, d_ff), F32)],
        scratch_shapes=[pltpu.VMEM((PAD_ROWS + tm + SUBLANES, dc), F32), pltpu.VMEM((tm, dc), F32),
                        pltpu.VMEM((tm, d), F32), pltpu.VMEM((tm, d), BF16),
                        pltpu.VMEM((tm, d_ff), BF16), pltpu.VMEM((2, d_ff), F32)],
        compiler_params=_params("arbitrary"),
        name="conformer_ffn_prompt",
    )(x2d, *mods_conf, *mods_ffn, *conf, *ffn_args)


def _dwconv_sample_kernel(x_ref, st_ref, w_ref, b_ref, o_ref, nst_ref, *, width, t):
    hist = width - 1
    bb, dc = x_ref.shape[1], x_ref.shape[2]
    plane = lambda j, cols: st_ref[j, :, cols] if j < hist else x_ref[j - hist, :, cols]
    for c in range(dc // HEAD_DIM):
        cols = slice(c * HEAD_DIM, (c + 1) * HEAD_DIM)
        acc = [jnp.zeros((bb, HEAD_DIM), F32) + b_ref[:, cols] for _ in range(t)]
        for j in range(hist + t):
            p = plane(j, cols)
            for ti in range(t):
                if 0 <= j - ti < width:
                    acc[ti] = acc[ti] + w_ref[j - ti:j - ti + 1, cols] * p
            if j >= t:
                nst_ref[j - t, :, cols] = p
        for ti in range(t):
            o_ref[ti, :, cols] = acc[ti]


def dwconv_sample(glu, states, layer, w_dw, b_dw):
    t, batch, dc = glu.shape
    width = w_dw.shape[0]
    hist = width - 1
    bb = 32
    b2 = b_dw.reshape(1, dc)
    return pl.pallas_call(
        functools.partial(_dwconv_sample_kernel, width=width, t=t),
        grid=(batch // bb,),
        in_specs=[pl.BlockSpec((t, bb, dc), lambda i: (0, i, 0)),
                  pl.BlockSpec((None, hist, bb, dc), lambda i: (layer, 0, i, 0)),
                  pl.BlockSpec(w_dw.shape, lambda i: (0, 0)),
                  pl.BlockSpec(b2.shape, lambda i: (0, 0))],
        out_specs=[pl.BlockSpec((t, bb, dc), lambda i: (0, i, 0)),
                   pl.BlockSpec((hist, bb, dc), lambda i: (0, i, 0))],
        out_shape=[jax.ShapeDtypeStruct((t, batch, dc), F32),
                   jax.ShapeDtypeStruct((hist, batch, dc), F32)],
        compiler_params=_params("parallel"),
        name="dwconv_sample",
    )(glu, states, w_dw, b2)


def _conf_tail_kernel(y_ref, gl_ref, bl_ref, w_ref, b2_ref, x_ref, gate_ref, g_ref, b_ref, o_ref):
    yn = _silu(_layer_norm(y_ref[...], gl_ref[...], bl_ref[...]))
    y = _dot(yn.astype(BF16), w_ref[...]) + b2_ref[...]
    v = ALPHA * x_ref[...] + (1.0 + gate_ref[...]) * y
    o_ref[...] = _layer_norm(v, g_ref[...], b_ref[...])


def conformer_tail(y, cf_g, cf_b, w2_bf, b2, x2d, gate, ln_g, ln_b, rows_per_group, tm):
    n, d = x2d.shape
    dc = y.shape[1]
    vecc = pl.BlockSpec((1, dc), lambda i: (0, 0))
    vec = pl.BlockSpec((1, d), lambda i: (0, 0))
    return pl.pallas_call(
        _conf_tail_kernel,
        grid=(n // tm,),
        in_specs=[pl.BlockSpec((tm, dc), lambda i: (i, 0)), vecc, vecc,
                  _resident(w2_bf), vec,
                  pl.BlockSpec((tm, d), lambda i: (i, 0)),
                  _mod_spec(gate, tm, rows_per_group), vec, vec],
        out_specs=pl.BlockSpec((tm, d), lambda i: (i, 0)),
        out_shape=jax.ShapeDtypeStruct((n, d), F32),
        compiler_params=_params("parallel"),
        name="conformer_tail",
    )(y, cf_g.reshape(1, dc), cf_b.reshape(1, dc), w2_bf, b2.reshape(1, d), x2d, gate,
      ln_g.reshape(1, d), ln_b.reshape(1, d))


def kernel(x_prompt, x_sample, cache_k, cache_v, state_ret, state_conv, state_ffn, page_table, c_prompt, c_sample, ab_w_in, ab_w_out, cf_w_pw1, cf_b_pw1, cf_w_dw, cf_b_dw, cf_ln_g, cf_ln_b, cf_w_pw2, cf_b_pw2, ffn_w_up, ffn_w_dw, ffn_b_dw, ffn_w_down, ada_w, ada_b, ln_g, ln_b):
    bp, seq, d = x_prompt.shape
    bs, ts, _ = x_sample.shape
    n_layers = ada_w.shape[0]
    d_ff = ffn_w_down.shape[1]
    past_len = page_table.shape[1] * cache_k.shape[2]
    np_, ns = bp * seq, bs * ts
    tm_p = min(256, seq)
    tm_s = bs

    ada = adaln_all(jnp.concatenate([c_prompt, c_sample], 0), ada_w, ada_b)

    def mods(l, k):
        return ada[l, k, :bp][:, None, :], ada[l, k, bp:][None]

    token_major = lambda a: jnp.swapaxes(a, 0, 1)
    xp = x_prompt.reshape(np_, d)
    xs = token_major(x_sample).reshape(ns, d)
    tabs_p = rotary_tables(jnp.arange(seq, dtype=jnp.int32))
    tabs_s = tuple(jnp.repeat(tb, bs, axis=0) for tb in
                   rotary_tables(past_len + jnp.arange(ts, dtype=jnp.int32)))

    w_up = ffn_w_up.astype(BF16)
    w_dn = ffn_w_down.astype(BF16)
    outs = {k: [] for k in ("kp", "vp", "ks", "vs", "rp", "rs", "cp", "cs", "fp", "fs")}
    for l in range(n_layers):
        i = l // 2
        sh_p, sh_s = mods(l, 0)
        sc_p, sc_s = mods(l, 1)
        gt_p, gt_s = mods(l, 2)
        if l % 2 == 0:
            w_in = ab_w_in[i].astype(BF16)
            w_out = ab_w_out[i].astype(BF16)
            rq, rk, rv, rg, mq, mk, mv, kb, vb, kmean = in_proj(
                xp, sh_p, sc_p, w_in, tabs_p, seq, tm_p, True, BF16)
            o_rp, s_p = retention_prompt(rq, rk, rv, rg, bp, seq)
            o_mp = moba_prompt(mq, kb, vb, kmean.reshape(-1, GROUP_W), bp, seq)
            outs["kp"].append(mk.reshape(bp, seq, N_HEADS, HEAD_DIM))
            outs["vp"].append(mv.reshape(bp, seq, N_HEADS, HEAD_DIM))
            outs["rp"].append(s_p)

            rq, rk, rv, rg, mq, mk, mv, kb, vb = in_proj(xs, sh_s, sc_s, w_in, tabs_s, ns, tm_s, False, F32)
            o_r, s_s = retention_sample(rq, rk, rv, rg, state_ret, i, ts)
            n_phys, page = cache_k.shape[1], cache_k.shape[2]
            ck = cache_k.reshape(-1, HEAD_DIM)
            cv = cache_v.reshape(-1, HEAD_DIM)
            mods_ffn_p = tuple(mods(l, k)[0] for k in (3, 4, 5))
            o_m, xp, prev_p = moba_sample_ffn_prompt(
                mq, kb, vb, ck, cv, page_table, i * n_phys, page, ts,
                o_rp, o_mp, w_out, xp, gt_p, ln_g[l, 0], ln_b[l, 0], *mods_ffn_p, w_up, ffn_w_dw[l], ffn_b_dw[l], w_dn, l, ln_g[l, 1], ln_b[l, 1], seq, tm_p)
            o_m = token_major(o_m[:, :ts]).reshape(ns, GROUP_W)
            xs = out_proj(o_r, o_m, w_out, xs, gt_s, ln_g[l, 0], ln_b[l, 0], ns, tm_s)
            outs["ks"].append(token_major(mk.reshape(ts, bs, N_HEADS, HEAD_DIM)))
            outs["vs"].append(token_major(mv.reshape(ts, bs, N_HEADS, HEAD_DIM)))
            outs["rs"].append(s_s)
        else:
            w1 = cf_w_pw1[i].astype(BF16)
            w2 = cf_w_pw2[i].astype(BF16)
            hist = cf_w_dw.shape[1] - 1
            mods_ffn_p = tuple(mods(l, k)[0] for k in (3, 4, 5))
            xp, tail, prev_p = conformer_ffn_prompt(
                xp, (sh_p, sc_p, gt_p), mods_ffn_p, w1, cf_b_pw1[i], cf_w_dw[i], cf_b_dw[i],
                cf_ln_g[i], cf_ln_b[i], w2, cf_b_pw2[i], ln_g[l, 0], ln_b[l, 0],
                w_up, ffn_w_dw[l], ffn_b_dw[l], w_dn, l, ln_g[l, 1], ln_b[l, 1], bp, seq, tm_p)
            outs["cp"].append(tail[:, PAD_ROWS - hist:])

            glu = conformer_glu(xs, sh_s, sc_s, w1, cf_b_pw1[i], ns, tm_s)
            y, nst = dwconv_sample(glu.reshape(ts, bs, -1), jnp.swapaxes(state_conv, 1, 2), i,
                                   cf_w_dw[i], cf_b_dw[i])
            xs = conformer_tail(y.reshape(ns, -1), cf_ln_g[i], cf_ln_b[i], w2, cf_b_pw2[i], xs, gt_s,
                                ln_g[l, 0], ln_b[l, 0], ns, tm_s)
            outs["cs"].append(token_major(nst))

        sh_p, sh_s = mods(l, 3)
        sc_p, sc_s = mods(l, 4)
        gt_p, gt_s = mods(l, 5)
        outs["fp"].append(prev_p.reshape(bp, seq // tm_p, 2, d_ff)[:, -1])
        xs, u_s = conv_ffn(xs, sh_s, sc_s, gt_s, w_up, ffn_w_dw[l], ffn_b_dw[l], w_dn, l,
                           ln_g[l, 1], ln_b[l, 1], ns, tm_s, ts,
                           history=(state_ffn[l, :, 0], state_ffn[l, :, 1]))
        outs["fs"].append(token_major(u_s.reshape(ts, bs, d_ff)[ts - 2:]))

    st = lambda k: outs[k][0][None] if len(outs[k]) == 1 else jnp.stack(outs[k])
    return (xp.reshape(bp, seq, d), token_major(xs.reshape(ts, bs, d)), st("kp"), st("vp"), st("ks"), st("vs"),
            st("rp"), st("rs"), st("cp"), st("cs"), st("fp"), st("fs"))
```

```python
import functools

import jax
import jax.numpy as jnp
from jax import lax
from jax.experimental import pallas as pl
from jax.experimental.pallas import tpu as pltpu

HEAD_DIM = 128
N_HEADS = 4
GROUP_W = N_HEADS * HEAD_DIM
N_IN_COLS = 7
RET_CHUNK = 128
MOBA_BLOCK = 256
MOBA_TOPK = 3
ROPE_THETA = 10000.0
DEPTH = 2
ALPHA = (2 * DEPTH) ** 0.25
LN_EPS = 1e-5
GN_EPS = 1e-6
NEG = -1e30
LOG2_E = 1.4426950408889634

F32 = jnp.float32
BF16 = jnp.bfloat16

_NT = (((1,), (1,)), ((), ()))
_TN = (((0,), (0,)), ((), ()))

VMEM_LIMIT = 56 * 1024 * 1024
SUBLANES = 8


def _params(*sem):
    return pltpu.CompilerParams(dimension_semantics=sem, vmem_limit_bytes=VMEM_LIMIT)


def _dot(a, b):
    return jnp.dot(a, b, preferred_element_type=F32)


def _layer_norm(v, g, b):
    mu = jnp.mean(v, axis=-1, keepdims=True)
    d = v - mu
    var = jnp.mean(d * d, axis=-1, keepdims=True)
    return d * lax.rsqrt(var + LN_EPS) * g + b


def _silu(x):
    return x * jax.nn.sigmoid(x)


def _mod_spec(mod, tm, rows_per_group):
    _, r, d = mod.shape
    if r == 1:
        return pl.BlockSpec((None, 1, d), lambda i: (i * tm // rows_per_group, 0, 0))
    assert r % tm == 0
    return pl.BlockSpec((None, tm, d), lambda i: (0, i % (r // tm), 0))


def _resident(a, layer=None):
    if layer is None:
        return pl.BlockSpec(a.shape, lambda *_: (0,) * a.ndim, pipeline_mode=pl.Buffered(1))
    return pl.BlockSpec((None,) + a.shape[1:], lambda *_: (layer,) + (0,) * (a.ndim - 1),
                        pipeline_mode=pl.Buffered(1))


def _adaln_kernel(c_ref, w_ref, b_ref, o_ref):
    a = _silu(c_ref[...]).astype(BF16)
    o_ref[...] = _dot(a, w_ref[...].astype(BF16)) + b_ref[...]


def adaln_all(c_all, ada_w, ada_b):
    n_layers, d, n_out = ada_w.shape
    m = c_all.shape[0]
    return pl.pallas_call(
        _adaln_kernel,
        grid=(n_layers, n_out // d),
        in_specs=[
            pl.BlockSpec((m, d), lambda l, j: (0, 0)),
            pl.BlockSpec((None, d, d), lambda l, j: (l, 0, j)),
            pl.BlockSpec((None, 1, d), lambda l, j: (l, 0, j)),
        ],
        out_specs=pl.BlockSpec((None, None, m, d), lambda l, j: (l, j, 0, 0)),
        out_shape=jax.ShapeDtypeStruct((n_layers, n_out // d, m, d), F32),
        compiler_params=_params("parallel", "parallel"),
        name="adaln",
    )(c_all, ada_w, ada_b.reshape(n_layers, 1, n_out))


def _in_proj_kernel(x_ref, sh_ref, sc_ref, w_ref, cr_ref, sr_ref, cm_ref, sm_ref,
                    rq_ref, rk_ref, rv_ref, rg_ref, mq_ref, mk_ref, mv_ref, kb_ref, vb_ref,
                    *km_refs):
    tm = x_ref.shape[0]
    h = (x_ref[...] * (1.0 + sc_ref[...]) + sh_ref[...]).astype(BF16)
    cr, sr, cm, sm = cr_ref[...], sr_ref[...], cm_ref[...], sm_ref[...]
    even = (lax.broadcasted_iota(jnp.int32, cr.shape, 1) & 1) == 0
    outs = (rq_ref, rk_ref, rv_ref, rg_ref, mq_ref, mk_ref, mv_ref)
    for g, o_ref in enumerate(outs):
        z = _dot(h, w_ref[:, g * GROUP_W:(g + 1) * GROUP_W])
        for hh in range(N_HEADS):
            cols = slice(hh * HEAD_DIM, (hh + 1) * HEAD_DIM)
            zs = z[:, cols]
            if g in (0, 1):
                nxt = pltpu.roll(zs, HEAD_DIM - 1, axis=1)
                prv = pltpu.roll(zs, 1, axis=1)
                zs = zs * cr + jnp.where(even, nxt, prv) * sr
                if g == 1:
                    zs = zs * (HEAD_DIM ** -0.5)
            elif g in (4, 5):
                zs = zs * cm + pltpu.roll(zs, HEAD_DIM // 2, axis=1) * sm
            if g in (5, 6):
                o_ref[pl.ds(hh, tm, stride=N_HEADS), :] = zs
                head_major = kb_ref if g == 5 else vb_ref
                head_major[:, cols] = zs.astype(head_major.dtype)
                if g == 5 and km_refs:
                    for r in range(tm // MOBA_BLOCK):
                        blk = zs[r * MOBA_BLOCK:(r + 1) * MOBA_BLOCK, :]
                        km_refs[0][r:r + 1, cols] = jnp.sum(blk, axis=0, keepdims=True) * (1.0 / MOBA_BLOCK)
            else:
                o_ref[:, cols] = zs.astype(o_ref.dtype)


def in_proj(x2d, shift, scale, w_bf, tabs, rows_per_group, tm, block_means, mxu_dtype):
    n, d = x2d.shape
    p_tiles = tabs[0].shape[0] // tm
    mod_spec = _mod_spec(shift, tm, rows_per_group)
    tab_spec = pl.BlockSpec((tm, HEAD_DIM), lambda i: (i % p_tiles, 0))
    wide = pl.BlockSpec((tm, GROUP_W), lambda i: (i, 0))
    tall = pl.BlockSpec((tm * N_HEADS, HEAD_DIM), lambda i: (i, 0))
    wide_shape = lambda dt: jax.ShapeDtypeStruct((n, GROUP_W), dt)
    tall_shape = jax.ShapeDtypeStruct((n * N_HEADS, HEAD_DIM), F32)
    out_specs = [wide] * 5 + [tall, tall, wide, wide]
    out_shape = [wide_shape(F32), wide_shape(F32), wide_shape(mxu_dtype), wide_shape(F32),
                 wide_shape(mxu_dtype), tall_shape, tall_shape, wide_shape(mxu_dtype), wide_shape(mxu_dtype)]
    if block_means:
        assert tm % MOBA_BLOCK == 0
        per = tm // MOBA_BLOCK
        out_specs.append(pl.BlockSpec((None, per, GROUP_W), lambda i: (i, 0, 0)))
        out_shape.append(jax.ShapeDtypeStruct((n // tm, per, GROUP_W), F32))
    return pl.pallas_call(
        _in_proj_kernel,
        grid=(n // tm,),
        in_specs=[pl.BlockSpec((tm, d), lambda i: (i, 0)), mod_spec, mod_spec,
                  _resident(w_bf), tab_spec, tab_spec, tab_spec, tab_spec],
        out_specs=out_specs,
        out_shape=out_shape,
        compiler_params=_params("parallel"),
        name="in_proj",
    )(x2d, shift, scale, w_bf, *tabs)


def rotary_tables(pos):
    half = HEAD_DIM // 2
    posf = pos.astype(F32)[:, None]
    inv_r = 1.0 / (ROPE_THETA ** jnp.linspace(0.0, 1.0, half, dtype=F32))
    ang_r = posf * inv_r[None, :]
    cr = jnp.repeat(jnp.cos(ang_r), 2, axis=1)
    sr = jnp.stack([-jnp.sin(ang_r), jnp.sin(ang_r)], -1).reshape(-1, HEAD_DIM)
    inv_m = ROPE_THETA ** (-jnp.arange(0, HEAD_DIM, 2, dtype=F32) / HEAD_DIM)
    ang_m = posf * inv_m[None, :]
    cm = jnp.concatenate([jnp.cos(ang_m), jnp.cos(ang_m)], -1)
    sm = jnp.concatenate([-jnp.sin(ang_m), jnp.sin(ang_m)], -1)
    return cr, sr, cm, sm


def retention_tables(chunk, rows):
    log_g = jnp.log1p(-jnp.exp2(-5.0 - jnp.arange(N_HEADS, dtype=F32)))
    idx = jnp.arange(chunk, dtype=F32)
    diff = idx[:, None] - idx[None, :]
    decay_in = jnp.where(diff[None] >= 0,
                         jnp.exp(jnp.maximum(diff, 0.0)[None] * log_g[:, None, None]), 0.0)
    q_dec = jnp.exp((idx + 1.0)[None, :] * log_g[:, None])
    k_dec = jnp.exp((chunk - 1.0 - idx)[None, :] * log_g[:, None])
    c_dec = jnp.exp(chunk * log_g)
    pad = rows - chunk
    decay_in = jnp.pad(decay_in, ((0, 0), (0, pad), (0, rows - chunk)))
    bc = lambda t: jnp.broadcast_to(jnp.pad(t, ((0, 0), (0, pad)))[:, :, None], (N_HEADS, rows, HEAD_DIM))
    c_b = jnp.broadcast_to(c_dec[:, None, None], (N_HEADS, 1, HEAD_DIM))
    return decay_in, bc(q_dec), bc(k_dec), c_b


def _ret_chunk(qc, kc, vc, s, din, qd, kd, cd):
    att = lax.dot_general(qc.astype(BF16), kc.astype(BF16), _NT,
                          preferred_element_type=F32) * din
    o = _dot(att.astype(BF16), vc) + _dot((qc * qd).astype(BF16), s.astype(BF16))
    s_new = cd * s + lax.dot_general((kc * kd).astype(BF16), vc, _TN,
                                     preferred_element_type=F32)
    return o, s_new


def _group_norm_gate(o, g):
    o = o * lax.rsqrt(jnp.mean(o * o, axis=-1, keepdims=True) + GN_EPS)
    return o * _silu(g)


def _ret_prompt_kernel(q_ref, k_ref, v_ref, g_ref, din_ref, qd_ref, kd_ref, cd_ref,
                       o_ref, sout_ref, s_ref, *, n_sub):
    j = pl.program_id(1)
    chunk = din_ref.shape[1]

    @pl.when(j == 0)
    def _():
        s_ref[...] = jnp.zeros_like(s_ref)

    for hh in range(N_HEADS):
        cols = slice(hh * HEAD_DIM, (hh + 1) * HEAD_DIM)
        din, qd, kd, cd = din_ref[hh], qd_ref[hh], kd_ref[hh], cd_ref[hh]
        for c in range(n_sub):
            rows = slice(c * chunk, (c + 1) * chunk)
            o, s_new = _ret_chunk(q_ref[rows, cols], k_ref[rows, cols], v_ref[rows, cols],
                                  s_ref[hh], din, qd, kd, cd)
            s_ref[hh] = s_new
            o_ref[rows, cols] = _group_norm_gate(o, g_ref[rows, cols]).astype(o_ref.dtype)

    @pl.when(j == pl.num_programs(1) - 1)
    def _():
        sout_ref[...] = s_ref[...]


def retention_prompt(rq, rk, rv, rg, batch, seq):
    chunk = min(2 * RET_CHUNK, seq)
    tc = min(2 * chunk, seq)
    nj = seq // tc
    tabs = retention_tables(chunk, chunk)
    tok = pl.BlockSpec((tc, GROUP_W), lambda b, j: (b * nj + j, 0))
    tab = pl.BlockSpec((N_HEADS, chunk, HEAD_DIM), lambda b, j: (0, 0, 0))
    return pl.pallas_call(
        functools.partial(_ret_prompt_kernel, n_sub=tc // chunk),
        grid=(batch, nj),
        in_specs=[tok, tok, tok, tok, pl.BlockSpec((N_HEADS, chunk, chunk), lambda b, j: (0, 0, 0)), tab, tab,
                  pl.BlockSpec((N_HEADS, 1, HEAD_DIM), lambda b, j: (0, 0, 0))],
        out_specs=[tok, pl.BlockSpec((None, N_HEADS, HEAD_DIM, HEAD_DIM), lambda b, j: (b, 0, 0, 0))],
        out_shape=[jax.ShapeDtypeStruct((batch * seq, GROUP_W), BF16),
                   jax.ShapeDtypeStruct((batch, N_HEADS, HEAD_DIM, HEAD_DIM), F32)],
        scratch_shapes=[pltpu.VMEM((N_HEADS, HEAD_DIM, HEAD_DIM), F32)],
        compiler_params=_params("parallel", "arbitrary"),
        name="retention_prompt",
    )(rq, rk, rv, rg, *tabs)


def _group_rows(ref, group, batch, t, cols):
    return [ref[pl.ds(pl.multiple_of(ti * batch + group * SUBLANES, SUBLANES), SUBLANES), cols]
            for ti in range(t)]


def _seq_tile(token_rows, row, n_rows):
    width = token_rows[0].shape[1]
    r_i = lax.broadcasted_iota(jnp.int32, (n_rows, width), 0)
    g_i = lax.broadcasted_iota(jnp.int32, token_rows[0].shape, 0)
    out = jnp.zeros((n_rows, width), F32)
    for ti, rows in enumerate(token_rows):
        if isinstance(row, int):
            picked = rows[row:row + 1, :]
        else:
            picked = jnp.sum(jnp.where(g_i == row, rows, 0.0), axis=0, keepdims=True)
        out = jnp.where(r_i == ti, picked, out)
    return out


def _ret_sample_kernel(q_ref, k_ref, v_ref, g_ref, s0_ref, din_ref, qd_ref, kd_ref, cd_ref,
                       o_ref, sout_ref, *, t, batch):
    group = pl.program_id(0)
    r_i = lax.broadcasted_iota(jnp.int32, (SUBLANES, HEAD_DIM), 0)
    for hh in range(N_HEADS):
        cols = slice(hh * HEAD_DIM, (hh + 1) * HEAD_DIM)
        din, qd, kd, cd = din_ref[hh], qd_ref[hh], kd_ref[hh], cd_ref[hh]
        q_t, k_t, v_t, g_t = (_group_rows(r, group, batch, t, cols) for r in (q_ref, k_ref, v_ref, g_ref))
        out_t = [jnp.zeros((SUBLANES, HEAD_DIM), F32) for _ in range(t)]
        for bl in range(SUBLANES):
            qc = _seq_tile(q_t, bl, RET_CHUNK)
            kc = _seq_tile(k_t, bl, RET_CHUNK)
            vc = _seq_tile(v_t, bl, RET_CHUNK).astype(BF16)
            o, s_new = _ret_chunk(qc, kc, vc, s0_ref[bl, hh], din, qd, kd, cd)
            sout_ref[bl, hh] = s_new
            gated = _group_norm_gate(o[0:SUBLANES, :], _seq_tile(g_t, bl, SUBLANES))
            for ti in range(t):
                out_t[ti] = jnp.where(r_i == bl, gated[ti:ti + 1, :], out_t[ti])
        for ti in range(t):
            o_ref[pl.ds(pl.multiple_of(ti * batch + group * SUBLANES, SUBLANES), SUBLANES), cols] = out_t[ti]


def retention_sample(rq, rk, rv, rg, states, layer, t):
    batch = states.shape[1]
    bb = SUBLANES
    assert batch % bb == 0
    tabs = retention_tables(t, RET_CHUNK)
    tok = pl.BlockSpec(rq.shape, lambda i: (0, 0))
    st_in = pl.BlockSpec((None, bb, N_HEADS, HEAD_DIM, HEAD_DIM), lambda i: (layer, i, 0, 0, 0))
    st = pl.BlockSpec((bb, N_HEADS, HEAD_DIM, HEAD_DIM), lambda i: (i, 0, 0, 0))
    tab = pl.BlockSpec((N_HEADS, RET_CHUNK, HEAD_DIM), lambda i: (0, 0, 0))
    return pl.pallas_call(
        functools.partial(_ret_sample_kernel, t=t, batch=batch),
        grid=(batch // bb,),
        in_specs=[tok, tok, tok, tok, st_in, tab, tab, tab,
                  pl.BlockSpec((N_HEADS, 1, HEAD_DIM), lambda i: (0, 0, 0))],
        out_specs=[tok, st],
        out_shape=[jax.ShapeDtypeStruct(rq.shape, F32),
                   jax.ShapeDtypeStruct(states.shape[1:], F32)],
        compiler_params=_params("arbitrary"),
        name="retention_sample",
    )(rq, rk, rv, rg, states, *tabs)


def _beaten_counts(rows, n_valid):
    counts = []
    for n in range(n_valid):
        cnt = jnp.zeros(rows[n].shape, jnp.int32)
        for m in range(n_valid):
            if m == n:
                continue
            beats = (rows[m] >= rows[n]) if m < n else (rows[m] > rows[n])
            cnt = cnt + beats.astype(jnp.int32)
        counts.append(cnt)
    return counts


def _moba_prompt_kernel(q_ref, k_ref, v_ref, km_ref, o_ref, s0_ref, s1_ref, *, nblk):
    blk = MOBA_BLOCK
    scale = HEAD_DIM ** -0.5
    kb = k_ref[...]
    vt = v_ref[...].astype(F32).T.astype(BF16)
    kmean = km_ref[...].astype(BF16)
    key_i = lax.broadcasted_iota(jnp.int32, (blk, blk), 0)
    qry_i = lax.broadcasted_iota(jnp.int32, (blk, blk), 1)
    causal = key_i <= qry_i

    def query_block(qi, s_ref):
        qb = q_ref[qi * blk:(qi + 1) * blk, :]
        bias = None
        if qi > MOBA_TOPK:
            gt = lax.dot_general(kmean, qb, _NT, preferred_element_type=F32)
            cnt = _beaten_counts([gt[m:m + 1, :] for m in range(qi)], qi)
            bias = [jnp.where(c < MOBA_TOPK, 0.0, NEG) for c in cnt]
        mx = jnp.full((1, blk), NEG, F32)
        for n in range(qi + 1):
            s = lax.dot_general(kb[n * blk:(n + 1) * blk], qb, _NT,
                                preferred_element_type=F32)
            if n == qi:
                s = jnp.where(causal, s, NEG)
            elif bias is not None:
                s = s + bias[n]
            s_ref[n * blk:(n + 1) * blk, :] = s
            mx = jnp.maximum(mx, jnp.max(s, axis=0, keepdims=True))
            yield
        l = jnp.zeros((1, blk), F32)
        ot = jnp.zeros((HEAD_DIM, blk), F32)
        for n in range(qi + 1):
            p = jnp.exp2((s_ref[n * blk:(n + 1) * blk, :] - mx) * (scale * LOG2_E))
            l = l + jnp.sum(p, axis=0, keepdims=True)
            ot = ot + _dot(vt[:, n * blk:(n + 1) * blk], p.astype(BF16))
            yield
        ot = ot * (1.0 / l)
        o_ref[qi * blk:(qi + 1) * blk, :] = ot.T.astype(o_ref.dtype)
        yield

    blocks = [query_block(qi, (s0_ref, s1_ref)[qi % 2]) for qi in range(nblk)]
    for _ in range(1):
        next(blocks[0])
    for qi in range(nblk):
        value_steps = qi + 2
        score_steps = qi + 2 if qi + 1 < nblk else 0
        for step in range(max(value_steps, score_steps)):
            if step < score_steps:
                next(blocks[qi + 1])
            if step < value_steps:
                next(blocks[qi])


def moba_prompt(mq, mk, mv, kmean, batch, seq):
    nblk = seq // MOBA_BLOCK
    assert nblk % SUBLANES == 0
    spec = pl.BlockSpec((seq, HEAD_DIM), lambda b, h: (b, h))
    return pl.pallas_call(
        functools.partial(_moba_prompt_kernel, nblk=nblk),
        grid=(batch, N_HEADS),
        in_specs=[spec, spec, spec, pl.BlockSpec((nblk, HEAD_DIM), lambda b, h: (b, h))],
        out_specs=spec,
        out_shape=jax.ShapeDtypeStruct((batch * seq, GROUP_W), BF16),
        scratch_shapes=[pltpu.VMEM((seq, MOBA_BLOCK), F32), pltpu.VMEM((seq, MOBA_BLOCK), F32)],
        compiler_params=_params("parallel", "parallel"),
        name="moba_prompt",
    )(mq, mk, mv, kmean)


def _page_rows(ref, page):
    return jnp.concatenate([ref[pl.ds(hh, page, stride=N_HEADS), :] for hh in range(N_HEADS)], axis=1)


def _moba_sample_phases(b, q_ref, kn_ref, vn_ref, k_refs, v_refs, o_ref, *, page, t):
    n_pages = len(k_refs)
    scale = HEAD_DIM ** -0.5
    ppb = MOBA_BLOCK // page
    nblk = n_pages // ppb
    n_pairs = N_HEADS * t
    batch = q_ref.shape[0] // t
    all_cols = slice(0, GROUP_W)
    group = b // SUBLANES
    row = b % SUBLANES
    q = _seq_tile(_group_rows(q_ref, group, batch, t, all_cols), row, SUBLANES)

    r_i = lax.broadcasted_iota(jnp.int32, (n_pairs, GROUP_W), 0)
    l_i = lax.broadcasted_iota(jnp.int32, (n_pairs, GROUP_W), 1)
    qm = jnp.zeros((n_pairs, GROUP_W), F32)
    for ti in range(t):
        for hh in range(N_HEADS):
            hit = (r_i == hh * t + ti) & (l_i >= hh * HEAD_DIM) & (l_i < (hh + 1) * HEAD_DIM)
            qm = jnp.where(hit, q[ti:ti + 1, :], qm)
    qm = qm.astype(BF16)

    zpad = jnp.zeros((page - SUBLANES, GROUP_W), F32)
    own = lambda ref: jnp.concatenate(
        [_seq_tile(_group_rows(ref, group, batch, t, all_cols), row, SUBLANES), zpad], axis=0).astype(BF16)
    k_own, v_own = own(kn_ref), own(vn_ref)

    b_i = lax.broadcasted_iota(jnp.int32, (page, GROUP_W), 0)
    kmean = jnp.zeros((page, GROUP_W), F32)
    scores = []
    for p in range(n_pages):
        kp = _page_rows(k_refs[p], page)
        scores.append(lax.dot_general(qm, kp.astype(BF16), _NT,
                                      preferred_element_type=F32) * scale)
        ks = jnp.sum(kp, axis=0, keepdims=True) * (1.0 / MOBA_BLOCK)
        kmean = kmean + jnp.where(b_i == p // ppb, ks, 0.0)
        yield
    gate = lax.dot_general(qm, kmean.astype(BF16), _NT, preferred_element_type=F32)
    cnt = _beaten_counts([gate[:, m:m + 1] for m in range(nblk)], nblk)
    keep = [jnp.broadcast_to(c, (n_pairs, page)) < MOBA_TOPK for c in cnt]

    s_own = lax.dot_general(qm, k_own, _NT, preferred_element_type=F32) * scale
    key_i = lax.broadcasted_iota(jnp.int32, (n_pairs, page), 1)
    tok_i = lax.rem(lax.broadcasted_iota(jnp.int32, (n_pairs, page), 0), t)
    s_own = jnp.where((key_i <= tok_i) & (key_i < t), s_own, NEG)

    mx = s_own
    for p in range(n_pages):
        scores[p] = jnp.where(keep[p // ppb], scores[p], NEG)
        mx = jnp.maximum(mx, scores[p])
    mx = jnp.max(mx, axis=1, keepdims=True)
    e_own = jnp.exp(s_own - mx)
    l = e_own
    probs = []
    for p in range(n_pages):
        e = jnp.exp(scores[p] - mx)
        l = l + e
        probs.append(e.astype(BF16))
    inv_l = 1.0 / jnp.sum(l, axis=1, keepdims=True)
    yield
    acc = _dot(e_own.astype(BF16), v_own)
    for p in range(n_pages):
        acc = acc + _dot(probs[p], _page_rows(v_refs[p], page).astype(BF16))
        yield
    acc = acc * inv_l
    o_ref[...] = jnp.zeros_like(o_ref)
    for hh in range(N_HEADS):
        cols = slice(hh * HEAD_DIM, (hh + 1) * HEAD_DIM)
        o_ref[0:t, cols] = acc[hh * t:(hh + 1) * t, cols]
    yield


def _mixer_out(or_ref, om_ref, w_ref, x_ref, gate_ref, g_ref, b_ref, o_ref, sub):
    for r0 in range(0, x_ref.shape[0], sub):
        rows = slice(r0, r0 + sub)
        y = (_dot(or_ref[rows, :].astype(BF16), w_ref[:GROUP_W, :])
             + _dot(om_ref[rows, :].astype(BF16), w_ref[GROUP_W:, :]))
        gate = gate_ref[...] if gate_ref.shape[0] == 1 else gate_ref[rows, :]
        v = ALPHA * x_ref[rows, :] + (1.0 + gate) * y
        o_ref[rows, :] = _layer_norm(v, g_ref[...], b_ref[...])


def _out_proj_kernel(or_ref, om_ref, w_ref, x_ref, gate_ref, g_ref, b_ref, o_ref, *, sub):
    _mixer_out(or_ref, om_ref, w_ref, x_ref, gate_ref, g_ref, b_ref, o_ref, sub)


def out_proj(o_r, o_m, w_bf, x2d, gate, ln_g, ln_b, rows_per_group, tm):
    n, d = x2d.shape
    vec = pl.BlockSpec((1, d), lambda i: (0, 0))
    return pl.pallas_call(
        functools.partial(_out_proj_kernel, sub=min(128, tm)),
        grid=(n // tm,),
        in_specs=[pl.BlockSpec((tm, GROUP_W), lambda i: (i, 0)),
                  pl.BlockSpec((tm, GROUP_W), lambda i: (i, 0)),
                  _resident(w_bf),
                  pl.BlockSpec((tm, d), lambda i: (i, 0)),
                  _mod_spec(gate, tm, rows_per_group), vec, vec],
        out_specs=pl.BlockSpec((tm, d), lambda i: (i, 0)),
        out_shape=jax.ShapeDtypeStruct((n, d), F32),
        compiler_params=_params("parallel"),
        name="out_proj",
    )(o_r, o_m, w_bf, x2d, gate, ln_g.reshape(1, d), ln_b.reshape(1, d))


def _run(phases):
    for _ in phases:
        pass


def _ffn_phases(x, sh_ref, sc_ref, gate_ref, wup_ref, wdw_ref, bdw_ref, wdn_ref, g_ref, b_ref,
                o_ref, prev_ref, h_s, act_s, hist_s, *, d_ff, cw, planes, chunks=None):
    tm = x.shape[0]
    n_chunks = d_ff // cw
    chunks = range(n_chunks) if chunks is None else chunks
    if chunks.start == 0:
        h_s[...] = (x * (1.0 + sc_ref[...]) + sh_ref[...]).astype(BF16)
    row = lax.broadcasted_iota(jnp.int32, (tm, cw), 0)
    for c in chunks:
        cu = slice(c * cw, (c + 1) * cw)
        cv = slice(d_ff + c * cw, d_ff + (c + 1) * cw)
        u = _dot(h_s[...], wup_ref[:, cu])
        v = _dot(h_s[...], wup_ref[:, cv])
        if planes:
            s2 = hist_s[0, :, cu]
            s1 = hist_s[1, :, cu]
            hist_s[0, :, cu] = s1
            hist_s[1, :, cu] = u
            prev_ref[:, cu] = u
        else:
            e1 = hist_s[1:2, cu]
            e2 = jnp.where(row == 0, hist_s[0:1, cu], hist_s[1:2, cu])
            hist_s[:, cu] = u[tm - 2:tm, :]
            prev_ref[:, cu] = u[tm - 2:tm, :]
            s1 = jnp.where(row >= 1, pltpu.roll(u, 1, axis=0), e1)
            s2 = jnp.where(row >= 2, pltpu.roll(u, 2, axis=0), e2)
        uc = wdw_ref[0:1, cu] * s2 + wdw_ref[1:2, cu] * s1 + wdw_ref[2:3, cu] * u + bdw_ref[:, cu]
        act = 0.5 * uc * (1.0 + lax.erf(uc * (2.0 ** -0.5))) * v
        act_s[:, cu] = act.astype(BF16)
        yield act[0:1, 0:HEAD_DIM]
    if chunks.stop != n_chunks:
        return
    fy = _dot(act_s[...], wdn_ref[...])
    vsum = ALPHA * x + (1.0 + gate_ref[...]) * fy
    o_ref[...] = _layer_norm(vsum, g_ref[...], b_ref[...])
    yield


def _ffn_kernel(*refs, d_ff, cw, tiles_per_seq, planes):
    if planes:
        (x_ref, sh_ref, sc_ref, gate_ref, wup_ref, wdw_ref, bdw_ref, wdn_ref, g_ref, b_ref,
         p0_ref, p1_ref, o_ref, prev_ref, h_s, act_s, hist_s) = refs
    else:
        (x_ref, sh_ref, sc_ref, gate_ref, wup_ref, wdw_ref, bdw_ref, wdn_ref, g_ref, b_ref,
         o_ref, prev_ref, h_s, act_s, hist_s) = refs

    @pl.when(pl.program_id(0) % tiles_per_seq == 0)
    def _():
        if planes:
            hist_s[0] = p0_ref[...]
            hist_s[1] = p1_ref[...]
        else:
            hist_s[...] = jnp.zeros_like(hist_s)

    _run(_ffn_phases(x_ref[...], sh_ref, sc_ref, gate_ref, wup_ref, wdw_ref, bdw_ref, wdn_ref,
                     g_ref, b_ref, o_ref, prev_ref, h_s, act_s, hist_s, d_ff=d_ff, cw=cw, planes=planes))


def conv_ffn(x2d, shift, scale, gate, w_up_bf, w_dw, b_dw, w_dn_bf, layer, ln_g, ln_b,
             rows_per_group, tm, tiles_per_seq, history=None):
    n, d = x2d.shape
    d_ff = w_dn_bf.shape[1]
    cw = 256
    planes = history is not None
    mod = _mod_spec(shift, tm, rows_per_group)
    vec = pl.BlockSpec((1, d), lambda i: (0, 0))
    b_dw2 = b_dw.reshape(1, d_ff)
    in_specs = [pl.BlockSpec((tm, d), lambda i: (i, 0)), mod, mod, mod,
                _resident(w_up_bf, layer), _resident(w_dw), _resident(b_dw2),
                _resident(w_dn_bf, layer), vec, vec]
    args = [x2d, shift, scale, gate, w_up_bf, w_dw, b_dw2, w_dn_bf,
            ln_g.reshape(1, d), ln_b.reshape(1, d)]
    scratch = [pltpu.VMEM((tm, d), BF16), pltpu.VMEM((tm, d_ff), BF16)]
    if planes:
        in_specs += [_resident(history[0]), _resident(history[1])]
        args += list(history)
        prev_spec = pl.BlockSpec((tm, d_ff), lambda i: (i, 0))
        prev_shape = jax.ShapeDtypeStruct((n, d_ff), F32)
        scratch.append(pltpu.VMEM((2, tm, d_ff), F32))
    else:
        prev_spec = pl.BlockSpec((None, 2, d_ff), lambda i: (i, 0, 0))
        prev_shape = jax.ShapeDtypeStruct((n // tm, 2, d_ff), F32)
        scratch.append(pltpu.VMEM((2, d_ff), F32))
    return pl.pallas_call(
        functools.partial(_ffn_kernel, d_ff=d_ff, cw=cw, tiles_per_seq=tiles_per_seq, planes=planes),
        grid=(n // tm,),
        in_specs=in_specs,
        out_specs=[pl.BlockSpec((tm, d), lambda i: (i, 0)), prev_spec],
        out_shape=[jax.ShapeDtypeStruct((n, d), F32), prev_shape],
        scratch_shapes=scratch,
        compiler_params=_params("arbitrary"),
        name="conv_ffn",
    )(*args)


def _moba_ffn_kernel(pt_ref, q_ref, kn_ref, vn_ref, *rest, n_pages, page, t, d_ff, cw, parts, tiles_per_seq):
    k_refs = rest[:n_pages]
    v_refs = rest[n_pages:2 * n_pages]
    (or_ref, omp_ref, wout_ref, x_ref, gate0_ref, g0_ref, b0_ref,
     sh_ref, sc_ref, gate_ref, wup_ref, wdw_ref, bdw_ref, wdn_ref, g_ref, b_ref,
     om_ref, o_ref, prev_ref, x1_s, h_s, act_s, hist_s) = rest[2 * n_pages:]
    i = pl.program_id(0)
    n_chunks = d_ff // cw
    per_part = -(-n_chunks // parts)

    @pl.when(i % (parts * tiles_per_seq) == 0)
    def _():
        hist_s[...] = jnp.zeros_like(hist_s)

    def step(part):
        chunks = range(part * per_part, min((part + 1) * per_part, n_chunks))
        if part == 0:
            _mixer_out(or_ref, omp_ref, wout_ref, x_ref, gate0_ref, g0_ref, b0_ref, x1_s, sub=128)
        ffn_ph = _ffn_phases(x1_s[...], sh_ref, sc_ref, gate_ref, wup_ref, wdw_ref, bdw_ref, wdn_ref,
                             g_ref, b_ref, o_ref, prev_ref, h_s, act_s, hist_s,
                             d_ff=d_ff, cw=cw, planes=False, chunks=chunks)
        moba_ph = _moba_sample_phases(i, q_ref, kn_ref, vn_ref, k_refs, v_refs, om_ref, page=page, t=t)
        for _ in range(n_pages):
            next(moba_ph)
        next(ffn_ph)
        next(moba_ph)
        for _ in range(len(chunks) - 1):
            next(ffn_ph)
        for _ in range(n_pages + 1):
            next(moba_ph)
        _run(ffn_ph)

    for part in range(parts):
        pl.when(i % parts == part)(functools.partial(step, part))


def moba_sample_ffn_prompt(mq, mk, mv, cache_k, cache_v, page_table, page_base, page, t,
                           o_r, o_m, w_out_bf, x2d, gate0, ln_g0, ln_b0,
                           shift, scale, gate, w_up_bf, w_dw, b_dw, w_dn_bf, layer, ln_g, ln_b,
                           seq, tm):
    batch, n_pages = page_table.shape
    n, d = x2d.shape
    d_ff = w_dn_bf.shape[1]
    assert (n_pages * page) % MOBA_BLOCK == 0 and t <= SUBLANES
    assert n_pages * page // MOBA_BLOCK <= page
    n_tiles = n // tm
    assert batch % n_tiles == 0
    parts = batch // n_tiles
    tok = pl.BlockSpec(mq.shape, lambda i, pt: (0, 0))

    def page_spec(p):
        return pl.BlockSpec((page * N_HEADS, HEAD_DIM), lambda i, pt: (page_base + pt[i, p], 0))

    const = lambda a, lead=None: pl.BlockSpec(
        a.shape if lead is None else (None,) + a.shape[1:],
        (lambda i, pt: (0,) * a.ndim) if lead is None else (lambda i, pt: (lead,) + (0,) * (a.ndim - 1)),
        pipeline_mode=pl.Buffered(1))
    mod = pl.BlockSpec((None, 1, d), lambda i, pt: ((i // parts) * tm // seq, 0, 0))
    b_dw2, g2, b2 = b_dw.reshape(1, d_ff), ln_g.reshape(1, d), ln_b.reshape(1, d)
    g0, b0 = ln_g0.reshape(1, d), ln_b0.reshape(1, d)
    tile = lambda w: pl.BlockSpec((tm, w), lambda i, pt: (i // parts, 0))
    specs = ([tok, tok, tok] + [page_spec(p) for p in range(n_pages)] * 2
             + [tile(GROUP_W), tile(GROUP_W), const(w_out_bf), tile(d), mod, const(g0), const(b0), mod, mod, mod,
                const(w_up_bf, layer), const(w_dw), const(b_dw2), const(w_dn_bf, layer), const(g2), const(b2)])
    grid_spec = pltpu.PrefetchScalarGridSpec(
        num_scalar_prefetch=1,
        grid=(batch,),
        in_specs=specs,
        out_specs=[pl.BlockSpec((None, SUBLANES, GROUP_W), lambda i, pt: (i, 0, 0)),
                   pl.BlockSpec((tm, d), lambda i, pt: (i // parts, 0)),
                   pl.BlockSpec((None, 2, d_ff), lambda i, pt: (i // parts, 0, 0))],
        scratch_shapes=[pltpu.VMEM((tm, d), F32), pltpu.VMEM((tm, d), BF16), pltpu.VMEM((tm, d_ff), BF16),
                        pltpu.VMEM((2, d_ff), F32)],
    )
    return pl.pallas_call(
        functools.partial(_moba_ffn_kernel, n_pages=n_pages, page=page, t=t, d_ff=d_ff, cw=256,
                          parts=parts, tiles_per_seq=seq // tm),
        grid_spec=grid_spec,
        out_shape=[jax.ShapeDtypeStruct((batch, SUBLANES, GROUP_W), F32),
                   jax.ShapeDtypeStruct((n, d), F32),
                   jax.ShapeDtypeStruct((n_tiles, 2, d_ff), F32)],
        compiler_params=_params("arbitrary"),
        name="moba_sample_ffn_prompt",
    )(page_table, mq, mk, mv, *([cache_k] * n_pages), *([cache_v] * n_pages),
      o_r, o_m, w_out_bf, x2d, gate0, g0, b0, shift, scale, gate, w_up_bf, w_dw, b_dw2, w_dn_bf, g2, b2)


def _glu_kernel(x_ref, sh_ref, sc_ref, w_ref, b_ref, o_ref, *, cw):
    h = (x_ref[...] * (1.0 + sc_ref[...]) + sh_ref[...]).astype(BF16)
    dc = o_ref.shape[1]
    for c in range(dc // cw):
        ca = slice(c * cw, (c + 1) * cw)
        cg = slice(dc + c * cw, dc + (c + 1) * cw)
        a = _dot(h, w_ref[:, ca]) + b_ref[:, ca]
        g = _dot(h, w_ref[:, cg]) + b_ref[:, cg]
        o_ref[:, ca] = a * jax.nn.sigmoid(g)


def conformer_glu(x2d, shift, scale, w1_bf, b1, rows_per_group, tm):
    n, d = x2d.shape
    dc = w1_bf.shape[1] // 2
    mod = _mod_spec(shift, tm, rows_per_group)
    b1r = b1.reshape(1, 2 * dc)
    return pl.pallas_call(
        functools.partial(_glu_kernel, cw=256),
        grid=(n // tm,),
        in_specs=[pl.BlockSpec((tm, d), lambda i: (i, 0)), mod, mod,
                  _resident(w1_bf), _resident(b1r)],
        out_specs=pl.BlockSpec((tm, dc), lambda i: (i, 0)),
        out_shape=jax.ShapeDtypeStruct((n, dc), F32),
        compiler_params=_params("parallel"),
        name="conformer_glu",
    )(x2d, shift, scale, w1_bf, b1r)


PAD_ROWS = 32


def _dwconv_rows(win_ref, w_ref, bias, r0, rs, cols, width):
    off = PAD_ROWS - (width - 1)
    y = bias
    for r in range(SUBLANES):
        z = None
        for m in range((off + width - 1) // SUBLANES + 1):
            k = SUBLANES * m + r - off
            if 0 <= k < width:
                lo = r0 + SUBLANES * m
                term = w_ref[k:k + 1, cols] * win_ref[lo:lo + rs + SUBLANES, cols]
                z = term if z is None else z + term
        if z is not None:
            y = y + z[r:r + rs, :]
    return y


def _ordered_after(x, token):
    zero = lax.shift_right_logical(lax.shift_right_logical(
        lax.bitcast_convert_type(token, jnp.uint32), jnp.uint32(16)), jnp.uint32(16))
    return lax.bitcast_convert_type(lax.bitcast_convert_type(x, jnp.uint32) | zero, F32)


def _conformer_phases(x, sh_ref, sc_ref, gate_ref, w1_ref, b1_ref, wdw_ref, bdw_ref,
                      gl_ref, bl_ref, w2_ref, b2_ref, g_ref, b_ref, o_ref, tail_ref, win_ref, y_ref,
                      *, width, rs, cw, after=None):
    tm, dc = y_ref.shape
    h = (x * (1.0 + sc_ref[...]) + sh_ref[...]).astype(BF16)
    for c in range(dc // cw):
        ca = slice(c * cw, (c + 1) * cw)
        cg = slice(dc + c * cw, dc + (c + 1) * cw)
        a = _dot(h, w1_ref[:, ca]) + b1_ref[:, ca]
        g = _dot(h, w1_ref[:, cg]) + b1_ref[:, cg]
        win_ref[PAD_ROWS:PAD_ROWS + tm, ca] = a * jax.nn.sigmoid(g)
    tail_ref[...] = win_ref[tm:tm + PAD_ROWS, :]
    yield
    for c in range(dc // HEAD_DIM):
        cols = slice(c * HEAD_DIM, (c + 1) * HEAD_DIM)
        for r0 in range(0, tm, rs):
            bias = bdw_ref[:, cols]
            if after is not None and after[0] is not None:
                bias = _ordered_after(bias, after[0])
            y_ref[r0:r0 + rs, cols] = _dwconv_rows(win_ref, wdw_ref, bias, r0, rs, cols, width)
            yield
    win_ref[0:PAD_ROWS, :] = win_ref[tm:tm + PAD_ROWS, :]
    yn = _silu(_layer_norm(y_ref[...], gl_ref[...], bl_ref[...]))
    y = _dot(yn.astype(BF16), w2_ref[...]) + b2_ref[...]
    v = ALPHA * x + (1.0 + gate_ref[...]) * y
    o_ref[...] = _layer_norm(v, g_ref[...], b_ref[...])
    yield


N_CONF_CONSTS = 10
N_FFN_CONSTS = 6


def _conformer_ffn_kernel(*refs, tiles_per_seq, width, rs, cw, d_ff):
    x_ref, shc_ref, scc_ref, gtc_ref, shf_ref, scf_ref, gtf_ref = refs[:7]
    conf = refs[7:7 + N_CONF_CONSTS]
    ffn = refs[7 + N_CONF_CONSTS:7 + N_CONF_CONSTS + N_FFN_CONSTS]
    o_ref, tail_ref, prev_ref, win_ref, y_ref, x1_s, h_s, act_s, hist_s = refs[7 + N_CONF_CONSTS + N_FFN_CONSTS:]
    i = pl.program_id(0)
    n_tiles = pl.num_programs(0) - 1
    tm, dc = y_ref.shape

    @pl.when(i == 0)
    def _():
        x1_s[...] = jnp.zeros_like(x1_s)
        win_ref[PAD_ROWS + tm:, :] = jnp.zeros((SUBLANES, dc), F32)

    @pl.when(jnp.minimum(i, n_tiles - 1) % tiles_per_seq == 0)
    def _():
        win_ref[0:PAD_ROWS, :] = jnp.zeros((PAD_ROWS, dc), F32)

    @pl.when(jnp.maximum(i - 1, 0) % tiles_per_seq == 0)
    def _():
        hist_s[...] = jnp.zeros_like(hist_s)

    ffn_ph = _ffn_phases(x1_s[...], shf_ref, scf_ref, gtf_ref, *ffn, o_ref, prev_ref, h_s, act_s, hist_s,
                         d_ff=d_ff, cw=cw, planes=False)
    token = [None]
    conf_ph = _conformer_phases(x_ref[...], shc_ref, scc_ref, gtc_ref, *conf, x1_s, tail_ref, win_ref, y_ref,
                                width=width, rs=rs, cw=cw, after=token)
    next(conf_ph)
    n_ffn = d_ff // cw
    n_conv = (dc // HEAD_DIM) * (tm // rs)
    per_chunk = -(-n_conv // n_ffn)
    for _ in range(n_ffn):
        token[0] = next(ffn_ph)
        for _ in range(per_chunk):
            next(conf_ph, None)
    _run(ffn_ph)
    _run(conf_ph)


def conformer_ffn_prompt(x2d, mods_conf, mods_ffn, w1_bf, b1, w_dw, b_dw, cf_g, cf_b, w2_bf, b2,
                         ln_g0, ln_b0, w_up_bf, ffn_w_dw, ffn_b_dw, w_dn_bf, layer, ln_g1, ln_b1,
                         batch, seq, tm):
    n, d = x2d.shape
    dc = w2_bf.shape[0]
    d_ff = w_dn_bf.shape[1]
    width = w_dw.shape[0]
    assert width - 1 <= PAD_ROWS and tm % PAD_ROWS == 0
    nj = seq // tm
    n_tiles = n // tm
    row = lambda a: a.reshape(1, -1)
    conf_tile = lambda i: jnp.minimum(i, n_tiles - 1)
    ffn_tile = lambda i: jnp.maximum(i - 1, 0)
    mod_c = pl.BlockSpec((None, 1, d), lambda i: (conf_tile(i) // nj, 0, 0))
    mod_f = pl.BlockSpec((None, 1, d), lambda i: (ffn_tile(i) // nj, 0, 0))
    conf = [w1_bf, row(b1), w_dw, row(b_dw), row(cf_g), row(cf_b), w2_bf, row(b2), row(ln_g0), row(ln_b0)]
    ffn = [ffn_w_dw, row(ffn_b_dw), row(ln_g1), row(ln_b1)]
    assert len(conf) == N_CONF_CONSTS and len(ffn) + 2 == N_FFN_CONSTS
    ffn_specs = [_resident(w_up_bf, layer), _resident(ffn[0]), _resident(ffn[1]),
                 _resident(w_dn_bf, layer), _resident(ffn[2]), _resident(ffn[3])]
    ffn_args = [w_up_bf, ffn[0], ffn[1], w_dn_bf, ffn[2], ffn[3]]
    return pl.pallas_call(
        functools.partial(_conformer_ffn_kernel, tiles_per_seq=nj, width=width, rs=64, cw=256, d_ff=d_ff),
        grid=(n_tiles + 1,),
        in_specs=[pl.BlockSpec((tm, d), lambda i: (conf_tile(i), 0)), mod_c, mod_c, mod_c,
                  mod_f, mod_f, mod_f] + [_resident(a) for a in conf] + ffn_specs,
        out_specs=[pl.BlockSpec((tm, d), lambda i: (ffn_tile(i), 0)),
                   pl.BlockSpec((None, PAD_ROWS, dc), lambda i: (conf_tile(i) // nj, 0, 0)),
                   pl.BlockSpec((None, 2, d_ff), lambda i: (ffn_tile(i), 0, 0))],
        out_shape=[jax.ShapeDtypeStruct((n, d), F32),
                   jax.ShapeDtypeStruct((batch, PAD_ROWS, dc), F32),
                   jax.ShapeDtypeStruct((n_tiles, 2, d_ff), F32)],
        scratch_shapes=[pltpu.VMEM((PAD_ROWS + tm + SUBLANES, dc), F32), pltpu.VMEM((tm, dc), F32),
                        pltpu.VMEM((tm, d), F32), pltpu.VMEM((tm, d), BF16),
                        pltpu.VMEM((tm, d_ff), BF16), pltpu.VMEM((2, d_ff), F32)],
        compiler_params=_params("arbitrary"),
        name="conformer_ffn_prompt",
    )(x2d, *mods_conf, *mods_ffn, *conf, *ffn_args)


def _dwconv_sample_kernel(x_ref, st_ref, w_ref, b_ref, o_ref, nst_ref, *, width, t):
    hist = width - 1
    bb, dc = x_ref.shape[1], x_ref.shape[2]
    plane = lambda j, cols: st_ref[j, :, cols] if j < hist else x_ref[j - hist, :, cols]
    for c in range(dc // HEAD_DIM):
        cols = slice(c * HEAD_DIM, (c + 1) * HEAD_DIM)
        acc = [jnp.zeros((bb, HEAD_DIM), F32) + b_ref[:, cols] for _ in range(t)]
        for j in range(hist + t):
            p = plane(j, cols)
            for ti in range(t):
                if 0 <= j - ti < width:
                    acc[ti] = acc[ti] + w_ref[j - ti:j - ti + 1, cols] * p
            if j >= t:
                nst_ref[j - t, :, cols] = p
        for ti in range(t):
            o_ref[ti, :, cols] = acc[ti]


def dwconv_sample(glu, states, layer, w_dw, b_dw):
    t, batch, dc = glu.shape
    width = w_dw.shape[0]
    hist = width - 1
    bb = 32
    b2 = b_dw.reshape(1, dc)
    return pl.pallas_call(
        functools.partial(_dwconv_sample_kernel, width=width, t=t),
        grid=(batch // bb,),
        in_specs=[pl.BlockSpec((t, bb, dc), lambda i: (0, i, 0)),
                  pl.BlockSpec((None, hist, bb, dc), lambda i: (layer, 0, i, 0)),
                  pl.BlockSpec(w_dw.shape, lambda i: (0, 0)),
                  pl.BlockSpec(b2.shape, lambda i: (0, 0))],
        out_specs=[pl.BlockSpec((t, bb, dc), lambda i: (0, i, 0)),
                   pl.BlockSpec((hist, bb, dc), lambda i: (0, i, 0))],
        out_shape=[jax.ShapeDtypeStruct((t, batch, dc), F32),
                   jax.ShapeDtypeStruct((hist, batch, dc), F32)],
        compiler_params=_params("parallel"),
        name="dwconv_sample",
    )(glu, states, w_dw, b2)


def _conf_tail_kernel(y_ref, gl_ref, bl_ref, w_ref, b2_ref, x_ref, gate_ref, g_ref, b_ref, o_ref):
    yn = _silu(_layer_norm(y_ref[...], gl_ref[...], bl_ref[...]))
    y = _dot(yn.astype(BF16), w_ref[...]) + b2_ref[...]
    v = ALPHA * x_ref[...] + (1.0 + gate_ref[...]) * y
    o_ref[...] = _layer_norm(v, g_ref[...], b_ref[...])


def conformer_tail(y, cf_g, cf_b, w2_bf, b2, x2d, gate, ln_g, ln_b, rows_per_group, tm):
    n, d = x2d.shape
    dc = y.shape[1]
    vecc = pl.BlockSpec((1, dc), lambda i: (0, 0))
    vec = pl.BlockSpec((1, d), lambda i: (0, 0))
    return pl.pallas_call(
        _conf_tail_kernel,
        grid=(n // tm,),
        in_specs=[pl.BlockSpec((tm, dc), lambda i: (i, 0)), vecc, vecc,
                  _resident(w2_bf), vec,
                  pl.BlockSpec((tm, d), lambda i: (i, 0)),
                  _mod_spec(gate, tm, rows_per_group), vec, vec],
        out_specs=pl.BlockSpec((tm, d), lambda i: (i, 0)),
        out_shape=jax.ShapeDtypeStruct((n, d), F32),
        compiler_params=_params("parallel"),
        name="conformer_tail",
    )(y, cf_g.reshape(1, dc), cf_b.reshape(1, dc), w2_bf, b2.reshape(1, d), x2d, gate,
      ln_g.reshape(1, d), ln_b.reshape(1, d))


def kernel(x_prompt, x_sample, cache_k, cache_v, state_ret, state_conv, state_ffn, page_table, c_prompt, c_sample, ab_w_in, ab_w_out, cf_w_pw1, cf_b_pw1, cf_w_dw, cf_b_dw, cf_ln_g, cf_ln_b, cf_w_pw2, cf_b_pw2, ffn_w_up, ffn_w_dw, ffn_b_dw, ffn_w_down, ada_w, ada_b, ln_g, ln_b):
    bp, seq, d = x_prompt.shape
    bs, ts, _ = x_sample.shape
    n_layers = ada_w.shape[0]
    d_ff = ffn_w_down.shape[1]
    past_len = page_table.shape[1] * cache_k.shape[2]
    np_, ns = bp * seq, bs * ts
    tm_p = min(256, seq)
    tm_s = bs

    ada = adaln_all(jnp.concatenate([c_prompt, c_sample], 0), ada_w, ada_b)

    def mods(l, k):
        return ada[l, k, :bp][:, None, :], ada[l, k, bp:][None]

    token_major = lambda a: jnp.swapaxes(a, 0, 1)
    xp = x_prompt.reshape(np_, d)
    xs = token_major(x_sample).reshape(ns, d)
    tabs_p = rotary_tables(jnp.arange(seq, dtype=jnp.int32))
    tabs_s = tuple(jnp.repeat(tb, bs, axis=0) for tb in
                   rotary_tables(past_len + jnp.arange(ts, dtype=jnp.int32)))

    w_up = ffn_w_up.astype(BF16)
    w_dn = ffn_w_down.astype(BF16)
    outs = {k: [] for k in ("kp", "vp", "ks", "vs", "rp", "rs", "cp", "cs", "fp", "fs")}
    for l in range(n_layers):
        i = l // 2
        sh_p, sh_s = mods(l, 0)
        sc_p, sc_s = mods(l, 1)
        gt_p, gt_s = mods(l, 2)
        if l % 2 == 0:
            w_in = ab_w_in[i].astype(BF16)
            w_out = ab_w_out[i].astype(BF16)
            rq, rk, rv, rg, mq, mk, mv, kb, vb, kmean = in_proj(
                xp, sh_p, sc_p, w_in, tabs_p, seq, tm_p, True, BF16)
            o_rp, s_p = retention_prompt(rq, rk, rv, rg, bp, seq)
            o_mp = moba_prompt(mq, kb, vb, kmean.reshape(-1, GROUP_W), bp, seq)
            outs["kp"].append(mk.reshape(bp, seq, N_HEADS, HEAD_DIM))
            outs["vp"].append(mv.reshape(bp, seq, N_HEADS, HEAD_DIM))
            outs["rp"].append(s_p)

            rq, rk, rv, rg, mq, mk, mv, kb, vb = in_proj(xs, sh_s, sc_s, w_in, tabs_s, ns, tm_s, False, F32)
            o_r, s_s = retention_sample(rq, rk, rv, rg, state_ret, i, ts)
            n_phys, page = cache_k.shape[1], cache_k.shape[2]
            ck = cache_k.reshape(-1, HEAD_DIM)
            cv = cache_v.reshape(-1, HEAD_DIM)
            mods_ffn_p = tuple(mods(l, k)[0] for k in (3, 4, 5))
            o_m, xp, prev_p = moba_sample_ffn_prompt(
                mq, kb, vb, ck, cv, page_table, i * n_phys, page, ts,
                o_rp, o_mp, w_out, xp, gt_p, ln_g[l, 0], ln_b[l, 0], *mods_ffn_p, w_up, ffn_w_dw[l], ffn_b_dw[l], w_dn, l, ln_g[l, 1], ln_b[l, 1], seq, tm_p)
            o_m = token_major(o_m[:, :ts]).reshape(ns, GROUP_W)
            xs = out_proj(o_r, o_m, w_out, xs, gt_s, ln_g[l, 0], ln_b[l, 0], ns, tm_s)
            outs["ks"].append(token_major(mk.reshape(ts, bs, N_HEADS, HEAD_DIM)))
            outs["vs"].append(token_major(mv.reshape(ts, bs, N_HEADS, HEAD_DIM)))
            outs["rs"].append(s_s)
        else:
            w1 = cf_w_pw1[i].astype(BF16)
            w2 = cf_w_pw2[i].astype(BF16)
            hist = cf_w_dw.shape[1] - 1
            mods_ffn_p = tuple(mods(l, k)[0] for k in (3, 4, 5))
            xp, tail, prev_p = conformer_ffn_prompt(
                xp, (sh_p, sc_p, gt_p), mods_ffn_p, w1, cf_b_pw1[i], cf_w_dw[i], cf_b_dw[i],
                cf_ln_g[i], cf_ln_b[i], w2, cf_b_pw2[i], ln_g[l, 0], ln_b[l, 0],
                w_up, ffn_w_dw[l], ffn_b_dw[l], w_dn, l, ln_g[l, 1], ln_b[l, 1], bp, seq, tm_p)
            outs["cp"].append(tail[:, PAD_ROWS - hist:])

            glu = conformer_glu(xs, sh_s, sc_s, w1, cf_b_pw1[i], ns, tm_s)
            y, nst = dwconv_sample(glu.reshape(ts, bs, -1), jnp.swapaxes(state_conv, 1, 2), i,
                                   cf_w_dw[i], cf_b_dw[i])
            xs = conformer_tail(y.reshape(ns, -1), cf_ln_g[i], cf_ln_b[i], w2, cf_b_pw2[i], xs, gt_s,
                                ln_g[l, 0], ln_b[l, 0], ns, tm_s)
            outs["cs"].append(token_major(nst))

        sh_p, sh_s = mods(l, 3)
        sc_p, sc_s = mods(l, 4)
        gt_p, gt_s = mods(l, 5)
        outs["fp"].append(prev_p.reshape(bp, seq // tm_p, 2, d_ff)[:, -1])
        xs, u_s = conv_ffn(xs, sh_s, sc_s, gt_s, w_up, ffn_w_dw[l], ffn_b_dw[l], w_dn, l,
                           ln_g[l, 1], ln_b[l, 1], ns, tm_s, ts,
                           history=(state_ffn[l, :, 0], state_ffn[l, :, 1]))
        outs["fs"].append(token_major(u_s.reshape(ts, bs, d_ff)[ts - 2:]))

    st = lambda k: outs[k][0][None] if len(outs[k]) == 1 else jnp.stack(outs[k])
    return (xp.reshape(bp, seq, d), token_major(xs.reshape(ts, bs, d)), st("kp"), st("vp"), st("ks"), st("vs"),
            st("rp"), st("rs"), st("cp"), st("cs"), st("fp"), st("fs"))
```

```python
import functools

import jax
import jax.numpy as jnp
from jax import lax
from jax.experimental import pallas as pl
from jax.experimental.pallas import tpu as pltpu

HEAD_DIM = 128
N_HEADS = 4
GROUP_W = N_HEADS * HEAD_DIM
N_IN_COLS = 7
RET_CHUNK = 128
MOBA_BLOCK = 256
MOBA_TOPK = 3
ROPE_THETA = 10000.0
DEPTH = 2
ALPHA = (2 * DEPTH) ** 0.25
LN_EPS = 1e-5
GN_EPS = 1e-6
NEG = -1e30
LOG2_E = 1.4426950408889634

F32 = jnp.float32
BF16 = jnp.bfloat16

_NT = (((1,), (1,)), ((), ()))
_TN = (((0,), (0,)), ((), ()))

VMEM_LIMIT = 56 * 1024 * 1024
SUBLANES = 8


def _params(*sem):
    return pltpu.CompilerParams(dimension_semantics=sem, vmem_limit_bytes=VMEM_LIMIT)


def _dot(a, b):
    return jnp.dot(a, b, preferred_element_type=F32)


def _layer_norm(v, g, b):
    mu = jnp.mean(v, axis=-1, keepdims=True)
    d = v - mu
    var = jnp.mean(d * d, axis=-1, keepdims=True)
    return d * lax.rsqrt(var + LN_EPS) * g + b


def _silu(x):
    return x * jax.nn.sigmoid(x)


def _mod_spec(mod, tm, rows_per_group):
    _, r, d = mod.shape
    if r == 1:
        return pl.BlockSpec((None, 1, d), lambda i: (i * tm // rows_per_group, 0, 0))
    assert r % tm == 0
    return pl.BlockSpec((None, tm, d), lambda i: (0, i % (r // tm), 0))


def _resident(a, layer=None):
    if layer is None:
        return pl.BlockSpec(a.shape, lambda *_: (0,) * a.ndim, pipeline_mode=pl.Buffered(1))
    return pl.BlockSpec((None,) + a.shape[1:], lambda *_: (layer,) + (0,) * (a.ndim - 1),
                        pipeline_mode=pl.Buffered(1))


def _adaln_kernel(c_ref, w_ref, b_ref, o_ref):
    a = _silu(c_ref[...]).astype(BF16)
    o_ref[...] = _dot(a, w_ref[...].astype(BF16)) + b_ref[...]


def adaln_all(c_all, ada_w, ada_b):
    n_layers, d, n_out = ada_w.shape
    m = c_all.shape[0]
    return pl.pallas_call(
        _adaln_kernel,
        grid=(n_layers, n_out // d),
        in_specs=[
            pl.BlockSpec((m, d), lambda l, j: (0, 0)),
            pl.BlockSpec((None, d, d), lambda l, j: (l, 0, j)),
            pl.BlockSpec((None, 1, d), lambda l, j: (l, 0, j)),
        ],
        out_specs=pl.BlockSpec((None, None, m, d), lambda l, j: (l, j, 0, 0)),
        out_shape=jax.ShapeDtypeStruct((n_layers, n_out // d, m, d), F32),
        compiler_params=_params("parallel", "parallel"),
        name="adaln",
    )(c_all, ada_w, ada_b.reshape(n_layers, 1, n_out))


def _in_proj_kernel(x_ref, sh_ref, sc_ref, w_ref, cr_ref, sr_ref, cm_ref, sm_ref,
                    rq_ref, rk_ref, rv_ref, rg_ref, mq_ref, mk_ref, mv_ref, kb_ref, vb_ref,
                    *km_refs):
    tm = x_ref.shape[0]
    h = (x_ref[...] * (1.0 + sc_ref[...]) + sh_ref[...]).astype(BF16)
    cr, sr, cm, sm = cr_ref[...], sr_ref[...], cm_ref[...], sm_ref[...]
    even = (lax.broadcasted_iota(jnp.int32, cr.shape, 1) & 1) == 0
    outs = (rq_ref, rk_ref, rv_ref, rg_ref, mq_ref, mk_ref, mv_ref)
    for g, o_ref in enumerate(outs):
        z = _dot(h, w_ref[:, g * GROUP_W:(g + 1) * GROUP_W])
        for hh in range(N_HEADS):
            cols = slice(hh * HEAD_DIM, (hh + 1) * HEAD_DIM)
            zs = z[:, cols]
            if g in (0, 1):
                nxt = pltpu.roll(zs, HEAD_DIM - 1, axis=1)
                prv = pltpu.roll(zs, 1, axis=1)
                zs = zs * cr + jnp.where(even, nxt, prv) * sr
                if g == 1:
                    zs = zs * (HEAD_DIM ** -0.5)
            elif g in (4, 5):
                zs = zs * cm + pltpu.roll(zs, HEAD_DIM // 2, axis=1) * sm
            if g in (5, 6):
                o_ref[pl.ds(hh, tm, stride=N_HEADS), :] = zs
                head_major = kb_ref if g == 5 else vb_ref
                head_major[:, cols] = zs.astype(head_major.dtype)
                if g == 5 and km_refs:
                    for r in range(tm // MOBA_BLOCK):
                        blk = zs[r * MOBA_BLOCK:(r + 1) * MOBA_BLOCK, :]
                        km_refs[0][r:r + 1, cols] = jnp.sum(blk, axis=0, keepdims=True) * (1.0 / MOBA_BLOCK)
            else:
                o_ref[:, cols] = zs.astype(o_ref.dtype)


def in_proj(x2d, shift, scale, w_bf, tabs, rows_per_group, tm, block_means, act_dtype):
    n, d = x2d.shape
    p_tiles = tabs[0].shape[0] // tm
    mod_spec = _mod_spec(shift, tm, rows_per_group)
    tab_spec = pl.BlockSpec((tm, HEAD_DIM), lambda i: (i % p_tiles, 0))
    wide = pl.BlockSpec((tm, GROUP_W), lambda i: (i, 0))
    tall = pl.BlockSpec((tm * N_HEADS, HEAD_DIM), lambda i: (i, 0))
    wide_shape = lambda dt: jax.ShapeDtypeStruct((n, GROUP_W), dt)
    tall_shape = jax.ShapeDtypeStruct((n * N_HEADS, HEAD_DIM), F32)
    out_specs = [wide] * 5 + [tall, tall, wide, wide]
    out_shape = [wide_shape(act_dtype)] * 5 + [tall_shape, tall_shape, wide_shape(act_dtype), wide_shape(act_dtype)]
    if block_means:
        assert tm % MOBA_BLOCK == 0
        per = tm // MOBA_BLOCK
        out_specs.append(pl.BlockSpec((None, per, GROUP_W), lambda i: (i, 0, 0)))
        out_shape.append(jax.ShapeDtypeStruct((n // tm, per, GROUP_W), F32))
    return pl.pallas_call(
        _in_proj_kernel,
        grid=(n // tm,),
        in_specs=[pl.BlockSpec((tm, d), lambda i: (i, 0)), mod_spec, mod_spec,
                  _resident(w_bf), tab_spec, tab_spec, tab_spec, tab_spec],
        out_specs=out_specs,
        out_shape=out_shape,
        compiler_params=_params("parallel"),
        name="in_proj",
    )(x2d, shift, scale, w_bf, *tabs)


def rotary_tables(pos):
    half = HEAD_DIM // 2
    posf = pos.astype(F32)[:, None]
    inv_r = 1.0 / (ROPE_THETA ** jnp.linspace(0.0, 1.0, half, dtype=F32))
    ang_r = posf * inv_r[None, :]
    cr = jnp.repeat(jnp.cos(ang_r), 2, axis=1)
    sr = jnp.stack([-jnp.sin(ang_r), jnp.sin(ang_r)], -1).reshape(-1, HEAD_DIM)
    inv_m = ROPE_THETA ** (-jnp.arange(0, HEAD_DIM, 2, dtype=F32) / HEAD_DIM)
    ang_m = posf * inv_m[None, :]
    cm = jnp.concatenate([jnp.cos(ang_m), jnp.cos(ang_m)], -1)
    sm = jnp.concatenate([-jnp.sin(ang_m), jnp.sin(ang_m)], -1)
    return cr, sr, cm, sm


def retention_tables(chunk, rows):
    log_g = jnp.log1p(-jnp.exp2(-5.0 - jnp.arange(N_HEADS, dtype=F32)))
    idx = jnp.arange(chunk, dtype=F32)
    diff = idx[:, None] - idx[None, :]
    decay_in = jnp.where(diff[None] >= 0,
                         jnp.exp(jnp.maximum(diff, 0.0)[None] * log_g[:, None, None]), 0.0)
    q_dec = jnp.exp((idx + 1.0)[None, :] * log_g[:, None])
    k_dec = jnp.exp((chunk - 1.0 - idx)[None, :] * log_g[:, None])
    c_dec = jnp.exp(chunk * log_g)
    pad = rows - chunk
    decay_in = jnp.pad(decay_in, ((0, 0), (0, pad), (0, rows - chunk)))
    bc = lambda t: jnp.broadcast_to(jnp.pad(t, ((0, 0), (0, pad)))[:, :, None], (N_HEADS, rows, HEAD_DIM))
    c_b = jnp.broadcast_to(c_dec[:, None, None], (N_HEADS, 1, HEAD_DIM))
    return decay_in, bc(q_dec), bc(k_dec), c_b


def _ret_chunk(qc, kc, vc, s, din, qd, kd, cd):
    att = lax.dot_general(qc.astype(BF16), kc.astype(BF16), _NT,
                          preferred_element_type=F32) * din
    o = _dot(att.astype(BF16), vc) + _dot((qc.astype(F32) * qd).astype(BF16), s.astype(BF16))
    s_new = cd * s + lax.dot_general((kc.astype(F32) * kd).astype(BF16), vc, _TN,
                                     preferred_element_type=F32)
    return o, s_new


def _group_norm_gate(o, g):
    o = o * lax.rsqrt(jnp.mean(o * o, axis=-1, keepdims=True) + GN_EPS)
    return o * _silu(g)


def _ret_prompt_phases(q_ref, k_ref, v_ref, g_ref, din_ref, qd_ref, kd_ref, cd_ref, o_ref, sout_ref):
    chunk = din_ref.shape[0]
    din, qd, kd, cd = din_ref[...], qd_ref[...], kd_ref[...], cd_ref[...]
    s = jnp.zeros((HEAD_DIM, HEAD_DIM), F32)
    for c in range(q_ref.shape[0] // chunk):
        rows = slice(c * chunk, (c + 1) * chunk)
        o, s = _ret_chunk(q_ref[rows, :], k_ref[rows, :], v_ref[rows, :], s, din, qd, kd, cd)
        o_ref[rows, :] = _group_norm_gate(o, g_ref[rows, :].astype(F32)).astype(o_ref.dtype)
        yield
    sout_ref[...] = s
    yield


def _group_rows(ref, group, batch, t, cols):
    return [ref[pl.ds(pl.multiple_of(ti * batch + group * SUBLANES, SUBLANES), SUBLANES), cols]
            for ti in range(t)]


def _seq_tile(token_rows, row, n_rows):
    width = token_rows[0].shape[1]
    r_i = lax.broadcasted_iota(jnp.int32, (n_rows, width), 0)
    g_i = lax.broadcasted_iota(jnp.int32, token_rows[0].shape, 0)
    out = jnp.zeros((n_rows, width), F32)
    for ti, rows in enumerate(token_rows):
        if isinstance(row, int):
            picked = rows[row:row + 1, :]
        else:
            picked = jnp.sum(jnp.where(g_i == row, rows, 0.0), axis=0, keepdims=True)
        out = jnp.where(r_i == ti, picked, out)
    return out


def _ret_sample_kernel(q_ref, k_ref, v_ref, g_ref, s0_ref, din_ref, qd_ref, kd_ref, cd_ref,
                       o_ref, sout_ref, *, t, batch):
    group = pl.program_id(0)
    r_i = lax.broadcasted_iota(jnp.int32, (SUBLANES, HEAD_DIM), 0)
    for hh in range(N_HEADS):
        cols = slice(hh * HEAD_DIM, (hh + 1) * HEAD_DIM)
        din, qd, kd, cd = din_ref[hh], qd_ref[hh], kd_ref[hh], cd_ref[hh]
        q_t, k_t, v_t, g_t = (_group_rows(r, group, batch, t, cols) for r in (q_ref, k_ref, v_ref, g_ref))
        out_t = [jnp.zeros((SUBLANES, HEAD_DIM), F32) for _ in range(t)]
        for bl in range(SUBLANES):
            qc = _seq_tile(q_t, bl, RET_CHUNK)
            kc = _seq_tile(k_t, bl, RET_CHUNK)
            vc = _seq_tile(v_t, bl, RET_CHUNK).astype(BF16)
            o, s_new = _ret_chunk(qc, kc, vc, s0_ref[bl, hh], din, qd, kd, cd)
            sout_ref[bl, hh] = s_new
            gated = _group_norm_gate(o[0:SUBLANES, :], _seq_tile(g_t, bl, SUBLANES))
            for ti in range(t):
                out_t[ti] = jnp.where(r_i == bl, gated[ti:ti + 1, :], out_t[ti])
        for ti in range(t):
            o_ref[pl.ds(pl.multiple_of(ti * batch + group * SUBLANES, SUBLANES), SUBLANES), cols] = out_t[ti]


def retention_sample(rq, rk, rv, rg, states, layer, t):
    batch = states.shape[1]
    bb = SUBLANES
    assert batch % bb == 0
    tabs = retention_tables(t, RET_CHUNK)
    tok = pl.BlockSpec(rq.shape, lambda i: (0, 0))
    st_in = pl.BlockSpec((None, bb, N_HEADS, HEAD_DIM, HEAD_DIM), lambda i: (layer, i, 0, 0, 0))
    st = pl.BlockSpec((bb, N_HEADS, HEAD_DIM, HEAD_DIM), lambda i: (i, 0, 0, 0))
    tab = pl.BlockSpec((N_HEADS, RET_CHUNK, HEAD_DIM), lambda i: (0, 0, 0))
    return pl.pallas_call(
        functools.partial(_ret_sample_kernel, t=t, batch=batch),
        grid=(batch // bb,),
        in_specs=[tok, tok, tok, tok, st_in, tab, tab, tab,
                  pl.BlockSpec((N_HEADS, 1, HEAD_DIM), lambda i: (0, 0, 0))],
        out_specs=[tok, st],
        out_shape=[jax.ShapeDtypeStruct(rq.shape, F32),
                   jax.ShapeDtypeStruct(states.shape[1:], F32)],
        compiler_params=_params("arbitrary"),
        name="retention_sample",
    )(rq, rk, rv, rg, states, *tabs)


def _beaten_counts(rows, n_valid):
    counts = []
    for n in range(n_valid):
        cnt = jnp.zeros(rows[n].shape, jnp.int32)
        for m in range(n_valid):
            if m == n:
                continue
            beats = (rows[m] >= rows[n]) if m < n else (rows[m] > rows[n])
            cnt = cnt + beats.astype(jnp.int32)
        counts.append(cnt)
    return counts


def _moba_prompt_phases(q_ref, k_ref, v_ref, km_ref, o_ref, s0_ref, s1_ref, *, nblk):
    blk = MOBA_BLOCK
    scale = HEAD_DIM ** -0.5
    kb = k_ref[...]
    vt = v_ref[...].astype(F32).T.astype(BF16)
    kmean = km_ref[...].astype(BF16)
    key_i = lax.broadcasted_iota(jnp.int32, (blk, blk), 0)
    qry_i = lax.broadcasted_iota(jnp.int32, (blk, blk), 1)
    causal = key_i <= qry_i

    def query_block(qi, s_ref):
        qb = q_ref[qi * blk:(qi + 1) * blk, :]
        bias = None
        if qi > MOBA_TOPK:
            gt = lax.dot_general(kmean, qb, _NT, preferred_element_type=F32)
            cnt = _beaten_counts([gt[m:m + 1, :] for m in range(qi)], qi)
            bias = [jnp.where(c < MOBA_TOPK, 0.0, NEG) for c in cnt]
        mx = jnp.full((1, blk), NEG, F32)
        for n in range(qi + 1):
            s = lax.dot_general(kb[n * blk:(n + 1) * blk], qb, _NT,
                                preferred_element_type=F32)
            if n == qi:
                s = jnp.where(causal, s, NEG)
            elif bias is not None:
                s = s + bias[n]
            s_ref[n * blk:(n + 1) * blk, :] = s
            mx = jnp.maximum(mx, jnp.max(s, axis=0, keepdims=True))
            yield
        l = jnp.zeros((1, blk), F32)
        ot = jnp.zeros((HEAD_DIM, blk), F32)
        for n in range(qi + 1):
            p = jnp.exp2((s_ref[n * blk:(n + 1) * blk, :] - mx) * (scale * LOG2_E))
            l = l + jnp.sum(p, axis=0, keepdims=True)
            ot = ot + _dot(vt[:, n * blk:(n + 1) * blk], p.astype(BF16))
            yield
        ot = ot * (1.0 / l)
        o_ref[qi * blk:(qi + 1) * blk, :] = ot.T.astype(o_ref.dtype)
        yield

    blocks = [query_block(qi, (s0_ref, s1_ref)[qi % 2]) for qi in range(nblk)]
    for _ in range(1):
        next(blocks[0])
    for qi in range(nblk):
        value_steps = qi + 2
        score_steps = qi + 2 if qi + 1 < nblk else 0
        for step in range(max(value_steps, score_steps)):
            if step < score_steps:
                next(blocks[qi + 1])
            if step < value_steps:
                next(blocks[qi])
            yield


def _attention_prompt_kernel(mq_ref, mk_ref, mv_ref, km_ref, rq_ref, rk_ref, rv_ref, rg_ref,
                             din_ref, qd_ref, kd_ref, cd_ref, om_ref, or_ref, sout_ref, s0_ref, s1_ref,
                             *, nblk):
    moba = _moba_prompt_phases(mq_ref, mk_ref, mv_ref, km_ref, om_ref, s0_ref, s1_ref, nblk=nblk)
    ret = _ret_prompt_phases(rq_ref, rk_ref, rv_ref, rg_ref, din_ref, qd_ref, kd_ref, cd_ref, or_ref, sout_ref)
    n_moba = sum(qi + 2 for qi in range(nblk))
    n_ret = rq_ref.shape[0] // din_ref.shape[0] + 1
    every = max(n_moba // n_ret, 1)
    for step, _ in enumerate(moba):
        if step % every == 0:
            next(ret, None)
    _run(ret)


def attention_prompt(mq, mk, mv, kmean, rq, rk, rv, rg, batch, seq):
    nblk = seq // MOBA_BLOCK
    assert nblk % SUBLANES == 0
    chunk = min(2 * RET_CHUNK, seq)
    din, qd, kd, cd = retention_tables(chunk, chunk)
    spec = pl.BlockSpec((seq, HEAD_DIM), lambda b, h: (b, h))
    head = lambda a: pl.BlockSpec((None,) + a.shape[1:], lambda b, h: (h, 0, 0))
    return pl.pallas_call(
        functools.partial(_attention_prompt_kernel, nblk=nblk),
        grid=(batch, N_HEADS),
        in_specs=[spec, spec, spec, pl.BlockSpec((nblk, HEAD_DIM), lambda b, h: (b, h)),
                  spec, spec, spec, spec, head(din), head(qd), head(kd), head(cd)],
        out_specs=[spec, spec,
                   pl.BlockSpec((None, None, HEAD_DIM, HEAD_DIM), lambda b, h: (b, h, 0, 0))],
        out_shape=[jax.ShapeDtypeStruct((batch * seq, GROUP_W), BF16),
                   jax.ShapeDtypeStruct((batch * seq, GROUP_W), BF16),
                   jax.ShapeDtypeStruct((batch, N_HEADS, HEAD_DIM, HEAD_DIM), F32)],
        scratch_shapes=[pltpu.VMEM((seq, MOBA_BLOCK), F32), pltpu.VMEM((seq, MOBA_BLOCK), F32)],
        compiler_params=_params("parallel", "parallel"),
        name="attention_prompt",
    )(mq, mk, mv, kmean, rq, rk, rv, rg, din, qd, kd, cd)


def _page_rows(ref, page):
    return jnp.concatenate([ref[pl.ds(hh, page, stride=N_HEADS), :] for hh in range(N_HEADS)], axis=1)


def _moba_sample_phases(b, q_ref, kn_ref, vn_ref, k_refs, v_refs, o_ref, *, page, t):
    n_pages = len(k_refs)
    scale = HEAD_DIM ** -0.5
    ppb = MOBA_BLOCK // page
    nblk = n_pages // ppb
    n_pairs = N_HEADS * t
    batch = q_ref.shape[0] // t
    all_cols = slice(0, GROUP_W)
    group = b // SUBLANES
    row = b % SUBLANES
    q = _seq_tile(_group_rows(q_ref, group, batch, t, all_cols), row, SUBLANES)

    r_i = lax.broadcasted_iota(jnp.int32, (n_pairs, GROUP_W), 0)
    l_i = lax.broadcasted_iota(jnp.int32, (n_pairs, GROUP_W), 1)
    qm = jnp.zeros((n_pairs, GROUP_W), F32)
    for ti in range(t):
        for hh in range(N_HEADS):
            hit = (r_i == hh * t + ti) & (l_i >= hh * HEAD_DIM) & (l_i < (hh + 1) * HEAD_DIM)
            qm = jnp.where(hit, q[ti:ti + 1, :], qm)
    qm = qm.astype(BF16)

    zpad = jnp.zeros((page - SUBLANES, GROUP_W), F32)
    own = lambda ref: jnp.concatenate(
        [_seq_tile(_group_rows(ref, group, batch, t, all_cols), row, SUBLANES), zpad], axis=0).astype(BF16)
    k_own, v_own = own(kn_ref), own(vn_ref)

    b_i = lax.broadcasted_iota(jnp.int32, (page, GROUP_W), 0)
    kmean = jnp.zeros((page, GROUP_W), F32)
    scores = []
    for p in range(n_pages):
        kp = _page_rows(k_refs[p], page)
        scores.append(lax.dot_general(qm, kp.astype(BF16), _NT,
                                      preferred_element_type=F32) * scale)
        ks = jnp.sum(kp, axis=0, keepdims=True) * (1.0 / MOBA_BLOCK)
        kmean = kmean + jnp.where(b_i == p // ppb, ks, 0.0)
        yield
    gate = lax.dot_general(qm, kmean.astype(BF16), _NT, preferred_element_type=F32)
    cnt = _beaten_counts([gate[:, m:m + 1] for m in range(nblk)], nblk)
    keep = [jnp.broadcast_to(c, (n_pairs, page)) < MOBA_TOPK for c in cnt]

    s_own = lax.dot_general(qm, k_own, _NT, preferred_element_type=F32) * scale
    key_i = lax.broadcasted_iota(jnp.int32, (n_pairs, page), 1)
    tok_i = lax.rem(lax.broadcasted_iota(jnp.int32, (n_pairs, page), 0), t)
    s_own = jnp.where((key_i <= tok_i) & (key_i < t), s_own, NEG)

    mx = s_own
    for p in range(n_pages):
        scores[p] = jnp.where(keep[p // ppb], scores[p], NEG)
        mx = jnp.maximum(mx, scores[p])
    mx = jnp.max(mx, axis=1, keepdims=True)
    e_own = jnp.exp(s_own - mx)
    l = e_own
    probs = []
    for p in range(n_pages):
        e = jnp.exp(scores[p] - mx)
        l = l + e
        probs.append(e.astype(BF16))
    inv_l = 1.0 / jnp.sum(l, axis=1, keepdims=True)
    yield
    acc = _dot(e_own.astype(BF16), v_own)
    for p in range(n_pages):
        acc = acc + _dot(probs[p], _page_rows(v_refs[p], page).astype(BF16))
        yield
    acc = acc * inv_l
    o_ref[...] = jnp.zeros_like(o_ref)
    for hh in range(N_HEADS):
        cols = slice(hh * HEAD_DIM, (hh + 1) * HEAD_DIM)
        o_ref[0:t, cols] = acc[hh * t:(hh + 1) * t, cols]
    yield


def _mixer_out(or_ref, om_ref, w_ref, x_ref, gate_ref, g_ref, b_ref, o_ref, sub):
    for r0 in range(0, x_ref.shape[0], sub):
        rows = slice(r0, r0 + sub)
        y = (_dot(or_ref[rows, :].astype(BF16), w_ref[:GROUP_W, :])
             + _dot(om_ref[rows, :].astype(BF16), w_ref[GROUP_W:, :]))
        gate = gate_ref[...] if gate_ref.shape[0] == 1 else gate_ref[rows, :]
        v = ALPHA * x_ref[rows, :] + (1.0 + gate) * y
        o_ref[rows, :] = _layer_norm(v, g_ref[...], b_ref[...])


def _out_proj_kernel(or_ref, om_ref, w_ref, x_ref, gate_ref, g_ref, b_ref, o_ref, *, sub):
    _mixer_out(or_ref, om_ref, w_ref, x_ref, gate_ref, g_ref, b_ref, o_ref, sub)


def out_proj(o_r, o_m, w_bf, x2d, gate, ln_g, ln_b, rows_per_group, tm):
    n, d = x2d.shape
    vec = pl.BlockSpec((1, d), lambda i: (0, 0))
    return pl.pallas_call(
        functools.partial(_out_proj_kernel, sub=min(128, tm)),
        grid=(n // tm,),
        in_specs=[pl.BlockSpec((tm, GROUP_W), lambda i: (i, 0)),
                  pl.BlockSpec((tm, GROUP_W), lambda i: (i, 0)),
                  _resident(w_bf),
                  pl.BlockSpec((tm, d), lambda i: (i, 0)),
                  _mod_spec(gate, tm, rows_per_group), vec, vec],
        out_specs=pl.BlockSpec((tm, d), lambda i: (i, 0)),
        out_shape=jax.ShapeDtypeStruct((n, d), F32),
        compiler_params=_params("parallel"),
        name="out_proj",
    )(o_r, o_m, w_bf, x2d, gate, ln_g.reshape(1, d), ln_b.reshape(1, d))


def _run(phases):
    for _ in phases:
        pass


def _ffn_phases(x, sh_ref, sc_ref, gate_ref, wup_ref, wdw_ref, bdw_ref, wdn_ref, g_ref, b_ref,
                o_ref, prev_ref, h_s, act_s, hist_s, *, d_ff, cw, planes, chunks=None):
    tm = x.shape[0]
    n_chunks = d_ff // cw
    chunks = range(n_chunks) if chunks is None else chunks
    if chunks.start == 0:
        h_s[...] = (x * (1.0 + sc_ref[...]) + sh_ref[...]).astype(BF16)
    row = lax.broadcasted_iota(jnp.int32, (tm, cw), 0)
    for c in chunks:
        cu = slice(c * cw, (c + 1) * cw)
        cv = slice(d_ff + c * cw, d_ff + (c + 1) * cw)
        u = _dot(h_s[...], wup_ref[:, cu])
        v = _dot(h_s[...], wup_ref[:, cv])
        if planes:
            s2 = hist_s[0, :, cu]
            s1 = hist_s[1, :, cu]
            hist_s[0, :, cu] = s1
            hist_s[1, :, cu] = u
            prev_ref[:, cu] = u
        else:
            e1 = hist_s[1:2, cu]
            e2 = jnp.where(row == 0, hist_s[0:1, cu], hist_s[1:2, cu])
            hist_s[:, cu] = u[tm - 2:tm, :]
            prev_ref[:, cu] = u[tm - 2:tm, :]
            s1 = jnp.where(row >= 1, pltpu.roll(u, 1, axis=0), e1)
            s2 = jnp.where(row >= 2, pltpu.roll(u, 2, axis=0), e2)
        uc = wdw_ref[0:1, cu] * s2 + wdw_ref[1:2, cu] * s1 + wdw_ref[2:3, cu] * u + bdw_ref[:, cu]
        act = 0.5 * uc * (1.0 + lax.erf(uc * (2.0 ** -0.5))) * v
        act_s[:, cu] = act.astype(BF16)
        yield act[0:1, 0:HEAD_DIM]
    if chunks.stop != n_chunks:
        return
    fy = _dot(act_s[...], wdn_ref[...])
    vsum = ALPHA * x + (1.0 + gate_ref[...]) * fy
    o_ref[...] = _layer_norm(vsum, g_ref[...], b_ref[...])
    yield


def _ffn_kernel(*refs, d_ff, cw, tiles_per_seq, planes):
    if planes:
        (x_ref, sh_ref, sc_ref, gate_ref, wup_ref, wdw_ref, bdw_ref, wdn_ref, g_ref, b_ref,
         p0_ref, p1_ref, o_ref, prev_ref, h_s, act_s, hist_s) = refs
    else:
        (x_ref, sh_ref, sc_ref, gate_ref, wup_ref, wdw_ref, bdw_ref, wdn_ref, g_ref, b_ref,
         o_ref, prev_ref, h_s, act_s, hist_s) = refs

    @pl.when(pl.program_id(0) % tiles_per_seq == 0)
    def _():
        if planes:
            hist_s[0] = p0_ref[...]
            hist_s[1] = p1_ref[...]
        else:
            hist_s[...] = jnp.zeros_like(hist_s)

    _run(_ffn_phases(x_ref[...], sh_ref, sc_ref, gate_ref, wup_ref, wdw_ref, bdw_ref, wdn_ref,
                     g_ref, b_ref, o_ref, prev_ref, h_s, act_s, hist_s, d_ff=d_ff, cw=cw, planes=planes))


def conv_ffn(x2d, shift, scale, gate, w_up_bf, w_dw, b_dw, w_dn_bf, layer, ln_g, ln_b,
             rows_per_group, tm, tiles_per_seq, history=None):
    n, d = x2d.shape
    d_ff = w_dn_bf.shape[1]
    cw = 256
    planes = history is not None
    mod = _mod_spec(shift, tm, rows_per_group)
    vec = pl.BlockSpec((1, d), lambda i: (0, 0))
    b_dw2 = b_dw.reshape(1, d_ff)
    in_specs = [pl.BlockSpec((tm, d), lambda i: (i, 0)), mod, mod, mod,
                _resident(w_up_bf, layer), _resident(w_dw), _resident(b_dw2),
                _resident(w_dn_bf, layer), vec, vec]
    args = [x2d, shift, scale, gate, w_up_bf, w_dw, b_dw2, w_dn_bf,
            ln_g.reshape(1, d), ln_b.reshape(1, d)]
    scratch = [pltpu.VMEM((tm, d), BF16), pltpu.VMEM((tm, d_ff), BF16)]
    if planes:
        in_specs += [_resident(history[0]), _resident(history[1])]
        args += list(history)
        prev_spec = pl.BlockSpec((tm, d_ff), lambda i: (i, 0))
        prev_shape = jax.ShapeDtypeStruct((n, d_ff), F32)
        scratch.append(pltpu.VMEM((2, tm, d_ff), F32))
    else:
        prev_spec = pl.BlockSpec((None, 2, d_ff), lambda i: (i, 0, 0))
        prev_shape = jax.ShapeDtypeStruct((n // tm, 2, d_ff), F32)
        scratch.append(pltpu.VMEM((2, d_ff), F32))
    return pl.pallas_call(
        functools.partial(_ffn_kernel, d_ff=d_ff, cw=cw, tiles_per_seq=tiles_per_seq, planes=planes),
        grid=(n // tm,),
        in_specs=in_specs,
        out_specs=[pl.BlockSpec((tm, d), lambda i: (i, 0)), prev_spec],
        out_shape=[jax.ShapeDtypeStruct((n, d), F32), prev_shape],
        scratch_shapes=scratch,
        compiler_params=_params("arbitrary"),
        name="conv_ffn",
    )(*args)


def _moba_ffn_kernel(pt_ref, q_ref, kn_ref, vn_ref, *rest, n_pages, page, t, d_ff, cw, parts, tiles_per_seq):
    k_refs = rest[:n_pages]
    v_refs = rest[n_pages:2 * n_pages]
    (or_ref, omp_ref, wout_ref, x_ref, gate0_ref, g0_ref, b0_ref,
     sh_ref, sc_ref, gate_ref, wup_ref, wdw_ref, bdw_ref, wdn_ref, g_ref, b_ref,
     om_ref, o_ref, prev_ref, x1_s, h_s, act_s, hist_s) = rest[2 * n_pages:]
    i = pl.program_id(0)
    n_chunks = d_ff // cw
    per_part = -(-n_chunks // parts)

    @pl.when(i % (parts * tiles_per_seq) == 0)
    def _():
        hist_s[...] = jnp.zeros_like(hist_s)

    def step(part):
        chunks = range(part * per_part, min((part + 1) * per_part, n_chunks))
        if part == 0:
            _mixer_out(or_ref, omp_ref, wout_ref, x_ref, gate0_ref, g0_ref, b0_ref, x1_s, sub=128)
        ffn_ph = _ffn_phases(x1_s[...], sh_ref, sc_ref, gate_ref, wup_ref, wdw_ref, bdw_ref, wdn_ref,
                             g_ref, b_ref, o_ref, prev_ref, h_s, act_s, hist_s,
                             d_ff=d_ff, cw=cw, planes=False, chunks=chunks)
        moba_ph = _moba_sample_phases(i, q_ref, kn_ref, vn_ref, k_refs, v_refs, om_ref, page=page, t=t)
        for _ in range(n_pages):
            next(moba_ph)
        next(ffn_ph)
        next(moba_ph)
        for _ in range(len(chunks) - 1):
            next(ffn_ph)
        for _ in range(n_pages + 1):
            next(moba_ph)
        _run(ffn_ph)

    for part in range(parts):
        pl.when(i % parts == part)(functools.partial(step, part))


def moba_sample_ffn_prompt(mq, mk, mv, cache_k, cache_v, page_table, page_base, page, t,
                           o_r, o_m, w_out_bf, x2d, gate0, ln_g0, ln_b0,
                           shift, scale, gate, w_up_bf, w_dw, b_dw, w_dn_bf, layer, ln_g, ln_b,
                           seq, tm):
    batch, n_pages = page_table.shape
    n, d = x2d.shape
    d_ff = w_dn_bf.shape[1]
    assert (n_pages * page) % MOBA_BLOCK == 0 and t <= SUBLANES
    assert n_pages * page // MOBA_BLOCK <= page
    n_tiles = n // tm
    assert batch % n_tiles == 0
    parts = batch // n_tiles
    tok = pl.BlockSpec(mq.shape, lambda i, pt: (0, 0))

    def page_spec(p):
        return pl.BlockSpec((page * N_HEADS, HEAD_DIM), lambda i, pt: (page_base + pt[i, p], 0))

    const = lambda a, lead=None: pl.BlockSpec(
        a.shape if lead is None else (None,) + a.shape[1:],
        (lambda i, pt: (0,) * a.ndim) if lead is None else (lambda i, pt: (lead,) + (0,) * (a.ndim - 1)),
        pipeline_mode=pl.Buffered(1))
    mod = pl.BlockSpec((None, 1, d), lambda i, pt: ((i // parts) * tm // seq, 0, 0))
    b_dw2, g2, b2 = b_dw.reshape(1, d_ff), ln_g.reshape(1, d), ln_b.reshape(1, d)
    g0, b0 = ln_g0.reshape(1, d), ln_b0.reshape(1, d)
    tile = lambda w: pl.BlockSpec((tm, w), lambda i, pt: (i // parts, 0))
    specs = ([tok, tok, tok] + [page_spec(p) for p in range(n_pages)] * 2
             + [tile(GROUP_W), tile(GROUP_W), const(w_out_bf), tile(d), mod, const(g0), const(b0), mod, mod, mod,
                const(w_up_bf, layer), const(w_dw), const(b_dw2), const(w_dn_bf, layer), const(g2), const(b2)])
    grid_spec = pltpu.PrefetchScalarGridSpec(
        num_scalar_prefetch=1,
        grid=(batch,),
        in_specs=specs,
        out_specs=[pl.BlockSpec((None, SUBLANES, GROUP_W), lambda i, pt: (i, 0, 0)),
                   pl.BlockSpec((tm, d), lambda i, pt: (i // parts, 0)),
                   pl.BlockSpec((None, 2, d_ff), lambda i, pt: (i // parts, 0, 0))],
        scratch_shapes=[pltpu.VMEM((tm, d), F32), pltpu.VMEM((tm, d), BF16), pltpu.VMEM((tm, d_ff), BF16),
                        pltpu.VMEM((2, d_ff), F32)],
    )
    return pl.pallas_call(
        functools.partial(_moba_ffn_kernel, n_pages=n_pages, page=page, t=t, d_ff=d_ff, cw=256,
                          parts=parts, tiles_per_seq=seq // tm),
        grid_spec=grid_spec,
        out_shape=[jax.ShapeDtypeStruct((batch, SUBLANES, GROUP_W), F32),
                   jax.ShapeDtypeStruct((n, d), F32),
                   jax.ShapeDtypeStruct((n_tiles, 2, d_ff), F32)],
        compiler_params=_params("arbitrary"),
        name="moba_sample_ffn_prompt",
    )(page_table, mq, mk, mv, *([cache_k] * n_pages), *([cache_v] * n_pages),
      o_r, o_m, w_out_bf, x2d, gate0, g0, b0, shift, scale, gate, w_up_bf, w_dw, b_dw2, w_dn_bf, g2, b2)


def _glu_kernel(x_ref, sh_ref, sc_ref, w_ref, b_ref, o_ref, *, cw):
    h = (x_ref[...] * (1.0 + sc_ref[...]) + sh_ref[...]).astype(BF16)
    dc = o_ref.shape[1]
    for c in range(dc // cw):
        ca = slice(c * cw, (c + 1) * cw)
        cg = slice(dc + c * cw, dc + (c + 1) * cw)
        a = _dot(h, w_ref[:, ca]) + b_ref[:, ca]
        g = _dot(h, w_ref[:, cg]) + b_ref[:, cg]
        o_ref[:, ca] = a * jax.nn.sigmoid(g)


def conformer_glu(x2d, shift, scale, w1_bf, b1, rows_per_group, tm):
    n, d = x2d.shape
    dc = w1_bf.shape[1] // 2
    mod = _mod_spec(shift, tm, rows_per_group)
    b1r = b1.reshape(1, 2 * dc)
    return pl.pallas_call(
        functools.partial(_glu_kernel, cw=256),
        grid=(n // tm,),
        in_specs=[pl.BlockSpec((tm, d), lambda i: (i, 0)), mod, mod,
                  _resident(w1_bf), _resident(b1r)],
        out_specs=pl.BlockSpec((tm, dc), lambda i: (i, 0)),
        out_shape=jax.ShapeDtypeStruct((n, dc), F32),
        compiler_params=_params("parallel"),
        name="conformer_glu",
    )(x2d, shift, scale, w1_bf, b1r)


PAD_ROWS = 32


def _dwconv_rows(win_ref, w_ref, bias, r0, rs, cols, width):
    off = PAD_ROWS - (width - 1)
    y = bias
    for r in range(SUBLANES):
        z = None
        for m in range((off + width - 1) // SUBLANES + 1):
            k = SUBLANES * m + r - off
            if 0 <= k < width:
                lo = r0 + SUBLANES * m
                term = w_ref[k:k + 1, cols] * win_ref[lo:lo + rs + SUBLANES, cols]
                z = term if z is None else z + term
        if z is not None:
            y = y + z[r:r + rs, :]
    return y


def _ordered_after(x, token):
    zero = lax.shift_right_logical(lax.shift_right_logical(
        lax.bitcast_convert_type(token, jnp.uint32), jnp.uint32(16)), jnp.uint32(16))
    return lax.bitcast_convert_type(lax.bitcast_convert_type(x, jnp.uint32) | zero, F32)


def _conformer_phases(x, sh_ref, sc_ref, gate_ref, w1_ref, b1_ref, wdw_ref, bdw_ref,
                      gl_ref, bl_ref, w2_ref, b2_ref, g_ref, b_ref, o_ref, tail_ref, win_ref, y_ref,
                      *, width, rs, cw, after=None):
    tm, dc = y_ref.shape
    h = (x * (1.0 + sc_ref[...]) + sh_ref[...]).astype(BF16)
    for c in range(dc // cw):
        ca = slice(c * cw, (c + 1) * cw)
        cg = slice(dc + c * cw, dc + (c + 1) * cw)
        a = _dot(h, w1_ref[:, ca]) + b1_ref[:, ca]
        g = _dot(h, w1_ref[:, cg]) + b1_ref[:, cg]
        win_ref[PAD_ROWS:PAD_ROWS + tm, ca] = a * jax.nn.sigmoid(g)
    tail_ref[...] = win_ref[tm:tm + PAD_ROWS, :]
    yield
    for c in range(dc // HEAD_DIM):
        cols = slice(c * HEAD_DIM, (c + 1) * HEAD_DIM)
        for r0 in range(0, tm, rs):
            bias = bdw_ref[:, cols]
            if after is not None and after[0] is not None:
                bias = _ordered_after(bias, after[0])
            y_ref[r0:r0 + rs, cols] = _dwconv_rows(win_ref, wdw_ref, bias, r0, rs, cols, width)
            yield
    win_ref[0:PAD_ROWS, :] = win_ref[tm:tm + PAD_ROWS, :]
    yn = _silu(_layer_norm(y_ref[...], gl_ref[...], bl_ref[...]))
    y = _dot(yn.astype(BF16), w2_ref[...]) + b2_ref[...]
    v = ALPHA * x + (1.0 + gate_ref[...]) * y
    o_ref[...] = _layer_norm(v, g_ref[...], b_ref[...])
    yield


N_CONF_CONSTS = 10
N_FFN_CONSTS = 6


def _conformer_ffn_kernel(*refs, tiles_per_seq, width, rs, cw, d_ff):
    x_ref, shc_ref, scc_ref, gtc_ref, shf_ref, scf_ref, gtf_ref = refs[:7]
    conf = refs[7:7 + N_CONF_CONSTS]
    ffn = refs[7 + N_CONF_CONSTS:7 + N_CONF_CONSTS + N_FFN_CONSTS]
    o_ref, tail_ref, prev_ref, win_ref, y_ref, x1_s, h_s, act_s, hist_s = refs[7 + N_CONF_CONSTS + N_FFN_CONSTS:]
    i = pl.program_id(0)
    n_tiles = pl.num_programs(0) - 1
    tm, dc = y_ref.shape

    @pl.when(i == 0)
    def _():
        x1_s[...] = jnp.zeros_like(x1_s)
        win_ref[PAD_ROWS + tm:, :] = jnp.zeros((SUBLANES, dc), F32)

    @pl.when(jnp.minimum(i, n_tiles - 1) % tiles_per_seq == 0)
    def _():
        win_ref[0:PAD_ROWS, :] = jnp.zeros((PAD_ROWS, dc), F32)

    @pl.when(jnp.maximum(i - 1, 0) % tiles_per_seq == 0)
    def _():
        hist_s[...] = jnp.zeros_like(hist_s)

    ffn_ph = _ffn_phases(x1_s[...], shf_ref, scf_ref, gtf_ref, *ffn, o_ref, prev_ref, h_s, act_s, hist_s,
                         d_ff=d_ff, cw=cw, planes=False)
    token = [None]
    conf_ph = _conformer_phases(x_ref[...], shc_ref, scc_ref, gtc_ref, *conf, x1_s, tail_ref, win_ref, y_ref,
                                width=width, rs=rs, cw=cw, after=token)
    next(conf_ph)
    n_ffn = d_ff // cw
    n_conv = (dc // HEAD_DIM) * (tm // rs)
    per_chunk = -(-n_conv // n_ffn)
    for _ in range(n_ffn):
        token[0] = next(ffn_ph)
        for _ in range(per_chunk):
            next(conf_ph, None)
    _run(ffn_ph)
    _run(conf_ph)


def conformer_ffn_prompt(x2d, mods_conf, mods_ffn, w1_bf, b1, w_dw, b_dw, cf_g, cf_b, w2_bf, b2,
                         ln_g0, ln_b0, w_up_bf, ffn_w_dw, ffn_b_dw, w_dn_bf, layer, ln_g1, ln_b1,
                         batch, seq, tm):
    n, d = x2d.shape
    dc = w2_bf.shape[0]
    d_ff = w_dn_bf.shape[1]
    width = w_dw.shape[0]
    assert width - 1 <= PAD_ROWS and tm % PAD_ROWS == 0
    nj = seq // tm
    n_tiles = n // tm
    row = lambda a: a.reshape(1, -1)
    conf_tile = lambda i: jnp.minimum(i, n_tiles - 1)
    ffn_tile = lambda i: jnp.maximum(i - 1, 0)
    mod_c = pl.BlockSpec((None, 1, d), lambda i: (conf_tile(i) // nj, 0, 0))
    mod_f = pl.BlockSpec((None, 1, d), lambda i: (ffn_tile(i) // nj, 0, 0))
    conf = [w1_bf, row(b1), w_dw, row(b_dw), row(cf_g), row(cf_b), w2_bf, row(b2), row(ln_g0), row(ln_b0)]
    ffn = [ffn_w_dw, row(ffn_b_dw), row(ln_g1), row(ln_b1)]
    assert len(conf) == N_CONF_CONSTS and len(ffn) + 2 == N_FFN_CONSTS
    ffn_specs = [_resident(w_up_bf, layer), _resident(ffn[0]), _resident(ffn[1]),
                 _resident(w_dn_bf, layer), _resident(ffn[2]), _resident(ffn[3])]
    ffn_args = [w_up_bf, ffn[0], ffn[1], w_dn_bf, ffn[2], ffn[3]]
    return pl.pallas_call(
        functools.partial(_conformer_ffn_kernel, tiles_per_seq=nj, width=width, rs=64, cw=256, d_ff=d_ff),
        grid=(n_tiles + 1,),
        in_specs=[pl.BlockSpec((tm, d), lambda i: (conf_tile(i), 0)), mod_c, mod_c, mod_c,
                  mod_f, mod_f, mod_f] + [_resident(a) for a in conf] + ffn_specs,
        out_specs=[pl.BlockSpec((tm, d), lambda i: (ffn_tile(i), 0)),
                   pl.BlockSpec((None, PAD_ROWS, dc), lambda i: (conf_tile(i) // nj, 0, 0)),
                   pl.BlockSpec((None, 2, d_ff), lambda i: (ffn_tile(i), 0, 0))],
        out_shape=[jax.ShapeDtypeStruct((n, d), F32),
                   jax.ShapeDtypeStruct((batch, PAD_ROWS, dc), F32),
                   jax.ShapeDtypeStruct((n_tiles, 2, d_ff), F32)],
        scratch_shapes=[pltpu.VMEM((PAD_ROWS + tm + SUBLANES, dc), F32), pltpu.VMEM((tm, dc), F32),
                        pltpu.VMEM((tm, d), F32), pltpu.VMEM((tm, d), BF16),
                        pltpu.VMEM((tm, d_ff), BF16), pltpu.VMEM((2, d_ff), F32)],
        compiler_params=_params("arbitrary"),
        name="conformer_ffn_prompt",
    )(x2d, *mods_conf, *mods_ffn, *conf, *ffn_args)


def _dwconv_sample_kernel(x_ref, st_ref, w_ref, b_ref, o_ref, nst_ref, *, width, t):
    hist = width - 1
    bb, dc = x_ref.shape[1], x_ref.shape[2]
    plane = lambda j, cols: st_ref[j, :, cols] if j < hist else x_ref[j - hist, :, cols]
    for c in range(dc // HEAD_DIM):
        cols = slice(c * HEAD_DIM, (c + 1) * HEAD_DIM)
        acc = [jnp.zeros((bb, HEAD_DIM), F32) + b_ref[:, cols] for _ in range(t)]
        for j in range(hist + t):
            p = plane(j, cols)
            for ti in range(t):
                if 0 <= j - ti < width:
                    acc[ti] = acc[ti] + w_ref[j - ti:j - ti + 1, cols] * p
            if j >= t:
                nst_ref[j - t, :, cols] = p
        for ti in range(t):
            o_ref[ti, :, cols] = acc[ti]


def dwconv_sample(glu, states, layer, w_dw, b_dw):
    t, batch, dc = glu.shape
    width = w_dw.shape[0]
    hist = width - 1
    bb = 32
    b2 = b_dw.reshape(1, dc)
    return pl.pallas_call(
        functools.partial(_dwconv_sample_kernel, width=width, t=t),
        grid=(batch // bb,),
        in_specs=[pl.BlockSpec((t, bb, dc), lambda i: (0, i, 0)),
                  pl.BlockSpec((None, hist, bb, dc), lambda i: (layer, 0, i, 0)),
                  pl.BlockSpec(w_dw.shape, lambda i: (0, 0)),
                  pl.BlockSpec(b2.shape, lambda i: (0, 0))],
        out_specs=[pl.BlockSpec((t, bb, dc), lambda i: (0, i, 0)),
                   pl.BlockSpec((hist, bb, dc), lambda i: (0, i, 0))],
        out_shape=[jax.ShapeDtypeStruct((t, batch, dc), F32),
                   jax.ShapeDtypeStruct((hist, batch, dc), F32)],
        compiler_params=_params("parallel"),
        name="dwconv_sample",
    )(glu, states, w_dw, b2)


def _conf_tail_kernel(y_ref, gl_ref, bl_ref, w_ref, b2_ref, x_ref, gate_ref, g_ref, b_ref, o_ref):
    yn = _silu(_layer_norm(y_ref[...], gl_ref[...], bl_ref[...]))
    y = _dot(yn.astype(BF16), w_ref[...]) + b2_ref[...]
    v = ALPHA * x_ref[...] + (1.0 + gate_ref[...]) * y
    o_ref[...] = _layer_norm(v, g_ref[...], b_ref[...])


def conformer_tail(y, cf_g, cf_b, w2_bf, b2, x2d, gate, ln_g, ln_b, rows_per_group, tm):
    n, d = x2d.shape
    dc = y.shape[1]
    vecc = pl.BlockSpec((1, dc), lambda i: (0, 0))
    vec = pl.BlockSpec((1, d), lambda i: (0, 0))
    return pl.pallas_call(
        _conf_tail_kernel,
        grid=(n // tm,),
        in_specs=[pl.BlockSpec((tm, dc), lambda i: (i, 0)), vecc, vecc,
                  _resident(w2_bf), vec,
                  pl.BlockSpec((tm, d), lambda i: (i, 0)),
                  _mod_spec(gate, tm, rows_per_group), vec, vec],
        out_specs=pl.BlockSpec((tm, d), lambda i: (i, 0)),
        out_shape=jax.ShapeDtypeStruct((n, d), F32),
        compiler_params=_params("parallel"),
        name="conformer_tail",
    )(y, cf_g.reshape(1, dc), cf_b.reshape(1, dc), w2_bf, b2.reshape(1, d), x2d, gate,
      ln_g.reshape(1, d), ln_b.reshape(1, d))


def kernel(x_prompt, x_sample, cache_k, cache_v, state_ret, state_conv, state_ffn, page_table, c_prompt, c_sample, ab_w_in, ab_w_out, cf_w_pw1, cf_b_pw1, cf_w_dw, cf_b_dw, cf_ln_g, cf_ln_b, cf_w_pw2, cf_b_pw2, ffn_w_up, ffn_w_dw, ffn_b_dw, ffn_w_down, ada_w, ada_b, ln_g, ln_b):
    bp, seq, d = x_prompt.shape
    bs, ts, _ = x_sample.shape
    n_layers = ada_w.shape[0]
    d_ff = ffn_w_down.shape[1]
    past_len = page_table.shape[1] * cache_k.shape[2]
    np_, ns = bp * seq, bs * ts
    tm_p = min(256, seq)
    tm_s = bs

    ada = adaln_all(jnp.concatenate([c_prompt, c_sample], 0), ada_w, ada_b)

    def mods(l, k):
        return ada[l, k, :bp][:, None, :], ada[l, k, bp:][None]

    token_major = lambda a: jnp.swapaxes(a, 0, 1)
    xp = x_prompt.reshape(np_, d)
    xs = token_major(x_sample).reshape(ns, d)
    tabs_p = rotary_tables(jnp.arange(seq, dtype=jnp.int32))
    tabs_s = tuple(jnp.repeat(tb, bs, axis=0) for tb in
                   rotary_tables(past_len + jnp.arange(ts, dtype=jnp.int32)))

    w_up = ffn_w_up.astype(BF16)
    w_dn = ffn_w_down.astype(BF16)
    outs = {k: [] for k in ("kp", "vp", "ks", "vs", "rp", "rs", "cp", "cs", "fp", "fs")}
    for l in range(n_layers):
        i = l // 2
        sh_p, sh_s = mods(l, 0)
        sc_p, sc_s = mods(l, 1)
        gt_p, gt_s = mods(l, 2)
        if l % 2 == 0:
            w_in = ab_w_in[i].astype(BF16)
            w_out = ab_w_out[i].astype(BF16)
            rq, rk, rv, rg, mq, mk, mv, kb, vb, kmean = in_proj(
                xp, sh_p, sc_p, w_in, tabs_p, seq, tm_p, True, BF16)
            o_mp, o_rp, s_p = attention_prompt(mq, kb, vb, kmean.reshape(-1, GROUP_W), rq, rk, rv, rg, bp, seq)
            outs["kp"].append(mk.reshape(bp, seq, N_HEADS, HEAD_DIM))
            outs["vp"].append(mv.reshape(bp, seq, N_HEADS, HEAD_DIM))
            outs["rp"].append(s_p)

            rq, rk, rv, rg, mq, mk, mv, kb, vb = in_proj(xs, sh_s, sc_s, w_in, tabs_s, ns, tm_s, False, F32)
            o_r, s_s = retention_sample(rq, rk, rv, rg, state_ret, i, ts)
            n_phys, page = cache_k.shape[1], cache_k.shape[2]
            ck = cache_k.reshape(-1, HEAD_DIM)
            cv = cache_v.reshape(-1, HEAD_DIM)
            mods_ffn_p = tuple(mods(l, k)[0] for k in (3, 4, 5))
            o_m, xp, prev_p = moba_sample_ffn_prompt(
                mq, kb, vb, ck, cv, page_table, i * n_phys, page, ts,
                o_rp, o_mp, w_out, xp, gt_p, ln_g[l, 0], ln_b[l, 0], *mods_ffn_p, w_up, ffn_w_dw[l], ffn_b_dw[l], w_dn, l, ln_g[l, 1], ln_b[l, 1], seq, tm_p)
            o_m = token_major(o_m[:, :ts]).reshape(ns, GROUP_W)
            xs = out_proj(o_r, o_m, w_out, xs, gt_s, ln_g[l, 0], ln_b[l, 0], ns, tm_s)
            outs["ks"].append(token_major(mk.reshape(ts, bs, N_HEADS, HEAD_DIM)))
            outs["vs"].append(token_major(mv.reshape(ts, bs, N_HEADS, HEAD_DIM)))
            outs["rs"].append(s_s)
        else:
            w1 = cf_w_pw1[i].astype(BF16)
            w2 = cf_w_pw2[i].astype(BF16)
            hist = cf_w_dw.shape[1] - 1
            mods_ffn_p = tuple(mods(l, k)[0] for k in (3, 4, 5))
            xp, tail, prev_p = conformer_ffn_prompt(
                xp, (sh_p, sc_p, gt_p), mods_ffn_p, w1, cf_b_pw1[i], cf_w_dw[i], cf_b_dw[i],
                cf_ln_g[i], cf_ln_b[i], w2, cf_b_pw2[i], ln_g[l, 0], ln_b[l, 0],
                w_up, ffn_w_dw[l], ffn_b_dw[l], w_dn, l, ln_g[l, 1], ln_b[l, 1], bp, seq, tm_p)
            outs["cp"].append(tail[:, PAD_ROWS - hist:])

            glu = conformer_glu(xs, sh_s, sc_s, w1, cf_b_pw1[i], ns, tm_s)
            y, nst = dwconv_sample(glu.reshape(ts, bs, -1), jnp.swapaxes(state_conv, 1, 2), i,
                                   cf_w_dw[i], cf_b_dw[i])
            xs = conformer_tail(y.reshape(ns, -1), cf_ln_g[i], cf_ln_b[i], w2, cf_b_pw2[i], xs, gt_s,
                                ln_g[l, 0], ln_b[l, 0], ns, tm_s)
            outs["cs"].append(token_major(nst))

        sh_p, sh_s = mods(l, 3)
        sc_p, sc_s = mods(l, 4)
        gt_p, gt_s = mods(l, 5)
        outs["fp"].append(prev_p.reshape(bp, seq // tm_p, 2, d_ff)[:, -1])
        xs, u_s = conv_ffn(xs, sh_s, sc_s, gt_s, w_up, ffn_w_dw[l], ffn_b_dw[l], w_dn, l,
                           ln_g[l, 1], ln_b[l, 1], ns, tm_s, ts,
                           history=(state_ffn[l, :, 0], state_ffn[l, :, 1]))
        outs["fs"].append(token_major(u_s.reshape(ts, bs, d_ff)[ts - 2:]))

    st = lambda k: outs[k][0][None] if len(outs[k]) == 1 else jnp.stack(outs[k])
    return (xp.reshape(bp, seq, d), token_major(xs.reshape(ts, bs, d)), st("kp"), st("vp"), st("ks"), st("vs"),
            st("rp"), st("rs"), st("cp"), st("cs"), st("fp"), st("fs"))
```

```python
import functools

import jax
import jax.numpy as jnp
from jax import lax
from jax.experimental import pallas as pl
from jax.experimental.pallas import tpu as pltpu

HEAD_DIM = 128
N_HEADS = 4
GROUP_W = N_HEADS * HEAD_DIM
RET_CHUNK = 128
MOBA_BLOCK = 256
MOBA_TOPK = 3
ROPE_THETA = 10000.0
DEPTH = 2
ALPHA = (2 * DEPTH) ** 0.25
LN_EPS = 1e-5
GN_EPS = 1e-6
NEG = -1e30
LOG2_E = 1.4426950408889634

F32 = jnp.float32
BF16 = jnp.bfloat16

_NT = (((1,), (1,)), ((), ()))
_TN = (((0,), (0,)), ((), ()))

VMEM_LIMIT = 56 * 1024 * 1024
SUBLANES = 8


def _params(*sem):
    return pltpu.CompilerParams(dimension_semantics=sem, vmem_limit_bytes=VMEM_LIMIT)


def _dot(a, b):
    return jnp.dot(a, b, preferred_element_type=F32)


def _layer_norm(v, g, b):
    mu = jnp.mean(v, axis=-1, keepdims=True)
    d = v - mu
    var = jnp.mean(d * d, axis=-1, keepdims=True)
    return d * lax.rsqrt(var + LN_EPS) * g + b


def _silu(x):
    return x * jax.nn.sigmoid(x)


def _mod_spec(mod, tm, rows_per_group):
    _, r, d = mod.shape
    if r == 1:
        return pl.BlockSpec((None, 1, d), lambda i: (i * tm // rows_per_group, 0, 0))
    assert r % tm == 0
    return pl.BlockSpec((None, tm, d), lambda i: (0, i % (r // tm), 0))


def _resident(a, layer=None):
    if layer is None:
        return pl.BlockSpec(a.shape, lambda *_: (0,) * a.ndim, pipeline_mode=pl.Buffered(1))
    return pl.BlockSpec((None,) + a.shape[1:], lambda *_: (layer,) + (0,) * (a.ndim - 1),
                        pipeline_mode=pl.Buffered(1))


def _adaln_kernel(c_ref, w_ref, b_ref, o_ref):
    a = _silu(c_ref[...]).astype(BF16)
    o_ref[...] = _dot(a, w_ref[...].astype(BF16)) + b_ref[...]


def adaln_all(c_all, ada_w, ada_b):
    n_layers, d, n_out = ada_w.shape
    m = c_all.shape[0]
    return pl.pallas_call(
        _adaln_kernel,
        grid=(n_layers, n_out // d),
        in_specs=[
            pl.BlockSpec((m, d), lambda l, j: (0, 0)),
            pl.BlockSpec((None, d, d), lambda l, j: (l, 0, j)),
            pl.BlockSpec((None, 1, d), lambda l, j: (l, 0, j)),
        ],
        out_specs=pl.BlockSpec((None, None, m, d), lambda l, j: (l, j, 0, 0)),
        out_shape=jax.ShapeDtypeStruct((n_layers, n_out // d, m, d), F32),
        compiler_params=_params("parallel", "parallel"),
        name="adaln",
    )(c_all, ada_w, ada_b.reshape(n_layers, 1, n_out))


def _in_proj_kernel(x_ref, sh_ref, sc_ref, w_ref, cr_ref, sr_ref, cm_ref, sm_ref,
                    rq_ref, rk_ref, rv_ref, rg_ref, mq_ref, mk_ref, mv_ref, kb_ref, vb_ref,
                    *km_refs):
    tm = x_ref.shape[0]
    h = (x_ref[...] * (1.0 + sc_ref[...]) + sh_ref[...]).astype(BF16)
    cr, sr, cm, sm = cr_ref[...], sr_ref[...], cm_ref[...], sm_ref[...]
    even = (lax.broadcasted_iota(jnp.int32, cr.shape, 1) & 1) == 0
    outs = (rq_ref, rk_ref, rv_ref, rg_ref, mq_ref, mk_ref, mv_ref)
    for g, o_ref in enumerate(outs):
        z = _dot(h, w_ref[:, g * GROUP_W:(g + 1) * GROUP_W])
        for hh in range(N_HEADS):
            cols = slice(hh * HEAD_DIM, (hh + 1) * HEAD_DIM)
            zs = z[:, cols]
            if g in (0, 1):
                nxt = pltpu.roll(zs, HEAD_DIM - 1, axis=1)
                prv = pltpu.roll(zs, 1, axis=1)
                zs = zs * cr + jnp.where(even, nxt, prv) * sr
                if g == 1:
                    zs = zs * (HEAD_DIM ** -0.5)
            elif g in (4, 5):
                zs = zs * cm + pltpu.roll(zs, HEAD_DIM // 2, axis=1) * sm
            if g in (5, 6):
                o_ref[pl.ds(hh, tm, stride=N_HEADS), :] = zs
                head_major = kb_ref if g == 5 else vb_ref
                head_major[:, cols] = zs.astype(head_major.dtype)
                if g == 5 and km_refs:
                    for r in range(tm // MOBA_BLOCK):
                        blk = zs[r * MOBA_BLOCK:(r + 1) * MOBA_BLOCK, :]
                        km_refs[0][r:r + 1, cols] = jnp.sum(blk, axis=0, keepdims=True) * (1.0 / MOBA_BLOCK)
            else:
                o_ref[:, cols] = zs.astype(o_ref.dtype)


def in_proj(x2d, shift, scale, w_bf, tabs, rows_per_group, tm, block_means, act_dtype):
    n, d = x2d.shape
    p_tiles = tabs[0].shape[0] // tm
    mod_spec = _mod_spec(shift, tm, rows_per_group)
    tab_spec = pl.BlockSpec((tm, HEAD_DIM), lambda i: (i % p_tiles, 0))
    wide = pl.BlockSpec((tm, GROUP_W), lambda i: (i, 0))
    tall = pl.BlockSpec((tm * N_HEADS, HEAD_DIM), lambda i: (i, 0))
    wide_shape = lambda dt: jax.ShapeDtypeStruct((n, GROUP_W), dt)
    tall_shape = jax.ShapeDtypeStruct((n * N_HEADS, HEAD_DIM), F32)
    out_specs = [wide] * 5 + [tall, tall, wide, wide]
    out_shape = [wide_shape(act_dtype)] * 5 + [tall_shape, tall_shape, wide_shape(act_dtype), wide_shape(act_dtype)]
    if block_means:
        assert tm % MOBA_BLOCK == 0
        per = tm // MOBA_BLOCK
        out_specs.append(pl.BlockSpec((None, per, GROUP_W), lambda i: (i, 0, 0)))
        out_shape.append(jax.ShapeDtypeStruct((n // tm, per, GROUP_W), F32))
    return pl.pallas_call(
        _in_proj_kernel,
        grid=(n // tm,),
        in_specs=[pl.BlockSpec((tm, d), lambda i: (i, 0)), mod_spec, mod_spec,
                  _resident(w_bf), tab_spec, tab_spec, tab_spec, tab_spec],
        out_specs=out_specs,
        out_shape=out_shape,
        compiler_params=_params("parallel"),
        name="in_proj",
    )(x2d, shift, scale, w_bf, *tabs)


def rotary_tables(pos):
    half = HEAD_DIM // 2
    posf = pos.astype(F32)[:, None]
    inv_r = 1.0 / (ROPE_THETA ** jnp.linspace(0.0, 1.0, half, dtype=F32))
    ang_r = posf * inv_r[None, :]
    cr = jnp.repeat(jnp.cos(ang_r), 2, axis=1)
    sr = jnp.stack([-jnp.sin(ang_r), jnp.sin(ang_r)], -1).reshape(-1, HEAD_DIM)
    inv_m = ROPE_THETA ** (-jnp.arange(0, HEAD_DIM, 2, dtype=F32) / HEAD_DIM)
    ang_m = posf * inv_m[None, :]
    cm = jnp.concatenate([jnp.cos(ang_m), jnp.cos(ang_m)], -1)
    sm = jnp.concatenate([-jnp.sin(ang_m), jnp.sin(ang_m)], -1)
    return cr, sr, cm, sm


def retention_tables(chunk, rows):
    log_g = jnp.log1p(-jnp.exp2(-5.0 - jnp.arange(N_HEADS, dtype=F32)))
    idx = jnp.arange(chunk, dtype=F32)
    diff = idx[:, None] - idx[None, :]
    decay_in = jnp.where(diff[None] >= 0,
                         jnp.exp(jnp.maximum(diff, 0.0)[None] * log_g[:, None, None]), 0.0)
    q_dec = jnp.exp((idx + 1.0)[None, :] * log_g[:, None])
    k_dec = jnp.exp((chunk - 1.0 - idx)[None, :] * log_g[:, None])
    c_dec = jnp.exp(chunk * log_g)
    pad = rows - chunk
    decay_in = jnp.pad(decay_in, ((0, 0), (0, pad), (0, rows - chunk)))
    bc = lambda t: jnp.broadcast_to(jnp.pad(t, ((0, 0), (0, pad)))[:, :, None], (N_HEADS, rows, HEAD_DIM))
    c_b = jnp.broadcast_to(c_dec[:, None, None], (N_HEADS, 1, HEAD_DIM))
    return decay_in, bc(q_dec), bc(k_dec), c_b


def _ret_chunk(qc, kc, vc, s, din, qd, kd, cd):
    att = lax.dot_general(qc.astype(BF16), kc.astype(BF16), _NT,
                          preferred_element_type=F32) * din
    o = _dot(att.astype(BF16), vc) + _dot((qc.astype(F32) * qd).astype(BF16), s.astype(BF16))
    s_new = cd * s + lax.dot_general((kc.astype(F32) * kd).astype(BF16), vc, _TN,
                                     preferred_element_type=F32)
    return o, s_new


def _group_norm_gate(o, g):
    o = o * lax.rsqrt(jnp.mean(o * o, axis=-1, keepdims=True) + GN_EPS)
    return o * _silu(g)


def _ret_prompt_phases(q_ref, k_ref, v_ref, g_ref, din_ref, qd_ref, kd_ref, cd_ref, o_ref, sout_ref):
    chunk = din_ref.shape[0]
    din, qd, kd, cd = din_ref[...], qd_ref[...], kd_ref[...], cd_ref[...]
    s = jnp.zeros((HEAD_DIM, HEAD_DIM), F32)
    for c in range(q_ref.shape[0] // chunk):
        rows = slice(c * chunk, (c + 1) * chunk)
        o, s = _ret_chunk(q_ref[rows, :], k_ref[rows, :], v_ref[rows, :], s, din, qd, kd, cd)
        o_ref[rows, :] = _group_norm_gate(o, g_ref[rows, :].astype(F32)).astype(o_ref.dtype)
        yield
    sout_ref[...] = s
    yield


def _group_rows(ref, group, batch, t, cols):
    return [ref[pl.ds(pl.multiple_of(ti * batch + group * SUBLANES, SUBLANES), SUBLANES), cols]
            for ti in range(t)]


def _seq_tile(token_rows, row, n_rows):
    width = token_rows[0].shape[1]
    r_i = lax.broadcasted_iota(jnp.int32, (n_rows, width), 0)
    g_i = lax.broadcasted_iota(jnp.int32, token_rows[0].shape, 0)
    out = jnp.zeros((n_rows, width), F32)
    for ti, rows in enumerate(token_rows):
        if isinstance(row, int):
            picked = rows[row:row + 1, :]
        else:
            picked = jnp.sum(jnp.where(g_i == row, rows, 0.0), axis=0, keepdims=True)
        out = jnp.where(r_i == ti, picked, out)
    return out


def _ret_sample_kernel(q_ref, k_ref, v_ref, g_ref, s0_ref, din_ref, qd_ref, kd_ref, cd_ref,
                       o_ref, sout_ref, *, t, batch):
    group = pl.program_id(0)
    r_i = lax.broadcasted_iota(jnp.int32, (SUBLANES, HEAD_DIM), 0)
    for hh in range(N_HEADS):
        cols = slice(hh * HEAD_DIM, (hh + 1) * HEAD_DIM)
        din, qd, kd, cd = din_ref[hh], qd_ref[hh], kd_ref[hh], cd_ref[hh]
        q_t, k_t, v_t, g_t = (_group_rows(r, group, batch, t, cols) for r in (q_ref, k_ref, v_ref, g_ref))
        out_t = [jnp.zeros((SUBLANES, HEAD_DIM), F32) for _ in range(t)]
        for bl in range(SUBLANES):
            qc = _seq_tile(q_t, bl, RET_CHUNK)
            kc = _seq_tile(k_t, bl, RET_CHUNK)
            vc = _seq_tile(v_t, bl, RET_CHUNK).astype(BF16)
            o, s_new = _ret_chunk(qc, kc, vc, s0_ref[bl, hh], din, qd, kd, cd)
            sout_ref[bl, hh] = s_new
            gated = _group_norm_gate(o[0:SUBLANES, :], _seq_tile(g_t, bl, SUBLANES))
            for ti in range(t):
                out_t[ti] = jnp.where(r_i == bl, gated[ti:ti + 1, :], out_t[ti])
        for ti in range(t):
            o_ref[pl.ds(pl.multiple_of(ti * batch + group * SUBLANES, SUBLANES), SUBLANES), cols] = out_t[ti]


def retention_sample(rq, rk, rv, rg, states, layer, t):
    batch = states.shape[1]
    bb = SUBLANES
    assert batch % bb == 0
    tabs = retention_tables(t, RET_CHUNK)
    tok = pl.BlockSpec(rq.shape, lambda i: (0, 0))
    st_in = pl.BlockSpec((None, bb, N_HEADS, HEAD_DIM, HEAD_DIM), lambda i: (layer, i, 0, 0, 0))
    st = pl.BlockSpec((bb, N_HEADS, HEAD_DIM, HEAD_DIM), lambda i: (i, 0, 0, 0))
    tab = pl.BlockSpec((N_HEADS, RET_CHUNK, HEAD_DIM), lambda i: (0, 0, 0))
    return pl.pallas_call(
        functools.partial(_ret_sample_kernel, t=t, batch=batch),
        grid=(batch // bb,),
        in_specs=[tok, tok, tok, tok, st_in, tab, tab, tab,
                  pl.BlockSpec((N_HEADS, 1, HEAD_DIM), lambda i: (0, 0, 0))],
        out_specs=[tok, st],
        out_shape=[jax.ShapeDtypeStruct(rq.shape, F32),
                   jax.ShapeDtypeStruct(states.shape[1:], F32)],
        compiler_params=_params("arbitrary"),
        name="retention_sample",
    )(rq, rk, rv, rg, states, *tabs)


def _beaten_counts(rows, n_valid):
    counts = []
    for n in range(n_valid):
        cnt = jnp.zeros(rows[n].shape, jnp.int32)
        for m in range(n_valid):
            if m == n:
                continue
            beats = (rows[m] >= rows[n]) if m < n else (rows[m] > rows[n])
            cnt = cnt + beats.astype(jnp.int32)
        counts.append(cnt)
    return counts


def _moba_prompt_phases(q_ref, k_ref, v_ref, km_ref, o_ref, s0_ref, s1_ref, *, nblk):
    blk = MOBA_BLOCK
    scale = HEAD_DIM ** -0.5
    kb = k_ref[...]
    vt = v_ref[...].astype(F32).T.astype(BF16)
    kmean = km_ref[...].astype(BF16)
    key_i = lax.broadcasted_iota(jnp.int32, (blk, blk), 0)
    qry_i = lax.broadcasted_iota(jnp.int32, (blk, blk), 1)
    causal = key_i <= qry_i

    def query_block(qi, s_ref):
        qb = q_ref[qi * blk:(qi + 1) * blk, :]
        bias = None
        if qi > MOBA_TOPK:
            gt = lax.dot_general(kmean, qb, _NT, preferred_element_type=F32)
            cnt = _beaten_counts([gt[m:m + 1, :] for m in range(qi)], qi)
            bias = [jnp.where(c < MOBA_TOPK, 0.0, NEG) for c in cnt]
        mx = jnp.full((1, blk), NEG, F32)
        for n in range(qi + 1):
            s = lax.dot_general(kb[n * blk:(n + 1) * blk], qb, _NT,
                                preferred_element_type=F32)
            if n == qi:
                s = jnp.where(causal, s, NEG)
            elif bias is not None:
                s = s + bias[n]
            s_ref[n * blk:(n + 1) * blk, :] = s
            mx = jnp.maximum(mx, jnp.max(s, axis=0, keepdims=True))
            yield
        l = jnp.zeros((1, blk), F32)
        ot = jnp.zeros((HEAD_DIM, blk), F32)
        for n in range(qi + 1):
            p = jnp.exp2((s_ref[n * blk:(n + 1) * blk, :] - mx) * (scale * LOG2_E))
            l = l + jnp.sum(p, axis=0, keepdims=True)
            ot = ot + _dot(vt[:, n * blk:(n + 1) * blk], p.astype(BF16))
            yield
        ot = ot * (1.0 / l)
        o_ref[qi * blk:(qi + 1) * blk, :] = ot.T.astype(o_ref.dtype)
        yield

    blocks = [query_block(qi, (s0_ref, s1_ref)[qi % 2]) for qi in range(nblk)]
    for _ in range(1):
        next(blocks[0])
    for qi in range(nblk):
        value_steps = qi + 2
        score_steps = qi + 2 if qi + 1 < nblk else 0
        for step in range(max(value_steps, score_steps)):
            if step < score_steps:
                next(blocks[qi + 1])
            if step < value_steps:
                next(blocks[qi])
            yield


def _attention_prompt_kernel(mq_ref, mk_ref, mv_ref, km_ref, rq_ref, rk_ref, rv_ref, rg_ref,
                             din_ref, qd_ref, kd_ref, cd_ref, om_ref, or_ref, sout_ref, s0_ref, s1_ref,
                             *, nblk):
    moba = _moba_prompt_phases(mq_ref, mk_ref, mv_ref, km_ref, om_ref, s0_ref, s1_ref, nblk=nblk)
    ret = _ret_prompt_phases(rq_ref, rk_ref, rv_ref, rg_ref, din_ref, qd_ref, kd_ref, cd_ref, or_ref, sout_ref)
    n_moba = sum(qi + 2 for qi in range(nblk))
    n_ret = rq_ref.shape[0] // din_ref.shape[0] + 1
    every = max(n_moba // n_ret, 1)
    for step, _ in enumerate(moba):
        if step % every == 0:
            next(ret, None)
    _run(ret)


def attention_prompt(mq, mk, mv, kmean, rq, rk, rv, rg, batch, seq):
    nblk = seq // MOBA_BLOCK
    assert nblk % SUBLANES == 0
    chunk = min(2 * RET_CHUNK, seq)
    din, qd, kd, cd = retention_tables(chunk, chunk)
    spec = pl.BlockSpec((seq, HEAD_DIM), lambda b, h: (b, h))
    head = lambda a: pl.BlockSpec((None,) + a.shape[1:], lambda b, h: (h, 0, 0))
    return pl.pallas_call(
        functools.partial(_attention_prompt_kernel, nblk=nblk),
        grid=(batch, N_HEADS),
        in_specs=[spec, spec, spec, pl.BlockSpec((nblk, HEAD_DIM), lambda b, h: (b, h)),
                  spec, spec, spec, spec, head(din), head(qd), head(kd), head(cd)],
        out_specs=[spec, spec,
                   pl.BlockSpec((None, None, HEAD_DIM, HEAD_DIM), lambda b, h: (b, h, 0, 0))],
        out_shape=[jax.ShapeDtypeStruct((batch * seq, GROUP_W), BF16),
                   jax.ShapeDtypeStruct((batch * seq, GROUP_W), BF16),
                   jax.ShapeDtypeStruct((batch, N_HEADS, HEAD_DIM, HEAD_DIM), F32)],
        scratch_shapes=[pltpu.VMEM((seq, MOBA_BLOCK), F32), pltpu.VMEM((seq, MOBA_BLOCK), F32)],
        compiler_params=_params("parallel", "parallel"),
        name="attention_prompt",
    )(mq, mk, mv, kmean, rq, rk, rv, rg, din, qd, kd, cd)


def _page_rows(ref, page):
    return jnp.concatenate([ref[pl.ds(hh, page, stride=N_HEADS), :] for hh in range(N_HEADS)], axis=1)


def _moba_sample_phases(b, q_ref, kn_ref, vn_ref, k_refs, v_refs, o_ref, *, page, t):
    n_pages = len(k_refs)
    scale = HEAD_DIM ** -0.5
    ppb = MOBA_BLOCK // page
    nblk = n_pages // ppb
    n_pairs = N_HEADS * t
    batch = q_ref.shape[0] // t
    all_cols = slice(0, GROUP_W)
    group = b // SUBLANES
    row = b % SUBLANES
    q = _seq_tile(_group_rows(q_ref, group, batch, t, all_cols), row, SUBLANES)

    r_i = lax.broadcasted_iota(jnp.int32, (n_pairs, GROUP_W), 0)
    l_i = lax.broadcasted_iota(jnp.int32, (n_pairs, GROUP_W), 1)
    qm = jnp.zeros((n_pairs, GROUP_W), F32)
    for ti in range(t):
        for hh in range(N_HEADS):
            hit = (r_i == hh * t + ti) & (l_i >= hh * HEAD_DIM) & (l_i < (hh + 1) * HEAD_DIM)
            qm = jnp.where(hit, q[ti:ti + 1, :], qm)
    qm = qm.astype(BF16)

    zpad = jnp.zeros((page - SUBLANES, GROUP_W), F32)
    own = lambda ref: jnp.concatenate(
        [_seq_tile(_group_rows(ref, group, batch, t, all_cols), row, SUBLANES), zpad], axis=0).astype(BF16)
    k_own, v_own = own(kn_ref), own(vn_ref)

    b_i = lax.broadcasted_iota(jnp.int32, (page, GROUP_W), 0)
    kmean = jnp.zeros((page, GROUP_W), F32)
    scores = []
    for p in range(n_pages):
        kp = _page_rows(k_refs[p], page)
        scores.append(lax.dot_general(qm, kp.astype(BF16), _NT,
                                      preferred_element_type=F32) * scale)
        ks = jnp.sum(kp, axis=0, keepdims=True) * (1.0 / MOBA_BLOCK)
        kmean = kmean + jnp.where(b_i == p // ppb, ks, 0.0)
        yield
    gate = lax.dot_general(qm, kmean.astype(BF16), _NT, preferred_element_type=F32)
    cnt = _beaten_counts([gate[:, m:m + 1] for m in range(nblk)], nblk)
    keep = [jnp.broadcast_to(c, (n_pairs, page)) < MOBA_TOPK for c in cnt]

    s_own = lax.dot_general(qm, k_own, _NT, preferred_element_type=F32) * scale
    key_i = lax.broadcasted_iota(jnp.int32, (n_pairs, page), 1)
    tok_i = lax.rem(lax.broadcasted_iota(jnp.int32, (n_pairs, page), 0), t)
    s_own = jnp.where((key_i <= tok_i) & (key_i < t), s_own, NEG)

    mx = s_own
    for p in range(n_pages):
        scores[p] = jnp.where(keep[p // ppb], scores[p], NEG)
        mx = jnp.maximum(mx, scores[p])
    mx = jnp.max(mx, axis=1, keepdims=True)
    e_own = jnp.exp(s_own - mx)
    l = e_own
    probs = []
    for p in range(n_pages):
        e = jnp.exp(scores[p] - mx)
        l = l + e
        probs.append(e.astype(BF16))
    inv_l = 1.0 / jnp.sum(l, axis=1, keepdims=True)
    yield
    acc = _dot(e_own.astype(BF16), v_own)
    for p in range(n_pages):
        acc = acc + _dot(probs[p], _page_rows(v_refs[p], page).astype(BF16))
        yield
    acc = acc * inv_l
    o_ref[...] = jnp.zeros_like(o_ref)
    for hh in range(N_HEADS):
        cols = slice(hh * HEAD_DIM, (hh + 1) * HEAD_DIM)
        o_ref[0:t, cols] = acc[hh * t:(hh + 1) * t, cols]
    yield


def _mixer_out(or_ref, om_ref, w_ref, x_ref, gate_ref, g_ref, b_ref, o_ref, sub):
    for r0 in range(0, x_ref.shape[0], sub):
        rows = slice(r0, r0 + sub)
        y = (_dot(or_ref[rows, :].astype(BF16), w_ref[:GROUP_W, :])
             + _dot(om_ref[rows, :].astype(BF16), w_ref[GROUP_W:, :]))
        gate = gate_ref[...] if gate_ref.shape[0] == 1 else gate_ref[rows, :]
        v = ALPHA * x_ref[rows, :] + (1.0 + gate) * y
        o_ref[rows, :] = _layer_norm(v, g_ref[...], b_ref[...])


def _out_proj_kernel(or_ref, om_ref, w_ref, x_ref, gate_ref, g_ref, b_ref, o_ref, *, sub):
    _mixer_out(or_ref, om_ref, w_ref, x_ref, gate_ref, g_ref, b_ref, o_ref, sub)


def out_proj(o_r, o_m, w_bf, x2d, gate, ln_g, ln_b, rows_per_group, tm):
    n, d = x2d.shape
    vec = pl.BlockSpec((1, d), lambda i: (0, 0))
    return pl.pallas_call(
        functools.partial(_out_proj_kernel, sub=min(128, tm)),
        grid=(n // tm,),
        in_specs=[pl.BlockSpec((tm, GROUP_W), lambda i: (i, 0)),
                  pl.BlockSpec((tm, GROUP_W), lambda i: (i, 0)),
                  _resident(w_bf),
                  pl.BlockSpec((tm, d), lambda i: (i, 0)),
                  _mod_spec(gate, tm, rows_per_group), vec, vec],
        out_specs=pl.BlockSpec((tm, d), lambda i: (i, 0)),
        out_shape=jax.ShapeDtypeStruct((n, d), F32),
        compiler_params=_params("parallel"),
        name="out_proj",
    )(o_r, o_m, w_bf, x2d, gate, ln_g.reshape(1, d), ln_b.reshape(1, d))


def _run(phases):
    for _ in phases:
        pass


def _ffn_phases(x, sh_ref, sc_ref, gate_ref, wup_ref, wdw_ref, bdw_ref, wdn_ref, g_ref, b_ref,
                o_ref, prev_ref, h_s, act_s, hist_s, *, d_ff, cw, planes, chunks=None):
    tm = x.shape[0]
    n_chunks = d_ff // cw
    chunks = range(n_chunks) if chunks is None else chunks
    if chunks.start == 0:
        h_s[...] = (x * (1.0 + sc_ref[...]) + sh_ref[...]).astype(BF16)
    row = lax.broadcasted_iota(jnp.int32, (tm, cw), 0)
    for c in chunks:
        cu = slice(c * cw, (c + 1) * cw)
        cv = slice(d_ff + c * cw, d_ff + (c + 1) * cw)
        u = _dot(h_s[...], wup_ref[:, cu])
        v = _dot(h_s[...], wup_ref[:, cv])
        if planes:
            s2 = hist_s[0, :, cu]
            s1 = hist_s[1, :, cu]
            hist_s[0, :, cu] = s1
            hist_s[1, :, cu] = u
            prev_ref[:, cu] = u
        else:
            e1 = hist_s[1:2, cu]
            e2 = jnp.where(row == 0, hist_s[0:1, cu], hist_s[1:2, cu])
            hist_s[:, cu] = u[tm - 2:tm, :]
            prev_ref[:, cu] = u[tm - 2:tm, :]
            s1 = jnp.where(row >= 1, pltpu.roll(u, 1, axis=0), e1)
            s2 = jnp.where(row >= 2, pltpu.roll(u, 2, axis=0), e2)
        uc = wdw_ref[0:1, cu] * s2 + wdw_ref[1:2, cu] * s1 + wdw_ref[2:3, cu] * u + bdw_ref[:, cu]
        act = 0.5 * uc * (1.0 + lax.erf(uc * (2.0 ** -0.5))) * v
        act_s[:, cu] = act.astype(BF16)
        yield act[0:1, 0:HEAD_DIM]
    if chunks.stop != n_chunks:
        return
    fy = _dot(act_s[...], wdn_ref[...])
    vsum = ALPHA * x + (1.0 + gate_ref[...]) * fy
    o_ref[...] = _layer_norm(vsum, g_ref[...], b_ref[...])
    yield


def _ffn_planes_kernel(x_ref, sh_ref, sc_ref, gate_ref, wup_ref, wdw_ref, bdw_ref, wdn_ref, g_ref, b_ref,
                       p0_ref, p1_ref, o_ref, prev_ref, h_s, act_s, hist_s, *, d_ff, cw):
    @pl.when(pl.program_id(0) == 0)
    def _():
        hist_s[0] = p0_ref[...]
        hist_s[1] = p1_ref[...]

    _run(_ffn_phases(x_ref[...], sh_ref, sc_ref, gate_ref, wup_ref, wdw_ref, bdw_ref, wdn_ref,
                     g_ref, b_ref, o_ref, prev_ref, h_s, act_s, hist_s, d_ff=d_ff, cw=cw, planes=True))


def conv_ffn_planes(x2d, shift, scale, gate, w_up_bf, w_dw, b_dw, w_dn_bf, layer, ln_g, ln_b,
                    tm, history):
    n, d = x2d.shape
    d_ff = w_dn_bf.shape[1]
    mod = _mod_spec(shift, tm, n)
    vec = pl.BlockSpec((1, d), lambda i: (0, 0))
    b_dw2 = b_dw.reshape(1, d_ff)
    return pl.pallas_call(
        functools.partial(_ffn_planes_kernel, d_ff=d_ff, cw=256),
        grid=(n // tm,),
        in_specs=[pl.BlockSpec((tm, d), lambda i: (i, 0)), mod, mod, mod,
                  _resident(w_up_bf, layer), _resident(w_dw), _resident(b_dw2),
                  _resident(w_dn_bf, layer), vec, vec, _resident(history[0]), _resident(history[1])],
        out_specs=[pl.BlockSpec((tm, d), lambda i: (i, 0)), pl.BlockSpec((tm, d_ff), lambda i: (i, 0))],
        out_shape=[jax.ShapeDtypeStruct((n, d), F32), jax.ShapeDtypeStruct((n, d_ff), F32)],
        scratch_shapes=[pltpu.VMEM((tm, d), BF16), pltpu.VMEM((tm, d_ff), BF16),
                        pltpu.VMEM((2, tm, d_ff), F32)],
        compiler_params=_params("arbitrary"),
        name="conv_ffn_planes",
    )(x2d, shift, scale, gate, w_up_bf, w_dw, b_dw2, w_dn_bf, ln_g.reshape(1, d), ln_b.reshape(1, d),
      *history)


def _moba_ffn_kernel(pt_ref, q_ref, kn_ref, vn_ref, *rest, n_pages, page, t, d_ff, cw, parts, tiles_per_seq):
    k_refs = rest[:n_pages]
    v_refs = rest[n_pages:2 * n_pages]
    (or_ref, omp_ref, wout_ref, x_ref, gate0_ref, g0_ref, b0_ref,
     sh_ref, sc_ref, gate_ref, wup_ref, wdw_ref, bdw_ref, wdn_ref, g_ref, b_ref,
     om_ref, o_ref, prev_ref, x1_s, h_s, act_s, hist_s) = rest[2 * n_pages:]
    i = pl.program_id(0)
    n_chunks = d_ff // cw
    per_part = -(-n_chunks // parts)

    @pl.when(i % (parts * tiles_per_seq) == 0)
    def _():
        hist_s[...] = jnp.zeros_like(hist_s)

    def step(part):
        chunks = range(part * per_part, min((part + 1) * per_part, n_chunks))
        if part == 0:
            _mixer_out(or_ref, omp_ref, wout_ref, x_ref, gate0_ref, g0_ref, b0_ref, x1_s, sub=128)
        ffn_ph = _ffn_phases(x1_s[...], sh_ref, sc_ref, gate_ref, wup_ref, wdw_ref, bdw_ref, wdn_ref,
                             g_ref, b_ref, o_ref, prev_ref, h_s, act_s, hist_s,
                             d_ff=d_ff, cw=cw, planes=False, chunks=chunks)
        moba_ph = _moba_sample_phases(i, q_ref, kn_ref, vn_ref, k_refs, v_refs, om_ref, page=page, t=t)
        for _ in range(n_pages):
            next(moba_ph)
        next(ffn_ph)
        next(moba_ph)
        for _ in range(len(chunks) - 1):
            next(ffn_ph)
        for _ in range(n_pages + 1):
            next(moba_ph)
        _run(ffn_ph)

    for part in range(parts):
        pl.when(i % parts == part)(functools.partial(step, part))


def moba_sample_ffn_prompt(mq, mk, mv, cache_k, cache_v, page_table, page_base, page, t,
                           o_r, o_m, w_out_bf, x2d, gate0, ln_g0, ln_b0,
                           shift, scale, gate, w_up_bf, w_dw, b_dw, w_dn_bf, layer, ln_g, ln_b,
                           seq, tm):
    batch, n_pages = page_table.shape
    n, d = x2d.shape
    d_ff = w_dn_bf.shape[1]
    assert (n_pages * page) % MOBA_BLOCK == 0 and t <= SUBLANES
    assert n_pages * page // MOBA_BLOCK <= page
    n_tiles = n // tm
    assert batch % n_tiles == 0
    parts = batch // n_tiles
    tok = pl.BlockSpec(mq.shape, lambda i, pt: (0, 0))

    def page_spec(p):
        return pl.BlockSpec((page * N_HEADS, HEAD_DIM), lambda i, pt: (page_base + pt[i, p], 0))

    const = lambda a, lead=None: pl.BlockSpec(
        a.shape if lead is None else (None,) + a.shape[1:],
        (lambda i, pt: (0,) * a.ndim) if lead is None else (lambda i, pt: (lead,) + (0,) * (a.ndim - 1)),
        pipeline_mode=pl.Buffered(1))
    mod = pl.BlockSpec((None, 1, d), lambda i, pt: ((i // parts) * tm // seq, 0, 0))
    b_dw2, g2, b2 = b_dw.reshape(1, d_ff), ln_g.reshape(1, d), ln_b.reshape(1, d)
    g0, b0 = ln_g0.reshape(1, d), ln_b0.reshape(1, d)
    tile = lambda w: pl.BlockSpec((tm, w), lambda i, pt: (i // parts, 0))
    specs = ([tok, tok, tok] + [page_spec(p) for p in range(n_pages)] * 2
             + [tile(GROUP_W), tile(GROUP_W), const(w_out_bf), tile(d), mod, const(g0), const(b0), mod, mod, mod,
                const(w_up_bf, layer), const(w_dw), const(b_dw2), const(w_dn_bf, layer), const(g2), const(b2)])
    grid_spec = pltpu.PrefetchScalarGridSpec(
        num_scalar_prefetch=1,
        grid=(batch,),
        in_specs=specs,
        out_specs=[pl.BlockSpec((None, SUBLANES, GROUP_W), lambda i, pt: (i, 0, 0)),
                   pl.BlockSpec((tm, d), lambda i, pt: (i // parts, 0)),
                   pl.BlockSpec((None, 2, d_ff), lambda i, pt: (i // parts, 0, 0))],
        scratch_shapes=[pltpu.VMEM((tm, d), F32), pltpu.VMEM((tm, d), BF16), pltpu.VMEM((tm, d_ff), BF16),
                        pltpu.VMEM((2, d_ff), F32)],
    )
    return pl.pallas_call(
        functools.partial(_moba_ffn_kernel, n_pages=n_pages, page=page, t=t, d_ff=d_ff, cw=256,
                          parts=parts, tiles_per_seq=seq // tm),
        grid_spec=grid_spec,
        out_shape=[jax.ShapeDtypeStruct((batch, SUBLANES, GROUP_W), F32),
                   jax.ShapeDtypeStruct((n, d), F32),
                   jax.ShapeDtypeStruct((n_tiles, 2, d_ff), F32)],
        compiler_params=_params("arbitrary"),
        name="moba_sample_ffn_prompt",
    )(page_table, mq, mk, mv, *([cache_k] * n_pages), *([cache_v] * n_pages),
      o_r, o_m, w_out_bf, x2d, gate0, g0, b0, shift, scale, gate, w_up_bf, w_dw, b_dw2, w_dn_bf, g2, b2)


def _glu_kernel(x_ref, sh_ref, sc_ref, w_ref, b_ref, o_ref, *, cw):
    h = (x_ref[...] * (1.0 + sc_ref[...]) + sh_ref[...]).astype(BF16)
    dc = o_ref.shape[1]
    for c in range(dc // cw):
        ca = slice(c * cw, (c + 1) * cw)
        cg = slice(dc + c * cw, dc + (c + 1) * cw)
        a = _dot(h, w_ref[:, ca]) + b_ref[:, ca]
        g = _dot(h, w_ref[:, cg]) + b_ref[:, cg]
        o_ref[:, ca] = a * jax.nn.sigmoid(g)


def conformer_glu(x2d, shift, scale, w1_bf, b1, rows_per_group, tm):
    n, d = x2d.shape
    dc = w1_bf.shape[1] // 2
    mod = _mod_spec(shift, tm, rows_per_group)
    b1r = b1.reshape(1, 2 * dc)
    return pl.pallas_call(
        functools.partial(_glu_kernel, cw=256),
        grid=(n // tm,),
        in_specs=[pl.BlockSpec((tm, d), lambda i: (i, 0)), mod, mod,
                  _resident(w1_bf), _resident(b1r)],
        out_specs=pl.BlockSpec((tm, dc), lambda i: (i, 0)),
        out_shape=jax.ShapeDtypeStruct((n, dc), F32),
        compiler_params=_params("parallel"),
        name="conformer_glu",
    )(x2d, shift, scale, w1_bf, b1r)


PAD_ROWS = 32


def _dwconv_rows(win_ref, w_ref, bias, r0, rs, cols, width):
    off = PAD_ROWS - (width - 1)
    y = bias
    for r in range(SUBLANES):
        z = None
        for m in range((off + width - 1) // SUBLANES + 1):
            k = SUBLANES * m + r - off
            if 0 <= k < width:
                lo = r0 + SUBLANES * m
                term = w_ref[k:k + 1, cols] * win_ref[lo:lo + rs + SUBLANES, cols]
                z = term if z is None else z + term
        if z is not None:
            y = y + z[r:r + rs, :]
    return y


def _ordered_after(x, token):
    zero = lax.shift_right_logical(lax.shift_right_logical(
        lax.bitcast_convert_type(token, jnp.uint32), jnp.uint32(16)), jnp.uint32(16))
    return lax.bitcast_convert_type(lax.bitcast_convert_type(x, jnp.uint32) | zero, F32)


def _conformer_phases(x, sh_ref, sc_ref, gate_ref, w1_ref, b1_ref, wdw_ref, bdw_ref,
                      gl_ref, bl_ref, w2_ref, b2_ref, g_ref, b_ref, o_ref, tail_ref, win_ref, y_ref,
                      *, width, rs, cw, after=None):
    tm, dc = y_ref.shape
    h = (x * (1.0 + sc_ref[...]) + sh_ref[...]).astype(BF16)
    for c in range(dc // cw):
        ca = slice(c * cw, (c + 1) * cw)
        cg = slice(dc + c * cw, dc + (c + 1) * cw)
        a = _dot(h, w1_ref[:, ca]) + b1_ref[:, ca]
        g = _dot(h, w1_ref[:, cg]) + b1_ref[:, cg]
        win_ref[PAD_ROWS:PAD_ROWS + tm, ca] = a * jax.nn.sigmoid(g)
    tail_ref[...] = win_ref[tm:tm + PAD_ROWS, :]
    yield
    for c in range(dc // HEAD_DIM):
        cols = slice(c * HEAD_DIM, (c + 1) * HEAD_DIM)
        for r0 in range(0, tm, rs):
            bias = bdw_ref[:, cols]
            if after is not None and after[0] is not None:
                bias = _ordered_after(bias, after[0])
            y_ref[r0:r0 + rs, cols] = _dwconv_rows(win_ref, wdw_ref, bias, r0, rs, cols, width)
            yield
    win_ref[0:PAD_ROWS, :] = win_ref[tm:tm + PAD_ROWS, :]
    yn = _silu(_layer_norm(y_ref[...], gl_ref[...], bl_ref[...]))
    y = _dot(yn.astype(BF16), w2_ref[...]) + b2_ref[...]
    v = ALPHA * x + (1.0 + gate_ref[...]) * y
    o_ref[...] = _layer_norm(v, g_ref[...], b_ref[...])
    yield


N_CONF_CONSTS = 10
N_FFN_CONSTS = 6


def _conformer_ffn_kernel(*refs, tiles_per_seq, width, rs, cw, d_ff):
    x_ref, shc_ref, scc_ref, gtc_ref, shf_ref, scf_ref, gtf_ref = refs[:7]
    conf = refs[7:7 + N_CONF_CONSTS]
    ffn = refs[7 + N_CONF_CONSTS:7 + N_CONF_CONSTS + N_FFN_CONSTS]
    o_ref, tail_ref, prev_ref, win_ref, y_ref, x1_s, h_s, act_s, hist_s = refs[7 + N_CONF_CONSTS + N_FFN_CONSTS:]
    i = pl.program_id(0)
    n_tiles = pl.num_programs(0) - 1
    tm, dc = y_ref.shape

    @pl.when(i == 0)
    def _():
        x1_s[...] = jnp.zeros_like(x1_s)
        win_ref[PAD_ROWS + tm:, :] = jnp.zeros((SUBLANES, dc), F32)

    @pl.when(jnp.minimum(i, n_tiles - 1) % tiles_per_seq == 0)
    def _():
        win_ref[0:PAD_ROWS, :] = jnp.zeros((PAD_ROWS, dc), F32)

    @pl.when(jnp.maximum(i - 1, 0) % tiles_per_seq == 0)
    def _():
        hist_s[...] = jnp.zeros_like(hist_s)

    ffn_ph = _ffn_phases(x1_s[...], shf_ref, scf_ref, gtf_ref, *ffn, o_ref, prev_ref, h_s, act_s, hist_s,
                         d_ff=d_ff, cw=cw, planes=False)
    token = [None]
    conf_ph = _conformer_phases(x_ref[...], shc_ref, scc_ref, gtc_ref, *conf, x1_s, tail_ref, win_ref, y_ref,
                                width=width, rs=rs, cw=cw, after=token)
    next(conf_ph)
    n_ffn = d_ff // cw
    n_conv = (dc // HEAD_DIM) * (tm // rs)
    per_chunk = -(-n_conv // n_ffn)
    for _ in range(n_ffn):
        token[0] = next(ffn_ph)
        for _ in range(per_chunk):
            next(conf_ph, None)
    _run(ffn_ph)
    _run(conf_ph)


def conformer_ffn_prompt(x2d, mods_conf, mods_ffn, w1_bf, b1, w_dw, b_dw, cf_g, cf_b, w2_bf, b2,
                         ln_g0, ln_b0, w_up_bf, ffn_w_dw, ffn_b_dw, w_dn_bf, layer, ln_g1, ln_b1,
                         batch, seq, tm):
    n, d = x2d.shape
    dc = w2_bf.shape[0]
    d_ff = w_dn_bf.shape[1]
    width = w_dw.shape[0]
    assert width - 1 <= PAD_ROWS and tm % PAD_ROWS == 0
    nj = seq // tm
    n_tiles = n // tm
    row = lambda a: a.reshape(1, -1)
    conf_tile = lambda i: jnp.minimum(i, n_tiles - 1)
    ffn_tile = lambda i: jnp.maximum(i - 1, 0)
    mod_c = pl.BlockSpec((None, 1, d), lambda i: (conf_tile(i) // nj, 0, 0))
    mod_f = pl.BlockSpec((None, 1, d), lambda i: (ffn_tile(i) // nj, 0, 0))
    conf = [w1_bf, row(b1), w_dw, row(b_dw), row(cf_g), row(cf_b), w2_bf, row(b2), row(ln_g0), row(ln_b0)]
    ffn = [ffn_w_dw, row(ffn_b_dw), row(ln_g1), row(ln_b1)]
    assert len(conf) == N_CONF_CONSTS and len(ffn) + 2 == N_FFN_CONSTS
    ffn_specs = [_resident(w_up_bf, layer), _resident(ffn[0]), _resident(ffn[1]),
                 _resident(w_dn_bf, layer), _resident(ffn[2]), _resident(ffn[3])]
    ffn_args = [w_up_bf, ffn[0], ffn[1], w_dn_bf, ffn[2], ffn[3]]
    return pl.pallas_call(
        functools.partial(_conformer_ffn_kernel, tiles_per_seq=nj, width=width, rs=64, cw=256, d_ff=d_ff),
        grid=(n_tiles + 1,),
        in_specs=[pl.BlockSpec((tm, d), lambda i: (conf_tile(i), 0)), mod_c, mod_c, mod_c,
                  mod_f, mod_f, mod_f] + [_resident(a) for a in conf] + ffn_specs,
        out_specs=[pl.BlockSpec((tm, d), lambda i: (ffn_tile(i), 0)),
                   pl.BlockSpec((None, PAD_ROWS, dc), lambda i: (conf_tile(i) // nj, 0, 0)),
                   pl.BlockSpec((None, 2, d_ff), lambda i: (ffn_tile(i), 0, 0))],
        out_shape=[jax.ShapeDtypeStruct((n, d), F32),
                   jax.ShapeDtypeStruct((batch, PAD_ROWS, dc), F32),
                   jax.ShapeDtypeStruct((n_tiles, 2, d_ff), F32)],
        scratch_shapes=[pltpu.VMEM((PAD_ROWS + tm + SUBLANES, dc), F32), pltpu.VMEM((tm, dc), F32),
                        pltpu.VMEM((tm, d), F32), pltpu.VMEM((tm, d), BF16),
                        pltpu.VMEM((tm, d_ff), BF16), pltpu.VMEM((2, d_ff), F32)],
        compiler_params=_params("arbitrary"),
        name="conformer_ffn_prompt",
    )(x2d, *mods_conf, *mods_ffn, *conf, *ffn_args)


def _dwconv_sample_kernel(x_ref, st_ref, w_ref, b_ref, o_ref, nst_ref, *, width, t):
    hist = width - 1
    bb, dc = x_ref.shape[1], x_ref.shape[2]
    plane = lambda j, cols: st_ref[j, :, cols] if j < hist else x_ref[j - hist, :, cols]
    for c in range(dc // HEAD_DIM):
        cols = slice(c * HEAD_DIM, (c + 1) * HEAD_DIM)
        acc = [jnp.zeros((bb, HEAD_DIM), F32) + b_ref[:, cols] for _ in range(t)]
        for j in range(hist + t):
            p = plane(j, cols)
            for ti in range(t):
                if 0 <= j - ti < width:
                    acc[ti] = acc[ti] + w_ref[j - ti:j - ti + 1, cols] * p
            if j >= t:
                nst_ref[j - t, :, cols] = p
        for ti in range(t):
            o_ref[ti, :, cols] = acc[ti]


def dwconv_sample(glu, states, layer, w_dw, b_dw):
    t, batch, dc = glu.shape
    width = w_dw.shape[0]
    hist = width - 1
    bb = 32
    b2 = b_dw.reshape(1, dc)
    return pl.pallas_call(
        functools.partial(_dwconv_sample_kernel, width=width, t=t),
        grid=(batch // bb,),
        in_specs=[pl.BlockSpec((t, bb, dc), lambda i: (0, i, 0)),
                  pl.BlockSpec((None, hist, bb, dc), lambda i: (layer, 0, i, 0)),
                  pl.BlockSpec(w_dw.shape, lambda i: (0, 0)),
                  pl.BlockSpec(b2.shape, lambda i: (0, 0))],
        out_specs=[pl.BlockSpec((t, bb, dc), lambda i: (0, i, 0)),
                   pl.BlockSpec((hist, bb, dc), lambda i: (0, i, 0))],
        out_shape=[jax.ShapeDtypeStruct((t, batch, dc), F32),
                   jax.ShapeDtypeStruct((hist, batch, dc), F32)],
        compiler_params=_params("parallel"),
        name="dwconv_sample",
    )(glu, states, w_dw, b2)


def _conf_tail_kernel(y_ref, gl_ref, bl_ref, w_ref, b2_ref, x_ref, gate_ref, g_ref, b_ref, o_ref):
    yn = _silu(_layer_norm(y_ref[...], gl_ref[...], bl_ref[...]))
    y = _dot(yn.astype(BF16), w_ref[...]) + b2_ref[...]
    v = ALPHA * x_ref[...] + (1.0 + gate_ref[...]) * y
    o_ref[...] = _layer_norm(v, g_ref[...], b_ref[...])


def conformer_tail(y, cf_g, cf_b, w2_bf, b2, x2d, gate, ln_g, ln_b, rows_per_group, tm):
    n, d = x2d.shape
    dc = y.shape[1]
    vecc = pl.BlockSpec((1, dc), lambda i: (0, 0))
    vec = pl.BlockSpec((1, d), lambda i: (0, 0))
    return pl.pallas_call(
        _conf_tail_kernel,
        grid=(n // tm,),
        in_specs=[pl.BlockSpec((tm, dc), lambda i: (i, 0)), vecc, vecc,
                  _resident(w2_bf), vec,
                  pl.BlockSpec((tm, d), lambda i: (i, 0)),
                  _mod_spec(gate, tm, rows_per_group), vec, vec],
        out_specs=pl.BlockSpec((tm, d), lambda i: (i, 0)),
        out_shape=jax.ShapeDtypeStruct((n, d), F32),
        compiler_params=_params("parallel"),
        name="conformer_tail",
    )(y, cf_g.reshape(1, dc), cf_b.reshape(1, dc), w2_bf, b2.reshape(1, d), x2d, gate,
      ln_g.reshape(1, d), ln_b.reshape(1, d))


def kernel(x_prompt, x_sample, cache_k, cache_v, state_ret, state_conv, state_ffn, page_table, c_prompt, c_sample, ab_w_in, ab_w_out, cf_w_pw1, cf_b_pw1, cf_w_dw, cf_b_dw, cf_ln_g, cf_ln_b, cf_w_pw2, cf_b_pw2, ffn_w_up, ffn_w_dw, ffn_b_dw, ffn_w_down, ada_w, ada_b, ln_g, ln_b):
    bp, seq, d = x_prompt.shape
    bs, ts, _ = x_sample.shape
    n_layers = ada_w.shape[0]
    d_ff = ffn_w_down.shape[1]
    past_len = page_table.shape[1] * cache_k.shape[2]
    np_, ns = bp * seq, bs * ts
    tm_p = min(256, seq)
    tm_s = bs

    ada = adaln_all(jnp.concatenate([c_prompt, c_sample], 0), ada_w, ada_b)

    def mods(l, k):
        return ada[l, k, :bp][:, None, :], ada[l, k, bp:][None]

    token_major = lambda a: jnp.swapaxes(a, 0, 1)
    xp = x_prompt.reshape(np_, d)
    xs = token_major(x_sample).reshape(ns, d)
    tabs_p = rotary_tables(jnp.arange(seq, dtype=jnp.int32))
    tabs_s = tuple(jnp.repeat(tb, bs, axis=0) for tb in
                   rotary_tables(past_len + jnp.arange(ts, dtype=jnp.int32)))

    w_up = ffn_w_up.astype(BF16)
    w_dn = ffn_w_down.astype(BF16)
    outs = {k: [] for k in ("kp", "vp", "ks", "vs", "rp", "rs", "cp", "cs", "fp", "fs")}
    for l in range(n_layers):
        i = l // 2
        sh_p, sh_s = mods(l, 0)
        sc_p, sc_s = mods(l, 1)
        gt_p, gt_s = mods(l, 2)
        if l % 2 == 0:
            w_in = ab_w_in[i].astype(BF16)
            w_out = ab_w_out[i].astype(BF16)
            rq, rk, rv, rg, mq, mk, mv, kb, vb, kmean = in_proj(
                xp, sh_p, sc_p, w_in, tabs_p, seq, tm_p, True, BF16)
            o_mp, o_rp, s_p = attention_prompt(mq, kb, vb, kmean.reshape(-1, GROUP_W), rq, rk, rv, rg, bp, seq)
            outs["kp"].append(mk.reshape(bp, seq, N_HEADS, HEAD_DIM))
            outs["vp"].append(mv.reshape(bp, seq, N_HEADS, HEAD_DIM))
            outs["rp"].append(s_p)

            rq, rk, rv, rg, mq, mk, mv, kb, vb = in_proj(xs, sh_s, sc_s, w_in, tabs_s, ns, tm_s, False, F32)
            o_r, s_s = retention_sample(rq, rk, rv, rg, state_ret, i, ts)
            n_phys, page = cache_k.shape[1], cache_k.shape[2]
            ck = cache_k.reshape(-1, HEAD_DIM)
            cv = cache_v.reshape(-1, HEAD_DIM)
            mods_ffn_p = tuple(mods(l, k)[0] for k in (3, 4, 5))
            o_m, xp, prev_p = moba_sample_ffn_prompt(
                mq, kb, vb, ck, cv, page_table, i * n_phys, page, ts,
                o_rp, o_mp, w_out, xp, gt_p, ln_g[l, 0], ln_b[l, 0], *mods_ffn_p, w_up, ffn_w_dw[l], ffn_b_dw[l], w_dn, l, ln_g[l, 1], ln_b[l, 1], seq, tm_p)
            o_m = token_major(o_m[:, :ts]).reshape(ns, GROUP_W)
            xs = out_proj(o_r, o_m, w_out, xs, gt_s, ln_g[l, 0], ln_b[l, 0], ns, tm_s)
            outs["ks"].append(token_major(mk.reshape(ts, bs, N_HEADS, HEAD_DIM)))
            outs["vs"].append(token_major(mv.reshape(ts, bs, N_HEADS, HEAD_DIM)))
            outs["rs"].append(s_s)
        else:
            w1 = cf_w_pw1[i].astype(BF16)
            w2 = cf_w_pw2[i].astype(BF16)
            hist = cf_w_dw.shape[1] - 1
            mods_ffn_p = tuple(mods(l, k)[0] for k in (3, 4, 5))
            xp, tail, prev_p = conformer_ffn_prompt(
                xp, (sh_p, sc_p, gt_p), mods_ffn_p, w1, cf_b_pw1[i], cf_w_dw[i], cf_b_dw[i],
                cf_ln_g[i], cf_ln_b[i], w2, cf_b_pw2[i], ln_g[l, 0], ln_b[l, 0],
                w_up, ffn_w_dw[l], ffn_b_dw[l], w_dn, l, ln_g[l, 1], ln_b[l, 1], bp, seq, tm_p)
            outs["cp"].append(tail[:, PAD_ROWS - hist:])

            glu = conformer_glu(xs, sh_s, sc_s, w1, cf_b_pw1[i], ns, tm_s)
            y, nst = dwconv_sample(glu.reshape(ts, bs, -1), jnp.swapaxes(state_conv, 1, 2), i,
                                   cf_w_dw[i], cf_b_dw[i])
            xs = conformer_tail(y.reshape(ns, -1), cf_ln_g[i], cf_ln_b[i], w2, cf_b_pw2[i], xs, gt_s,
                                ln_g[l, 0], ln_b[l, 0], ns, tm_s)
            outs["cs"].append(token_major(nst))

        outs["fp"].append(prev_p.reshape(bp, seq // tm_p, 2, d_ff)[:, -1])
        sh_s, sc_s, gt_s = (mods(l, k)[1] for k in (3, 4, 5))
        xs, u_s = conv_ffn_planes(xs, sh_s, sc_s, gt_s, w_up, ffn_w_dw[l], ffn_b_dw[l], w_dn, l,
                                  ln_g[l, 1], ln_b[l, 1], tm_s, (state_ffn[l, :, 0], state_ffn[l, :, 1]))
        outs["fs"].append(token_major(u_s.reshape(ts, bs, d_ff)[ts - 2:]))

    st = lambda k: outs[k][0][None] if len(outs[k]) == 1 else jnp.stack(outs[k])
    return (xp.reshape(bp, seq, d), token_major(xs.reshape(ts, bs, d)), st("kp"), st("vp"), st("ks"), st("vs"),
            st("rp"), st("rs"), st("cp"), st("cs"), st("fp"), st("fs"))
```

```python
import functools

import jax
import jax.numpy as jnp
from jax import lax
from jax.experimental import pallas as pl
from jax.experimental.pallas import tpu as pltpu

HEAD_DIM = 128
N_HEADS = 4
GROUP_W = N_HEADS * HEAD_DIM
RET_CHUNK = 128
MOBA_BLOCK = 256
MOBA_TOPK = 3
ROPE_THETA = 10000.0
DEPTH = 2
ALPHA = (2 * DEPTH) ** 0.25
LN_EPS = 1e-5
GN_EPS = 1e-6
NEG = -1e30
LOG2_E = 1.4426950408889634

F32 = jnp.float32
BF16 = jnp.bfloat16

_NT = (((1,), (1,)), ((), ()))
_TN = (((0,), (0,)), ((), ()))

VMEM_LIMIT = 56 * 1024 * 1024
SUBLANES = 8


def _params(*sem):
    return pltpu.CompilerParams(dimension_semantics=sem, vmem_limit_bytes=VMEM_LIMIT)


def _dot(a, b):
    return jnp.dot(a, b, preferred_element_type=F32)


def _layer_norm(v, g, b):
    mu = jnp.mean(v, axis=-1, keepdims=True)
    d = v - mu
    var = jnp.mean(d * d, axis=-1, keepdims=True)
    return d * lax.rsqrt(var + LN_EPS) * g + b


def _silu(x):
    return x * jax.nn.sigmoid(x)


def _mod_spec(mod, tm, rows_per_group):
    _, r, d = mod.shape
    if r == 1:
        return pl.BlockSpec((None, 1, d), lambda i: (i * tm // rows_per_group, 0, 0))
    assert r % tm == 0
    return pl.BlockSpec((None, tm, d), lambda i: (0, i % (r // tm), 0))


def _resident(a, layer=None):
    if layer is None:
        return pl.BlockSpec(a.shape, lambda *_: (0,) * a.ndim, pipeline_mode=pl.Buffered(1))
    return pl.BlockSpec((None,) + a.shape[1:], lambda *_: (layer,) + (0,) * (a.ndim - 1),
                        pipeline_mode=pl.Buffered(1))


def _adaln_kernel(c_ref, w_ref, b_ref, o_ref):
    a = _silu(c_ref[...]).astype(BF16)
    d = c_ref.shape[1]
    for v in range(o_ref.shape[0]):
        cols = slice(v * d, (v + 1) * d)
        o_ref[v] = _dot(a, w_ref[:, cols].astype(BF16)) + b_ref[:, cols]


def adaln_all(c_all, ada_w, ada_b):
    n_layers, d, n_out = ada_w.shape
    m = c_all.shape[0]
    per = 2 if (n_out // d) % 2 == 0 else 1
    return pl.pallas_call(
        _adaln_kernel,
        grid=(n_layers, n_out // (per * d)),
        in_specs=[
            pl.BlockSpec((m, d), lambda l, j: (0, 0)),
            pl.BlockSpec((None, d, per * d), lambda l, j: (l, 0, j)),
            pl.BlockSpec((None, 1, per * d), lambda l, j: (l, 0, j)),
        ],
        out_specs=pl.BlockSpec((None, per, m, d), lambda l, j: (l, j, 0, 0)),
        out_shape=jax.ShapeDtypeStruct((n_layers, n_out // d, m, d), F32),
        compiler_params=_params("parallel", "parallel"),
        name="adaln",
    )(c_all, ada_w, ada_b.reshape(n_layers, 1, n_out))


def _in_proj_kernel(x_ref, sh_ref, sc_ref, w_ref, cr_ref, sr_ref, cm_ref, sm_ref,
                    rq_ref, rk_ref, rv_ref, rg_ref, mq_ref, mk_ref, mv_ref, kb_ref, vb_ref,
                    *km_refs):
    tm = x_ref.shape[0]
    sub = min(tm, MOBA_BLOCK)
    outs = (rq_ref, rk_ref, rv_ref, rg_ref, mq_ref, mk_ref, mv_ref)
    for r0 in range(0, tm, sub):
        rows = slice(r0, r0 + sub)
        sc = sc_ref[...] if sc_ref.shape[0] == 1 else sc_ref[rows, :]
        sh = sh_ref[...] if sh_ref.shape[0] == 1 else sh_ref[rows, :]
        h = (x_ref[rows, :] * (1.0 + sc) + sh).astype(BF16)
        cr, sr, cm, sm = cr_ref[rows, :], sr_ref[rows, :], cm_ref[rows, :], sm_ref[rows, :]
        even = (lax.broadcasted_iota(jnp.int32, cr.shape, 1) & 1) == 0
        for g, o_ref in enumerate(outs):
            z = _dot(h, w_ref[:, g * GROUP_W:(g + 1) * GROUP_W])
            for hh in range(N_HEADS):
                cols = slice(hh * HEAD_DIM, (hh + 1) * HEAD_DIM)
                zs = z[:, cols]
                if g in (0, 1):
                    nxt = pltpu.roll(zs, HEAD_DIM - 1, axis=1)
                    prv = pltpu.roll(zs, 1, axis=1)
                    zs = zs * cr + jnp.where(even, nxt, prv) * sr
                    if g == 1:
                        zs = zs * (HEAD_DIM ** -0.5)
                elif g in (4, 5):
                    zs = zs * cm + pltpu.roll(zs, HEAD_DIM // 2, axis=1) * sm
                if g in (5, 6):
                    o_ref[pl.ds(r0 * N_HEADS + hh, sub, stride=N_HEADS), :] = zs
                    head_major = kb_ref if g == 5 else vb_ref
                    head_major[rows, cols] = zs.astype(head_major.dtype)
                    if g == 5 and km_refs:
                        km_refs[0][r0 // MOBA_BLOCK:r0 // MOBA_BLOCK + 1, cols] = (
                            jnp.sum(zs, axis=0, keepdims=True) * (1.0 / MOBA_BLOCK))
                else:
                    o_ref[rows, cols] = zs.astype(o_ref.dtype)


def in_proj(x2d, shift, scale, w_bf, tabs, rows_per_group, tm, block_means, act_dtype):
    n, d = x2d.shape
    p_tiles = tabs[0].shape[0] // tm
    mod_spec = _mod_spec(shift, tm, rows_per_group)
    tab_spec = pl.BlockSpec((tm, HEAD_DIM), lambda i: (i % p_tiles, 0))
    wide = pl.BlockSpec((tm, GROUP_W), lambda i: (i, 0))
    tall = pl.BlockSpec((tm * N_HEADS, HEAD_DIM), lambda i: (i, 0))
    wide_shape = lambda dt: jax.ShapeDtypeStruct((n, GROUP_W), dt)
    tall_shape = jax.ShapeDtypeStruct((n * N_HEADS, HEAD_DIM), F32)
    out_specs = [wide] * 5 + [tall, tall, wide, wide]
    out_shape = [wide_shape(act_dtype)] * 5 + [tall_shape, tall_shape, wide_shape(act_dtype), wide_shape(act_dtype)]
    if block_means:
        assert tm % MOBA_BLOCK == 0
        per = tm // MOBA_BLOCK
        out_specs.append(pl.BlockSpec((None, per, GROUP_W), lambda i: (i, 0, 0)))
        out_shape.append(jax.ShapeDtypeStruct((n // tm, per, GROUP_W), F32))
    return pl.pallas_call(
        _in_proj_kernel,
        grid=(n // tm,),
        in_specs=[pl.BlockSpec((tm, d), lambda i: (i, 0)), mod_spec, mod_spec,
                  _resident(w_bf), tab_spec, tab_spec, tab_spec, tab_spec],
        out_specs=out_specs,
        out_shape=out_shape,
        compiler_params=_params("parallel"),
        name="in_proj",
    )(x2d, shift, scale, w_bf, *tabs)


def rotary_tables(pos):
    half = HEAD_DIM // 2
    posf = pos.astype(F32)[:, None]
    inv_r = 1.0 / (ROPE_THETA ** jnp.linspace(0.0, 1.0, half, dtype=F32))
    ang_r = posf * inv_r[None, :]
    cr = jnp.repeat(jnp.cos(ang_r), 2, axis=1)
    sr = jnp.stack([-jnp.sin(ang_r), jnp.sin(ang_r)], -1).reshape(-1, HEAD_DIM)
    inv_m = ROPE_THETA ** (-jnp.arange(0, HEAD_DIM, 2, dtype=F32) / HEAD_DIM)
    ang_m = posf * inv_m[None, :]
    cm = jnp.concatenate([jnp.cos(ang_m), jnp.cos(ang_m)], -1)
    sm = jnp.concatenate([-jnp.sin(ang_m), jnp.sin(ang_m)], -1)
    return cr, sr, cm, sm


def retention_tables(chunk, rows):
    log_g = jnp.log1p(-jnp.exp2(-5.0 - jnp.arange(N_HEADS, dtype=F32)))
    idx = jnp.arange(chunk, dtype=F32)
    diff = idx[:, None] - idx[None, :]
    decay_in = jnp.where(diff[None] >= 0,
                         jnp.exp(jnp.maximum(diff, 0.0)[None] * log_g[:, None, None]), 0.0)
    q_dec = jnp.exp((idx + 1.0)[None, :] * log_g[:, None])
    k_dec = jnp.exp((chunk - 1.0 - idx)[None, :] * log_g[:, None])
    c_dec = jnp.exp(chunk * log_g)
    pad = rows - chunk
    decay_in = jnp.pad(decay_in, ((0, 0), (0, pad), (0, rows - chunk)))
    bc = lambda t: jnp.broadcast_to(jnp.pad(t, ((0, 0), (0, pad)))[:, :, None], (N_HEADS, rows, HEAD_DIM))
    c_b = jnp.broadcast_to(c_dec[:, None, None], (N_HEADS, 1, HEAD_DIM))
    return decay_in, bc(q_dec), bc(k_dec), c_b


def _ret_chunk(qc, kc, vc, s, din, qd, kd, cd):
    att = lax.dot_general(qc.astype(BF16), kc.astype(BF16), _NT,
                          preferred_element_type=F32) * din
    o = _dot(att.astype(BF16), vc) + _dot((qc.astype(F32) * qd).astype(BF16), s.astype(BF16))
    s_new = cd * s + lax.dot_general((kc.astype(F32) * kd).astype(BF16), vc, _TN,
                                     preferred_element_type=F32)
    return o, s_new


def _group_norm_gate(o, g):
    o = o * lax.rsqrt(jnp.mean(o * o, axis=-1, keepdims=True) + GN_EPS)
    return o * _silu(g)


def _ret_prompt_phases(q_ref, k_ref, v_ref, g_ref, din_ref, qd_ref, kd_ref, cd_ref, o_ref, sout_ref):
    chunk = din_ref.shape[0]
    din, qd, kd, cd = din_ref[...], qd_ref[...], kd_ref[...], cd_ref[...]
    s = jnp.zeros((HEAD_DIM, HEAD_DIM), F32)
    for c in range(q_ref.shape[0] // chunk):
        rows = slice(c * chunk, (c + 1) * chunk)
        o, s = _ret_chunk(q_ref[rows, :], k_ref[rows, :], v_ref[rows, :], s, din, qd, kd, cd)
        o_ref[rows, :] = _group_norm_gate(o, g_ref[rows, :].astype(F32)).astype(o_ref.dtype)
        yield
    sout_ref[...] = s
    yield


def _group_rows(ref, group, batch, t, cols):
    return [ref[pl.ds(pl.multiple_of(ti * batch + group * SUBLANES, SUBLANES), SUBLANES), cols]
            for ti in range(t)]


def _seq_tile(token_rows, row, n_rows):
    width = token_rows[0].shape[1]
    r_i = lax.broadcasted_iota(jnp.int32, (n_rows, width), 0)
    g_i = lax.broadcasted_iota(jnp.int32, token_rows[0].shape, 0)
    out = jnp.zeros((n_rows, width), F32)
    for ti, rows in enumerate(token_rows):
        if isinstance(row, int):
            picked = rows[row:row + 1, :]
        else:
            picked = jnp.sum(jnp.where(g_i == row, rows, 0.0), axis=0, keepdims=True)
        out = jnp.where(r_i == ti, picked, out)
    return out


def _ret_sample_kernel(q_ref, k_ref, v_ref, g_ref, s0_ref, din_ref, qd_ref, kd_ref, cd_ref,
                       o_ref, sout_ref, *, t, batch):
    group = pl.program_id(0)
    r_i = lax.broadcasted_iota(jnp.int32, (SUBLANES, HEAD_DIM), 0)
    for hh in range(N_HEADS):
        cols = slice(hh * HEAD_DIM, (hh + 1) * HEAD_DIM)
        din, qd, kd, cd = din_ref[hh], qd_ref[hh], kd_ref[hh], cd_ref[hh]
        q_t, k_t, v_t, g_t = (_group_rows(r, group, batch, t, cols) for r in (q_ref, k_ref, v_ref, g_ref))
        out_t = [jnp.zeros((SUBLANES, HEAD_DIM), F32) for _ in range(t)]
        for bl in range(SUBLANES):
            qc = _seq_tile(q_t, bl, RET_CHUNK)
            kc = _seq_tile(k_t, bl, RET_CHUNK)
            vc = _seq_tile(v_t, bl, RET_CHUNK).astype(BF16)
            o, s_new = _ret_chunk(qc, kc, vc, s0_ref[bl, hh], din, qd, kd, cd)
            sout_ref[bl, hh] = s_new
            gated = _group_norm_gate(o[0:SUBLANES, :], _seq_tile(g_t, bl, SUBLANES))
            for ti in range(t):
                out_t[ti] = jnp.where(r_i == bl, gated[ti:ti + 1, :], out_t[ti])
        for ti in range(t):
            o_ref[pl.ds(pl.multiple_of(ti * batch + group * SUBLANES, SUBLANES), SUBLANES), cols] = out_t[ti]


def retention_sample(rq, rk, rv, rg, states, layer, t):
    batch = states.shape[1]
    bb = SUBLANES
    assert batch % bb == 0
    tabs = retention_tables(t, RET_CHUNK)
    tok = pl.BlockSpec(rq.shape, lambda i: (0, 0))
    st_in = pl.BlockSpec((None, bb, N_HEADS, HEAD_DIM, HEAD_DIM), lambda i: (layer, i, 0, 0, 0))
    st = pl.BlockSpec((bb, N_HEADS, HEAD_DIM, HEAD_DIM), lambda i: (i, 0, 0, 0))
    tab = pl.BlockSpec((N_HEADS, RET_CHUNK, HEAD_DIM), lambda i: (0, 0, 0))
    return pl.pallas_call(
        functools.partial(_ret_sample_kernel, t=t, batch=batch),
        grid=(batch // bb,),
        in_specs=[tok, tok, tok, tok, st_in, tab, tab, tab,
                  pl.BlockSpec((N_HEADS, 1, HEAD_DIM), lambda i: (0, 0, 0))],
        out_specs=[tok, st],
        out_shape=[jax.ShapeDtypeStruct(rq.shape, F32),
                   jax.ShapeDtypeStruct(states.shape[1:], F32)],
        compiler_params=_params("arbitrary"),
        name="retention_sample",
    )(rq, rk, rv, rg, states, *tabs)


def _beaten_counts(rows, n_valid):
    counts = []
    for n in range(n_valid):
        cnt = jnp.zeros(rows[n].shape, jnp.int32)
        for m in range(n_valid):
            if m == n:
                continue
            beats = (rows[m] >= rows[n]) if m < n else (rows[m] > rows[n])
            cnt = cnt + beats.astype(jnp.int32)
        counts.append(cnt)
    return counts


def _moba_prompt_phases(q_ref, k_ref, v_ref, km_ref, o_ref, s0_ref, s1_ref, *, nblk):
    blk = MOBA_BLOCK
    scale = HEAD_DIM ** -0.5
    kb = k_ref[...]
    vt = v_ref[...].astype(F32).T.astype(BF16)
    kmean = km_ref[...].astype(BF16)
    key_i = lax.broadcasted_iota(jnp.int32, (blk, blk), 0)
    qry_i = lax.broadcasted_iota(jnp.int32, (blk, blk), 1)
    causal = key_i <= qry_i

    def query_block(qi, s_ref):
        qb = q_ref[qi * blk:(qi + 1) * blk, :]
        bias = None
        if qi > MOBA_TOPK:
            gt = lax.dot_general(kmean, qb, _NT, preferred_element_type=F32)
            cnt = _beaten_counts([gt[m:m + 1, :] for m in range(qi)], qi)
            bias = [jnp.where(c < MOBA_TOPK, 0.0, NEG) for c in cnt]
        mx = jnp.full((1, blk), NEG, F32)
        for n in range(qi + 1):
            s = lax.dot_general(kb[n * blk:(n + 1) * blk], qb, _NT,
                                preferred_element_type=F32)
            if n == qi:
                s = jnp.where(causal, s, NEG)
            elif bias is not None:
                s = s + bias[n]
            s_ref[n * blk:(n + 1) * blk, :] = s
            mx = jnp.maximum(mx, jnp.max(s, axis=0, keepdims=True))
            yield
        l = jnp.zeros((1, blk), F32)
        ot = jnp.zeros((HEAD_DIM, blk), F32)
        for n in range(qi + 1):
            p = jnp.exp2((s_ref[n * blk:(n + 1) * blk, :] - mx) * (scale * LOG2_E))
            l = l + jnp.sum(p, axis=0, keepdims=True)
            ot = ot + _dot(vt[:, n * blk:(n + 1) * blk], p.astype(BF16))
            yield
        ot = ot * (1.0 / l)
        o_ref[qi * blk:(qi + 1) * blk, :] = ot.T.astype(o_ref.dtype)
        yield

    blocks = [query_block(qi, (s0_ref, s1_ref)[qi % 2]) for qi in range(nblk)]
    for _ in range(1):
        next(blocks[0])
    for qi in range(nblk):
        value_steps = qi + 2
        score_steps = qi + 2 if qi + 1 < nblk else 0
        for step in range(max(value_steps, score_steps)):
            if step < score_steps:
                next(blocks[qi + 1])
            if step < value_steps:
                next(blocks[qi])
            yield


def _attention_prompt_kernel(mq_ref, mk_ref, mv_ref, km_ref, rq_ref, rk_ref, rv_ref, rg_ref,
                             din_ref, qd_ref, kd_ref, cd_ref, om_ref, or_ref, sout_ref, s0_ref, s1_ref,
                             *, nblk):
    moba = _moba_prompt_phases(mq_ref, mk_ref, mv_ref, km_ref, om_ref, s0_ref, s1_ref, nblk=nblk)
    ret = _ret_prompt_phases(rq_ref, rk_ref, rv_ref, rg_ref, din_ref, qd_ref, kd_ref, cd_ref, or_ref, sout_ref)
    n_moba = sum(qi + 2 for qi in range(nblk))
    n_ret = rq_ref.shape[0] // din_ref.shape[0] + 1
    every = max(n_moba // n_ret, 1)
    for step, _ in enumerate(moba):
        if step % every == 0:
            next(ret, None)
    _run(ret)


def attention_prompt(mq, mk, mv, kmean, rq, rk, rv, rg, batch, seq):
    nblk = seq // MOBA_BLOCK
    assert nblk % SUBLANES == 0
    chunk = min(2 * RET_CHUNK, seq)
    din, qd, kd, cd = retention_tables(chunk, chunk)
    spec = pl.BlockSpec((seq, HEAD_DIM), lambda b, h: (b, h))
    head = lambda a: pl.BlockSpec((None,) + a.shape[1:], lambda b, h: (h, 0, 0))
    return pl.pallas_call(
        functools.partial(_attention_prompt_kernel, nblk=nblk),
        grid=(batch, N_HEADS),
        in_specs=[spec, spec, spec, pl.BlockSpec((nblk, HEAD_DIM), lambda b, h: (b, h)),
                  spec, spec, spec, spec, head(din), head(qd), head(kd), head(cd)],
        out_specs=[spec, spec,
                   pl.BlockSpec((None, None, HEAD_DIM, HEAD_DIM), lambda b, h: (b, h, 0, 0))],
        out_shape=[jax.ShapeDtypeStruct((batch * seq, GROUP_W), BF16),
                   jax.ShapeDtypeStruct((batch * seq, GROUP_W), BF16),
                   jax.ShapeDtypeStruct((batch, N_HEADS, HEAD_DIM, HEAD_DIM), F32)],
        scratch_shapes=[pltpu.VMEM((seq, MOBA_BLOCK), F32), pltpu.VMEM((seq, MOBA_BLOCK), F32)],
        compiler_params=_params("parallel", "parallel"),
        name="attention_prompt",
    )(mq, mk, mv, kmean, rq, rk, rv, rg, din, qd, kd, cd)


def _page_rows(ref, page):
    return jnp.concatenate([ref[pl.ds(hh, page, stride=N_HEADS), :] for hh in range(N_HEADS)], axis=1)


def _moba_sample_phases(b, q_ref, kn_ref, vn_ref, k_refs, v_refs, o_ref, *, page, t):
    n_pages = len(k_refs)
    scale = HEAD_DIM ** -0.5
    ppb = MOBA_BLOCK // page
    nblk = n_pages // ppb
    n_pairs = N_HEADS * t
    batch = q_ref.shape[0] // t
    all_cols = slice(0, GROUP_W)
    group = b // SUBLANES
    row = b % SUBLANES
    q = _seq_tile(_group_rows(q_ref, group, batch, t, all_cols), row, SUBLANES)

    r_i = lax.broadcasted_iota(jnp.int32, (n_pairs, GROUP_W), 0)
    l_i = lax.broadcasted_iota(jnp.int32, (n_pairs, GROUP_W), 1)
    qm = jnp.zeros((n_pairs, GROUP_W), F32)
    for ti in range(t):
        for hh in range(N_HEADS):
            hit = (r_i == hh * t + ti) & (l_i >= hh * HEAD_DIM) & (l_i < (hh + 1) * HEAD_DIM)
            qm = jnp.where(hit, q[ti:ti + 1, :], qm)
    qm = qm.astype(BF16)

    zpad = jnp.zeros((page - SUBLANES, GROUP_W), F32)
    own = lambda ref: jnp.concatenate(
        [_seq_tile(_group_rows(ref, group, batch, t, all_cols), row, SUBLANES), zpad], axis=0).astype(BF16)
    k_own, v_own = own(kn_ref), own(vn_ref)

    b_i = lax.broadcasted_iota(jnp.int32, (page, GROUP_W), 0)
    kmean = jnp.zeros((page, GROUP_W), F32)
    scores = []
    for p in range(n_pages):
        kp = _page_rows(k_refs[p], page)
        scores.append(lax.dot_general(qm, kp.astype(BF16), _NT,
                                      preferred_element_type=F32) * scale)
        ks = jnp.sum(kp, axis=0, keepdims=True) * (1.0 / MOBA_BLOCK)
        kmean = kmean + jnp.where(b_i == p // ppb, ks, 0.0)
        yield
    gate = lax.dot_general(qm, kmean.astype(BF16), _NT, preferred_element_type=F32)
    cnt = _beaten_counts([gate[:, m:m + 1] for m in range(nblk)], nblk)
    keep = [jnp.broadcast_to(c, (n_pairs, page)) < MOBA_TOPK for c in cnt]

    s_own = lax.dot_general(qm, k_own, _NT, preferred_element_type=F32) * scale
    key_i = lax.broadcasted_iota(jnp.int32, (n_pairs, page), 1)
    tok_i = lax.rem(lax.broadcasted_iota(jnp.int32, (n_pairs, page), 0), t)
    s_own = jnp.where((key_i <= tok_i) & (key_i < t), s_own, NEG)

    mx = s_own
    for p in range(n_pages):
        scores[p] = jnp.where(keep[p // ppb], scores[p], NEG)
        mx = jnp.maximum(mx, scores[p])
    mx = jnp.max(mx, axis=1, keepdims=True)
    e_own = jnp.exp(s_own - mx)
    l = e_own
    probs = []
    for p in range(n_pages):
        e = jnp.exp(scores[p] - mx)
        l = l + e
        probs.append(e.astype(BF16))
    inv_l = 1.0 / jnp.sum(l, axis=1, keepdims=True)
    yield
    acc = _dot(e_own.astype(BF16), v_own)
    for p in range(n_pages):
        acc = acc + _dot(probs[p], _page_rows(v_refs[p], page).astype(BF16))
        yield
    acc = acc * inv_l
    o_ref[...] = jnp.zeros_like(o_ref)
    for hh in range(N_HEADS):
        cols = slice(hh * HEAD_DIM, (hh + 1) * HEAD_DIM)
        o_ref[0:t, cols] = acc[hh * t:(hh + 1) * t, cols]
    yield


def _mixer_out(or_ref, om_ref, w_ref, x_ref, gate_ref, g_ref, b_ref, o_ref, sub):
    for r0 in range(0, x_ref.shape[0], sub):
        rows = slice(r0, r0 + sub)
        y = (_dot(or_ref[rows, :].astype(BF16), w_ref[:GROUP_W, :])
             + _dot(om_ref[rows, :].astype(BF16), w_ref[GROUP_W:, :]))
        gate = gate_ref[...] if gate_ref.shape[0] == 1 else gate_ref[rows, :]
        v = ALPHA * x_ref[rows, :] + (1.0 + gate) * y
        o_ref[rows, :] = _layer_norm(v, g_ref[...], b_ref[...])


def _out_proj_kernel(or_ref, om_ref, w_ref, x_ref, gate_ref, g_ref, b_ref, o_ref, *, sub):
    _mixer_out(or_ref, om_ref, w_ref, x_ref, gate_ref, g_ref, b_ref, o_ref, sub)


def out_proj(o_r, o_m, w_bf, x2d, gate, ln_g, ln_b, rows_per_group, tm):
    n, d = x2d.shape
    vec = pl.BlockSpec((1, d), lambda i: (0, 0))
    return pl.pallas_call(
        functools.partial(_out_proj_kernel, sub=min(128, tm)),
        grid=(n // tm,),
        in_specs=[pl.BlockSpec((tm, GROUP_W), lambda i: (i, 0)),
                  pl.BlockSpec((tm, GROUP_W), lambda i: (i, 0)),
                  _resident(w_bf),
                  pl.BlockSpec((tm, d), lambda i: (i, 0)),
                  _mod_spec(gate, tm, rows_per_group), vec, vec],
        out_specs=pl.BlockSpec((tm, d), lambda i: (i, 0)),
        out_shape=jax.ShapeDtypeStruct((n, d), F32),
        compiler_params=_params("parallel"),
        name="out_proj",
    )(o_r, o_m, w_bf, x2d, gate, ln_g.reshape(1, d), ln_b.reshape(1, d))


def _run(phases):
    for _ in phases:
        pass


def _ffn_phases(x, sh_ref, sc_ref, gate_ref, wup_ref, wdw_ref, bdw_ref, wdn_ref, g_ref, b_ref,
                o_ref, prev_ref, h_s, act_s, hist_s, *, d_ff, cw, planes, chunks=None):
    tm = x.shape[0]
    n_chunks = d_ff // cw
    chunks = range(n_chunks) if chunks is None else chunks
    if chunks.start == 0:
        h_s[...] = (x * (1.0 + sc_ref[...]) + sh_ref[...]).astype(BF16)
    row = lax.broadcasted_iota(jnp.int32, (tm, cw), 0)
    for c in chunks:
        cu = slice(c * cw, (c + 1) * cw)
        cv = slice(d_ff + c * cw, d_ff + (c + 1) * cw)
        u = _dot(h_s[...], wup_ref[:, cu])
        v = _dot(h_s[...], wup_ref[:, cv])
        if planes:
            s2 = hist_s[0, :, cu]
            s1 = hist_s[1, :, cu]
            hist_s[0, :, cu] = s1
            hist_s[1, :, cu] = u
            prev_ref[:, cu] = u
        else:
            e1 = hist_s[1:2, cu]
            e2 = jnp.where(row == 0, hist_s[0:1, cu], hist_s[1:2, cu])
            hist_s[:, cu] = u[tm - 2:tm, :]
            prev_ref[:, cu] = u[tm - 2:tm, :]
            s1 = jnp.where(row >= 1, pltpu.roll(u, 1, axis=0), e1)
            s2 = jnp.where(row >= 2, pltpu.roll(u, 2, axis=0), e2)
        uc = wdw_ref[0:1, cu] * s2 + wdw_ref[1:2, cu] * s1 + wdw_ref[2:3, cu] * u + bdw_ref[:, cu]
        act = 0.5 * uc * (1.0 + lax.erf(uc * (2.0 ** -0.5))) * v
        act_s[:, cu] = act.astype(BF16)
        yield act[0:1, 0:HEAD_DIM]
    if chunks.stop != n_chunks:
        return
    fy = _dot(act_s[...], wdn_ref[...])
    vsum = ALPHA * x + (1.0 + gate_ref[...]) * fy
    o_ref[...] = _layer_norm(vsum, g_ref[...], b_ref[...])
    yield


def _ffn_planes_kernel(x_ref, sh_ref, sc_ref, gate_ref, wup_ref, wdw_ref, bdw_ref, wdn_ref, g_ref, b_ref,
                       p0_ref, p1_ref, o_ref, prev_ref, h_s, act_s, hist_s, *, d_ff, cw):
    @pl.when(pl.program_id(0) == 0)
    def _():
        hist_s[0] = p0_ref[...]
        hist_s[1] = p1_ref[...]

    _run(_ffn_phases(x_ref[...], sh_ref, sc_ref, gate_ref, wup_ref, wdw_ref, bdw_ref, wdn_ref,
                     g_ref, b_ref, o_ref, prev_ref, h_s, act_s, hist_s, d_ff=d_ff, cw=cw, planes=True))


def conv_ffn_planes(x2d, shift, scale, gate, w_up_bf, w_dw, b_dw, w_dn_bf, layer, ln_g, ln_b,
                    tm, history):
    n, d = x2d.shape
    d_ff = w_dn_bf.shape[1]
    mod = _mod_spec(shift, tm, n)
    vec = pl.BlockSpec((1, d), lambda i: (0, 0))
    b_dw2 = b_dw.reshape(1, d_ff)
    return pl.pallas_call(
        functools.partial(_ffn_planes_kernel, d_ff=d_ff, cw=256),
        grid=(n // tm,),
        in_specs=[pl.BlockSpec((tm, d), lambda i: (i, 0)), mod, mod, mod,
                  _resident(w_up_bf, layer), _resident(w_dw), _resident(b_dw2),
                  _resident(w_dn_bf, layer), vec, vec, _resident(history[0]), _resident(history[1])],
        out_specs=[pl.BlockSpec((tm, d), lambda i: (i, 0)), pl.BlockSpec((tm, d_ff), lambda i: (i, 0))],
        out_shape=[jax.ShapeDtypeStruct((n, d), F32), jax.ShapeDtypeStruct((n, d_ff), F32)],
        scratch_shapes=[pltpu.VMEM((tm, d), BF16), pltpu.VMEM((tm, d_ff), BF16),
                        pltpu.VMEM((2, tm, d_ff), F32)],
        compiler_params=_params("arbitrary"),
        name="conv_ffn_planes",
    )(x2d, shift, scale, gate, w_up_bf, w_dw, b_dw2, w_dn_bf, ln_g.reshape(1, d), ln_b.reshape(1, d),
      *history)


def _moba_ffn_kernel(pt_ref, q_ref, kn_ref, vn_ref, *rest, n_pages, page, t, d_ff, cw, parts, tiles_per_seq):
    k_refs = rest[:n_pages]
    v_refs = rest[n_pages:2 * n_pages]
    (or_ref, omp_ref, wout_ref, x_ref, gate0_ref, g0_ref, b0_ref,
     sh_ref, sc_ref, gate_ref, wup_ref, wdw_ref, bdw_ref, wdn_ref, g_ref, b_ref,
     om_ref, o_ref, prev_ref, x1_s, h_s, act_s, hist_s) = rest[2 * n_pages:]
    i = pl.program_id(0)
    n_chunks = d_ff // cw
    per_part = -(-n_chunks // parts)

    @pl.when(i % (parts * tiles_per_seq) == 0)
    def _():
        hist_s[...] = jnp.zeros_like(hist_s)

    def step(part):
        chunks = range(part * per_part, min((part + 1) * per_part, n_chunks))
        if part == 0:
            _mixer_out(or_ref, omp_ref, wout_ref, x_ref, gate0_ref, g0_ref, b0_ref, x1_s, sub=128)
        ffn_ph = _ffn_phases(x1_s[...], sh_ref, sc_ref, gate_ref, wup_ref, wdw_ref, bdw_ref, wdn_ref,
                             g_ref, b_ref, o_ref, prev_ref, h_s, act_s, hist_s,
                             d_ff=d_ff, cw=cw, planes=False, chunks=chunks)
        moba_ph = _moba_sample_phases(i, q_ref, kn_ref, vn_ref, k_refs, v_refs, om_ref, page=page, t=t)
        for _ in range(n_pages):
            next(moba_ph)
        next(ffn_ph)
        next(moba_ph)
        for _ in range(len(chunks) - 1):
            next(ffn_ph)
        for _ in range(n_pages + 1):
            next(moba_ph)
        _run(ffn_ph)

    for part in range(parts):
        pl.when(i % parts == part)(functools.partial(step, part))


def moba_sample_ffn_prompt(mq, mk, mv, cache_k, cache_v, page_table, page_base, page, t,
                           o_r, o_m, w_out_bf, x2d, gate0, ln_g0, ln_b0,
                           shift, scale, gate, w_up_bf, w_dw, b_dw, w_dn_bf, layer, ln_g, ln_b,
                           seq, tm):
    batch, n_pages = page_table.shape
    n, d = x2d.shape
    d_ff = w_dn_bf.shape[1]
    assert (n_pages * page) % MOBA_BLOCK == 0 and t <= SUBLANES
    assert n_pages * page // MOBA_BLOCK <= page
    n_tiles = n // tm
    assert batch % n_tiles == 0
    parts = batch // n_tiles
    tok = pl.BlockSpec(mq.shape, lambda i, pt: (0, 0))

    def page_spec(p):
        return pl.BlockSpec((page * N_HEADS, HEAD_DIM), lambda i, pt: (page_base + pt[i, p], 0))

    const = lambda a, lead=None: pl.BlockSpec(
        a.shape if lead is None else (None,) + a.shape[1:],
        (lambda i, pt: (0,) * a.ndim) if lead is None else (lambda i, pt: (lead,) + (0,) * (a.ndim - 1)),
        pipeline_mode=pl.Buffered(1))
    mod = pl.BlockSpec((None, 1, d), lambda i, pt: ((i // parts) * tm // seq, 0, 0))
    b_dw2, g2, b2 = b_dw.reshape(1, d_ff), ln_g.reshape(1, d), ln_b.reshape(1, d)
    g0, b0 = ln_g0.reshape(1, d), ln_b0.reshape(1, d)
    tile = lambda w: pl.BlockSpec((tm, w), lambda i, pt: (i // parts, 0))
    specs = ([tok, tok, tok] + [page_spec(p) for p in range(n_pages)] * 2
             + [tile(GROUP_W), tile(GROUP_W), const(w_out_bf), tile(d), mod, const(g0), const(b0), mod, mod, mod,
                const(w_up_bf, layer), const(w_dw), const(b_dw2), const(w_dn_bf, layer), const(g2), const(b2)])
    grid_spec = pltpu.PrefetchScalarGridSpec(
        num_scalar_prefetch=1,
        grid=(batch,),
        in_specs=specs,
        out_specs=[pl.BlockSpec((None, SUBLANES, GROUP_W), lambda i, pt: (i, 0, 0)),
                   pl.BlockSpec((tm, d), lambda i, pt: (i // parts, 0)),
                   pl.BlockSpec((None, 2, d_ff), lambda i, pt: (i // parts, 0, 0))],
        scratch_shapes=[pltpu.VMEM((tm, d), F32), pltpu.VMEM((tm, d), BF16), pltpu.VMEM((tm, d_ff), BF16),
                        pltpu.VMEM((2, d_ff), F32)],
    )
    return pl.pallas_call(
        functools.partial(_moba_ffn_kernel, n_pages=n_pages, page=page, t=t, d_ff=d_ff, cw=256,
                          parts=parts, tiles_per_seq=seq // tm),
        grid_spec=grid_spec,
        out_shape=[jax.ShapeDtypeStruct((batch, SUBLANES, GROUP_W), F32),
                   jax.ShapeDtypeStruct((n, d), F32),
                   jax.ShapeDtypeStruct((n_tiles, 2, d_ff), F32)],
        compiler_params=_params("arbitrary"),
        name="moba_sample_ffn_prompt",
    )(page_table, mq, mk, mv, *([cache_k] * n_pages), *([cache_v] * n_pages),
      o_r, o_m, w_out_bf, x2d, gate0, g0, b0, shift, scale, gate, w_up_bf, w_dw, b_dw2, w_dn_bf, g2, b2)


def _glu_kernel(x_ref, sh_ref, sc_ref, w_ref, b_ref, o_ref, *, cw):
    h = (x_ref[...] * (1.0 + sc_ref[...]) + sh_ref[...]).astype(BF16)
    dc = o_ref.shape[1]
    for c in range(dc // cw):
        ca = slice(c * cw, (c + 1) * cw)
        cg = slice(dc + c * cw, dc + (c + 1) * cw)
        a = _dot(h, w_ref[:, ca]) + b_ref[:, ca]
        g = _dot(h, w_ref[:, cg]) + b_ref[:, cg]
        o_ref[:, ca] = a * jax.nn.sigmoid(g)


def conformer_glu(x2d, shift, scale, w1_bf, b1, rows_per_group, tm):
    n, d = x2d.shape
    dc = w1_bf.shape[1] // 2
    mod = _mod_spec(shift, tm, rows_per_group)
    b1r = b1.reshape(1, 2 * dc)
    return pl.pallas_call(
        functools.partial(_glu_kernel, cw=256),
        grid=(n // tm,),
        in_specs=[pl.BlockSpec((tm, d), lambda i: (i, 0)), mod, mod,
                  _resident(w1_bf), _resident(b1r)],
        out_specs=pl.BlockSpec((tm, dc), lambda i: (i, 0)),
        out_shape=jax.ShapeDtypeStruct((n, dc), F32),
        compiler_params=_params("parallel"),
        name="conformer_glu",
    )(x2d, shift, scale, w1_bf, b1r)


PAD_ROWS = 32


def _dwconv_rows(win_ref, w_ref, bias, r0, rs, cols, width):
    off = PAD_ROWS - (width - 1)
    y = bias
    for r in range(SUBLANES):
        z = None
        for m in range((off + width - 1) // SUBLANES + 1):
            k = SUBLANES * m + r - off
            if 0 <= k < width:
                lo = r0 + SUBLANES * m
                term = w_ref[k:k + 1, cols] * win_ref[lo:lo + rs + SUBLANES, cols]
                z = term if z is None else z + term
        if z is not None:
            y = y + z[r:r + rs, :]
    return y


def _ordered_after(x, token):
    zero = lax.shift_right_logical(lax.shift_right_logical(
        lax.bitcast_convert_type(token, jnp.uint32), jnp.uint32(16)), jnp.uint32(16))
    return lax.bitcast_convert_type(lax.bitcast_convert_type(x, jnp.uint32) | zero, F32)


def _conformer_phases(x, sh_ref, sc_ref, gate_ref, w1_ref, b1_ref, wdw_ref, bdw_ref,
                      gl_ref, bl_ref, w2_ref, b2_ref, g_ref, b_ref, o_ref, tail_ref, win_ref, y_ref,
                      *, width, rs, cw, after=None):
    tm, dc = y_ref.shape
    h = (x * (1.0 + sc_ref[...]) + sh_ref[...]).astype(BF16)
    for c in range(dc // cw):
        ca = slice(c * cw, (c + 1) * cw)
        cg = slice(dc + c * cw, dc + (c + 1) * cw)
        a = _dot(h, w1_ref[:, ca]) + b1_ref[:, ca]
        g = _dot(h, w1_ref[:, cg]) + b1_ref[:, cg]
        win_ref[PAD_ROWS:PAD_ROWS + tm, ca] = a * jax.nn.sigmoid(g)
    tail_ref[...] = win_ref[tm:tm + PAD_ROWS, :]
    yield
    for c in range(dc // HEAD_DIM):
        cols = slice(c * HEAD_DIM, (c + 1) * HEAD_DIM)
        for r0 in range(0, tm, rs):
            bias = bdw_ref[:, cols]
            if after is not None and after[0] is not None:
                bias = _ordered_after(bias, after[0])
            y_ref[r0:r0 + rs, cols] = _dwconv_rows(win_ref, wdw_ref, bias, r0, rs, cols, width)
            yield
    win_ref[0:PAD_ROWS, :] = win_ref[tm:tm + PAD_ROWS, :]
    yn = _silu(_layer_norm(y_ref[...], gl_ref[...], bl_ref[...]))
    y = _dot(yn.astype(BF16), w2_ref[...]) + b2_ref[...]
    v = ALPHA * x + (1.0 + gate_ref[...]) * y
    o_ref[...] = _layer_norm(v, g_ref[...], b_ref[...])
    yield


N_CONF_CONSTS = 10
N_FFN_CONSTS = 6


def _conformer_ffn_kernel(*refs, tiles_per_seq, width, rs, cw, d_ff):
    x_ref, shc_ref, scc_ref, gtc_ref, shf_ref, scf_ref, gtf_ref = refs[:7]
    conf = refs[7:7 + N_CONF_CONSTS]
    ffn = refs[7 + N_CONF_CONSTS:7 + N_CONF_CONSTS + N_FFN_CONSTS]
    o_ref, tail_ref, prev_ref, win_ref, y_ref, x1_s, h_s, act_s, hist_s = refs[7 + N_CONF_CONSTS + N_FFN_CONSTS:]
    i = pl.program_id(0)
    n_tiles = pl.num_programs(0) - 1
    tm, dc = y_ref.shape

    @pl.when(i == 0)
    def _():
        x1_s[...] = jnp.zeros_like(x1_s)
        win_ref[PAD_ROWS + tm:, :] = jnp.zeros((SUBLANES, dc), F32)

    @pl.when(jnp.minimum(i, n_tiles - 1) % tiles_per_seq == 0)
    def _():
        win_ref[0:PAD_ROWS, :] = jnp.zeros((PAD_ROWS, dc), F32)

    @pl.when(jnp.maximum(i - 1, 0) % tiles_per_seq == 0)
    def _():
        hist_s[...] = jnp.zeros_like(hist_s)

    ffn_ph = _ffn_phases(x1_s[...], shf_ref, scf_ref, gtf_ref, *ffn, o_ref, prev_ref, h_s, act_s, hist_s,
                         d_ff=d_ff, cw=cw, planes=False)
    token = [None]
    conf_ph = _conformer_phases(x_ref[...], shc_ref, scc_ref, gtc_ref, *conf, x1_s, tail_ref, win_ref, y_ref,
                                width=width, rs=rs, cw=cw, after=token)
    next(conf_ph)
    n_ffn = d_ff // cw
    n_conv = (dc // HEAD_DIM) * (tm // rs)
    per_chunk = -(-n_conv // n_ffn)
    for _ in range(n_ffn):
        token[0] = next(ffn_ph)
        for _ in range(per_chunk):
            next(conf_ph, None)
    _run(ffn_ph)
    _run(conf_ph)


def conformer_ffn_prompt(x2d, mods_conf, mods_ffn, w1_bf, b1, w_dw, b_dw, cf_g, cf_b, w2_bf, b2,
                         ln_g0, ln_b0, w_up_bf, ffn_w_dw, ffn_b_dw, w_dn_bf, layer, ln_g1, ln_b1,
                         batch, seq, tm):
    n, d = x2d.shape
    dc = w2_bf.shape[0]
    d_ff = w_dn_bf.shape[1]
    width = w_dw.shape[0]
    assert width - 1 <= PAD_ROWS and tm % PAD_ROWS == 0
    nj = seq // tm
    n_tiles = n // tm
    row = lambda a: a.reshape(1, -1)
    conf_tile = lambda i: jnp.minimum(i, n_tiles - 1)
    ffn_tile = lambda i: jnp.maximum(i - 1, 0)
    mod_c = pl.BlockSpec((None, 1, d), lambda i: (conf_tile(i) // nj, 0, 0))
    mod_f = pl.BlockSpec((None, 1, d), lambda i: (ffn_tile(i) // nj, 0, 0))
    conf = [w1_bf, row(b1), w_dw, row(b_dw), row(cf_g), row(cf_b), w2_bf, row(b2), row(ln_g0), row(ln_b0)]
    ffn = [ffn_w_dw, row(ffn_b_dw), row(ln_g1), row(ln_b1)]
    assert len(conf) == N_CONF_CONSTS and len(ffn) + 2 == N_FFN_CONSTS
    ffn_specs = [_resident(w_up_bf, layer), _resident(ffn[0]), _resident(ffn[1]),
                 _resident(w_dn_bf, layer), _resident(ffn[2]), _resident(ffn[3])]
    ffn_args = [w_up_bf, ffn[0], ffn[1], w_dn_bf, ffn[2], ffn[3]]
    return pl.pallas_call(
        functools.partial(_conformer_ffn_kernel, tiles_per_seq=nj, width=width, rs=64, cw=256, d_ff=d_ff),
        grid=(n_tiles + 1,),
        in_specs=[pl.BlockSpec((tm, d), lambda i: (conf_tile(i), 0)), mod_c, mod_c, mod_c,
                  mod_f, mod_f, mod_f] + [_resident(a) for a in conf] + ffn_specs,
        out_specs=[pl.BlockSpec((tm, d), lambda i: (ffn_tile(i), 0)),
                   pl.BlockSpec((None, PAD_ROWS, dc), lambda i: (conf_tile(i) // nj, 0, 0)),
                   pl.BlockSpec((None, 2, d_ff), lambda i: (ffn_tile(i), 0, 0))],
        out_shape=[jax.ShapeDtypeStruct((n, d), F32),
                   jax.ShapeDtypeStruct((batch, PAD_ROWS, dc), F32),
                   jax.ShapeDtypeStruct((n_tiles, 2, d_ff), F32)],
        scratch_shapes=[pltpu.VMEM((PAD_ROWS + tm + SUBLANES, dc), F32), pltpu.VMEM((tm, dc), F32),
                        pltpu.VMEM((tm, d), F32), pltpu.VMEM((tm, d), BF16),
                        pltpu.VMEM((tm, d_ff), BF16), pltpu.VMEM((2, d_ff), F32)],
        compiler_params=_params("arbitrary"),
        name="conformer_ffn_prompt",
    )(x2d, *mods_conf, *mods_ffn, *conf, *ffn_args)


def _dwconv_sample_kernel(x_ref, st_ref, w_ref, b_ref, o_ref, nst_ref, *, width, t):
    hist = width - 1
    bb, dc = x_ref.shape[1], x_ref.shape[2]
    plane = lambda j, cols: st_ref[j, :, cols] if j < hist else x_ref[j - hist, :, cols]
    for c in range(dc // HEAD_DIM):
        cols = slice(c * HEAD_DIM, (c + 1) * HEAD_DIM)
        acc = [jnp.zeros((bb, HEAD_DIM), F32) + b_ref[:, cols] for _ in range(t)]
        for j in range(hist + t):
            p = plane(j, cols)
            for ti in range(t):
                if 0 <= j - ti < width:
                    acc[ti] = acc[ti] + w_ref[j - ti:j - ti + 1, cols] * p
            if j >= t:
                nst_ref[j - t, :, cols] = p
        for ti in range(t):
            o_ref[ti, :, cols] = acc[ti]


def dwconv_sample(glu, states, layer, w_dw, b_dw):
    t, batch, dc = glu.shape
    width = w_dw.shape[0]
    hist = width - 1
    bb = 32
    b2 = b_dw.reshape(1, dc)
    return pl.pallas_call(
        functools.partial(_dwconv_sample_kernel, width=width, t=t),
        grid=(batch // bb,),
        in_specs=[pl.BlockSpec((t, bb, dc), lambda i: (0, i, 0)),
                  pl.BlockSpec((None, hist, bb, dc), lambda i: (layer, 0, i, 0)),
                  pl.BlockSpec(w_dw.shape, lambda i: (0, 0)),
                  pl.BlockSpec(b2.shape, lambda i: (0, 0))],
        out_specs=[pl.BlockSpec((t, bb, dc), lambda i: (0, i, 0)),
                   pl.BlockSpec((hist, bb, dc), lambda i: (0, i, 0))],
        out_shape=[jax.ShapeDtypeStruct((t, batch, dc), F32),
                   jax.ShapeDtypeStruct((hist, batch, dc), F32)],
        compiler_params=_params("parallel"),
        name="dwconv_sample",
    )(glu, states, w_dw, b2)


def _conf_tail_kernel(y_ref, gl_ref, bl_ref, w_ref, b2_ref, x_ref, gate_ref, g_ref, b_ref, o_ref):
    yn = _silu(_layer_norm(y_ref[...], gl_ref[...], bl_ref[...]))
    y = _dot(yn.astype(BF16), w_ref[...]) + b2_ref[...]
    v = ALPHA * x_ref[...] + (1.0 + gate_ref[...]) * y
    o_ref[...] = _layer_norm(v, g_ref[...], b_ref[...])


def conformer_tail(y, cf_g, cf_b, w2_bf, b2, x2d, gate, ln_g, ln_b, rows_per_group, tm):
    n, d = x2d.shape
    dc = y.shape[1]
    vecc = pl.BlockSpec((1, dc), lambda i: (0, 0))
    vec = pl.BlockSpec((1, d), lambda i: (0, 0))
    return pl.pallas_call(
        _conf_tail_kernel,
        grid=(n // tm,),
        in_specs=[pl.BlockSpec((tm, dc), lambda i: (i, 0)), vecc, vecc,
                  _resident(w2_bf), vec,
                  pl.BlockSpec((tm, d), lambda i: (i, 0)),
                  _mod_spec(gate, tm, rows_per_group), vec, vec],
        out_specs=pl.BlockSpec((tm, d), lambda i: (i, 0)),
        out_shape=jax.ShapeDtypeStruct((n, d), F32),
        compiler_params=_params("parallel"),
        name="conformer_tail",
    )(y, cf_g.reshape(1, dc), cf_b.reshape(1, dc), w2_bf, b2.reshape(1, d), x2d, gate,
      ln_g.reshape(1, d), ln_b.reshape(1, d))


def kernel(x_prompt, x_sample, cache_k, cache_v, state_ret, state_conv, state_ffn, page_table, c_prompt, c_sample, ab_w_in, ab_w_out, cf_w_pw1, cf_b_pw1, cf_w_dw, cf_b_dw, cf_ln_g, cf_ln_b, cf_w_pw2, cf_b_pw2, ffn_w_up, ffn_w_dw, ffn_b_dw, ffn_w_down, ada_w, ada_b, ln_g, ln_b):
    bp, seq, d = x_prompt.shape
    bs, ts, _ = x_sample.shape
    n_layers = ada_w.shape[0]
    d_ff = ffn_w_down.shape[1]
    past_len = page_table.shape[1] * cache_k.shape[2]
    np_, ns = bp * seq, bs * ts
    tm_p = min(256, seq)
    tm_s = bs

    ada = adaln_all(jnp.concatenate([c_prompt, c_sample], 0), ada_w, ada_b)

    def mods(l, k):
        return ada[l, k, :bp][:, None, :], ada[l, k, bp:][None]

    token_major = lambda a: jnp.swapaxes(a, 0, 1)
    xp = x_prompt.reshape(np_, d)
    xs = token_major(x_sample).reshape(ns, d)
    tabs_p = rotary_tables(jnp.arange(seq, dtype=jnp.int32))
    tabs_s = tuple(jnp.repeat(tb, bs, axis=0) for tb in
                   rotary_tables(past_len + jnp.arange(ts, dtype=jnp.int32)))

    w_up = ffn_w_up.astype(BF16)
    w_dn = ffn_w_down.astype(BF16)
    outs = {k: [] for k in ("kp", "vp", "ks", "vs", "rp", "rs", "cp", "cs", "fp", "fs")}
    for l in range(n_layers):
        i = l // 2
        sh_p, sh_s = mods(l, 0)
        sc_p, sc_s = mods(l, 1)
        gt_p, gt_s = mods(l, 2)
        if l % 2 == 0:
            w_in = ab_w_in[i].astype(BF16)
            w_out = ab_w_out[i].astype(BF16)
            rq, rk, rv, rg, mq, mk, mv, kb, vb, kmean = in_proj(
                xp, sh_p, sc_p, w_in, tabs_p, seq, min(2 * tm_p, seq), True, BF16)
            o_mp, o_rp, s_p = attention_prompt(mq, kb, vb, kmean.reshape(-1, GROUP_W), rq, rk, rv, rg, bp, seq)
            outs["kp"].append(mk.reshape(bp, seq, N_HEADS, HEAD_DIM))
            outs["vp"].append(mv.reshape(bp, seq, N_HEADS, HEAD_DIM))
            outs["rp"].append(s_p)

            rq, rk, rv, rg, mq, mk, mv, kb, vb = in_proj(xs, sh_s, sc_s, w_in, tabs_s, ns, tm_s, False, F32)
            o_r, s_s = retention_sample(rq, rk, rv, rg, state_ret, i, ts)
            n_phys, page = cache_k.shape[1], cache_k.shape[2]
            ck = cache_k.reshape(-1, HEAD_DIM)
            cv = cache_v.reshape(-1, HEAD_DIM)
            mods_ffn_p = tuple(mods(l, k)[0] for k in (3, 4, 5))
            o_m, xp, prev_p = moba_sample_ffn_prompt(
                mq, kb, vb, ck, cv, page_table, i * n_phys, page, ts,
                o_rp, o_mp, w_out, xp, gt_p, ln_g[l, 0], ln_b[l, 0], *mods_ffn_p, w_up, ffn_w_dw[l], ffn_b_dw[l], w_dn, l, ln_g[l, 1], ln_b[l, 1], seq, tm_p)
            o_m = token_major(o_m[:, :ts]).reshape(ns, GROUP_W)
            xs = out_proj(o_r, o_m, w_out, xs, gt_s, ln_g[l, 0], ln_b[l, 0], ns, tm_s)
            outs["ks"].append(token_major(mk.reshape(ts, bs, N_HEADS, HEAD_DIM)))
            outs["vs"].append(token_major(mv.reshape(ts, bs, N_HEADS, HEAD_DIM)))
            outs["rs"].append(s_s)
        else:
            w1 = cf_w_pw1[i].astype(BF16)
            w2 = cf_w_pw2[i].astype(BF16)
            hist = cf_w_dw.shape[1] - 1
            mods_ffn_p = tuple(mods(l, k)[0] for k in (3, 4, 5))
            xp, tail, prev_p = conformer_ffn_prompt(
                xp, (sh_p, sc_p, gt_p), mods_ffn_p, w1, cf_b_pw1[i], cf_w_dw[i], cf_b_dw[i],
                cf_ln_g[i], cf_ln_b[i], w2, cf_b_pw2[i], ln_g[l, 0], ln_b[l, 0],
                w_up, ffn_w_dw[l], ffn_b_dw[l], w_dn, l, ln_g[l, 1], ln_b[l, 1], bp, seq, tm_p)
            outs["cp"].append(tail[:, PAD_ROWS - hist:])

            glu = conformer_glu(xs, sh_s, sc_s, w1, cf_b_pw1[i], ns, tm_s)
            y, nst = dwconv_sample(glu.reshape(ts, bs, -1), jnp.swapaxes(state_conv, 1, 2), i,
                                   cf_w_dw[i], cf_b_dw[i])
            xs = conformer_tail(y.reshape(ns, -1), cf_ln_g[i], cf_ln_b[i], w2, cf_b_pw2[i], xs, gt_s,
                                ln_g[l, 0], ln_b[l, 0], ns, tm_s)
            outs["cs"].append(token_major(nst))

        outs["fp"].append(prev_p.reshape(bp, seq // tm_p, 2, d_ff)[:, -1])
        sh_s, sc_s, gt_s = (mods(l, k)[1] for k in (3, 4, 5))
        xs, u_s = conv_ffn_planes(xs, sh_s, sc_s, gt_s, w_up, ffn_w_dw[l], ffn_b_dw[l], w_dn, l,
                                  ln_g[l, 1], ln_b[l, 1], tm_s, (state_ffn[l, :, 0], state_ffn[l, :, 1]))
        outs["fs"].append(token_major(u_s.reshape(ts, bs, d_ff)[ts - 2:]))

    st = lambda k: outs[k][0][None] if len(outs[k]) == 1 else jnp.stack(outs[k])
    return (xp.reshape(bp, seq, d), token_major(xs.reshape(ts, bs, d)), st("kp"), st("vp"), st("ks"), st("vs"),
            st("rp"), st("rs"), st("cp"), st("cs"), st("fp"), st("fs"))
```

```python
import functools

import jax
import jax.numpy as jnp
from jax import lax
from jax.experimental import pallas as pl
from jax.experimental.pallas import tpu as pltpu

HEAD_DIM = 128
N_HEADS = 4
GROUP_W = N_HEADS * HEAD_DIM
RET_CHUNK = 128
MOBA_BLOCK = 256
MOBA_TOPK = 3
ROPE_THETA = 10000.0
DEPTH = 2
ALPHA = (2 * DEPTH) ** 0.25
LN_EPS = 1e-5
GN_EPS = 1e-6
NEG = -1e30
LOG2_E = 1.4426950408889634

F32 = jnp.float32
BF16 = jnp.bfloat16

_NT = (((1,), (1,)), ((), ()))
_TN = (((0,), (0,)), ((), ()))

VMEM_LIMIT = 56 * 1024 * 1024
SUBLANES = 8


def _params(*sem):
    return pltpu.CompilerParams(dimension_semantics=sem, vmem_limit_bytes=VMEM_LIMIT)


def _dot(a, b):
    return jnp.dot(a, b, preferred_element_type=F32)


def _layer_norm(v, g, b):
    mu = jnp.mean(v, axis=-1, keepdims=True)
    d = v - mu
    var = jnp.mean(d * d, axis=-1, keepdims=True)
    return d * lax.rsqrt(var + LN_EPS) * g + b


def _silu(x):
    return x * jax.nn.sigmoid(x)


def _mod_spec(mod, tm, rows_per_group):
    _, r, d = mod.shape
    if r == 1:
        return pl.BlockSpec((None, 1, d), lambda i: (i * tm // rows_per_group, 0, 0))
    assert r % tm == 0
    return pl.BlockSpec((None, tm, d), lambda i: (0, i % (r // tm), 0))


def _resident(a, layer=None):
    if layer is None:
        return pl.BlockSpec(a.shape, lambda *_: (0,) * a.ndim, pipeline_mode=pl.Buffered(1))
    return pl.BlockSpec((None,) + a.shape[1:], lambda *_: (layer,) + (0,) * (a.ndim - 1),
                        pipeline_mode=pl.Buffered(1))


CAST_BLOCK_BYTES = 4 * 1024 * 1024


def _cast_kernel(x_ref, o_ref):
    o_ref[...] = x_ref[...].astype(o_ref.dtype)


def to_bf16(w):
    cols = w.shape[-1]
    w2 = w.reshape(-1, cols)
    rows = w2.shape[0]
    br = max(SUBLANES, CAST_BLOCK_BYTES // (cols * 4) // SUBLANES * SUBLANES)
    while rows % br:
        br -= SUBLANES
    out = pl.pallas_call(
        _cast_kernel,
        grid=(rows // br,),
        in_specs=[pl.BlockSpec((br, cols), lambda i: (i, 0))],
        out_specs=pl.BlockSpec((br, cols), lambda i: (i, 0)),
        out_shape=jax.ShapeDtypeStruct((rows, cols), BF16),
        compiler_params=_params("parallel"),
        name="to_bf16",
    )(w2)
    return out.reshape(w.shape)


def _adaln_kernel(c_ref, w_ref, b_ref, o_ref):
    a = _silu(c_ref[...]).astype(BF16)
    d = c_ref.shape[1]
    for v in range(o_ref.shape[0]):
        cols = slice(v * d, (v + 1) * d)
        o_ref[v] = _dot(a, w_ref[:, cols].astype(BF16)) + b_ref[:, cols]


def adaln_all(c_all, ada_w, ada_b):
    n_layers, d, n_out = ada_w.shape
    m = c_all.shape[0]
    per = 2 if (n_out // d) % 2 == 0 else 1
    return pl.pallas_call(
        _adaln_kernel,
        grid=(n_layers, n_out // (per * d)),
        in_specs=[
            pl.BlockSpec((m, d), lambda l, j: (0, 0)),
            pl.BlockSpec((None, d, per * d), lambda l, j: (l, 0, j)),
            pl.BlockSpec((None, 1, per * d), lambda l, j: (l, 0, j)),
        ],
        out_specs=pl.BlockSpec((None, per, m, d), lambda l, j: (l, j, 0, 0)),
        out_shape=jax.ShapeDtypeStruct((n_layers, n_out // d, m, d), F32),
        compiler_params=_params("parallel", "parallel"),
        name="adaln",
    )(c_all, ada_w, ada_b.reshape(n_layers, 1, n_out))


def _in_proj_kernel(x_ref, sh_ref, sc_ref, w_ref, cr_ref, sr_ref, cm_ref, sm_ref,
                    rq_ref, rk_ref, rv_ref, rg_ref, mq_ref, mk_ref, mv_ref, kb_ref, vb_ref,
                    *km_refs):
    tm = x_ref.shape[0]
    sub = min(tm, MOBA_BLOCK)
    outs = (rq_ref, rk_ref, rv_ref, rg_ref, mq_ref, mk_ref, mv_ref)
    for r0 in range(0, tm, sub):
        rows = slice(r0, r0 + sub)
        sc = sc_ref[...] if sc_ref.shape[0] == 1 else sc_ref[rows, :]
        sh = sh_ref[...] if sh_ref.shape[0] == 1 else sh_ref[rows, :]
        h = (x_ref[rows, :] * (1.0 + sc) + sh).astype(BF16)
        cr, sr, cm, sm = cr_ref[rows, :], sr_ref[rows, :], cm_ref[rows, :], sm_ref[rows, :]
        even = (lax.broadcasted_iota(jnp.int32, cr.shape, 1) & 1) == 0
        for g, o_ref in enumerate(outs):
            z = _dot(h, w_ref[:, g * GROUP_W:(g + 1) * GROUP_W])
            for hh in range(N_HEADS):
                cols = slice(hh * HEAD_DIM, (hh + 1) * HEAD_DIM)
                zs = z[:, cols]
                if g in (0, 1):
                    nxt = pltpu.roll(zs, HEAD_DIM - 1, axis=1)
                    prv = pltpu.roll(zs, 1, axis=1)
                    zs = zs * cr + jnp.where(even, nxt, prv) * sr
                    if g == 1:
                        zs = zs * (HEAD_DIM ** -0.5)
                elif g in (4, 5):
                    zs = zs * cm + pltpu.roll(zs, HEAD_DIM // 2, axis=1) * sm
                if g in (5, 6):
                    o_ref[pl.ds(r0 * N_HEADS + hh, sub, stride=N_HEADS), :] = zs
                    head_major = kb_ref if g == 5 else vb_ref
                    head_major[rows, cols] = zs.astype(head_major.dtype)
                    if g == 5 and km_refs:
                        km_refs[0][r0 // MOBA_BLOCK:r0 // MOBA_BLOCK + 1, cols] = (
                            jnp.sum(zs, axis=0, keepdims=True) * (1.0 / MOBA_BLOCK))
                else:
                    o_ref[rows, cols] = zs.astype(o_ref.dtype)


def in_proj(x2d, shift, scale, w_bf, tabs, rows_per_group, tm, block_means, act_dtype):
    n, d = x2d.shape
    p_tiles = tabs[0].shape[0] // tm
    mod_spec = _mod_spec(shift, tm, rows_per_group)
    tab_spec = pl.BlockSpec((tm, HEAD_DIM), lambda i: (i % p_tiles, 0))
    wide = pl.BlockSpec((tm, GROUP_W), lambda i: (i, 0))
    tall = pl.BlockSpec((tm * N_HEADS, HEAD_DIM), lambda i: (i, 0))
    wide_shape = lambda dt: jax.ShapeDtypeStruct((n, GROUP_W), dt)
    tall_shape = jax.ShapeDtypeStruct((n * N_HEADS, HEAD_DIM), F32)
    out_specs = [wide] * 5 + [tall, tall, wide, wide]
    out_shape = [wide_shape(act_dtype)] * 5 + [tall_shape, tall_shape, wide_shape(act_dtype), wide_shape(act_dtype)]
    if block_means:
        assert tm % MOBA_BLOCK == 0
        per = tm // MOBA_BLOCK
        out_specs.append(pl.BlockSpec((None, per, GROUP_W), lambda i: (i, 0, 0)))
        out_shape.append(jax.ShapeDtypeStruct((n // tm, per, GROUP_W), F32))
    return pl.pallas_call(
        _in_proj_kernel,
        grid=(n // tm,),
        in_specs=[pl.BlockSpec((tm, d), lambda i: (i, 0)), mod_spec, mod_spec,
                  _resident(w_bf), tab_spec, tab_spec, tab_spec, tab_spec],
        out_specs=out_specs,
        out_shape=out_shape,
        compiler_params=_params("parallel"),
        name="in_proj",
    )(x2d, shift, scale, w_bf, *tabs)


def rotary_tables(pos):
    half = HEAD_DIM // 2
    posf = pos.astype(F32)[:, None]
    inv_r = 1.0 / (ROPE_THETA ** jnp.linspace(0.0, 1.0, half, dtype=F32))
    ang_r = posf * inv_r[None, :]
    cr = jnp.repeat(jnp.cos(ang_r), 2, axis=1)
    sr = jnp.stack([-jnp.sin(ang_r), jnp.sin(ang_r)], -1).reshape(-1, HEAD_DIM)
    inv_m = ROPE_THETA ** (-jnp.arange(0, HEAD_DIM, 2, dtype=F32) / HEAD_DIM)
    ang_m = posf * inv_m[None, :]
    cm = jnp.concatenate([jnp.cos(ang_m), jnp.cos(ang_m)], -1)
    sm = jnp.concatenate([-jnp.sin(ang_m), jnp.sin(ang_m)], -1)
    return cr, sr, cm, sm


def retention_tables(chunk, rows):
    log_g = jnp.log1p(-jnp.exp2(-5.0 - jnp.arange(N_HEADS, dtype=F32)))
    idx = jnp.arange(chunk, dtype=F32)
    diff = idx[:, None] - idx[None, :]
    decay_in = jnp.where(diff[None] >= 0,
                         jnp.exp(jnp.maximum(diff, 0.0)[None] * log_g[:, None, None]), 0.0)
    q_dec = jnp.exp((idx + 1.0)[None, :] * log_g[:, None])
    k_dec = jnp.exp((chunk - 1.0 - idx)[None, :] * log_g[:, None])
    c_dec = jnp.exp(chunk * log_g)
    pad = rows - chunk
    decay_in = jnp.pad(decay_in, ((0, 0), (0, pad), (0, rows - chunk)))
    bc = lambda t: jnp.broadcast_to(jnp.pad(t, ((0, 0), (0, pad)))[:, :, None], (N_HEADS, rows, HEAD_DIM))
    c_b = jnp.broadcast_to(c_dec[:, None, None], (N_HEADS, 1, HEAD_DIM))
    return decay_in, bc(q_dec), bc(k_dec), c_b


def _ret_chunk(qc, kc, vc, s, din, qd, kd, cd):
    att = lax.dot_general(qc.astype(BF16), kc.astype(BF16), _NT,
                          preferred_element_type=F32) * din
    o = _dot(att.astype(BF16), vc) + _dot((qc.astype(F32) * qd).astype(BF16), s.astype(BF16))
    s_new = cd * s + lax.dot_general((kc.astype(F32) * kd).astype(BF16), vc, _TN,
                                     preferred_element_type=F32)
    return o, s_new


def _group_norm_gate(o, g):
    o = o * lax.rsqrt(jnp.mean(o * o, axis=-1, keepdims=True) + GN_EPS)
    return o * _silu(g)


def _ret_prompt_phases(q_ref, k_ref, v_ref, g_ref, din_ref, qd_ref, kd_ref, cd_ref, o_ref, sout_ref):
    chunk = din_ref.shape[0]
    din, qd, kd, cd = din_ref[...], qd_ref[...], kd_ref[...], cd_ref[...]
    s = jnp.zeros((HEAD_DIM, HEAD_DIM), F32)
    for c in range(q_ref.shape[0] // chunk):
        rows = slice(c * chunk, (c + 1) * chunk)
        o, s = _ret_chunk(q_ref[rows, :], k_ref[rows, :], v_ref[rows, :], s, din, qd, kd, cd)
        o_ref[rows, :] = _group_norm_gate(o, g_ref[rows, :].astype(F32)).astype(o_ref.dtype)
        yield
    sout_ref[...] = s
    yield


def _group_rows(ref, group, batch, t, cols):
    return [ref[pl.ds(pl.multiple_of(ti * batch + group * SUBLANES, SUBLANES), SUBLANES), cols]
            for ti in range(t)]


def _seq_tile(token_rows, row, n_rows):
    width = token_rows[0].shape[1]
    r_i = lax.broadcasted_iota(jnp.int32, (n_rows, width), 0)
    g_i = lax.broadcasted_iota(jnp.int32, token_rows[0].shape, 0)
    out = jnp.zeros((n_rows, width), F32)
    for ti, rows in enumerate(token_rows):
        if isinstance(row, int):
            picked = rows[row:row + 1, :]
        else:
            picked = jnp.sum(jnp.where(g_i == row, rows, 0.0), axis=0, keepdims=True)
        out = jnp.where(r_i == ti, picked, out)
    return out


def _ret_sample_kernel(q_ref, k_ref, v_ref, g_ref, s0_ref, din_ref, qd_ref, kd_ref, cd_ref,
                       o_ref, sout_ref, *, t, batch):
    group = pl.program_id(0)
    r_i = lax.broadcasted_iota(jnp.int32, (SUBLANES, HEAD_DIM), 0)
    for hh in range(N_HEADS):
        cols = slice(hh * HEAD_DIM, (hh + 1) * HEAD_DIM)
        din, qd, kd, cd = din_ref[hh], qd_ref[hh], kd_ref[hh], cd_ref[hh]
        q_t, k_t, v_t, g_t = (_group_rows(r, group, batch, t, cols) for r in (q_ref, k_ref, v_ref, g_ref))
        out_t = [jnp.zeros((SUBLANES, HEAD_DIM), F32) for _ in range(t)]
        for bl in range(SUBLANES):
            qc = _seq_tile(q_t, bl, RET_CHUNK)
            kc = _seq_tile(k_t, bl, RET_CHUNK)
            vc = _seq_tile(v_t, bl, RET_CHUNK).astype(BF16)
            o, s_new = _ret_chunk(qc, kc, vc, s0_ref[bl, hh], din, qd, kd, cd)
            sout_ref[bl, hh] = s_new
            gated = _group_norm_gate(o[0:SUBLANES, :], _seq_tile(g_t, bl, SUBLANES))
            for ti in range(t):
                out_t[ti] = jnp.where(r_i == bl, gated[ti:ti + 1, :], out_t[ti])
        for ti in range(t):
            o_ref[pl.ds(pl.multiple_of(ti * batch + group * SUBLANES, SUBLANES), SUBLANES), cols] = out_t[ti]


def retention_sample(rq, rk, rv, rg, states, layer, t):
    batch = states.shape[1]
    bb = SUBLANES
    assert batch % bb == 0
    tabs = retention_tables(t, RET_CHUNK)
    tok = pl.BlockSpec(rq.shape, lambda i: (0, 0))
    st_in = pl.BlockSpec((None, bb, N_HEADS, HEAD_DIM, HEAD_DIM), lambda i: (layer, i, 0, 0, 0))
    st = pl.BlockSpec((bb, N_HEADS, HEAD_DIM, HEAD_DIM), lambda i: (i, 0, 0, 0))
    tab = pl.BlockSpec((N_HEADS, RET_CHUNK, HEAD_DIM), lambda i: (0, 0, 0))
    return pl.pallas_call(
        functools.partial(_ret_sample_kernel, t=t, batch=batch),
        grid=(batch // bb,),
        in_specs=[tok, tok, tok, tok, st_in, tab, tab, tab,
                  pl.BlockSpec((N_HEADS, 1, HEAD_DIM), lambda i: (0, 0, 0))],
        out_specs=[tok, st],
        out_shape=[jax.ShapeDtypeStruct(rq.shape, F32),
                   jax.ShapeDtypeStruct(states.shape[1:], F32)],
        compiler_params=_params("arbitrary"),
        name="retention_sample",
    )(rq, rk, rv, rg, states, *tabs)


def _beaten_counts(rows, n_valid):
    counts = []
    for n in range(n_valid):
        cnt = jnp.zeros(rows[n].shape, jnp.int32)
        for m in range(n_valid):
            if m == n:
                continue
            beats = (rows[m] >= rows[n]) if m < n else (rows[m] > rows[n])
            cnt = cnt + beats.astype(jnp.int32)
        counts.append(cnt)
    return counts


def _moba_prompt_phases(q_ref, k_ref, v_ref, km_ref, o_ref, s0_ref, s1_ref, *, nblk):
    blk = MOBA_BLOCK
    scale = HEAD_DIM ** -0.5
    kb = k_ref[...]
    vt = v_ref[...].astype(F32).T.astype(BF16)
    kmean = km_ref[...].astype(BF16)
    key_i = lax.broadcasted_iota(jnp.int32, (blk, blk), 0)
    qry_i = lax.broadcasted_iota(jnp.int32, (blk, blk), 1)
    causal = key_i <= qry_i

    def query_block(qi, s_ref):
        qb = q_ref[qi * blk:(qi + 1) * blk, :]
        bias = None
        if qi > MOBA_TOPK:
            gt = lax.dot_general(kmean, qb, _NT, preferred_element_type=F32)
            cnt = _beaten_counts([gt[m:m + 1, :] for m in range(qi)], qi)
            bias = [jnp.where(c < MOBA_TOPK, 0.0, NEG) for c in cnt]
        mx = jnp.full((1, blk), NEG, F32)
        for n in range(qi + 1):
            s = lax.dot_general(kb[n * blk:(n + 1) * blk], qb, _NT,
                                preferred_element_type=F32)
            if n == qi:
                s = jnp.where(causal, s, NEG)
            elif bias is not None:
                s = s + bias[n]
            s_ref[n * blk:(n + 1) * blk, :] = s
            mx = jnp.maximum(mx, jnp.max(s, axis=0, keepdims=True))
            yield
        l = jnp.zeros((1, blk), F32)
        ot = jnp.zeros((HEAD_DIM, blk), F32)
        for n in range(qi + 1):
            p = jnp.exp2((s_ref[n * blk:(n + 1) * blk, :] - mx) * (scale * LOG2_E))
            l = l + jnp.sum(p, axis=0, keepdims=True)
            ot = ot + _dot(vt[:, n * blk:(n + 1) * blk], p.astype(BF16))
            yield
        ot = ot * (1.0 / l)
        o_ref[qi * blk:(qi + 1) * blk, :] = ot.T.astype(o_ref.dtype)
        yield

    blocks = [query_block(qi, (s0_ref, s1_ref)[qi % 2]) for qi in range(nblk)]
    for _ in range(1):
        next(blocks[0])
    for qi in range(nblk):
        value_steps = qi + 2
        score_steps = qi + 2 if qi + 1 < nblk else 0
        for step in range(max(value_steps, score_steps)):
            if step < score_steps:
                next(blocks[qi + 1])
            if step < value_steps:
                next(blocks[qi])
            yield


def _attention_prompt_kernel(mq_ref, mk_ref, mv_ref, km_ref, rq_ref, rk_ref, rv_ref, rg_ref,
                             din_ref, qd_ref, kd_ref, cd_ref, om_ref, or_ref, sout_ref, s0_ref, s1_ref,
                             *, nblk):
    moba = _moba_prompt_phases(mq_ref, mk_ref, mv_ref, km_ref, om_ref, s0_ref, s1_ref, nblk=nblk)
    ret = _ret_prompt_phases(rq_ref, rk_ref, rv_ref, rg_ref, din_ref, qd_ref, kd_ref, cd_ref, or_ref, sout_ref)
    n_moba = sum(qi + 2 for qi in range(nblk))
    n_ret = rq_ref.shape[0] // din_ref.shape[0] + 1
    every = max(n_moba // n_ret, 1)
    for step, _ in enumerate(moba):
        if step % every == 0:
            next(ret, None)
    _run(ret)


def attention_prompt(mq, mk, mv, kmean, rq, rk, rv, rg, batch, seq):
    nblk = seq // MOBA_BLOCK
    assert nblk % SUBLANES == 0
    chunk = min(2 * RET_CHUNK, seq)
    din, qd, kd, cd = retention_tables(chunk, chunk)
    spec = pl.BlockSpec((seq, HEAD_DIM), lambda b, h: (b, h))
    head = lambda a: pl.BlockSpec((None,) + a.shape[1:], lambda b, h: (h, 0, 0))
    return pl.pallas_call(
        functools.partial(_attention_prompt_kernel, nblk=nblk),
        grid=(batch, N_HEADS),
        in_specs=[spec, spec, spec, pl.BlockSpec((nblk, HEAD_DIM), lambda b, h: (b, h)),
                  spec, spec, spec, spec, head(din), head(qd), head(kd), head(cd)],
        out_specs=[spec, spec,
                   pl.BlockSpec((None, None, HEAD_DIM, HEAD_DIM), lambda b, h: (b, h, 0, 0))],
        out_shape=[jax.ShapeDtypeStruct((batch * seq, GROUP_W), BF16),
                   jax.ShapeDtypeStruct((batch * seq, GROUP_W), BF16),
                   jax.ShapeDtypeStruct((batch, N_HEADS, HEAD_DIM, HEAD_DIM), F32)],
        scratch_shapes=[pltpu.VMEM((seq, MOBA_BLOCK), F32), pltpu.VMEM((seq, MOBA_BLOCK), F32)],
        compiler_params=_params("parallel", "parallel"),
        name="attention_prompt",
    )(mq, mk, mv, kmean, rq, rk, rv, rg, din, qd, kd, cd)


def _page_rows(ref, page):
    return jnp.concatenate([ref[pl.ds(hh, page, stride=N_HEADS), :] for hh in range(N_HEADS)], axis=1)


def _moba_sample_phases(b, q_ref, kn_ref, vn_ref, k_refs, v_refs, o_ref, *, page, t):
    n_pages = len(k_refs)
    scale = HEAD_DIM ** -0.5
    ppb = MOBA_BLOCK // page
    nblk = n_pages // ppb
    n_pairs = N_HEADS * t
    batch = q_ref.shape[0] // t
    all_cols = slice(0, GROUP_W)
    group = b // SUBLANES
    row = b % SUBLANES
    q = _seq_tile(_group_rows(q_ref, group, batch, t, all_cols), row, SUBLANES)

    r_i = lax.broadcasted_iota(jnp.int32, (n_pairs, GROUP_W), 0)
    l_i = lax.broadcasted_iota(jnp.int32, (n_pairs, GROUP_W), 1)
    qm = jnp.zeros((n_pairs, GROUP_W), F32)
    for ti in range(t):
        for hh in range(N_HEADS):
            hit = (r_i == hh * t + ti) & (l_i >= hh * HEAD_DIM) & (l_i < (hh + 1) * HEAD_DIM)
            qm = jnp.where(hit, q[ti:ti + 1, :], qm)
    qm = qm.astype(BF16)

    zpad = jnp.zeros((page - SUBLANES, GROUP_W), F32)
    own = lambda ref: jnp.concatenate(
        [_seq_tile(_group_rows(ref, group, batch, t, all_cols), row, SUBLANES), zpad], axis=0).astype(BF16)
    k_own, v_own = own(kn_ref), own(vn_ref)

    b_i = lax.broadcasted_iota(jnp.int32, (page, GROUP_W), 0)
    kmean = jnp.zeros((page, GROUP_W), F32)
    scores = []
    for p in range(n_pages):
        kp = _page_rows(k_refs[p], page)
        scores.append(lax.dot_general(qm, kp.astype(BF16), _NT,
                                      preferred_element_type=F32) * scale)
        ks = jnp.sum(kp, axis=0, keepdims=True) * (1.0 / MOBA_BLOCK)
        kmean = kmean + jnp.where(b_i == p // ppb, ks, 0.0)
        yield
    gate = lax.dot_general(qm, kmean.astype(BF16), _NT, preferred_element_type=F32)
    cnt = _beaten_counts([gate[:, m:m + 1] for m in range(nblk)], nblk)
    keep = [jnp.broadcast_to(c, (n_pairs, page)) < MOBA_TOPK for c in cnt]

    s_own = lax.dot_general(qm, k_own, _NT, preferred_element_type=F32) * scale
    key_i = lax.broadcasted_iota(jnp.int32, (n_pairs, page), 1)
    tok_i = lax.rem(lax.broadcasted_iota(jnp.int32, (n_pairs, page), 0), t)
    s_own = jnp.where((key_i <= tok_i) & (key_i < t), s_own, NEG)

    mx = s_own
    for p in range(n_pages):
        scores[p] = jnp.where(keep[p // ppb], scores[p], NEG)
        mx = jnp.maximum(mx, scores[p])
    mx = jnp.max(mx, axis=1, keepdims=True)
    e_own = jnp.exp(s_own - mx)
    l = e_own
    probs = []
    for p in range(n_pages):
        e = jnp.exp(scores[p] - mx)
        l = l + e
        probs.append(e.astype(BF16))
    inv_l = 1.0 / jnp.sum(l, axis=1, keepdims=True)
    yield
    acc = _dot(e_own.astype(BF16), v_own)
    for p in range(n_pages):
        acc = acc + _dot(probs[p], _page_rows(v_refs[p], page).astype(BF16))
        yield
    acc = acc * inv_l
    o_ref[...] = jnp.zeros_like(o_ref)
    for hh in range(N_HEADS):
        cols = slice(hh * HEAD_DIM, (hh + 1) * HEAD_DIM)
        o_ref[0:t, cols] = acc[hh * t:(hh + 1) * t, cols]
    yield


def _mixer_out(or_ref, om_ref, w_ref, x_ref, gate_ref, g_ref, b_ref, o_ref, sub):
    for r0 in range(0, x_ref.shape[0], sub):
        rows = slice(r0, r0 + sub)
        y = (_dot(or_ref[rows, :].astype(BF16), w_ref[:GROUP_W, :])
             + _dot(om_ref[rows, :].astype(BF16), w_ref[GROUP_W:, :]))
        gate = gate_ref[...] if gate_ref.shape[0] == 1 else gate_ref[rows, :]
        v = ALPHA * x_ref[rows, :] + (1.0 + gate) * y
        o_ref[rows, :] = _layer_norm(v, g_ref[...], b_ref[...])


def _out_proj_kernel(or_ref, om_ref, w_ref, x_ref, gate_ref, g_ref, b_ref, o_ref, *, sub):
    _mixer_out(or_ref, om_ref, w_ref, x_ref, gate_ref, g_ref, b_ref, o_ref, sub)


def out_proj(o_r, o_m, w_bf, x2d, gate, ln_g, ln_b, rows_per_group, tm):
    n, d = x2d.shape
    vec = pl.BlockSpec((1, d), lambda i: (0, 0))
    return pl.pallas_call(
        functools.partial(_out_proj_kernel, sub=min(128, tm)),
        grid=(n // tm,),
        in_specs=[pl.BlockSpec((tm, GROUP_W), lambda i: (i, 0)),
                  pl.BlockSpec((tm, GROUP_W), lambda i: (i, 0)),
                  _resident(w_bf),
                  pl.BlockSpec((tm, d), lambda i: (i, 0)),
                  _mod_spec(gate, tm, rows_per_group), vec, vec],
        out_specs=pl.BlockSpec((tm, d), lambda i: (i, 0)),
        out_shape=jax.ShapeDtypeStruct((n, d), F32),
        compiler_params=_params("parallel"),
        name="out_proj",
    )(o_r, o_m, w_bf, x2d, gate, ln_g.reshape(1, d), ln_b.reshape(1, d))


def _run(phases):
    for _ in phases:
        pass


def _ffn_phases(x, sh_ref, sc_ref, gate_ref, wup_ref, wdw_ref, bdw_ref, wdn_ref, g_ref, b_ref,
                o_ref, prev_ref, h_s, act_s, hist_s, *, d_ff, cw, planes, chunks=None):
    tm = x.shape[0]
    n_chunks = d_ff // cw
    chunks = range(n_chunks) if chunks is None else chunks
    if chunks.start == 0:
        h_s[...] = (x * (1.0 + sc_ref[...]) + sh_ref[...]).astype(BF16)
    row = lax.broadcasted_iota(jnp.int32, (tm, cw), 0)
    for c in chunks:
        cu = slice(c * cw, (c + 1) * cw)
        cv = slice(d_ff + c * cw, d_ff + (c + 1) * cw)
        u = _dot(h_s[...], wup_ref[:, cu])
        v = _dot(h_s[...], wup_ref[:, cv])
        if planes:
            s2 = hist_s[0, :, cu]
            s1 = hist_s[1, :, cu]
            hist_s[0, :, cu] = s1
            hist_s[1, :, cu] = u
            prev_ref[:, cu] = u
        else:
            e1 = hist_s[1:2, cu]
            e2 = jnp.where(row == 0, hist_s[0:1, cu], hist_s[1:2, cu])
            hist_s[:, cu] = u[tm - 2:tm, :]
            prev_ref[:, cu] = u[tm - 2:tm, :]
            s1 = jnp.where(row >= 1, pltpu.roll(u, 1, axis=0), e1)
            s2 = jnp.where(row >= 2, pltpu.roll(u, 2, axis=0), e2)
        uc = wdw_ref[0:1, cu] * s2 + wdw_ref[1:2, cu] * s1 + wdw_ref[2:3, cu] * u + bdw_ref[:, cu]
        act = 0.5 * uc * (1.0 + lax.erf(uc * (2.0 ** -0.5))) * v
        act_s[:, cu] = act.astype(BF16)
        yield act[0:1, 0:HEAD_DIM]
    if chunks.stop != n_chunks:
        return
    fy = _dot(act_s[...], wdn_ref[...])
    vsum = ALPHA * x + (1.0 + gate_ref[...]) * fy
    o_ref[...] = _layer_norm(vsum, g_ref[...], b_ref[...])
    yield


def _ffn_planes_kernel(x_ref, sh_ref, sc_ref, gate_ref, wup_ref, wdw_ref, bdw_ref, wdn_ref, g_ref, b_ref,
                       p0_ref, p1_ref, o_ref, prev_ref, h_s, act_s, hist_s, *, d_ff, cw):
    @pl.when(pl.program_id(0) == 0)
    def _():
        hist_s[0] = p0_ref[...]
        hist_s[1] = p1_ref[...]

    _run(_ffn_phases(x_ref[...], sh_ref, sc_ref, gate_ref, wup_ref, wdw_ref, bdw_ref, wdn_ref,
                     g_ref, b_ref, o_ref, prev_ref, h_s, act_s, hist_s, d_ff=d_ff, cw=cw, planes=True))


def conv_ffn_planes(x2d, shift, scale, gate, w_up_bf, w_dw, b_dw, w_dn_bf, layer, ln_g, ln_b,
                    tm, history):
    n, d = x2d.shape
    d_ff = w_dn_bf.shape[1]
    mod = _mod_spec(shift, tm, n)
    vec = pl.BlockSpec((1, d), lambda i: (0, 0))
    b_dw2 = b_dw.reshape(1, d_ff)
    return pl.pallas_call(
        functools.partial(_ffn_planes_kernel, d_ff=d_ff, cw=256),
        grid=(n // tm,),
        in_specs=[pl.BlockSpec((tm, d), lambda i: (i, 0)), mod, mod, mod,
                  _resident(w_up_bf, layer), _resident(w_dw), _resident(b_dw2),
                  _resident(w_dn_bf, layer), vec, vec, _resident(history[0]), _resident(history[1])],
        out_specs=[pl.BlockSpec((tm, d), lambda i: (i, 0)), pl.BlockSpec((tm, d_ff), lambda i: (i, 0))],
        out_shape=[jax.ShapeDtypeStruct((n, d), F32), jax.ShapeDtypeStruct((n, d_ff), F32)],
        scratch_shapes=[pltpu.VMEM((tm, d), BF16), pltpu.VMEM((tm, d_ff), BF16),
                        pltpu.VMEM((2, tm, d_ff), F32)],
        compiler_params=_params("arbitrary"),
        name="conv_ffn_planes",
    )(x2d, shift, scale, gate, w_up_bf, w_dw, b_dw2, w_dn_bf, ln_g.reshape(1, d), ln_b.reshape(1, d),
      *history)


def _moba_ffn_kernel(pt_ref, q_ref, kn_ref, vn_ref, *rest, n_pages, page, t, d_ff, cw, parts, tiles_per_seq):
    k_refs = rest[:n_pages]
    v_refs = rest[n_pages:2 * n_pages]
    (or_ref, omp_ref, wout_ref, x_ref, gate0_ref, g0_ref, b0_ref,
     sh_ref, sc_ref, gate_ref, wup_ref, wdw_ref, bdw_ref, wdn_ref, g_ref, b_ref,
     om_ref, o_ref, prev_ref, x1_s, h_s, act_s, hist_s) = rest[2 * n_pages:]
    i = pl.program_id(0)
    n_chunks = d_ff // cw
    per_part = -(-n_chunks // parts)

    @pl.when(i % (parts * tiles_per_seq) == 0)
    def _():
        hist_s[...] = jnp.zeros_like(hist_s)

    def step(part):
        chunks = range(part * per_part, min((part + 1) * per_part, n_chunks))
        if part == 0:
            _mixer_out(or_ref, omp_ref, wout_ref, x_ref, gate0_ref, g0_ref, b0_ref, x1_s, sub=128)
        ffn_ph = _ffn_phases(x1_s[...], sh_ref, sc_ref, gate_ref, wup_ref, wdw_ref, bdw_ref, wdn_ref,
                             g_ref, b_ref, o_ref, prev_ref, h_s, act_s, hist_s,
                             d_ff=d_ff, cw=cw, planes=False, chunks=chunks)
        moba_ph = _moba_sample_phases(i, q_ref, kn_ref, vn_ref, k_refs, v_refs, om_ref, page=page, t=t)
        for _ in range(n_pages):
            next(moba_ph)
        next(ffn_ph)
        next(moba_ph)
        for _ in range(len(chunks) - 1):
            next(ffn_ph)
        for _ in range(n_pages + 1):
            next(moba_ph)
        _run(ffn_ph)

    for part in range(parts):
        pl.when(i % parts == part)(functools.partial(step, part))


def moba_sample_ffn_prompt(mq, mk, mv, cache_k, cache_v, page_table, page_base, page, t,
                           o_r, o_m, w_out_bf, x2d, gate0, ln_g0, ln_b0,
                           shift, scale, gate, w_up_bf, w_dw, b_dw, w_dn_bf, layer, ln_g, ln_b,
                           seq, tm):
    batch, n_pages = page_table.shape
    n, d = x2d.shape
    d_ff = w_dn_bf.shape[1]
    assert (n_pages * page) % MOBA_BLOCK == 0 and t <= SUBLANES
    assert n_pages * page // MOBA_BLOCK <= page
    n_tiles = n // tm
    assert batch % n_tiles == 0
    parts = batch // n_tiles
    tok = pl.BlockSpec(mq.shape, lambda i, pt: (0, 0))

    def page_spec(p):
        return pl.BlockSpec((page * N_HEADS, HEAD_DIM), lambda i, pt: (page_base + pt[i, p], 0))

    const = lambda a, lead=None: pl.BlockSpec(
        a.shape if lead is None else (None,) + a.shape[1:],
        (lambda i, pt: (0,) * a.ndim) if lead is None else (lambda i, pt: (lead,) + (0,) * (a.ndim - 1)),
        pipeline_mode=pl.Buffered(1))
    mod = pl.BlockSpec((None, 1, d), lambda i, pt: ((i // parts) * tm // seq, 0, 0))
    b_dw2, g2, b2 = b_dw.reshape(1, d_ff), ln_g.reshape(1, d), ln_b.reshape(1, d)
    g0, b0 = ln_g0.reshape(1, d), ln_b0.reshape(1, d)
    tile = lambda w: pl.BlockSpec((tm, w), lambda i, pt: (i // parts, 0))
    specs = ([tok, tok, tok] + [page_spec(p) for p in range(n_pages)] * 2
             + [tile(GROUP_W), tile(GROUP_W), const(w_out_bf), tile(d), mod, const(g0), const(b0), mod, mod, mod,
                const(w_up_bf, layer), const(w_dw), const(b_dw2), const(w_dn_bf, layer), const(g2), const(b2)])
    grid_spec = pltpu.PrefetchScalarGridSpec(
        num_scalar_prefetch=1,
        grid=(batch,),
        in_specs=specs,
        out_specs=[pl.BlockSpec((None, SUBLANES, GROUP_W), lambda i, pt: (i, 0, 0)),
                   pl.BlockSpec((tm, d), lambda i, pt: (i // parts, 0)),
                   pl.BlockSpec((None, 2, d_ff), lambda i, pt: (i // parts, 0, 0))],
        scratch_shapes=[pltpu.VMEM((tm, d), F32), pltpu.VMEM((tm, d), BF16), pltpu.VMEM((tm, d_ff), BF16),
                        pltpu.VMEM((2, d_ff), F32)],
    )
    return pl.pallas_call(
        functools.partial(_moba_ffn_kernel, n_pages=n_pages, page=page, t=t, d_ff=d_ff, cw=256,
                          parts=parts, tiles_per_seq=seq // tm),
        grid_spec=grid_spec,
        out_shape=[jax.ShapeDtypeStruct((batch, SUBLANES, GROUP_W), F32),
                   jax.ShapeDtypeStruct((n, d), F32),
                   jax.ShapeDtypeStruct((n_tiles, 2, d_ff), F32)],
        compiler_params=_params("arbitrary"),
        name="moba_sample_ffn_prompt",
    )(page_table, mq, mk, mv, *([cache_k] * n_pages), *([cache_v] * n_pages),
      o_r, o_m, w_out_bf, x2d, gate0, g0, b0, shift, scale, gate, w_up_bf, w_dw, b_dw2, w_dn_bf, g2, b2)


def _glu_kernel(x_ref, sh_ref, sc_ref, w_ref, b_ref, o_ref, *, cw):
    h = (x_ref[...] * (1.0 + sc_ref[...]) + sh_ref[...]).astype(BF16)
    dc = o_ref.shape[1]
    for c in range(dc // cw):
        ca = slice(c * cw, (c + 1) * cw)
        cg = slice(dc + c * cw, dc + (c + 1) * cw)
        a = _dot(h, w_ref[:, ca]) + b_ref[:, ca]
        g = _dot(h, w_ref[:, cg]) + b_ref[:, cg]
        o_ref[:, ca] = a * jax.nn.sigmoid(g)


def conformer_glu(x2d, shift, scale, w1_bf, b1, rows_per_group, tm):
    n, d = x2d.shape
    dc = w1_bf.shape[1] // 2
    mod = _mod_spec(shift, tm, rows_per_group)
    b1r = b1.reshape(1, 2 * dc)
    return pl.pallas_call(
        functools.partial(_glu_kernel, cw=256),
        grid=(n // tm,),
        in_specs=[pl.BlockSpec((tm, d), lambda i: (i, 0)), mod, mod,
                  _resident(w1_bf), _resident(b1r)],
        out_specs=pl.BlockSpec((tm, dc), lambda i: (i, 0)),
        out_shape=jax.ShapeDtypeStruct((n, dc), F32),
        compiler_params=_params("parallel"),
        name="conformer_glu",
    )(x2d, shift, scale, w1_bf, b1r)


PAD_ROWS = 32


def _dwconv_rows(win_ref, w_ref, bias, r0, rs, cols, width):
    off = PAD_ROWS - (width - 1)
    y = bias
    for r in range(SUBLANES):
        z = None
        for m in range((off + width - 1) // SUBLANES + 1):
            k = SUBLANES * m + r - off
            if 0 <= k < width:
                lo = r0 + SUBLANES * m
                term = w_ref[k:k + 1, cols] * win_ref[lo:lo + rs + SUBLANES, cols]
                z = term if z is None else z + term
        if z is not None:
            y = y + z[r:r + rs, :]
    return y


def _ordered_after(x, token):
    zero = lax.shift_right_logical(lax.shift_right_logical(
        lax.bitcast_convert_type(token, jnp.uint32), jnp.uint32(16)), jnp.uint32(16))
    return lax.bitcast_convert_type(lax.bitcast_convert_type(x, jnp.uint32) | zero, F32)


def _conformer_phases(x, sh_ref, sc_ref, gate_ref, w1_ref, b1_ref, wdw_ref, bdw_ref,
                      gl_ref, bl_ref, w2_ref, b2_ref, g_ref, b_ref, o_ref, tail_ref, win_ref, y_ref,
                      *, width, rs, cw, after=None):
    tm, dc = y_ref.shape
    h = (x * (1.0 + sc_ref[...]) + sh_ref[...]).astype(BF16)
    for c in range(dc // cw):
        ca = slice(c * cw, (c + 1) * cw)
        cg = slice(dc + c * cw, dc + (c + 1) * cw)
        a = _dot(h, w1_ref[:, ca]) + b1_ref[:, ca]
        g = _dot(h, w1_ref[:, cg]) + b1_ref[:, cg]
        win_ref[PAD_ROWS:PAD_ROWS + tm, ca] = a * jax.nn.sigmoid(g)
    tail_ref[...] = win_ref[tm:tm + PAD_ROWS, :]
    yield
    for c in range(dc // HEAD_DIM):
        cols = slice(c * HEAD_DIM, (c + 1) * HEAD_DIM)
        for r0 in range(0, tm, rs):
            bias = bdw_ref[:, cols]
            if after is not None and after[0] is not None:
                bias = _ordered_after(bias, after[0])
            y_ref[r0:r0 + rs, cols] = _dwconv_rows(win_ref, wdw_ref, bias, r0, rs, cols, width)
            yield
    win_ref[0:PAD_ROWS, :] = win_ref[tm:tm + PAD_ROWS, :]
    yn = _silu(_layer_norm(y_ref[...], gl_ref[...], bl_ref[...]))
    y = _dot(yn.astype(BF16), w2_ref[...]) + b2_ref[...]
    v = ALPHA * x + (1.0 + gate_ref[...]) * y
    o_ref[...] = _layer_norm(v, g_ref[...], b_ref[...])
    yield


N_CONF_CONSTS = 10
N_FFN_CONSTS = 6


def _conformer_ffn_kernel(*refs, tiles_per_seq, width, rs, cw, d_ff):
    x_ref, shc_ref, scc_ref, gtc_ref, shf_ref, scf_ref, gtf_ref = refs[:7]
    conf = refs[7:7 + N_CONF_CONSTS]
    ffn = refs[7 + N_CONF_CONSTS:7 + N_CONF_CONSTS + N_FFN_CONSTS]
    o_ref, tail_ref, prev_ref, win_ref, y_ref, x1_s, h_s, act_s, hist_s = refs[7 + N_CONF_CONSTS + N_FFN_CONSTS:]
    i = pl.program_id(0)
    n_tiles = pl.num_programs(0) - 1
    tm, dc = y_ref.shape

    @pl.when(i == 0)
    def _():
        x1_s[...] = jnp.zeros_like(x1_s)
        win_ref[PAD_ROWS + tm:, :] = jnp.zeros((SUBLANES, dc), F32)

    @pl.when(jnp.minimum(i, n_tiles - 1) % tiles_per_seq == 0)
    def _():
        win_ref[0:PAD_ROWS, :] = jnp.zeros((PAD_ROWS, dc), F32)

    @pl.when(jnp.maximum(i - 1, 0) % tiles_per_seq == 0)
    def _():
        hist_s[...] = jnp.zeros_like(hist_s)

    ffn_ph = _ffn_phases(x1_s[...], shf_ref, scf_ref, gtf_ref, *ffn, o_ref, prev_ref, h_s, act_s, hist_s,
                         d_ff=d_ff, cw=cw, planes=False)
    token = [None]
    conf_ph = _conformer_phases(x_ref[...], shc_ref, scc_ref, gtc_ref, *conf, x1_s, tail_ref, win_ref, y_ref,
                                width=width, rs=rs, cw=cw, after=token)
    next(conf_ph)
    n_ffn = d_ff // cw
    n_conv = (dc // HEAD_DIM) * (tm // rs)
    per_chunk = -(-n_conv // n_ffn)
    for _ in range(n_ffn):
        token[0] = next(ffn_ph)
        for _ in range(per_chunk):
            next(conf_ph, None)
    _run(ffn_ph)
    _run(conf_ph)


def conformer_ffn_prompt(x2d, mods_conf, mods_ffn, w1_bf, b1, w_dw, b_dw, cf_g, cf_b, w2_bf, b2,
                         ln_g0, ln_b0, w_up_bf, ffn_w_dw, ffn_b_dw, w_dn_bf, layer, ln_g1, ln_b1,
                         batch, seq, tm):
    n, d = x2d.shape
    dc = w2_bf.shape[0]
    d_ff = w_dn_bf.shape[1]
    width = w_dw.shape[0]
    assert width - 1 <= PAD_ROWS and tm % PAD_ROWS == 0
    nj = seq // tm
    n_tiles = n // tm
    row = lambda a: a.reshape(1, -1)
    conf_tile = lambda i: jnp.minimum(i, n_tiles - 1)
    ffn_tile = lambda i: jnp.maximum(i - 1, 0)
    mod_c = pl.BlockSpec((None, 1, d), lambda i: (conf_tile(i) // nj, 0, 0))
    mod_f = pl.BlockSpec((None, 1, d), lambda i: (ffn_tile(i) // nj, 0, 0))
    conf = [w1_bf, row(b1), w_dw, row(b_dw), row(cf_g), row(cf_b), w2_bf, row(b2), row(ln_g0), row(ln_b0)]
    ffn = [ffn_w_dw, row(ffn_b_dw), row(ln_g1), row(ln_b1)]
    assert len(conf) == N_CONF_CONSTS and len(ffn) + 2 == N_FFN_CONSTS
    ffn_specs = [_resident(w_up_bf, layer), _resident(ffn[0]), _resident(ffn[1]),
                 _resident(w_dn_bf, layer), _resident(ffn[2]), _resident(ffn[3])]
    ffn_args = [w_up_bf, ffn[0], ffn[1], w_dn_bf, ffn[2], ffn[3]]
    return pl.pallas_call(
        functools.partial(_conformer_ffn_kernel, tiles_per_seq=nj, width=width, rs=64, cw=256, d_ff=d_ff),
        grid=(n_tiles + 1,),
        in_specs=[pl.BlockSpec((tm, d), lambda i: (conf_tile(i), 0)), mod_c, mod_c, mod_c,
                  mod_f, mod_f, mod_f] + [_resident(a) for a in conf] + ffn_specs,
        out_specs=[pl.BlockSpec((tm, d), lambda i: (ffn_tile(i), 0)),
                   pl.BlockSpec((None, PAD_ROWS, dc), lambda i: (conf_tile(i) // nj, 0, 0)),
                   pl.BlockSpec((None, 2, d_ff), lambda i: (ffn_tile(i), 0, 0))],
        out_shape=[jax.ShapeDtypeStruct((n, d), F32),
                   jax.ShapeDtypeStruct((batch, PAD_ROWS, dc), F32),
                   jax.ShapeDtypeStruct((n_tiles, 2, d_ff), F32)],
        scratch_shapes=[pltpu.VMEM((PAD_ROWS + tm + SUBLANES, dc), F32), pltpu.VMEM((tm, dc), F32),
                        pltpu.VMEM((tm, d), F32), pltpu.VMEM((tm, d), BF16),
                        pltpu.VMEM((tm, d_ff), BF16), pltpu.VMEM((2, d_ff), F32)],
        compiler_params=_params("arbitrary"),
        name="conformer_ffn_prompt",
    )(x2d, *mods_conf, *mods_ffn, *conf, *ffn_args)


def _dwconv_sample_kernel(x_ref, st_ref, w_ref, b_ref, o_ref, nst_ref, *, width, t):
    hist = width - 1
    bb, dc = x_ref.shape[1], x_ref.shape[2]
    plane = lambda j, cols: st_ref[j, :, cols] if j < hist else x_ref[j - hist, :, cols]
    for c in range(dc // HEAD_DIM):
        cols = slice(c * HEAD_DIM, (c + 1) * HEAD_DIM)
        acc = [jnp.zeros((bb, HEAD_DIM), F32) + b_ref[:, cols] for _ in range(t)]
        for j in range(hist + t):
            p = plane(j, cols)
            for ti in range(t):
                if 0 <= j - ti < width:
                    acc[ti] = acc[ti] + w_ref[j - ti:j - ti + 1, cols] * p
            if j >= t:
                nst_ref[j - t, :, cols] = p
        for ti in range(t):
            o_ref[ti, :, cols] = acc[ti]


def dwconv_sample(glu, states, layer, w_dw, b_dw):
    t, batch, dc = glu.shape
    width = w_dw.shape[0]
    hist = width - 1
    bb = 32
    b2 = b_dw.reshape(1, dc)
    return pl.pallas_call(
        functools.partial(_dwconv_sample_kernel, width=width, t=t),
        grid=(batch // bb,),
        in_specs=[pl.BlockSpec((t, bb, dc), lambda i: (0, i, 0)),
                  pl.BlockSpec((None, hist, bb, dc), lambda i: (layer, 0, i, 0)),
                  pl.BlockSpec(w_dw.shape, lambda i: (0, 0)),
                  pl.BlockSpec(b2.shape, lambda i: (0, 0))],
        out_specs=[pl.BlockSpec((t, bb, dc), lambda i: (0, i, 0)),
                   pl.BlockSpec((hist, bb, dc), lambda i: (0, i, 0))],
        out_shape=[jax.ShapeDtypeStruct((t, batch, dc), F32),
                   jax.ShapeDtypeStruct((hist, batch, dc), F32)],
        compiler_params=_params("parallel"),
        name="dwconv_sample",
    )(glu, states, w_dw, b2)


def _conf_tail_kernel(y_ref, gl_ref, bl_ref, w_ref, b2_ref, x_ref, gate_ref, g_ref, b_ref, o_ref):
    yn = _silu(_layer_norm(y_ref[...], gl_ref[...], bl_ref[...]))
    y = _dot(yn.astype(BF16), w_ref[...]) + b2_ref[...]
    v = ALPHA * x_ref[...] + (1.0 + gate_ref[...]) * y
    o_ref[...] = _layer_norm(v, g_ref[...], b_ref[...])


def conformer_tail(y, cf_g, cf_b, w2_bf, b2, x2d, gate, ln_g, ln_b, rows_per_group, tm):
    n, d = x2d.shape
    dc = y.shape[1]
    vecc = pl.BlockSpec((1, dc), lambda i: (0, 0))
    vec = pl.BlockSpec((1, d), lambda i: (0, 0))
    return pl.pallas_call(
        _conf_tail_kernel,
        grid=(n // tm,),
        in_specs=[pl.BlockSpec((tm, dc), lambda i: (i, 0)), vecc, vecc,
                  _resident(w2_bf), vec,
                  pl.BlockSpec((tm, d), lambda i: (i, 0)),
                  _mod_spec(gate, tm, rows_per_group), vec, vec],
        out_specs=pl.BlockSpec((tm, d), lambda i: (i, 0)),
        out_shape=jax.ShapeDtypeStruct((n, d), F32),
        compiler_params=_params("parallel"),
        name="conformer_tail",
    )(y, cf_g.reshape(1, dc), cf_b.reshape(1, dc), w2_bf, b2.reshape(1, d), x2d, gate,
      ln_g.reshape(1, d), ln_b.reshape(1, d))


def kernel(x_prompt, x_sample, cache_k, cache_v, state_ret, state_conv, state_ffn, page_table, c_prompt, c_sample, ab_w_in, ab_w_out, cf_w_pw1, cf_b_pw1, cf_w_dw, cf_b_dw, cf_ln_g, cf_ln_b, cf_w_pw2, cf_b_pw2, ffn_w_up, ffn_w_dw, ffn_b_dw, ffn_w_down, ada_w, ada_b, ln_g, ln_b):
    bp, seq, d = x_prompt.shape
    bs, ts, _ = x_sample.shape
    n_layers = ada_w.shape[0]
    d_ff = ffn_w_down.shape[1]
    past_len = page_table.shape[1] * cache_k.shape[2]
    np_, ns = bp * seq, bs * ts
    tm_p = min(256, seq)
    tm_s = bs

    ada = adaln_all(jnp.concatenate([c_prompt, c_sample], 0), ada_w, ada_b)

    def mods(l, k):
        return ada[l, k, :bp][:, None, :], ada[l, k, bp:][None]

    token_major = lambda a: jnp.swapaxes(a, 0, 1)
    xp = x_prompt.reshape(np_, d)
    xs = token_major(x_sample).reshape(ns, d)
    tabs_p = rotary_tables(jnp.arange(seq, dtype=jnp.int32))
    tabs_s = tuple(jnp.repeat(tb, bs, axis=0) for tb in
                   rotary_tables(past_len + jnp.arange(ts, dtype=jnp.int32)))

    w_up = to_bf16(ffn_w_up)
    w_dn = to_bf16(ffn_w_down)
    outs = {k: [] for k in ("kp", "vp", "ks", "vs", "rp", "rs", "cp", "cs", "fp", "fs")}
    for l in range(n_layers):
        i = l // 2
        sh_p, sh_s = mods(l, 0)
        sc_p, sc_s = mods(l, 1)
        gt_p, gt_s = mods(l, 2)
        if l % 2 == 0:
            w_in = to_bf16(ab_w_in[i])
            w_out = to_bf16(ab_w_out[i])
            rq, rk, rv, rg, mq, mk, mv, kb, vb, kmean = in_proj(
                xp, sh_p, sc_p, w_in, tabs_p, seq, min(2 * tm_p, seq), True, BF16)
            o_mp, o_rp, s_p = attention_prompt(mq, kb, vb, kmean.reshape(-1, GROUP_W), rq, rk, rv, rg, bp, seq)
            outs["kp"].append(mk.reshape(bp, seq, N_HEADS, HEAD_DIM))
            outs["vp"].append(mv.reshape(bp, seq, N_HEADS, HEAD_DIM))
            outs["rp"].append(s_p)

            rq, rk, rv, rg, mq, mk, mv, kb, vb = in_proj(xs, sh_s, sc_s, w_in, tabs_s, ns, tm_s, False, F32)
            o_r, s_s = retention_sample(rq, rk, rv, rg, state_ret, i, ts)
            n_phys, page = cache_k.shape[1], cache_k.shape[2]
            ck = cache_k.reshape(-1, HEAD_DIM)
            cv = cache_v.reshape(-1, HEAD_DIM)
            mods_ffn_p = tuple(mods(l, k)[0] for k in (3, 4, 5))
            o_m, xp, prev_p = moba_sample_ffn_prompt(
                mq, kb, vb, ck, cv, page_table, i * n_phys, page, ts,
                o_rp, o_mp, w_out, xp, gt_p, ln_g[l, 0], ln_b[l, 0], *mods_ffn_p, w_up, ffn_w_dw[l], ffn_b_dw[l], w_dn, l, ln_g[l, 1], ln_b[l, 1], seq, tm_p)
            o_m = token_major(o_m[:, :ts]).reshape(ns, GROUP_W)
            xs = out_proj(o_r, o_m, w_out, xs, gt_s, ln_g[l, 0], ln_b[l, 0], ns, tm_s)
            outs["ks"].append(token_major(mk.reshape(ts, bs, N_HEADS, HEAD_DIM)))
            outs["vs"].append(token_major(mv.reshape(ts, bs, N_HEADS, HEAD_DIM)))
            outs["rs"].append(s_s)
        else:
            w1 = to_bf16(cf_w_pw1[i])
            w2 = to_bf16(cf_w_pw2[i])
            hist = cf_w_dw.shape[1] - 1
            mods_ffn_p = tuple(mods(l, k)[0] for k in (3, 4, 5))
            xp, tail, prev_p = conformer_ffn_prompt(
                xp, (sh_p, sc_p, gt_p), mods_ffn_p, w1, cf_b_pw1[i], cf_w_dw[i], cf_b_dw[i],
                cf_ln_g[i], cf_ln_b[i], w2, cf_b_pw2[i], ln_g[l, 0], ln_b[l, 0],
                w_up, ffn_w_dw[l], ffn_b_dw[l], w_dn, l, ln_g[l, 1], ln_b[l, 1], bp, seq, tm_p)
            outs["cp"].append(tail[:, PAD_ROWS - hist:])

            glu = conformer_glu(xs, sh_s, sc_s, w1, cf_b_pw1[i], ns, tm_s)
            y, nst = dwconv_sample(glu.reshape(ts, bs, -1), jnp.swapaxes(state_conv, 1, 2), i,
                                   cf_w_dw[i], cf_b_dw[i])
            xs = conformer_tail(y.reshape(ns, -1), cf_ln_g[i], cf_ln_b[i], w2, cf_b_pw2[i], xs, gt_s,
                                ln_g[l, 0], ln_b[l, 0], ns, tm_s)
            outs["cs"].append(token_major(nst))

        outs["fp"].append(prev_p.reshape(bp, seq // tm_p, 2, d_ff)[:, -1])
        sh_s, sc_s, gt_s = (mods(l, k)[1] for k in (3, 4, 5))
        xs, u_s = conv_ffn_planes(xs, sh_s, sc_s, gt_s, w_up, ffn_w_dw[l], ffn_b_dw[l], w_dn, l,
                                  ln_g[l, 1], ln_b[l, 1], tm_s, (state_ffn[l, :, 0], state_ffn[l, :, 1]))
        outs["fs"].append(token_major(u_s.reshape(ts, bs, d_ff)[ts - 2:]))

    st = lambda k: outs[k][0][None] if len(outs[k]) == 1 else jnp.stack(outs[k])
    return (xp.reshape(bp, seq, d), token_major(xs.reshape(ts, bs, d)), st("kp"), st("vp"), st("ks"), st("vs"),
            st("rp"), st("rs"), st("cp"), st("cs"), st("fp"), st("fs"))
```
